```python
import math
import jax, jax.numpy as jnp
from jax import lax
import numpy as np


D_MODEL = 2048
BATCH = 4
SEQ = 4096
DEPTH = 2

GRID_W = 64
CTX_LEN = 256
EPS = 1e-6

MLA_HEADS = 8
MLA_NOPE = 128
MLA_ROPE = 64
MLA_QK = MLA_NOPE + MLA_ROPE
MLA_V = 128
MLA_Q_LORA = 512
MLA_KV_LORA = 512
MLA_WIDTH = MLA_HEADS * MLA_V
ROPE_BASE = 10000.0
ROPE_PAIRS_PER_AXIS = MLA_ROPE // 4
Q_BLOCK = 128

HY_WIDTH = 512
HY_ORDER = 2
HY_BANDS = 16
HY_EMB = 1 + 2 * HY_BANDS
HY_HIDDEN = 64
HY_FILTER_OUT = HY_ORDER * 2 * HY_WIDTH
HY_SHIFT = 0.05
HY_DECAY_MIN = 3.0
HY_DECAY_MAX = 15.0
SHORT_CONV = 3

ML_HEADS = 4
ML_HEAD_DIM = 128
ML_WIDTH = ML_HEADS * ML_HEAD_DIM
ML_CHUNK = 128
ML_GATES = 4 * ML_HEADS

MIX_WIDTH = MLA_WIDTH + HY_WIDTH + ML_WIDTH
FFN_HIDDEN = ((8 * D_MODEL // 3 + 255) // 256) * 256

IN_SIZES = (MLA_Q_LORA, MLA_KV_LORA, MLA_ROPE, 3 * HY_WIDTH, 2 * ML_WIDTH, ML_WIDTH, ML_WIDTH, ML_GATES)
IN_SPLITS = tuple(sum(IN_SIZES[:i + 1]) for i in range(len(IN_SIZES) - 1))
N_IN = sum(IN_SIZES)

kernel_name = 'hybrid_mla_hyena_mlstm_dit_block'


def rmsnorm(x, g):
    xf = x.astype(jnp.float32)
    y = xf * lax.rsqrt(jnp.mean(xf * xf, axis=-1, keepdims=True) + EPS)
    return (y * g.astype(jnp.float32)).astype(x.dtype)


def modulate(h, shift, scale):
    return h * (1 + scale) + shift


def axial_rope(n_tokens):
    rows = n_tokens // GRID_W
    row = jnp.repeat(jnp.arange(rows, dtype=jnp.float32), GRID_W)
    col = jnp.tile(jnp.arange(GRID_W, dtype=jnp.float32), rows)
    freqs = ROPE_BASE ** (-jnp.arange(ROPE_PAIRS_PER_AXIS, dtype=jnp.float32) / ROPE_PAIRS_PER_AXIS)
    ang = jnp.concatenate([row[:, None] * freqs, col[:, None] * freqs], axis=-1)
    return jnp.cos(ang), jnp.sin(ang)


def apply_rope(x, cos, sin):
    half = x.shape[-1] // 2
    x1, x2 = x[..., :half], x[..., half:]
    cos = cos[None, :, None, :].astype(x.dtype)
    sin = sin[None, :, None, :].astype(x.dtype)
    return jnp.concatenate([x1 * cos - x2 * sin, x2 * cos + x1 * sin], axis=-1)


def short_conv(u, w, b):
    L = u.shape[1]
    pad = SHORT_CONV // 2
    up = jnp.pad(u, ((0, 0), (pad, pad), (0, 0)))
    out = b
    for j in range(SHORT_CONV):
        out = out + up[:, j:j + L] * w[j]
    return out


def mla_heads(cq, ckv, krope, p, rope):
    B, L, _ = cq.shape
    q = (rmsnorm(cq, p['mla_qa_norm']) @ p['mla_w_uq']).reshape(B, L, MLA_HEADS, MLA_QK)
    kv = (rmsnorm(ckv, p['mla_kva_norm']) @ p['mla_w_ukv']).reshape(B, L, MLA_HEADS, MLA_NOPE + MLA_V)
    k_rope = jnp.broadcast_to(krope[:, :, None, :], (B, L, MLA_HEADS, MLA_ROPE))
    k = jnp.concatenate([kv[..., :MLA_NOPE], k_rope], axis=-1)
    v = kv[..., MLA_NOPE:]
    q = rmsnorm(q, p['mla_q_norm'])
    k = rmsnorm(k, p['mla_k_norm'])
    if rope is not None:
        cos, sin = rope
        q = jnp.concatenate([q[..., :MLA_NOPE], apply_rope(q[..., MLA_NOPE:], cos, sin)], axis=-1)
        k = jnp.concatenate([k[..., :MLA_NOPE], apply_rope(k[..., MLA_NOPE:], cos, sin)], axis=-1)
    return q.transpose(0, 2, 1, 3), k.transpose(0, 2, 1, 3), v.transpose(0, 2, 1, 3)


def attend(q, k, v):
    s = jnp.einsum('bhqd,bhkd->bhqk', q, k).astype(jnp.float32) * (MLA_QK ** -0.5)
    pr = jax.nn.softmax(s, axis=-1).astype(v.dtype)
    return jnp.einsum('bhqk,bhkd->bhqd', pr, v)


def block_attention(q, k, v):
    B, H, L, d = q.shape
    nb = L // Q_BLOCK
    qb = jnp.moveaxis(q.reshape(B, H, nb, Q_BLOCK, d), 2, 0)
    ob = lax.map(lambda qi: attend(qi, k, v), qb)
    return jnp.moveaxis(ob, 0, 2).reshape(B, H, L, -1)


def hyena_filter_fft(L, p):
    f32 = lambda a: a.astype(jnp.float32)
    t = jnp.arange(L, dtype=jnp.float32) / L
    bands = jnp.arange(1, HY_BANDS + 1, dtype=jnp.float32)
    ang = 2.0 * math.pi * t[:, None] * bands
    feats = jnp.concatenate([t[:, None], jnp.cos(ang), jnp.sin(ang)], axis=-1)
    h = jnp.sin(feats @ f32(p['hy_w1']) + f32(p['hy_b1']))
    h = jnp.sin(h @ f32(p['hy_w2']) + f32(p['hy_b2']))
    h = (h @ f32(p['hy_w3'])) * (jnp.exp(-t[:, None] * f32(p['hy_decay'])) + HY_SHIFT)
    h = h.reshape(L, HY_ORDER, 2, HY_WIDTH)
    h_fwd, h_bwd = h[:, :, 0], h[:, :, 1]
    buf = jnp.concatenate([h_fwd, jnp.zeros((1, HY_ORDER, HY_WIDTH), jnp.float32), h_bwd[:0:-1]], axis=0)
    return jnp.fft.rfft(buf, axis=0)


def fft_conv(z, kf):
    L = z.shape[1]
    zf = jnp.fft.rfft(z.astype(jnp.float32), n=2 * L, axis=1)
    y = jnp.fft.irfft(zf * kf[None], n=2 * L, axis=1)[:, :L]
    return y.astype(z.dtype)


def hyena(u, p):
    L = u.shape[1]
    v, x1, x2 = jnp.split(short_conv(u, p['hy_conv_w'], p['hy_conv_b']), 3, axis=-1)
    kf = hyena_filter_fft(L, p)
    z = v
    for n, gate in enumerate((x1, x2)):
        z = gate * (fft_conv(z, kf[:, n]) + p['hy_skip'][n] * z)
    return z


def mlstm_inputs(qk, v, gates, p):
    B, L, _ = qk.shape
    qk = jax.nn.silu(short_conv(qk, p['ml_conv_w'], p['ml_conv_b']))
    q, k = jnp.split(qk, 2, axis=-1)
    heads = lambda t: t.reshape(B, L, ML_HEADS, ML_HEAD_DIM).transpose(0, 2, 1, 3).astype(jnp.float32)
    q = heads(q) * (ML_HEAD_DIM ** -0.5)
    k = heads(k)
    v = heads(v)
    g = (gates + p['ml_gate_b']).astype(jnp.float32).reshape(B, L, 4, ML_HEADS).transpose(2, 0, 3, 1)
    log_i = g[:2]
    log_f = jax.nn.log_sigmoid(g[2:])
    return q, k, v, log_i, log_f


def mlstm_scan(q, k, v, log_i, log_f, state):
    B, H, L, _ = q.shape
    nc = L // ML_CHUNK
    chunks = lambda t: jnp.moveaxis(t.reshape((B, H, nc, ML_CHUNK) + t.shape[3:]), 2, 0)
    causal = jnp.tril(jnp.ones((ML_CHUNK, ML_CHUNK), dtype=bool))

    def step(carry, inp):
        C, n, m = carry
        qc, kc, vc, lic, lfc = inp
        b = jnp.cumsum(lfc, axis=-1)
        d = jnp.where(causal, b[..., :, None] - b[..., None, :] + lic[..., None, :], -jnp.inf)
        inter = b + m[..., None]
        m_t = jnp.maximum(inter, jnp.max(d, axis=-1))
        dexp = jnp.exp(d - m_t[..., None])
        inter_w = jnp.exp(inter - m_t)
        s = jnp.einsum('bhtd,bhsd->bhts', qc, kc) * dexp
        num = inter_w[..., None] * jnp.einsum('bhtd,bhde->bhte', qc, C) + jnp.einsum('bhts,bhse->bhte', s, vc)
        den = inter_w * jnp.einsum('bhtd,bhd->bht', qc, n) + jnp.sum(s, axis=-1)
        h = num / jnp.maximum(jnp.abs(den), jnp.exp(-m_t))[..., None]
        b_last = b[..., -1]
        g_s = b_last[..., None] - b + lic
        m_new = jnp.maximum(b_last + m, jnp.max(g_s, axis=-1))
        a = jnp.exp(b_last + m - m_new)
        w = jnp.exp(g_s - m_new[..., None])
        C_new = a[..., None, None] * C + jnp.einsum('bhsd,bhse->bhde', kc * w[..., None], vc)
        n_new = a[..., None] * n + jnp.einsum('bhs,bhsd->bhd', w, kc)
        return (C_new, n_new, m_new), h

    state, h = lax.scan(step, state, (chunks(q), chunks(k), chunks(v), chunks(log_i), chunks(log_f)))
    return jnp.moveaxis(h, 0, 2).reshape(B, H, L, -1), state


def mlstm_bidir(ctx_in, lat_in):
    qc, kc, vc, lic, lfc = ctx_in
    ql, kl, vl, lil, lfl = lat_in
    B = qc.shape[0]
    zero = (jnp.zeros((B, ML_HEADS, ML_HEAD_DIM, ML_HEAD_DIM), jnp.float32),
            jnp.zeros((B, ML_HEADS, ML_HEAD_DIM), jnp.float32),
            jnp.zeros((B, ML_HEADS), jnp.float32))
    flip = lambda t: jnp.flip(t, axis=2)
    h_cf, s_f = mlstm_scan(qc, kc, vc, lic[0], lfc[0], zero)
    h_lf, _ = mlstm_scan(ql, kl, vl, lil[0], lfl[0], s_f)
    h_cb, s_b = mlstm_scan(flip(qc), flip(kc), flip(vc), flip(lic[1]), flip(lfc[1]), zero)
    h_lb, _ = mlstm_scan(flip(ql), flip(kl), flip(vl), flip(lil[1]), flip(lfl[1]), s_b)
    return h_cf + flip(h_cb), h_lf + flip(h_lb)


def combine_groups(a, y, h, o, p):
    g = p['mix_norm_g']
    ml = rmsnorm(h.transpose(0, 2, 1, 3).astype(o.dtype), g[MLA_WIDTH + HY_WIDTH:].reshape(ML_HEADS, ML_HEAD_DIM))
    ml = ml.reshape(o.shape) * jax.nn.sigmoid(o)
    cat = jnp.concatenate([rmsnorm(a, g[:MLA_WIDTH]), rmsnorm(y, g[MLA_WIDTH:MLA_WIDTH + HY_WIDTH]), ml], axis=-1)
    return cat @ p['w_out']


def mixer(pl, pc, need_ctx, p, rope):
    lat = jnp.split(pl, IN_SPLITS, axis=-1)
    ctx = jnp.split(pc, IN_SPLITS, axis=-1)
    merge = lambda t: t.transpose(0, 2, 1, 3).reshape(t.shape[0], t.shape[2], -1)
    q_l, k_l, v_l = mla_heads(lat[0], lat[1], lat[2], p, rope)
    q_c, k_c, v_c = mla_heads(ctx[0], ctx[1], ctx[2], p, None)
    k_all = jnp.concatenate([k_c, k_l], axis=2)
    v_all = jnp.concatenate([v_c, v_l], axis=2)
    a_l = merge(block_attention(q_l, k_all, v_all))
    y_l = hyena(lat[3], p)
    h_c, h_l = mlstm_bidir(mlstm_inputs(ctx[4], ctx[5], ctx[7], p), mlstm_inputs(lat[4], lat[5], lat[7], p))
    out_l = combine_groups(a_l, y_l, h_l, lat[6], p)
    out_c = None
    if need_ctx:
        a_c = merge(attend(q_c, k_c, v_c))
        y_c = hyena(ctx[3], p)
        out_c = combine_groups(a_c, y_c, h_c, ctx[6], p)
    return out_l, out_c


def swiglu(h, w1, w2):
    gate, up = jnp.split(h @ w1, 2, axis=-1)
    return (jax.nn.silu(gate) * up) @ w2


def setup_inputs(seed: int = 0) -> dict:
    key = jax.random.key(seed)
    ks = iter(jax.random.split(key, 40))
    f32 = jnp.float32

    def nrm(shape, std):
        return std * jax.random.normal(next(ks), shape, f32)

    def gain(shape):
        return 1.0 + 0.02 * jax.random.normal(next(ks), shape, f32)

    x = nrm((BATCH, SEQ, D_MODEL), 1.0)
    c = nrm((BATCH, D_MODEL), 1.0)
    ctx = nrm((BATCH, CTX_LEN, D_MODEL), 1.0)
    c_ctx = nrm((D_MODEL,), 1.0)
    ada_w = nrm((DEPTH, D_MODEL, 6 * D_MODEL), 0.5 * D_MODEL ** -0.5)
    ada_b = nrm((DEPTH, 6 * D_MODEL), 0.01)
    norm1_g = gain((DEPTH, D_MODEL))
    norm2_g = gain((DEPTH, D_MODEL))
    w_in = nrm((DEPTH, D_MODEL, N_IN), D_MODEL ** -0.5)
    mla_qa_norm = gain((DEPTH, MLA_Q_LORA))
    mla_kva_norm = gain((DEPTH, MLA_KV_LORA))
    mla_w_uq = nrm((DEPTH, MLA_Q_LORA, MLA_HEADS * MLA_QK), MLA_Q_LORA ** -0.5)
    mla_w_ukv = nrm((DEPTH, MLA_KV_LORA, MLA_HEADS * (MLA_NOPE + MLA_V)), MLA_KV_LORA ** -0.5)
    mla_q_norm = gain((DEPTH, MLA_QK))
    mla_k_norm = gain((DEPTH, MLA_QK))
    hy_conv_w = nrm((DEPTH, SHORT_CONV, 3 * HY_WIDTH), SHORT_CONV ** -0.5)
    hy_conv_b = nrm((DEPTH, 3 * HY_WIDTH), 0.01)
    hy_w1 = nrm((DEPTH, HY_EMB, HY_HIDDEN), 1.0)
    hy_b1 = nrm((DEPTH, HY_HIDDEN), 0.1)
    hy_w2 = nrm((DEPTH, HY_HIDDEN, HY_HIDDEN), HY_HIDDEN ** -0.5)
    hy_b2 = nrm((DEPTH, HY_HIDDEN), 0.1)
    hy_w3 = nrm((DEPTH, HY_HIDDEN, HY_FILTER_OUT), 0.01)
    hy_decay = jax.random.uniform(next(ks), (DEPTH, HY_FILTER_OUT), f32, HY_DECAY_MIN, HY_DECAY_MAX)
    hy_skip = nrm((DEPTH, HY_ORDER, HY_WIDTH), 0.5)
    ml_conv_w = nrm((DEPTH, SHORT_CONV, 2 * ML_WIDTH), SHORT_CONV ** -0.5)
    ml_conv_b = nrm((DEPTH, 2 * ML_WIDTH), 0.01)
    ml_gate_b = jnp.concatenate([nrm((DEPTH, 2 * ML_HEADS), 0.1),
                                 jax.random.uniform(next(ks), (DEPTH, 2 * ML_HEADS), f32, 3.0, 6.0)], axis=-1)
    mix_norm_g = gain((DEPTH, MIX_WIDTH))
    w_out = nrm((DEPTH, MIX_WIDTH, D_MODEL), MIX_WIDTH ** -0.5)
    ffn_w1 = nrm((DEPTH, D_MODEL, 2 * FFN_HIDDEN), D_MODEL ** -0.5)
    ffn_w2 = nrm((DEPTH, FFN_HIDDEN, D_MODEL), FFN_HIDDEN ** -0.5)
    return {'x': x, 'c': c, 'ctx': ctx, 'c_ctx': c_ctx, 'ada_w': ada_w, 'ada_b': ada_b,
            'norm1_g': norm1_g, 'norm2_g': norm2_g, 'w_in': w_in,
            'mla_qa_norm': mla_qa_norm, 'mla_kva_norm': mla_kva_norm, 'mla_w_uq': mla_w_uq,
            'mla_w_ukv': mla_w_ukv, 'mla_q_norm': mla_q_norm, 'mla_k_norm': mla_k_norm,
            'hy_conv_w': hy_conv_w, 'hy_conv_b': hy_conv_b, 'hy_w1': hy_w1, 'hy_b1': hy_b1,
            'hy_w2': hy_w2, 'hy_b2': hy_b2, 'hy_w3': hy_w3, 'hy_decay': hy_decay, 'hy_skip': hy_skip,
            'ml_conv_w': ml_conv_w, 'ml_conv_b': ml_conv_b, 'ml_gate_b': ml_gate_b,
            'mix_norm_g': mix_norm_g, 'w_out': w_out, 'ffn_w1': ffn_w1, 'ffn_w2': ffn_w2}


def reference(x, c, ctx, c_ctx, ada_w, ada_b, norm1_g, norm2_g, w_in,
              mla_qa_norm, mla_kva_norm, mla_w_uq, mla_w_ukv, mla_q_norm, mla_k_norm,
              hy_conv_w, hy_conv_b, hy_w1, hy_b1, hy_w2, hy_b2, hy_w3, hy_decay, hy_skip,
              ml_conv_w, ml_conv_b, ml_gate_b, mix_norm_g, w_out, ffn_w1, ffn_w2):
    rope = axial_rope(x.shape[1])
    silu_c = jax.nn.silu(c)
    silu_cc = jax.nn.silu(c_ctx)
    for l in range(DEPTH):
        need_ctx = l < DEPTH - 1
        p = {'mla_qa_norm': mla_qa_norm[l], 'mla_kva_norm': mla_kva_norm[l], 'mla_w_uq': mla_w_uq[l],
             'mla_w_ukv': mla_w_ukv[l], 'mla_q_norm': mla_q_norm[l], 'mla_k_norm': mla_k_norm[l],
             'hy_conv_w': hy_conv_w[l], 'hy_conv_b': hy_conv_b[l], 'hy_w1': hy_w1[l], 'hy_b1': hy_b1[l],
             'hy_w2': hy_w2[l], 'hy_b2': hy_b2[l], 'hy_w3': hy_w3[l], 'hy_decay': hy_decay[l],
             'hy_skip': hy_skip[l], 'ml_conv_w': ml_conv_w[l], 'ml_conv_b': ml_conv_b[l],
             'ml_gate_b': ml_gate_b[l], 'mix_norm_g': mix_norm_g[l], 'w_out': w_out[l]}
        mod_l = jnp.split((silu_c @ ada_w[l] + ada_b[l])[:, None, :], 6, axis=-1)
        mod_c = jnp.split(silu_cc @ ada_w[l] + ada_b[l], 6, axis=-1)
        xn = modulate(rmsnorm(x, norm1_g[l]), mod_l[0], mod_l[1])
        cn = modulate(rmsnorm(ctx, norm1_g[l]), mod_c[0], mod_c[1])
        y_l, y_c = mixer(xn @ w_in[l], cn @ w_in[l], need_ctx, p, rope)
        x = x + mod_l[2] * y_l
        x = x + mod_l[5] * swiglu(modulate(rmsnorm(x, norm2_g[l]), mod_l[3], mod_l[4]), ffn_w1[l], ffn_w2[l])
        if need_ctx:
            ctx = ctx + mod_c[2] * y_c
            ctx = ctx + mod_c[5] * swiglu(modulate(rmsnorm(ctx, norm2_g[l]), mod_c[3], mod_c[4]), ffn_w1[l], ffn_w2[l])
    return x
```

```python
import functools
import math

import jax
import jax.numpy as jnp
from jax import lax
from jax.experimental import pallas as pl
from jax.experimental.pallas import tpu as pltpu

D_MODEL = 2048
DEPTH = 2
GRID_W = 64
EPS = 1e-6

MLA_HEADS = 8
MLA_NOPE = 128
MLA_ROPE = 64
MLA_QK = MLA_NOPE + MLA_ROPE
MLA_V = 128
MLA_Q_LORA = 512
MLA_KV_LORA = 512
MLA_WIDTH = MLA_HEADS * MLA_V
ROPE_BASE = 10000.0
ROPE_PAIRS_PER_AXIS = MLA_ROPE // 4
QK_PAD = 256

HY_WIDTH = 512
HY_ORDER = 2
HY_BANDS = 16
HY_SHIFT = 0.05
SHORT_CONV = 3

ML_HEADS = 4
ML_HEAD_DIM = 128
ML_WIDTH = ML_HEADS * ML_HEAD_DIM
ML_CHUNK = 128
ML_GATES = 4 * ML_HEADS

MIX_WIDTH = MLA_WIDTH + HY_WIDTH + ML_WIDTH
FFN_HIDDEN = ((8 * D_MODEL // 3 + 255) // 256) * 256

IN_SIZES = (MLA_Q_LORA, MLA_KV_LORA, MLA_ROPE, 3 * HY_WIDTH, 2 * ML_WIDTH, ML_WIDTH, ML_WIDTH, ML_GATES)
N_IN = sum(IN_SIZES)

P_CQ, P_CKV, P_HY, P_MLQK, P_MLV, P_MLO, P_SMALL = 0, 512, 1024, 2560, 3584, 4096, 4608
N_IN_PAD = 4864

VMEM_LIMIT_BYTES = 56 * 1024 * 1024

BF16 = jnp.bfloat16
F32 = jnp.float32


def _cparams(sem):
    return pltpu.CompilerParams(dimension_semantics=sem, vmem_limit_bytes=VMEM_LIMIT_BYTES)


def _ada_kernel(s_ref, w_ref, b_ref, o_ref):
    w = w_ref[0].astype(BF16)
    o_ref[0] = jnp.dot(s_ref[...], w, preferred_element_type=F32) + b_ref[0]


def ada_modulation(silu_rows, ada_w, ada_b, tn=1024):
    depth, d, n = ada_w.shape
    return pl.pallas_call(
        _ada_kernel,
        grid=(depth, n // tn),
        in_specs=[pl.BlockSpec((8, d), lambda l, j: (0, 0)),
                  pl.BlockSpec((1, d, tn), lambda l, j: (l, 0, j)),
                  pl.BlockSpec((1, 1, tn), lambda l, j: (l, 0, j))],
        out_specs=pl.BlockSpec((1, 8, tn), lambda l, j: (l, 0, j)),
        out_shape=jax.ShapeDtypeStruct((depth, 8, n), F32),
        compiler_params=_cparams(("parallel", "parallel")),
        name="ada_modulation",
    )(silu_rows, ada_w, ada_b.reshape(depth, 1, n))


def _normed(x, g, sc, sh):
    y = x * lax.rsqrt(jnp.mean(x * x, axis=-1, keepdims=True) + EPS) * g
    return y * (1.0 + sc) + sh


def _norm_mm_kernel(x_ref, g_ref, sc_ref, sh_ref, w_ref, o_ref, xn_ref):
    @pl.when(pl.program_id(2) == 0)
    def _():
        xn_ref[...] = _normed(x_ref[0], g_ref[...], sc_ref[0], sh_ref[0]).astype(BF16)

    o_ref[0] = jnp.dot(xn_ref[...], w_ref[...], preferred_element_type=F32).astype(o_ref.dtype)


def norm_mm(x, g, sc, sh, w, *, x_col=0, k=None, tm, tn, out_dtype=F32, name="norm_mm"):
    b, s, _ = x.shape
    k = w.shape[0] if k is None else k
    n = w.shape[1]
    return pl.pallas_call(
        _norm_mm_kernel,
        grid=(b, s // tm, n // tn),
        in_specs=[pl.BlockSpec((1, tm, k), lambda bi, i, j: (bi, i, x_col)),
                  pl.BlockSpec((1, k), lambda bi, i, j: (0, 0)),
                  pl.BlockSpec((1, 1, k), lambda bi, i, j: (bi, 0, 0)),
                  pl.BlockSpec((1, 1, k), lambda bi, i, j: (bi, 0, 0)),
                  pl.BlockSpec((k, tn), lambda bi, i, j: (0, j))],
        out_specs=pl.BlockSpec((1, tm, tn), lambda bi, i, j: (bi, i, j)),
        out_shape=jax.ShapeDtypeStruct((b, s, n), out_dtype),
        scratch_shapes=[pltpu.VMEM((tm, k), BF16)],
        compiler_params=_cparams(("parallel", "parallel", "arbitrary")),
        name=name,
    )(x, g.reshape(1, k), sc, sh, w)


def _norm_swiglu_kernel(x_ref, g_ref, sc_ref, sh_ref, wg_ref, wu_ref, o_ref, xn_ref):
    @pl.when(pl.program_id(2) == 0)
    def _():
        xn_ref[...] = _normed(x_ref[0], g_ref[...], sc_ref[0], sh_ref[0]).astype(BF16)

    xn = xn_ref[...]
    gate = jnp.dot(xn, wg_ref[...], preferred_element_type=F32)
    up = jnp.dot(xn, wu_ref[...], preferred_element_type=F32)
    o_ref[0] = (gate * jax.nn.sigmoid(gate) * up).astype(o_ref.dtype)


def norm_swiglu(x, g, sc, sh, w1, *, tm, tn, name="norm_swiglu"):
    b, s, d = x.shape
    h = w1.shape[1] // 2
    nj = h // tn
    return pl.pallas_call(
        _norm_swiglu_kernel,
        grid=(b, s // tm, nj),
        in_specs=[pl.BlockSpec((1, tm, d), lambda bi, i, j: (bi, i, 0)),
                  pl.BlockSpec((1, d), lambda bi, i, j: (0, 0)),
                  pl.BlockSpec((1, 1, d), lambda bi, i, j: (bi, 0, 0)),
                  pl.BlockSpec((1, 1, d), lambda bi, i, j: (bi, 0, 0)),
                  pl.BlockSpec((d, tn), lambda bi, i, j: (0, j)),
                  pl.BlockSpec((d, tn), lambda bi, i, j: (0, j + nj))],
        out_specs=pl.BlockSpec((1, tm, tn), lambda bi, i, j: (bi, i, j)),
        out_shape=jax.ShapeDtypeStruct((b, s, h), BF16),
        scratch_shapes=[pltpu.VMEM((tm, d), BF16)],
        compiler_params=_cparams(("parallel", "parallel", "arbitrary")),
        name=name,
    )(x, g.reshape(1, d), sc, sh, w1, w1)


def _mm_res_kernel(a_ref, w_ref, r_ref, gt_ref, o_ref):
    y = jnp.dot(a_ref[0], w_ref[...], preferred_element_type=F32)
    o_ref[0] = r_ref[0] + gt_ref[0] * y


def mm_residual(a, w, res, gate, *, tm, tn, name="mm_residual"):
    b, s, k = a.shape
    n = w.shape[1]
    return pl.pallas_call(
        _mm_res_kernel,
        grid=(b, s // tm, n // tn),
        in_specs=[pl.BlockSpec((1, tm, k), lambda bi, i, j: (bi, i, 0)),
                  pl.BlockSpec((k, tn), lambda bi, i, j: (0, j)),
                  pl.BlockSpec((1, tm, tn), lambda bi, i, j: (bi, i, j)),
                  pl.BlockSpec((1, 1, tn), lambda bi, i, j: (bi, 0, j))],
        out_specs=pl.BlockSpec((1, tm, tn), lambda bi, i, j: (bi, i, j)),
        out_shape=jax.ShapeDtypeStruct((b, s, n), F32),
        compiler_params=_cparams(("parallel", "parallel", "parallel")),
        name=name,
    )(a, w, res, gate)


def _attn_kernel(qt_ref, k_ref, vt_ref, o_ref):
    s = jnp.dot(k_ref[0, 0], qt_ref[0, 0], preferred_element_type=F32)
    m = jnp.max(s, axis=0, keepdims=True)
    p = jnp.exp2(s - m)
    l = jnp.sum(p, axis=0, keepdims=True)
    o = jnp.dot(vt_ref[0, 0], p.astype(BF16), preferred_element_type=F32)
    o_ref[0] = o / l


def attention_t(qt, k, vt, *, tq=256):
    b, h, dk, lq = qt.shape
    lk = k.shape[2]
    dv = vt.shape[2]
    return pl.pallas_call(
        _attn_kernel,
        grid=(b, h, lq // tq),
        in_specs=[pl.BlockSpec((1, 1, dk, tq), lambda bi, hi, i: (bi, hi, 0, i)),
                  pl.BlockSpec((1, 1, lk, dk), lambda bi, hi, i: (bi, hi, 0, 0)),
                  pl.BlockSpec((1, 1, dv, lk), lambda bi, hi, i: (bi, hi, 0, 0))],
        out_specs=pl.BlockSpec((1, dv, tq), lambda bi, hi, i: (bi, hi, i)),
        out_shape=jax.ShapeDtypeStruct((b, h * dv, lq), F32),
        compiler_params=_cparams(("parallel", "parallel", "parallel")),
        name="attention",
    )(qt, k, vt)


def _rms(x, g):
    return x * lax.rsqrt(jnp.mean(x * x, axis=-1, keepdims=True) + EPS) * g


def _axial_rope(n_tokens):
    rows = n_tokens // GRID_W
    row = jnp.repeat(jnp.arange(rows, dtype=F32), GRID_W)
    col = jnp.tile(jnp.arange(GRID_W, dtype=F32), rows)
    freqs = ROPE_BASE ** (-jnp.arange(ROPE_PAIRS_PER_AXIS, dtype=F32) / ROPE_PAIRS_PER_AXIS)
    ang = jnp.concatenate([row[:, None] * freqs, col[:, None] * freqs], axis=-1)
    return jnp.cos(ang), jnp.sin(ang)


def _apply_rope(x, cos, sin):
    half = x.shape[-1] // 2
    x1, x2 = x[..., :half], x[..., half:]
    cos = cos[None, :, None, :]
    sin = sin[None, :, None, :]
    return jnp.concatenate([x1 * cos - x2 * sin, x2 * cos + x1 * sin], axis=-1)


def _mla_qkv(q_raw, kv_raw, krope, g_q, g_k, rope):
    b, l, _ = q_raw.shape
    q = q_raw.reshape(b, l, MLA_HEADS, MLA_QK)
    kv = kv_raw.reshape(b, l, MLA_HEADS, MLA_NOPE + MLA_V)
    k_rope = jnp.broadcast_to(krope[:, :, None, :], (b, l, MLA_HEADS, MLA_ROPE))
    k = jnp.concatenate([kv[..., :MLA_NOPE], k_rope], axis=-1)
    v = kv[..., MLA_NOPE:]
    q = _rms(q, g_q)
    k = _rms(k, g_k)
    if rope is not None:
        cos, sin = rope
        q = jnp.concatenate([q[..., :MLA_NOPE], _apply_rope(q[..., MLA_NOPE:], cos, sin)], axis=-1)
        k = jnp.concatenate([k[..., :MLA_NOPE], _apply_rope(k[..., MLA_NOPE:], cos, sin)], axis=-1)
    return q, k, v


def _short_conv(u, w, b):
    L = u.shape[1]
    pad = SHORT_CONV // 2
    up = jnp.pad(u, ((0, 0), (pad, pad), (0, 0)))
    out = b
    for j in range(SHORT_CONV):
        out = out + up[:, j:j + L] * w[j]
    return out


def _hyena_filter_fft(L, p):
    t = jnp.arange(L, dtype=F32) / L
    bands = jnp.arange(1, HY_BANDS + 1, dtype=F32)
    ang = 2.0 * math.pi * t[:, None] * bands
    feats = jnp.concatenate([t[:, None], jnp.cos(ang), jnp.sin(ang)], axis=-1)
    hp = lax.Precision.HIGHEST
    h = jnp.sin(jnp.dot(feats, p['hy_w1'], precision=hp) + p['hy_b1'])
    h = jnp.sin(jnp.dot(h, p['hy_w2'], precision=hp) + p['hy_b2'])
    h = jnp.dot(h, p['hy_w3'], precision=hp) * (jnp.exp(-t[:, None] * p['hy_decay']) + HY_SHIFT)
    h = h.reshape(L, HY_ORDER, 2, HY_WIDTH)
    h_fwd, h_bwd = h[:, :, 0], h[:, :, 1]
    buf = jnp.concatenate([h_fwd, jnp.zeros((1, HY_ORDER, HY_WIDTH), F32), h_bwd[:0:-1]], axis=0)
    return jnp.fft.rfft(buf, axis=0)


def _fft_conv(z, kf):
    L = z.shape[1]
    zf = jnp.fft.rfft(z, n=2 * L, axis=1)
    return jnp.fft.irfft(zf * kf[None], n=2 * L, axis=1)[:, :L]


def _hyena(u, p):
    L = u.shape[1]
    v, x1, x2 = jnp.split(_short_conv(u, p['hy_conv_w'], p['hy_conv_b']), 3, axis=-1)
    kf = _hyena_filter_fft(L, p)
    z = v
    for n, gate in enumerate((x1, x2)):
        z = gate * (_fft_conv(z, kf[:, n]) + p['hy_skip'][n] * z)
    return z


def _mlstm_inputs(qk, v, gates, p):
    B, L, _ = qk.shape
    qk = jax.nn.silu(_short_conv(qk, p['ml_conv_w'], p['ml_conv_b']))
    q, k = jnp.split(qk, 2, axis=-1)
    heads = lambda t: t.reshape(B, L, ML_HEADS, ML_HEAD_DIM).transpose(0, 2, 1, 3)
    q = heads(q) * (ML_HEAD_DIM ** -0.5)
    k = heads(k)
    v = heads(v)
    g = (gates + p['ml_gate_b']).reshape(B, L, 4, ML_HEADS).transpose(2, 0, 3, 1)
    return q, k, v, g[:2], jax.nn.log_sigmoid(g[2:])


def _mlstm_scan(q, k, v, log_i, log_f, state):
    B, H, L, _ = q.shape
    nc = L // ML_CHUNK
    chunks = lambda t: jnp.moveaxis(t.reshape((B, H, nc, ML_CHUNK) + t.shape[3:]), 2, 0)
    causal = jnp.tril(jnp.ones((ML_CHUNK, ML_CHUNK), dtype=bool))

    def step(carry, inp):
        C, n, m = carry
        qc, kc, vc, lic, lfc = inp
        b = jnp.cumsum(lfc, axis=-1)
        d = jnp.where(causal, b[..., :, None] - b[..., None, :] + lic[..., None, :], -jnp.inf)
        inter = b + m[..., None]
        m_t = jnp.maximum(inter, jnp.max(d, axis=-1))
        dexp = jnp.exp(d - m_t[..., None])
        inter_w = jnp.exp(inter - m_t)
        s = jnp.einsum('bhtd,bhsd->bhts', qc, kc) * dexp
        num = inter_w[..., None] * jnp.einsum('bhtd,bhde->bhte', qc, C) + jnp.einsum('bhts,bhse->bhte', s, vc)
        den = inter_w * jnp.einsum('bhtd,bhd->bht', qc, n) + jnp.sum(s, axis=-1)
        h = num / jnp.maximum(jnp.abs(den), jnp.exp(-m_t))[..., None]
        b_last = b[..., -1]
        g_s = b_last[..., None] - b + lic
        m_new = jnp.maximum(b_last + m, jnp.max(g_s, axis=-1))
        a = jnp.exp(b_last + m - m_new)
        w = jnp.exp(g_s - m_new[..., None])
        C_new = a[..., None, None] * C + jnp.einsum('bhsd,bhse->bhde', kc * w[..., None], vc)
        n_new = a[..., None] * n + jnp.einsum('bhs,bhsd->bhd', w, kc)
        return (C_new, n_new, m_new), h

    state, h = lax.scan(step, state, (chunks(q), chunks(k), chunks(v), chunks(log_i), chunks(log_f)))
    return jnp.moveaxis(h, 0, 2).reshape(B, H, L, -1), state


def _mlstm_bidir(ctx_in, lat_in):
    qc, kc, vc, lic, lfc = ctx_in
    ql, kl, vl, lil, lfl = lat_in
    B = qc.shape[0]
    zero = (jnp.zeros((B, ML_HEADS, ML_HEAD_DIM, ML_HEAD_DIM), F32),
            jnp.zeros((B, ML_HEADS, ML_HEAD_DIM), F32),
            jnp.zeros((B, ML_HEADS), F32))
    flip = lambda t: jnp.flip(t, axis=2)
    h_cf, s_f = _mlstm_scan(qc, kc, vc, lic[0], lfc[0], zero)
    h_lf, _ = _mlstm_scan(ql, kl, vl, lil[0], lfl[0], s_f)
    h_cb, s_b = _mlstm_scan(flip(qc), flip(kc), flip(vc), flip(lic[1]), flip(lfc[1]), zero)
    h_lb, _ = _mlstm_scan(flip(ql), flip(kl), flip(vl), flip(lil[1]), flip(lfl[1]), s_b)
    return h_cf + flip(h_cb), h_lf + flip(h_lb)


def _combine(a, y, h, o, g):
    ml = _rms(h.transpose(0, 2, 1, 3), g[MLA_WIDTH + HY_WIDTH:].reshape(ML_HEADS, ML_HEAD_DIM))
    ml = ml.reshape(o.shape) * jax.nn.sigmoid(o)
    cat = jnp.concatenate([_rms(a, g[:MLA_WIDTH]), _rms(y, g[MLA_WIDTH:MLA_WIDTH + HY_WIDTH]), ml], axis=-1)
    return cat.astype(BF16)


def _prep_w_in(w_in_l):
    cq, ckv, kr, hy, mqk, mv, mo, gt = jnp.split(w_in_l, np_cumsum(IN_SIZES)[:-1], axis=1)
    pad = jnp.zeros((D_MODEL, N_IN_PAD - P_SMALL - MLA_ROPE - ML_GATES), w_in_l.dtype)
    return jnp.concatenate([cq, ckv, hy, mqk, mv, mo, kr, gt, pad], axis=1).astype(BF16)


def np_cumsum(sizes):
    out, acc = [], 0
    for s in sizes:
        acc += s
        out.append(acc)
    return out


def _attention_operands(q, k, v):
    scale = (MLA_QK ** -0.5) * math.log2(math.e)
    padw = ((0, 0), (0, 0), (0, 0), (0, QK_PAD - MLA_QK))
    qt = jnp.pad(q * scale, padw).astype(BF16).transpose(0, 2, 3, 1)
    kk = jnp.pad(k, padw).astype(BF16).transpose(0, 2, 1, 3)
    vt = v.astype(BF16).transpose(0, 2, 3, 1)
    return qt, kk, vt


def _mixer_inputs(proj, p, rope, tm):
    b, l, _ = proj.shape
    zero = jnp.zeros((b, 1, MLA_Q_LORA), F32)
    q_raw = norm_mm(proj, p['mla_qa_norm'], zero, zero, p['mla_w_uq'], x_col=0, k=MLA_Q_LORA, tm=tm, tn=512,
                    name="mla_q_up")
    kv_raw = norm_mm(proj, p['mla_kva_norm'], zero, zero, p['mla_w_ukv'], x_col=1, k=MLA_KV_LORA, tm=tm, tn=512,
                     name="mla_kv_up")
    krope = proj[..., P_SMALL:P_SMALL + MLA_ROPE]
    q, k, v = _mla_qkv(q_raw, kv_raw, krope, p['mla_q_norm'], p['mla_k_norm'], rope)
    hy_u = proj[..., P_HY:P_HY + 3 * HY_WIDTH]
    ml_in = _mlstm_inputs(proj[..., P_MLQK:P_MLQK + 2 * ML_WIDTH], proj[..., P_MLV:P_MLV + ML_WIDTH],
                          proj[..., P_SMALL + MLA_ROPE:P_SMALL + MLA_ROPE + ML_GATES], p)
    o = proj[..., P_MLO:P_MLO + ML_WIDTH]
    return (q, k, v), hy_u, ml_in, o


def kernel(x, c, ctx, c_ctx, ada_w, ada_b, norm1_g, norm2_g, w_in, mla_qa_norm, mla_kva_norm, mla_w_uq,
           mla_w_ukv, mla_q_norm, mla_k_norm, hy_conv_w, hy_conv_b, hy_w1, hy_b1, hy_w2, hy_b2, hy_w3,
           hy_decay, hy_skip, ml_conv_w, ml_conv_b, ml_gate_b, mix_norm_g, w_out, ffn_w1, ffn_w2):
    B, S, D = x.shape
    LC = ctx.shape[1]
    rope = _axial_rope(S)

    silu_rows = jnp.concatenate([jax.nn.silu(c), jax.nn.silu(c_ctx)[None], jnp.zeros((8 - B - 1, D), F32)], axis=0)
    mods = ada_modulation(silu_rows.astype(BF16), ada_w, ada_b)

    for l in range(DEPTH):
        need_ctx = l < DEPTH - 1
        p = {'mla_qa_norm': mla_qa_norm[l], 'mla_kva_norm': mla_kva_norm[l],
             'mla_w_uq': mla_w_uq[l].astype(BF16), 'mla_w_ukv': mla_w_ukv[l].astype(BF16),
             'mla_q_norm': mla_q_norm[l], 'mla_k_norm': mla_k_norm[l],
             'hy_conv_w': hy_conv_w[l], 'hy_conv_b': hy_conv_b[l], 'hy_w1': hy_w1[l], 'hy_b1': hy_b1[l],
             'hy_w2': hy_w2[l], 'hy_b2': hy_b2[l], 'hy_w3': hy_w3[l], 'hy_decay': hy_decay[l],
             'hy_skip': hy_skip[l], 'ml_conv_w': ml_conv_w[l], 'ml_conv_b': ml_conv_b[l],
             'ml_gate_b': ml_gate_b[l]}
        w_in_l = _prep_w_in(w_in[l])
        w_out_l = w_out[l].astype(BF16)
        w1_l = ffn_w1[l].astype(BF16)
        w2_l = ffn_w2[l].astype(BF16)
        g_mix = mix_norm_g[l]

        mod_l = [m[:, None, :] for m in jnp.split(mods[l, :B], 6, axis=-1)]
        mod_c = [jnp.broadcast_to(m[:, None, :], (B, 1, D)) for m in jnp.split(mods[l, B:B + 1], 6, axis=-1)]

        proj_l = norm_mm(x, norm1_g[l], mod_l[1], mod_l[0], w_in_l, tm=1024, tn=256, name="in_proj")
        proj_c = norm_mm(ctx, norm1_g[l], mod_c[1], mod_c[0], w_in_l, tm=LC, tn=256, name="in_proj_ctx")

        (q_l, k_l, v_l), hy_l, ml_l, o_l = _mixer_inputs(proj_l, p, rope, 1024)
        (q_c, k_c, v_c), hy_c, ml_c, o_c = _mixer_inputs(proj_c, p, None, LC)

        k_all = jnp.concatenate([k_c, k_l], axis=1)
        v_all = jnp.concatenate([v_c, v_l], axis=1)
        qt, kk, vt = _attention_operands(q_l, k_all, v_all)
        a_l = attention_t(qt, kk, vt).transpose(0, 2, 1)
        y_l = _hyena(hy_l, p)
        h_c, h_l = _mlstm_bidir(ml_c, ml_l)
        cat_l = _combine(a_l, y_l, h_l, o_l, g_mix)
        x = mm_residual(cat_l, w_out_l, x, mod_l[2], tm=1024, tn=512, name="out_proj")
        hid = norm_swiglu(x, norm2_g[l], mod_l[4], mod_l[3], w1_l, tm=1024, tn=512, name="ffn_up")
        x = mm_residual(hid, w2_l, x, mod_l[5], tm=1024, tn=512, name="ffn_down")

        if need_ctx:
            qt, kk, vt = _attention_operands(q_c, k_c, v_c)
            a_c = attention_t(qt, kk, vt, tq=LC).transpose(0, 2, 1)
            y_c = _hyena(hy_c, p)
            cat_c = _combine(a_c, y_c, h_c, o_c, g_mix)
            ctx = mm_residual(cat_c, w_out_l, ctx, mod_c[2], tm=LC, tn=512, name="out_proj_ctx")
            hid = norm_swiglu(ctx, norm2_g[l], mod_c[4], mod_c[3], w1_l, tm=LC, tn=512, name="ffn_up_ctx")
            ctx = mm_residual(hid, w2_l, ctx, mod_c[5], tm=LC, tn=512, name="ffn_down_ctx")
    return x
```

```python
import functools
import math

import jax
import jax.numpy as jnp
import numpy as np
from jax import lax
from jax.experimental import pallas as pl
from jax.experimental.pallas import tpu as pltpu

D_MODEL = 2048
DEPTH = 2
GRID_W = 64
EPS = 1e-6

MLA_HEADS = 8
MLA_NOPE = 128
MLA_ROPE = 64
MLA_QK = MLA_NOPE + MLA_ROPE
MLA_V = 128
MLA_Q_LORA = 512
MLA_KV_LORA = 512
MLA_WIDTH = MLA_HEADS * MLA_V
ROPE_BASE = 10000.0
ROPE_PAIRS_PER_AXIS = MLA_ROPE // 4
QK_PAD = 256

HY_WIDTH = 512
HY_ORDER = 2
HY_BANDS = 16
HY_SHIFT = 0.05
SHORT_CONV = 3

ML_HEADS = 4
ML_HEAD_DIM = 128
ML_WIDTH = ML_HEADS * ML_HEAD_DIM
ML_CHUNK = 128
ML_GATES = 4 * ML_HEADS

MIX_WIDTH = MLA_WIDTH + HY_WIDTH + ML_WIDTH
FFN_HIDDEN = ((8 * D_MODEL // 3 + 255) // 256) * 256

IN_SIZES = (MLA_Q_LORA, MLA_KV_LORA, MLA_ROPE, 3 * HY_WIDTH, 2 * ML_WIDTH, ML_WIDTH, ML_WIDTH, ML_GATES)
N_IN = sum(IN_SIZES)

P_CQ, P_CKV, P_HY, P_MLQK, P_MLV, P_MLO, P_SMALL = 0, 512, 1024, 2560, 3584, 4096, 4608
N_IN_PAD = 4864

VMEM_LIMIT_BYTES = 56 * 1024 * 1024

BF16 = jnp.bfloat16
F32 = jnp.float32


def _cparams(sem):
    return pltpu.CompilerParams(dimension_semantics=sem, vmem_limit_bytes=VMEM_LIMIT_BYTES)


def _ada_kernel(s_ref, w_ref, b_ref, o_ref):
    w = w_ref[0].astype(BF16)
    o_ref[0] = jnp.dot(s_ref[...], w, preferred_element_type=F32) + b_ref[0]


def ada_modulation(silu_rows, ada_w, ada_b, tn=1024):
    depth, d, n = ada_w.shape
    return pl.pallas_call(
        _ada_kernel,
        grid=(depth, n // tn),
        in_specs=[pl.BlockSpec((8, d), lambda l, j: (0, 0)),
                  pl.BlockSpec((1, d, tn), lambda l, j: (l, 0, j)),
                  pl.BlockSpec((1, 1, tn), lambda l, j: (l, 0, j))],
        out_specs=pl.BlockSpec((1, 8, tn), lambda l, j: (l, 0, j)),
        out_shape=jax.ShapeDtypeStruct((depth, 8, n), F32),
        compiler_params=_cparams(("parallel", "parallel")),
        name="ada_modulation",
    )(silu_rows, ada_w, ada_b.reshape(depth, 1, n))


def _normed(x, g, sc, sh):
    y = x * lax.rsqrt(jnp.mean(x * x, axis=-1, keepdims=True) + EPS) * g
    return y * (1.0 + sc) + sh


def _norm_mm_kernel(x_ref, g_ref, sc_ref, sh_ref, w_ref, o_ref, xn_ref):
    @pl.when(pl.program_id(2) == 0)
    def _():
        xn_ref[...] = _normed(x_ref[0], g_ref[...], sc_ref[0], sh_ref[0]).astype(BF16)

    o_ref[0] = jnp.dot(xn_ref[...], w_ref[...], preferred_element_type=F32).astype(o_ref.dtype)


def norm_mm(x, g, sc, sh, w, *, x_col=0, k=None, tm, tn, out_dtype=F32, name="norm_mm"):
    b, s, _ = x.shape
    k = w.shape[0] if k is None else k
    n = w.shape[1]
    return pl.pallas_call(
        _norm_mm_kernel,
        grid=(b, s // tm, n // tn),
        in_specs=[pl.BlockSpec((1, tm, k), lambda bi, i, j: (bi, i, x_col)),
                  pl.BlockSpec((1, k), lambda bi, i, j: (0, 0)),
                  pl.BlockSpec((1, 1, k), lambda bi, i, j: (bi, 0, 0)),
                  pl.BlockSpec((1, 1, k), lambda bi, i, j: (bi, 0, 0)),
                  pl.BlockSpec((k, tn), lambda bi, i, j: (0, j))],
        out_specs=pl.BlockSpec((1, tm, tn), lambda bi, i, j: (bi, i, j)),
        out_shape=jax.ShapeDtypeStruct((b, s, n), out_dtype),
        scratch_shapes=[pltpu.VMEM((tm, k), BF16)],
        compiler_params=_cparams(("parallel", "parallel", "arbitrary")),
        name=name,
    )(x, g.reshape(1, k), sc, sh, w)


def _norm_swiglu_kernel(x_ref, g_ref, sc_ref, sh_ref, wg_ref, wu_ref, o_ref, xn_ref):
    @pl.when(pl.program_id(2) == 0)
    def _():
        xn_ref[...] = _normed(x_ref[0], g_ref[...], sc_ref[0], sh_ref[0]).astype(BF16)

    xn = xn_ref[...]
    gate = jnp.dot(xn, wg_ref[...], preferred_element_type=F32)
    up = jnp.dot(xn, wu_ref[...], preferred_element_type=F32)
    o_ref[0] = (gate * jax.nn.sigmoid(gate) * up).astype(o_ref.dtype)


def norm_swiglu(x, g, sc, sh, w1, *, tm, tn, name="norm_swiglu"):
    b, s, d = x.shape
    h = w1.shape[1] // 2
    nj = h // tn
    return pl.pallas_call(
        _norm_swiglu_kernel,
        grid=(b, s // tm, nj),
        in_specs=[pl.BlockSpec((1, tm, d), lambda bi, i, j: (bi, i, 0)),
                  pl.BlockSpec((1, d), lambda bi, i, j: (0, 0)),
                  pl.BlockSpec((1, 1, d), lambda bi, i, j: (bi, 0, 0)),
                  pl.BlockSpec((1, 1, d), lambda bi, i, j: (bi, 0, 0)),
                  pl.BlockSpec((d, tn), lambda bi, i, j: (0, j)),
                  pl.BlockSpec((d, tn), lambda bi, i, j: (0, j + nj))],
        out_specs=pl.BlockSpec((1, tm, tn), lambda bi, i, j: (bi, i, j)),
        out_shape=jax.ShapeDtypeStruct((b, s, h), BF16),
        scratch_shapes=[pltpu.VMEM((tm, d), BF16)],
        compiler_params=_cparams(("parallel", "parallel", "arbitrary")),
        name=name,
    )(x, g.reshape(1, d), sc, sh, w1, w1)


def _mm_res_kernel(a_ref, w_ref, r_ref, gt_ref, o_ref):
    y = jnp.dot(a_ref[0], w_ref[...], preferred_element_type=F32)
    o_ref[0] = r_ref[0] + gt_ref[0] * y


def mm_residual(a, w, res, gate, *, tm, tn, name="mm_residual"):
    b, s, k = a.shape
    n = w.shape[1]
    return pl.pallas_call(
        _mm_res_kernel,
        grid=(b, s // tm, n // tn),
        in_specs=[pl.BlockSpec((1, tm, k), lambda bi, i, j: (bi, i, 0)),
                  pl.BlockSpec((k, tn), lambda bi, i, j: (0, j)),
                  pl.BlockSpec((1, tm, tn), lambda bi, i, j: (bi, i, j)),
                  pl.BlockSpec((1, 1, tn), lambda bi, i, j: (bi, 0, j))],
        out_specs=pl.BlockSpec((1, tm, tn), lambda bi, i, j: (bi, i, j)),
        out_shape=jax.ShapeDtypeStruct((b, s, n), F32),
        compiler_params=_cparams(("parallel", "parallel", "parallel")),
        name=name,
    )(a, w, res, gate)


def _attn_kernel(qt_ref, k_ref, vt_ref, o_ref):
    s = jnp.dot(k_ref[0, 0], qt_ref[0, 0], preferred_element_type=F32)
    m = jnp.max(s, axis=0, keepdims=True)
    p = jnp.exp2(s - m)
    l = jnp.sum(p, axis=0, keepdims=True)
    o = jnp.dot(vt_ref[0, 0], p.astype(BF16), preferred_element_type=F32)
    o_ref[0] = o / l


def attention_t(qt, k, vt, *, tq=256):
    b, h, dk, lq = qt.shape
    lk = k.shape[2]
    dv = vt.shape[2]
    return pl.pallas_call(
        _attn_kernel,
        grid=(b, h, lq // tq),
        in_specs=[pl.BlockSpec((1, 1, dk, tq), lambda bi, hi, i: (bi, hi, 0, i)),
                  pl.BlockSpec((1, 1, lk, dk), lambda bi, hi, i: (bi, hi, 0, 0)),
                  pl.BlockSpec((1, 1, dv, lk), lambda bi, hi, i: (bi, hi, 0, 0))],
        out_specs=pl.BlockSpec((1, dv, tq), lambda bi, hi, i: (bi, hi, i)),
        out_shape=jax.ShapeDtypeStruct((b, h * dv, lq), F32),
        compiler_params=_cparams(("parallel", "parallel", "parallel")),
        name="attention",
    )(qt, k, vt)


HY_NA = 64
HY_NB = 128
HY_N = HY_NA * HY_NB
HY_KG = 8


@functools.lru_cache(maxsize=None)
def _dft_tables():
    na, nb, n = HY_NA, HY_NB, HY_N
    a = np.arange(na // 2)[None, :]
    ka = np.arange(na)[:, None]
    ang = 2.0 * np.pi * ((a * ka) % na) / na
    c, s = np.cos(ang), np.sin(ang)
    g1 = np.block([[c, s], [-s, c]])
    g1_inv = g1.T.copy()
    a_full = np.arange(na)[None, :]
    ang = 2.0 * np.pi * ((a_full * ka) % na) / na
    g1_real = np.concatenate([np.cos(ang), -np.sin(ang)], axis=0)
    b = np.arange(nb)[None, None, :]
    k = np.arange(na)[:, None, None] + na * np.arange(nb)[None, :, None]
    ang = 2.0 * np.pi * ((b * k) % n) / n
    cr, si = np.cos(ang), np.sin(ang)
    g2 = np.concatenate([np.concatenate([cr, si], axis=2), np.concatenate([-si, cr], axis=2)], axis=1)
    g2_inv = np.transpose(g2, (0, 2, 1)).copy()
    return (g1.astype(np.float32), g1_inv.astype(np.float32), g1_real.astype(np.float32),
            g2.astype(np.float32), g2_inv.astype(np.float32))


def _short_conv_kernel(u_ref, w_ref, b_ref, o_ref, *, silu):
    u = u_ref[0]
    n = u.shape[0]
    rows = lax.broadcasted_iota(jnp.int32, u.shape, 0)
    prev = jnp.where(rows == 0, 0.0, pltpu.roll(u, 1, axis=0))
    nxt = jnp.where(rows == n - 1, 0.0, pltpu.roll(u, n - 1, axis=0))
    y = b_ref[...] + prev * w_ref[0:1, :] + u * w_ref[1:2, :] + nxt * w_ref[2:3, :]
    if silu:
        y = y * jax.nn.sigmoid(y)
    o_ref[0, 0] = y


def short_conv_groups(proj, col0, w, b, *, group, tc, silu, name):
    bsz, l, _ = proj.shape
    width = w.shape[1]
    per = group // tc
    return pl.pallas_call(
        functools.partial(_short_conv_kernel, silu=silu),
        grid=(bsz, width // tc),
        in_specs=[pl.BlockSpec((1, l, tc), lambda bi, j: (bi, 0, col0 // tc + j)),
                  pl.BlockSpec((SHORT_CONV, tc), lambda bi, j: (0, j)),
                  pl.BlockSpec((1, tc), lambda bi, j: (0, j))],
        out_specs=pl.BlockSpec((1, 1, l, tc), lambda bi, j: (j // per, bi, 0, j % per)),
        out_shape=jax.ShapeDtypeStruct((width // group, bsz, l, group), F32),
        compiler_params=_cparams(("parallel", "parallel")),
        name=name,
    )(proj, w, b.reshape(1, width))


def _hy_filter_kernel(f_ref, w1_ref, b1_ref, w2_ref, b2_ref, w3_ref, dec_ref, o_ref, *, zero_row):
    hp = lax.Precision.HIGHEST
    f = f_ref[...]
    tl = f.shape[0]
    h = jnp.sin(jnp.dot(f, w1_ref[...], precision=hp, preferred_element_type=F32) + b1_ref[...])
    h = jnp.sin(jnp.dot(h, w2_ref[...], precision=hp, preferred_element_type=F32) + b2_ref[...])
    h = jnp.dot(h, w3_ref[0], precision=hp, preferred_element_type=F32)
    h = h * (jnp.exp(-f[:, 0:1] * dec_ref[0]) + HY_SHIFT)
    rows = lax.broadcasted_iota(jnp.int32, h.shape, 0) + pl.program_id(0) * tl
    o_ref[...] = jnp.where(rows == zero_row, 0.0, h)


def hyena_filter_buffer(L, p, tl=512):
    n = 2 * L
    r = jnp.arange(n, dtype=F32)
    t = jnp.where(r < L, r, n - r) / L
    bands = jnp.arange(1, HY_BANDS + 1, dtype=F32)
    ang = 2.0 * math.pi * t[:, None] * bands
    emb = 1 + 2 * HY_BANDS
    feats = jnp.concatenate([t[:, None], jnp.cos(ang), jnp.sin(ang), jnp.zeros((n, 128 - emb), F32)], axis=-1)
    hid = p['hy_w1'].shape[1]
    w1 = jnp.concatenate([p['hy_w1'], jnp.zeros((128 - emb, hid), F32)], axis=0)
    oc = HY_ORDER * HY_WIDTH
    w3 = p['hy_w3'].reshape(hid, HY_ORDER, 2, HY_WIDTH).transpose(2, 0, 1, 3).reshape(2, hid, oc)
    dec = p['hy_decay'].reshape(HY_ORDER, 2, HY_WIDTH).transpose(1, 0, 2).reshape(2, 1, oc)
    half = L // tl
    return pl.pallas_call(
        functools.partial(_hy_filter_kernel, zero_row=L),
        grid=(n // tl,),
        in_specs=[pl.BlockSpec((tl, 128), lambda i: (i, 0)),
                  pl.BlockSpec((128, hid), lambda i: (0, 0)),
                  pl.BlockSpec((1, hid), lambda i: (0, 0)),
                  pl.BlockSpec((hid, hid), lambda i: (0, 0)),
                  pl.BlockSpec((1, hid), lambda i: (0, 0)),
                  pl.BlockSpec((1, hid, oc), lambda i: (i // half, 0, 0)),
                  pl.BlockSpec((1, 1, oc), lambda i: (i // half, 0, 0))],
        out_specs=pl.BlockSpec((tl, oc), lambda i: (i, 0)),
        out_shape=jax.ShapeDtypeStruct((n, oc), F32),
        compiler_params=_cparams(("parallel",)),
        name="hyena_filter",
    )(feats, w1, p['hy_b1'].reshape(1, hid), p['hy_w2'], p['hy_b2'].reshape(1, hid), w3, dec)


def _dft_first_kernel(g_ref, x_ref, o_ref, *, precise):
    if precise:
        o_ref[0] = jnp.dot(g_ref[...], x_ref[0], precision=lax.Precision.HIGHEST, preferred_element_type=F32)
    else:
        y = jnp.dot(g_ref[...], x_ref[0].astype(BF16), preferred_element_type=F32)
        o_ref[0] = y.astype(o_ref.dtype)


def dft_first(g, x, *, n_out, offset=0, tn=8192, precise=False):
    _, r, lanes = x.shape
    m = g.shape[0]
    return pl.pallas_call(
        functools.partial(_dft_first_kernel, precise=precise),
        grid=(n_out, lanes // tn),
        in_specs=[pl.BlockSpec((m, r), lambda p, j: (0, 0)),
                  pl.BlockSpec((1, r, tn), lambda p, j: (offset + p, 0, j))],
        out_specs=pl.BlockSpec((1, m, tn), lambda p, j: (p, 0, j)),
        out_shape=jax.ShapeDtypeStruct((n_out, m, lanes), F32 if precise else BF16),
        compiler_params=_cparams(("parallel", "parallel")),
        name="dft_first",
    )(g, x)


def _dft_filter_mid_kernel(g_ref, y_ref, o_ref, *, scale):
    hp = lax.Precision.HIGHEST
    for i in range(g_ref.shape[0]):
        y = jnp.concatenate([y_ref[0, i], y_ref[1, i]], axis=0)
        o_ref[i] = scale * jnp.dot(g_ref[i], y, precision=hp, preferred_element_type=F32)


def dft_filter_mid(g2, y1):
    _, na, nb, oc = y1.shape
    kg = HY_KG // 2
    return pl.pallas_call(
        functools.partial(_dft_filter_mid_kernel, scale=1.0 / HY_N),
        grid=(na // kg,),
        in_specs=[pl.BlockSpec((kg, 2 * nb, 2 * nb), lambda g: (g, 0, 0)),
                  pl.BlockSpec((2, kg, nb, oc), lambda g: (0, g, 0, 0))],
        out_specs=pl.BlockSpec((kg, 2 * nb, oc), lambda g: (g, 0, 0)),
        out_shape=jax.ShapeDtypeStruct((na, 2 * nb, oc), F32),
        compiler_params=_cparams(("parallel",)),
        name="dft_filter_mid",
    )(g2, y1)


def _dft_mid_kernel(gf_ref, gi_ref, y_ref, h_ref, o_ref):
    nb = y_ref.shape[3]
    for i in range(gf_ref.shape[0]):
        y = jnp.concatenate([y_ref[0, 0, i], y_ref[0, 1, i]], axis=0)
        x = jnp.dot(gf_ref[i], y, preferred_element_type=F32)
        xr, xi = x[:nb], x[nb:]
        hr, hi = h_ref[i, :nb], h_ref[i, nb:]
        prod = jnp.concatenate([xr * hr - xi * hi, xr * hi + xi * hr], axis=0).astype(BF16)
        u = jnp.dot(gi_ref[i], prod, preferred_element_type=F32)
        o_ref[0, 0, i] = u[:nb].astype(o_ref.dtype)
        o_ref[0, 1, i] = u[nb:].astype(o_ref.dtype)


def dft_mid(g2, g2_inv, y1, h, order):
    npair, _, na, nb, c = y1.shape
    kg = HY_KG
    return pl.pallas_call(
        _dft_mid_kernel,
        grid=(npair, na // kg),
        in_specs=[pl.BlockSpec((kg, 2 * nb, 2 * nb), lambda p, g: (g, 0, 0)),
                  pl.BlockSpec((kg, 2 * nb, 2 * nb), lambda p, g: (g, 0, 0)),
                  pl.BlockSpec((1, 2, kg, nb, c), lambda p, g: (p, 0, g, 0, 0)),
                  pl.BlockSpec((kg, 2 * nb, c), lambda p, g: (g, 0, order))],
        out_specs=pl.BlockSpec((1, 2, kg, nb, c), lambda p, g: (p, 0, g, 0, 0)),
        out_shape=jax.ShapeDtypeStruct(y1.shape, BF16),
        compiler_params=_cparams(("parallel", "parallel")),
        name="dft_mid",
    )(g2, g2_inv, y1, h)


def _dft_last_kernel(g_ref, u_ref, z_ref, gate_ref, skip_ref, o_ref):
    conv = jnp.dot(g_ref[...], u_ref[0], preferred_element_type=F32)
    o_ref[0] = gate_ref[0] * (conv + skip_ref[...] * z_ref[0])


def dft_last(g1_inv, u, z, z_off, gate, gate_off, skip_lanes, *, tn=8192):
    npair, m2, lanes = u.shape
    r = g1_inv.shape[0]
    return pl.pallas_call(
        _dft_last_kernel,
        grid=(npair, lanes // tn),
        in_specs=[pl.BlockSpec((r, m2), lambda p, j: (0, 0)),
                  pl.BlockSpec((1, m2, tn), lambda p, j: (p, 0, j)),
                  pl.BlockSpec((1, r, tn), lambda p, j: (z_off + p, 0, j)),
                  pl.BlockSpec((1, r, tn), lambda p, j: (gate_off + p, 0, j)),
                  pl.BlockSpec((1, tn), lambda p, j: (0, 0))],
        out_specs=pl.BlockSpec((1, r, tn), lambda p, j: (p, 0, j)),
        out_shape=jax.ShapeDtypeStruct((npair, r, lanes), F32),
        compiler_params=_cparams(("parallel", "parallel")),
        name="dft_last",
    )(g1_inv, u, z, gate, skip_lanes)


def hyena_long(proj, p):
    bsz, l, _ = proj.shape
    assert 2 * l == HY_N and bsz % 2 == 0
    c = HY_WIDTH
    npair = bsz // 2
    lanes = HY_NB * c
    g1, g1_inv, g1_real, g2, g2_inv = _dft_tables()
    g1_bf, g1_inv_bf = jnp.asarray(g1, BF16), jnp.asarray(g1_inv, BF16)
    g2_bf, g2_inv_bf = jnp.asarray(g2, BF16), jnp.asarray(g2_inv, BF16)

    buf = hyena_filter_buffer(l, p)
    y1h = dft_first(jnp.asarray(g1_real), buf.reshape(1, HY_NA, HY_NB * HY_ORDER * c), n_out=1, precise=True)
    hspec = dft_filter_mid(jnp.asarray(g2), y1h.reshape(2, HY_NA, HY_NB, HY_ORDER * c))

    vx = short_conv_groups(proj, P_HY, p['hy_conv_w'], p['hy_conv_b'], group=c, tc=128, silu=False,
                           name="hyena_short_conv")
    vx = vx.reshape(3 * npair, HY_NA, lanes)
    z, z_off = vx, 0
    for n in range(HY_ORDER):
        y1 = dft_first(g1_bf, z, n_out=npair, offset=z_off)
        u = dft_mid(g2_bf, g2_inv_bf, y1.reshape(npair, 2, HY_NA, HY_NB, c), hspec, n)
        skip_lanes = jnp.tile(p['hy_skip'][n], 8192 // c).reshape(1, 8192)
        z = dft_last(g1_inv_bf, u.reshape(npair, 2 * HY_NA, lanes), z, z_off, vx, (n + 1) * npair, skip_lanes)
        z_off = 0
    return z.reshape(bsz, l, c)


def _log_sigmoid(x):
    return jnp.minimum(x, 0.0) - jnp.log1p(jnp.exp(-jnp.abs(x)))


def _mlstm_kernel(qc_ref, kc_ref, vc_ref, gc_ref, gtc_ref, ql_ref, kl_ref, vl_ref, gl_ref, gtl_ref,
                  bias_ref, biast_ref, oc_ref, ol_ref, c_ref, m_ref, *, nc_ctx):
    d = ML_HEAD_DIM
    t = ML_CHUNK
    nh = ML_HEADS
    direction = pl.program_id(0)
    step = pl.program_id(2)

    @pl.when(step == 0)
    def _():
        c_ref[...] = jnp.zeros_like(c_ref)
        m_ref[...] = jnp.zeros_like(m_ref)

    def chunk(q_ref, k_ref, v_ref, g_ref, gt_ref, o_ref):
        g_cols = g_ref[0, 0] + bias_ref[0]
        g_rows = gt_ref[0, 0] + biast_ref[0]
        lf_cols = _log_sigmoid(g_cols[:, nh:])
        li_rows = g_rows[:nh]
        lf_rows = _log_sigmoid(g_rows[nh:])
        r = lax.broadcasted_iota(jnp.int32, (t, t), 0)
        cidx = lax.broadcasted_iota(jnp.int32, (t, t), 1)
        sign = 1 - 2 * direction
        mask = (cidx - r) * sign <= 0
        mask_t = (r - cidx) * sign <= 0
        tri = mask.astype(F32)
        tri_t = mask_t.astype(F32)
        lane = lax.broadcasted_iota(jnp.int32, (t, d), 1)
        ones_col = jnp.where(lane == 0, 1.0, 0.0).astype(BF16)
        for h in range(nh):
            sl = slice(h * d, (h + 1) * d)
            bc = jnp.sum(tri * lf_rows[h:h + 1, :], axis=1, keepdims=True)
            br = jnp.sum(tri_t * lf_cols[:, h:h + 1], axis=0, keepdims=True)
            lir = li_rows[h:h + 1, :]
            m_prev = m_ref[h, 0:1, 0:1]
            dmat = jnp.where(mask, bc - br + lir, -jnp.inf)
            inter = bc + m_prev
            m_t = jnp.maximum(inter, jnp.max(dmat, axis=1, keepdims=True))
            dexp = jnp.exp(dmat - m_t)
            inter_w = jnp.exp(inter - m_t)
            qh = (q_ref[0, 0, :, sl] * (d ** -0.5)).astype(BF16)
            kt = k_ref[0, 0, :, sl].T
            v_ext = jnp.concatenate([v_ref[0, :, sl].astype(BF16), ones_col], axis=1)
            s = jnp.dot(qh, kt.astype(BF16), preferred_element_type=F32) * dexp
            c_ext = c_ref[h]
            acc = inter_w * jnp.dot(qh, c_ext.astype(BF16), preferred_element_type=F32)
            acc = acc + jnp.dot(s.astype(BF16), v_ext, preferred_element_type=F32)
            den = jnp.maximum(jnp.abs(acc[:, d:d + 1]), jnp.exp(-m_t))
            o_ref[0, 0, :, sl] = acc[:, :d] / den
            b_last = jnp.sum(lf_rows[h:h + 1, :], axis=1, keepdims=True)
            g_row = b_last - br + lir
            m_new = jnp.maximum(b_last + m_prev, jnp.max(g_row, axis=1, keepdims=True))
            a = jnp.exp(b_last + m_prev - m_new)
            w_row = jnp.exp(g_row - m_new)
            c_ref[h] = a * c_ext + jnp.dot((kt * w_row).astype(BF16), v_ext, preferred_element_type=F32)
            m_ref[h] = jnp.broadcast_to(m_new, m_ref.shape[1:])

    @pl.when(step < nc_ctx)
    def _():
        chunk(qc_ref, kc_ref, vc_ref, gc_ref, gtc_ref, oc_ref)

    @pl.when(step >= nc_ctx)
    def _():
        chunk(ql_ref, kl_ref, vl_ref, gl_ref, gtl_ref, ol_ref)


def mlstm_bidir(qk_c, proj_c, gates_c, qk_l, proj_l, gates_l, gate_bias):
    _, bsz, lc, w = qk_c.shape
    ll = qk_l.shape[2]
    t, nh = ML_CHUNK, ML_HEADS
    nc_c, nc_l = lc // t, ll // t

    def by_dir(g):
        g4 = g.reshape(g.shape[:-1] + (4, nh))
        return jnp.stack([jnp.concatenate([g4[..., dd, :], g4[..., 2 + dd, :]], axis=-1) for dd in range(2)])

    gd_c, gd_l = by_dir(gates_c), by_dir(gates_l)
    gt_c, gt_l = gd_c.transpose(0, 1, 3, 2), gd_l.transpose(0, 1, 3, 2)
    bias = by_dir(gate_bias.reshape(1, ML_GATES))
    bias_t = bias.transpose(0, 2, 1)

    def idx(step_off, nc):
        def f(dd, g):
            cc = jnp.clip(g - step_off, 0, nc - 1)
            return cc + dd * (nc - 1 - 2 * cc)
        return f

    ic, il = idx(0, nc_c), idx(nc_c, nc_l)
    vcol = P_MLV // w

    def seg_specs(ix):
        return [pl.BlockSpec((1, 1, t, w), lambda dd, b, g: (0, b, ix(dd, g), 0)),
                pl.BlockSpec((1, 1, t, w), lambda dd, b, g: (1, b, ix(dd, g), 0)),
                pl.BlockSpec((1, t, w), lambda dd, b, g: (b, ix(dd, g), vcol)),
                pl.BlockSpec((1, 1, t, 2 * nh), lambda dd, b, g: (dd, b, ix(dd, g), 0)),
                pl.BlockSpec((1, 1, 2 * nh, t), lambda dd, b, g: (dd, b, 0, ix(dd, g)))]

    return pl.pallas_call(
        functools.partial(_mlstm_kernel, nc_ctx=nc_c),
        grid=(2, bsz, nc_c + nc_l),
        in_specs=seg_specs(ic) + seg_specs(il) + [
            pl.BlockSpec((1, 1, 2 * nh), lambda dd, b, g: (dd, 0, 0)),
            pl.BlockSpec((1, 2 * nh, 1), lambda dd, b, g: (dd, 0, 0))],
        out_specs=[pl.BlockSpec((1, 1, t, w), lambda dd, b, g: (dd, b, ic(dd, g), 0)),
                   pl.BlockSpec((1, 1, t, w), lambda dd, b, g: (dd, b, il(dd, g), 0))],
        out_shape=[jax.ShapeDtypeStruct((2, bsz, lc, w), F32), jax.ShapeDtypeStruct((2, bsz, ll, w), F32)],
        scratch_shapes=[pltpu.VMEM((nh, ML_HEAD_DIM, 2 * ML_HEAD_DIM), F32), pltpu.VMEM((nh, 8, 128), F32)],
        compiler_params=_cparams(("parallel", "parallel", "arbitrary")),
        name="mlstm",
    )(qk_c, qk_c, proj_c, gd_c, gt_c, qk_l, qk_l, proj_l, gd_l, gt_l, bias, bias_t)


def _rms(x, g):
    return x * lax.rsqrt(jnp.mean(x * x, axis=-1, keepdims=True) + EPS) * g


def _axial_rope(n_tokens):
    rows = n_tokens // GRID_W
    row = jnp.repeat(jnp.arange(rows, dtype=F32), GRID_W)
    col = jnp.tile(jnp.arange(GRID_W, dtype=F32), rows)
    freqs = ROPE_BASE ** (-jnp.arange(ROPE_PAIRS_PER_AXIS, dtype=F32) / ROPE_PAIRS_PER_AXIS)
    ang = jnp.concatenate([row[:, None] * freqs, col[:, None] * freqs], axis=-1)
    return jnp.cos(ang), jnp.sin(ang)


def _apply_rope(x, cos, sin):
    half = x.shape[-1] // 2
    x1, x2 = x[..., :half], x[..., half:]
    cos = cos[None, :, None, :]
    sin = sin[None, :, None, :]
    return jnp.concatenate([x1 * cos - x2 * sin, x2 * cos + x1 * sin], axis=-1)


def _mla_qkv(q_raw, kv_raw, krope, g_q, g_k, rope):
    b, l, _ = q_raw.shape
    q = q_raw.reshape(b, l, MLA_HEADS, MLA_QK)
    kv = kv_raw.reshape(b, l, MLA_HEADS, MLA_NOPE + MLA_V)
    k_rope = jnp.broadcast_to(krope[:, :, None, :], (b, l, MLA_HEADS, MLA_ROPE))
    k = jnp.concatenate([kv[..., :MLA_NOPE], k_rope], axis=-1)
    v = kv[..., MLA_NOPE:]
    q = _rms(q, g_q)
    k = _rms(k, g_k)
    if rope is not None:
        cos, sin = rope
        q = jnp.concatenate([q[..., :MLA_NOPE], _apply_rope(q[..., MLA_NOPE:], cos, sin)], axis=-1)
        k = jnp.concatenate([k[..., :MLA_NOPE], _apply_rope(k[..., MLA_NOPE:], cos, sin)], axis=-1)
    return q, k, v


def _short_conv(u, w, b):
    L = u.shape[1]
    pad = SHORT_CONV // 2
    up = jnp.pad(u, ((0, 0), (pad, pad), (0, 0)))
    out = b
    for j in range(SHORT_CONV):
        out = out + up[:, j:j + L] * w[j]
    return out


def _hyena_filter_fft(L, p):
    t = jnp.arange(L, dtype=F32) / L
    bands = jnp.arange(1, HY_BANDS + 1, dtype=F32)
    ang = 2.0 * math.pi * t[:, None] * bands
    feats = jnp.concatenate([t[:, None], jnp.cos(ang), jnp.sin(ang)], axis=-1)
    hp = lax.Precision.HIGHEST
    h = jnp.sin(jnp.dot(feats, p['hy_w1'], precision=hp) + p['hy_b1'])
    h = jnp.sin(jnp.dot(h, p['hy_w2'], precision=hp) + p['hy_b2'])
    h = jnp.dot(h, p['hy_w3'], precision=hp) * (jnp.exp(-t[:, None] * p['hy_decay']) + HY_SHIFT)
    h = h.reshape(L, HY_ORDER, 2, HY_WIDTH)
    h_fwd, h_bwd = h[:, :, 0], h[:, :, 1]
    buf = jnp.concatenate([h_fwd, jnp.zeros((1, HY_ORDER, HY_WIDTH), F32), h_bwd[:0:-1]], axis=0)
    return jnp.fft.rfft(buf, axis=0)


def _fft_conv(z, kf):
    L = z.shape[1]
    zf = jnp.fft.rfft(z, n=2 * L, axis=1)
    return jnp.fft.irfft(zf * kf[None], n=2 * L, axis=1)[:, :L]


def _hyena(u, p):
    L = u.shape[1]
    v, x1, x2 = jnp.split(_short_conv(u, p['hy_conv_w'], p['hy_conv_b']), 3, axis=-1)
    kf = _hyena_filter_fft(L, p)
    z = v
    for n, gate in enumerate((x1, x2)):
        z = gate * (_fft_conv(z, kf[:, n]) + p['hy_skip'][n] * z)
    return z


def _mlstm_inputs(qk, v, gates, p):
    B, L, _ = qk.shape
    qk = jax.nn.silu(_short_conv(qk, p['ml_conv_w'], p['ml_conv_b']))
    q, k = jnp.split(qk, 2, axis=-1)
    heads = lambda t: t.reshape(B, L, ML_HEADS, ML_HEAD_DIM).transpose(0, 2, 1, 3)
    q = heads(q) * (ML_HEAD_DIM ** -0.5)
    k = heads(k)
    v = heads(v)
    g = (gates + p['ml_gate_b']).reshape(B, L, 4, ML_HEADS).transpose(2, 0, 3, 1)
    return q, k, v, g[:2], jax.nn.log_sigmoid(g[2:])


def _mlstm_scan(q, k, v, log_i, log_f, state):
    B, H, L, _ = q.shape
    nc = L // ML_CHUNK
    chunks = lambda t: jnp.moveaxis(t.reshape((B, H, nc, ML_CHUNK) + t.shape[3:]), 2, 0)
    causal = jnp.tril(jnp.ones((ML_CHUNK, ML_CHUNK), dtype=bool))

    def step(carry, inp):
        C, n, m = carry
        qc, kc, vc, lic, lfc = inp
        b = jnp.cumsum(lfc, axis=-1)
        d = jnp.where(causal, b[..., :, None] - b[..., None, :] + lic[..., None, :], -jnp.inf)
        inter = b + m[..., None]
        m_t = jnp.maximum(inter, jnp.max(d, axis=-1))
        dexp = jnp.exp(d - m_t[..., None])
        inter_w = jnp.exp(inter - m_t)
        s = jnp.einsum('bhtd,bhsd->bhts', qc, kc) * dexp
        num = inter_w[..., None] * jnp.einsum('bhtd,bhde->bhte', qc, C) + jnp.einsum('bhts,bhse->bhte', s, vc)
        den = inter_w * jnp.einsum('bhtd,bhd->bht', qc, n) + jnp.sum(s, axis=-1)
        h = num / jnp.maximum(jnp.abs(den), jnp.exp(-m_t))[..., None]
        b_last = b[..., -1]
        g_s = b_last[..., None] - b + lic
        m_new = jnp.maximum(b_last + m, jnp.max(g_s, axis=-1))
        a = jnp.exp(b_last + m - m_new)
        w = jnp.exp(g_s - m_new[..., None])
        C_new = a[..., None, None] * C + jnp.einsum('bhsd,bhse->bhde', kc * w[..., None], vc)
        n_new = a[..., None] * n + jnp.einsum('bhs,bhsd->bhd', w, kc)
        return (C_new, n_new, m_new), h

    state, h = lax.scan(step, state, (chunks(q), chunks(k), chunks(v), chunks(log_i), chunks(log_f)))
    return jnp.moveaxis(h, 0, 2).reshape(B, H, L, -1), state


def _mlstm_bidir(ctx_in, lat_in):
    qc, kc, vc, lic, lfc = ctx_in
    ql, kl, vl, lil, lfl = lat_in
    B = qc.shape[0]
    zero = (jnp.zeros((B, ML_HEADS, ML_HEAD_DIM, ML_HEAD_DIM), F32),
            jnp.zeros((B, ML_HEADS, ML_HEAD_DIM), F32),
            jnp.zeros((B, ML_HEADS), F32))
    flip = lambda t: jnp.flip(t, axis=2)
    h_cf, s_f = _mlstm_scan(qc, kc, vc, lic[0], lfc[0], zero)
    h_lf, _ = _mlstm_scan(ql, kl, vl, lil[0], lfl[0], s_f)
    h_cb, s_b = _mlstm_scan(flip(qc), flip(kc), flip(vc), flip(lic[1]), flip(lfc[1]), zero)
    h_lb, _ = _mlstm_scan(flip(ql), flip(kl), flip(vl), flip(lil[1]), flip(lfl[1]), s_b)
    return h_cf + flip(h_cb), h_lf + flip(h_lb)


def _combine(a, y, h, o, g):
    h = (h[0] + h[1]).reshape(o.shape[:2] + (ML_HEADS, ML_HEAD_DIM))
    ml = _rms(h, g[MLA_WIDTH + HY_WIDTH:].reshape(ML_HEADS, ML_HEAD_DIM))
    ml = ml.reshape(o.shape) * jax.nn.sigmoid(o)
    cat = jnp.concatenate([_rms(a, g[:MLA_WIDTH]), _rms(y, g[MLA_WIDTH:MLA_WIDTH + HY_WIDTH]), ml], axis=-1)
    return cat.astype(BF16)


def _prep_w_in(w_in_l):
    cq, ckv, kr, hy, mqk, mv, mo, gt = jnp.split(w_in_l, np_cumsum(IN_SIZES)[:-1], axis=1)
    pad = jnp.zeros((D_MODEL, N_IN_PAD - P_SMALL - MLA_ROPE - ML_GATES), w_in_l.dtype)
    return jnp.concatenate([cq, ckv, hy, mqk, mv, mo, kr, gt, pad], axis=1).astype(BF16)


def np_cumsum(sizes):
    out, acc = [], 0
    for s in sizes:
        acc += s
        out.append(acc)
    return out


def _attention_operands(q, k, v):
    scale = (MLA_QK ** -0.5) * math.log2(math.e)
    padw = ((0, 0), (0, 0), (0, 0), (0, QK_PAD - MLA_QK))
    qt = jnp.pad(q * scale, padw).astype(BF16).transpose(0, 2, 3, 1)
    kk = jnp.pad(k, padw).astype(BF16).transpose(0, 2, 1, 3)
    vt = v.astype(BF16).transpose(0, 2, 3, 1)
    return qt, kk, vt


def _mixer_inputs(proj, p, rope, tm):
    b, l, _ = proj.shape
    zero = jnp.zeros((b, 1, MLA_Q_LORA), F32)
    q_raw = norm_mm(proj, p['mla_qa_norm'], zero, zero, p['mla_w_uq'], x_col=0, k=MLA_Q_LORA, tm=tm, tn=512,
                    name="mla_q_up")
    kv_raw = norm_mm(proj, p['mla_kva_norm'], zero, zero, p['mla_w_ukv'], x_col=1, k=MLA_KV_LORA, tm=tm, tn=512,
                     name="mla_kv_up")
    krope = proj[..., P_SMALL:P_SMALL + MLA_ROPE]
    q, k, v = _mla_qkv(q_raw, kv_raw, krope, p['mla_q_norm'], p['mla_k_norm'], rope)
    hy_u = proj[..., P_HY:P_HY + 3 * HY_WIDTH]
    qk = short_conv_groups(proj, P_MLQK, p['ml_conv_w'], p['ml_conv_b'], group=ML_WIDTH, tc=128, silu=True,
                           name="mlstm_short_conv")
    ml_in = (qk, proj[..., P_SMALL + MLA_ROPE:P_SMALL + MLA_ROPE + ML_GATES])
    o = proj[..., P_MLO:P_MLO + ML_WIDTH]
    return (q, k, v), hy_u, ml_in, o


def kernel(x, c, ctx, c_ctx, ada_w, ada_b, norm1_g, norm2_g, w_in, mla_qa_norm, mla_kva_norm, mla_w_uq,
           mla_w_ukv, mla_q_norm, mla_k_norm, hy_conv_w, hy_conv_b, hy_w1, hy_b1, hy_w2, hy_b2, hy_w3,
           hy_decay, hy_skip, ml_conv_w, ml_conv_b, ml_gate_b, mix_norm_g, w_out, ffn_w1, ffn_w2):
    B, S, D = x.shape
    LC = ctx.shape[1]
    rope = _axial_rope(S)

    silu_rows = jnp.concatenate([jax.nn.silu(c), jax.nn.silu(c_ctx)[None], jnp.zeros((8 - B - 1, D), F32)], axis=0)
    mods = ada_modulation(silu_rows.astype(BF16), ada_w, ada_b)

    for l in range(DEPTH):
        need_ctx = l < DEPTH - 1
        p = {'mla_qa_norm': mla_qa_norm[l], 'mla_kva_norm': mla_kva_norm[l],
             'mla_w_uq': mla_w_uq[l].astype(BF16), 'mla_w_ukv': mla_w_ukv[l].astype(BF16),
             'mla_q_norm': mla_q_norm[l], 'mla_k_norm': mla_k_norm[l],
             'hy_conv_w': hy_conv_w[l], 'hy_conv_b': hy_conv_b[l], 'hy_w1': hy_w1[l], 'hy_b1': hy_b1[l],
             'hy_w2': hy_w2[l], 'hy_b2': hy_b2[l], 'hy_w3': hy_w3[l], 'hy_decay': hy_decay[l],
             'hy_skip': hy_skip[l], 'ml_conv_w': ml_conv_w[l], 'ml_conv_b': ml_conv_b[l],
             'ml_gate_b': ml_gate_b[l]}
        w_in_l = _prep_w_in(w_in[l])
        w_out_l = w_out[l].astype(BF16)
        w1_l = ffn_w1[l].astype(BF16)
        w2_l = ffn_w2[l].astype(BF16)
        g_mix = mix_norm_g[l]

        mod_l = [m[:, None, :] for m in jnp.split(mods[l, :B], 6, axis=-1)]
        mod_c = [jnp.broadcast_to(m[:, None, :], (B, 1, D)) for m in jnp.split(mods[l, B:B + 1], 6, axis=-1)]

        proj_l = norm_mm(x, norm1_g[l], mod_l[1], mod_l[0], w_in_l, tm=1024, tn=256, name="in_proj")
        proj_c = norm_mm(ctx, norm1_g[l], mod_c[1], mod_c[0], w_in_l, tm=LC, tn=256, name="in_proj_ctx")

        (q_l, k_l, v_l), hy_l, ml_l, o_l = _mixer_inputs(proj_l, p, rope, 1024)
        (q_c, k_c, v_c), hy_c, ml_c, o_c = _mixer_inputs(proj_c, p, None, LC)

        k_all = jnp.concatenate([k_c, k_l], axis=1)
        v_all = jnp.concatenate([v_c, v_l], axis=1)
        qt, kk, vt = _attention_operands(q_l, k_all, v_all)
        a_l = attention_t(qt, kk, vt).transpose(0, 2, 1)
        y_l = hyena_long(proj_l, p)
        h_c, h_l = mlstm_bidir(ml_c[0], proj_c, ml_c[1], ml_l[0], proj_l, ml_l[1], p['ml_gate_b'])
        cat_l = _combine(a_l, y_l, h_l, o_l, g_mix)
        x = mm_residual(cat_l, w_out_l, x, mod_l[2], tm=1024, tn=512, name="out_proj")
        hid = norm_swiglu(x, norm2_g[l], mod_l[4], mod_l[3], w1_l, tm=1024, tn=512, name="ffn_up")
        x = mm_residual(hid, w2_l, x, mod_l[5], tm=1024, tn=512, name="ffn_down")

        if need_ctx:
            qt, kk, vt = _attention_operands(q_c, k_c, v_c)
            a_c = attention_t(qt, kk, vt, tq=LC).transpose(0, 2, 1)
            y_c = _hyena(hy_c, p)
            cat_c = _combine(a_c, y_c, h_c, o_c, g_mix)
            ctx = mm_residual(cat_c, w_out_l, ctx, mod_c[2], tm=LC, tn=512, name="out_proj_ctx")
            hid = norm_swiglu(ctx, norm2_g[l], mod_c[4], mod_c[3], w1_l, tm=LC, tn=512, name="ffn_up_ctx")
            ctx = mm_residual(hid, w2_l, ctx, mod_c[5], tm=LC, tn=512, name="ffn_down_ctx")
    return x
```

```python
import functools
import math

import jax
import jax.numpy as jnp
import numpy as np
from jax import lax
from jax.experimental import pallas as pl
from jax.experimental.pallas import tpu as pltpu

D_MODEL = 2048
DEPTH = 2
GRID_W = 64
EPS = 1e-6

MLA_HEADS = 8
MLA_NOPE = 128
MLA_ROPE = 64
MLA_QK = MLA_NOPE + MLA_ROPE
MLA_V = 128
MLA_Q_LORA = 512
MLA_KV_LORA = 512
MLA_WIDTH = MLA_HEADS * MLA_V
ROPE_BASE = 10000.0
ROPE_PAIRS_PER_AXIS = MLA_ROPE // 4
QK_PAD = 256

HY_WIDTH = 512
HY_ORDER = 2
HY_BANDS = 16
HY_SHIFT = 0.05
SHORT_CONV = 3

ML_HEADS = 4
ML_HEAD_DIM = 128
ML_WIDTH = ML_HEADS * ML_HEAD_DIM
ML_CHUNK = 128
ML_GATES = 4 * ML_HEADS

MIX_WIDTH = MLA_WIDTH + HY_WIDTH + ML_WIDTH
FFN_HIDDEN = ((8 * D_MODEL // 3 + 255) // 256) * 256

IN_SIZES = (MLA_Q_LORA, MLA_KV_LORA, MLA_ROPE, 3 * HY_WIDTH, 2 * ML_WIDTH, ML_WIDTH, ML_WIDTH, ML_GATES)
N_IN = sum(IN_SIZES)

P_CQ, P_CKV, P_HY, P_MLQK, P_MLV, P_MLO = 0, 512, 1024, 2560, 3584, 4096
N_MAIN = 4608
N_SMALL = 128
S_GATES = MLA_ROPE

VMEM_LIMIT_BYTES = 56 * 1024 * 1024

BF16 = jnp.bfloat16
F32 = jnp.float32


def _cparams(sem):
    return pltpu.CompilerParams(dimension_semantics=sem, vmem_limit_bytes=VMEM_LIMIT_BYTES)


def _ada_kernel(s_ref, w_ref, b_ref, o_ref):
    w = w_ref[0].astype(BF16)
    o_ref[0] = jnp.dot(s_ref[...], w, preferred_element_type=F32) + b_ref[0]


def ada_modulation(silu_rows, ada_w, ada_b, tn=1024):
    depth, d, n = ada_w.shape
    return pl.pallas_call(
        _ada_kernel,
        grid=(depth, n // tn),
        in_specs=[pl.BlockSpec((8, d), lambda l, j: (0, 0)),
                  pl.BlockSpec((1, d, tn), lambda l, j: (l, 0, j)),
                  pl.BlockSpec((1, 1, tn), lambda l, j: (l, 0, j))],
        out_specs=pl.BlockSpec((1, 8, tn), lambda l, j: (l, 0, j)),
        out_shape=jax.ShapeDtypeStruct((depth, 8, n), F32),
        compiler_params=_cparams(("parallel", "parallel")),
        name="ada_modulation",
    )(silu_rows, ada_w, ada_b.reshape(depth, 1, n))


MM_TN = 512


def column_tiles(w, tn):
    k, n = w.shape
    return w.reshape(k, n // tn, tn).transpose(1, 0, 2)


def _normed(x, g, sc, sh):
    y = x * lax.rsqrt(jnp.mean(x * x, axis=-1, keepdims=True) + EPS) * g
    return y * (1.0 + sc) + sh


def _in_proj_kernel(x_ref, g_ref, sc_ref, sh_ref, w_ref, ws_ref, o_ref, os_ref, xn_ref):
    @pl.when(pl.program_id(2) == 0)
    def _():
        xn = _normed(x_ref[0], g_ref[...], sc_ref[0], sh_ref[0]).astype(BF16)
        xn_ref[...] = xn
        os_ref[0] = jnp.dot(xn, ws_ref[...], preferred_element_type=F32)

    o_ref[0] = jnp.dot(xn_ref[...], w_ref[0], preferred_element_type=F32).astype(o_ref.dtype)


def in_proj(x, g, sc, sh, w_main, w_small, *, tm, tn, name="in_proj"):
    b, s, k = x.shape
    assert w_main.shape[1:] == (k, tn)
    n = w_main.shape[0] * tn
    ns = w_small.shape[1]
    return pl.pallas_call(
        _in_proj_kernel,
        grid=(b, s // tm, n // tn),
        in_specs=[pl.BlockSpec((1, tm, k), lambda bi, i, j: (bi, i, 0)),
                  pl.BlockSpec((1, k), lambda bi, i, j: (0, 0)),
                  pl.BlockSpec((1, 1, k), lambda bi, i, j: (bi, 0, 0)),
                  pl.BlockSpec((1, 1, k), lambda bi, i, j: (bi, 0, 0)),
                  pl.BlockSpec((1, k, tn), lambda bi, i, j: (j, 0, 0)),
                  pl.BlockSpec((k, ns), lambda bi, i, j: (0, 0))],
        out_specs=[pl.BlockSpec((1, tm, tn), lambda bi, i, j: (bi, i, j)),
                   pl.BlockSpec((1, tm, ns), lambda bi, i, j: (bi, i, 0))],
        out_shape=[jax.ShapeDtypeStruct((b, s, n), BF16), jax.ShapeDtypeStruct((b, s, ns), F32)],
        scratch_shapes=[pltpu.VMEM((tm, k), BF16)],
        compiler_params=_cparams(("parallel", "parallel", "arbitrary")),
        name=name,
    )(x, g.reshape(1, k), sc, sh, w_main, w_small)


def _norm_swiglu_kernel(x_ref, g_ref, sc_ref, sh_ref, wg_ref, wu_ref, o_ref, xn_ref):
    @pl.when(pl.program_id(2) == 0)
    def _():
        xn_ref[...] = _normed(x_ref[0], g_ref[...], sc_ref[0], sh_ref[0]).astype(BF16)

    xn = xn_ref[...]
    gate = jnp.dot(xn, wg_ref[0], preferred_element_type=F32)
    up = jnp.dot(xn, wu_ref[0], preferred_element_type=F32)
    o_ref[0] = (gate * (0.5 * jnp.tanh(0.5 * gate) + 0.5) * up).astype(o_ref.dtype)


def norm_swiglu(x, g, sc, sh, w1, *, tm, tn, name="norm_swiglu"):
    b, s, d = x.shape
    assert w1.shape[1:] == (d, tn)
    nj = w1.shape[0] // 2
    h = nj * tn
    return pl.pallas_call(
        _norm_swiglu_kernel,
        grid=(b, s // tm, nj),
        in_specs=[pl.BlockSpec((1, tm, d), lambda bi, i, j: (bi, i, 0)),
                  pl.BlockSpec((1, d), lambda bi, i, j: (0, 0)),
                  pl.BlockSpec((1, 1, d), lambda bi, i, j: (bi, 0, 0)),
                  pl.BlockSpec((1, 1, d), lambda bi, i, j: (bi, 0, 0)),
                  pl.BlockSpec((1, d, tn), lambda bi, i, j: (j, 0, 0)),
                  pl.BlockSpec((1, d, tn), lambda bi, i, j: (j + nj, 0, 0))],
        out_specs=pl.BlockSpec((1, tm, tn), lambda bi, i, j: (bi, i, j)),
        out_shape=jax.ShapeDtypeStruct((b, s, h), BF16),
        scratch_shapes=[pltpu.VMEM((tm, d), BF16)],
        compiler_params=_cparams(("parallel", "parallel", "arbitrary")),
        name=name,
    )(x, g.reshape(1, d), sc, sh, w1, w1)


def _mm_res_kernel(a_ref, w_ref, r_ref, gt_ref, o_ref):
    y = jnp.dot(a_ref[0], w_ref[0], preferred_element_type=F32)
    o_ref[0] = r_ref[0] + gt_ref[0] * y


def mm_residual(a, w, res, gate, *, tm, tn, name="mm_residual"):
    b, s, k = a.shape
    assert w.shape[1:] == (k, tn)
    n = w.shape[0] * tn
    return pl.pallas_call(
        _mm_res_kernel,
        grid=(b, s // tm, n // tn),
        in_specs=[pl.BlockSpec((1, tm, k), lambda bi, i, j: (bi, i, 0)),
                  pl.BlockSpec((1, k, tn), lambda bi, i, j: (j, 0, 0)),
                  pl.BlockSpec((1, tm, tn), lambda bi, i, j: (bi, i, j)),
                  pl.BlockSpec((1, 1, tn), lambda bi, i, j: (bi, 0, j))],
        out_specs=pl.BlockSpec((1, tm, tn), lambda bi, i, j: (bi, i, j)),
        out_shape=jax.ShapeDtypeStruct((b, s, n), F32),
        compiler_params=_cparams(("parallel", "parallel", "parallel")),
        name=name,
    )(a, w, res, gate)


ATT_KEY_CHUNK = 512


def _attn_kernel(qt_ref, *refs, n_seg):
    o_ref = refs[2 * n_seg]
    qt = qt_ref[0, 0]
    tq = qt.shape[1]
    chunks = []
    for sgi in range(n_seg):
        k_ref, vt_ref = refs[2 * sgi], refs[2 * sgi + 1]
        lk = k_ref.shape[2]
        ch = min(ATT_KEY_CHUNK, lk)
        chunks += [(k_ref, vt_ref, c * ch, ch) for c in range(lk // ch)]

    def scores(k_ref, start, ch):
        return jnp.dot(k_ref[0, 0, start:start + ch, :], qt, preferred_element_type=F32)

    m8 = jnp.full((8, tq), -jnp.inf, F32)
    for k_ref, _, start, ch in chunks:
        m8 = jnp.maximum(m8, jnp.max(scores(k_ref, start, ch).reshape(ch // 8, 8, tq), axis=0))
    m = jnp.max(m8, axis=0, keepdims=True)
    l8 = jnp.zeros((8, tq), F32)
    acc = jnp.zeros((vt_ref.shape[2], tq), F32)
    for k_ref, vt_ref, start, ch in chunks:
        p = jnp.exp2(scores(k_ref, start, ch) - m)
        l8 = l8 + jnp.sum(p.reshape(ch // 8, 8, tq), axis=0)
        acc = acc + jnp.dot(vt_ref[0, 0, :, start:start + ch], p.astype(BF16), preferred_element_type=F32)
    l = jnp.sum(l8, axis=0, keepdims=True)
    o_ref[0] = (acc / l).T


def attention(qt, segments, *, tq=256):
    b, h, dk, lq = qt.shape
    dv = segments[0][1].shape[2]
    in_specs = [pl.BlockSpec((1, 1, dk, tq), lambda bi, hi, i: (bi, hi, 0, i))]
    operands = [qt]
    for k, vt in segments:
        lk = k.shape[2]
        in_specs += [pl.BlockSpec((1, 1, lk, dk), lambda bi, hi, i: (bi, hi, 0, 0)),
                     pl.BlockSpec((1, 1, dv, lk), lambda bi, hi, i: (bi, hi, 0, 0))]
        operands += [k, vt]
    return pl.pallas_call(
        functools.partial(_attn_kernel, n_seg=len(segments)),
        grid=(b, h, lq // tq),
        in_specs=in_specs,
        out_specs=pl.BlockSpec((1, tq, dv), lambda bi, hi, i: (bi, i, hi)),
        out_shape=jax.ShapeDtypeStruct((b, lq, h * dv), F32),
        compiler_params=_cparams(("parallel", "parallel", "parallel")),
        name="attention",
    )(*operands)


HY_NA = 64
HY_NB = 128
HY_N = HY_NA * HY_NB
HY_KG = 8


@functools.lru_cache(maxsize=None)
def _dft_tables():
    na, nb, n = HY_NA, HY_NB, HY_N
    a = np.arange(na // 2)[None, :]
    ka = np.arange(na)[:, None]
    ang = 2.0 * np.pi * ((a * ka) % na) / na
    c, s = np.cos(ang), np.sin(ang)
    g1 = np.block([[c, s], [-s, c]])
    g1_inv = g1.T.copy()
    a_full = np.arange(na)[None, :]
    ang = 2.0 * np.pi * ((a_full * ka) % na) / na
    g1_real = np.concatenate([np.cos(ang), -np.sin(ang)], axis=0)
    b = np.arange(nb)[None, None, :]
    k = np.arange(na)[:, None, None] + na * np.arange(nb)[None, :, None]
    ang = 2.0 * np.pi * ((b * k) % n) / n
    cr, si = np.cos(ang), np.sin(ang)
    g2 = np.concatenate([np.concatenate([cr, si], axis=2), np.concatenate([-si, cr], axis=2)], axis=1)
    g2_inv = np.transpose(g2, (0, 2, 1)).copy()
    return (g1.astype(np.float32), g1_inv.astype(np.float32), g1_real.astype(np.float32),
            g2.astype(np.float32), g2_inv.astype(np.float32))


def _short_conv_kernel(u_ref, w_ref, b_ref, o_ref, *, silu):
    u = u_ref[0].astype(F32)
    n = u.shape[0]
    rows = lax.broadcasted_iota(jnp.int32, u.shape, 0)
    prev = jnp.where(rows == 0, 0.0, pltpu.roll(u, 1, axis=0))
    nxt = jnp.where(rows == n - 1, 0.0, pltpu.roll(u, n - 1, axis=0))
    y = b_ref[...] + prev * w_ref[0:1, :] + u * w_ref[1:2, :] + nxt * w_ref[2:3, :]
    if silu:
        y = y * jax.nn.sigmoid(y)
    o_ref[0, 0] = y


def short_conv_groups(proj, col0, w, b, *, group, tc, silu, name):
    bsz, l, _ = proj.shape
    width = w.shape[1]
    per = group // tc
    return pl.pallas_call(
        functools.partial(_short_conv_kernel, silu=silu),
        grid=(bsz, width // tc),
        in_specs=[pl.BlockSpec((1, l, tc), lambda bi, j: (bi, 0, col0 // tc + j)),
                  pl.BlockSpec((SHORT_CONV, tc), lambda bi, j: (0, j)),
                  pl.BlockSpec((1, tc), lambda bi, j: (0, j))],
        out_specs=pl.BlockSpec((1, 1, l, tc), lambda bi, j: (j // per, bi, 0, j % per)),
        out_shape=jax.ShapeDtypeStruct((width // group, bsz, l, group), F32),
        compiler_params=_cparams(("parallel", "parallel")),
        name=name,
    )(proj, w, b.reshape(1, width))


def _hy_filter_kernel(f_ref, w1_ref, b1_ref, w2_ref, b2_ref, w3_ref, dec_ref, o_ref, *, zero_row):
    hp = lax.Precision.HIGHEST
    f = f_ref[...]
    tl = f.shape[0]
    h = jnp.sin(jnp.dot(f, w1_ref[...], precision=hp, preferred_element_type=F32) + b1_ref[...])
    h = jnp.sin(jnp.dot(h, w2_ref[...], precision=hp, preferred_element_type=F32) + b2_ref[...])
    h = jnp.dot(h, w3_ref[0], precision=hp, preferred_element_type=F32)
    h = h * (jnp.exp(-f[:, 0:1] * dec_ref[0]) + HY_SHIFT)
    rows = lax.broadcasted_iota(jnp.int32, h.shape, 0) + pl.program_id(0) * tl
    o_ref[...] = jnp.where(rows == zero_row, 0.0, h)


def hyena_filter_buffer(L, p, tl=512):
    n = 2 * L
    r = jnp.arange(n, dtype=F32)
    t = jnp.where(r < L, r, n - r) / L
    bands = jnp.arange(1, HY_BANDS + 1, dtype=F32)
    ang = 2.0 * math.pi * t[:, None] * bands
    emb = 1 + 2 * HY_BANDS
    feats = jnp.concatenate([t[:, None], jnp.cos(ang), jnp.sin(ang), jnp.zeros((n, 128 - emb), F32)], axis=-1)
    hid = p['hy_w1'].shape[1]
    w1 = jnp.concatenate([p['hy_w1'], jnp.zeros((128 - emb, hid), F32)], axis=0)
    oc = HY_ORDER * HY_WIDTH
    w3 = p['hy_w3'].reshape(hid, HY_ORDER, 2, HY_WIDTH).transpose(2, 0, 1, 3).reshape(2, hid, oc)
    dec = p['hy_decay'].reshape(HY_ORDER, 2, HY_WIDTH).transpose(1, 0, 2).reshape(2, 1, oc)
    half = L // tl
    return pl.pallas_call(
        functools.partial(_hy_filter_kernel, zero_row=L),
        grid=(n // tl,),
        in_specs=[pl.BlockSpec((tl, 128), lambda i: (i, 0)),
                  pl.BlockSpec((128, hid), lambda i: (0, 0)),
                  pl.BlockSpec((1, hid), lambda i: (0, 0)),
                  pl.BlockSpec((hid, hid), lambda i: (0, 0)),
                  pl.BlockSpec((1, hid), lambda i: (0, 0)),
                  pl.BlockSpec((1, hid, oc), lambda i: (i // half, 0, 0)),
                  pl.BlockSpec((1, 1, oc), lambda i: (i // half, 0, 0))],
        out_specs=pl.BlockSpec((tl, oc), lambda i: (i, 0)),
        out_shape=jax.ShapeDtypeStruct((n, oc), F32),
        compiler_params=_cparams(("parallel",)),
        name="hyena_filter",
    )(feats, w1, p['hy_b1'].reshape(1, hid), p['hy_w2'], p['hy_b2'].reshape(1, hid), w3, dec)


def _dft_first_kernel(g_ref, x_ref, o_ref, *, precise):
    if precise:
        o_ref[0] = jnp.dot(g_ref[...], x_ref[0], precision=lax.Precision.HIGHEST, preferred_element_type=F32)
    else:
        y = jnp.dot(g_ref[...].astype(BF16), x_ref[0].astype(BF16), preferred_element_type=F32)
        o_ref[0] = y.astype(o_ref.dtype)


def dft_first(g, x, *, n_out, offset=0, tn=8192, precise=False):
    _, r, lanes = x.shape
    m = g.shape[0]
    return pl.pallas_call(
        functools.partial(_dft_first_kernel, precise=precise),
        grid=(n_out, lanes // tn),
        in_specs=[pl.BlockSpec((m, r), lambda p, j: (0, 0)),
                  pl.BlockSpec((1, r, tn), lambda p, j: (offset + p, 0, j))],
        out_specs=pl.BlockSpec((1, m, tn), lambda p, j: (p, 0, j)),
        out_shape=jax.ShapeDtypeStruct((n_out, m, lanes), F32 if precise else BF16),
        compiler_params=_cparams(("parallel", "parallel")),
        name="dft_first",
    )(g, x)


def _dft_filter_mid_kernel(g_ref, y_ref, o_ref, *, scale):
    hp = lax.Precision.HIGHEST
    for i in range(g_ref.shape[0]):
        y = jnp.concatenate([y_ref[0, i], y_ref[1, i]], axis=0)
        o_ref[i] = scale * jnp.dot(g_ref[i], y, precision=hp, preferred_element_type=F32)


def dft_filter_mid(g2, y1):
    _, na, nb, oc = y1.shape
    kg = HY_KG // 2
    return pl.pallas_call(
        functools.partial(_dft_filter_mid_kernel, scale=1.0 / HY_N),
        grid=(na // kg,),
        in_specs=[pl.BlockSpec((kg, 2 * nb, 2 * nb), lambda g: (g, 0, 0)),
                  pl.BlockSpec((2, kg, nb, oc), lambda g: (0, g, 0, 0))],
        out_specs=pl.BlockSpec((kg, 2 * nb, oc), lambda g: (g, 0, 0)),
        out_shape=jax.ShapeDtypeStruct((na, 2 * nb, oc), F32),
        compiler_params=_cparams(("parallel",)),
        name="dft_filter_mid",
    )(g2, y1)


def _dft_mid_kernel(gf_ref, gi_ref, y_ref, h_ref, o_ref):
    nb = y_ref.shape[3]
    for i in range(gf_ref.shape[0]):
        y = jnp.concatenate([y_ref[0, 0, i], y_ref[0, 1, i]], axis=0)
        x = jnp.dot(gf_ref[i].astype(BF16), y, preferred_element_type=F32)
        xr, xi = x[:nb], x[nb:]
        hr, hi = h_ref[i, :nb], h_ref[i, nb:]
        prod = jnp.concatenate([xr * hr - xi * hi, xr * hi + xi * hr], axis=0).astype(BF16)
        u = jnp.dot(gi_ref[i].astype(BF16), prod, preferred_element_type=F32)
        o_ref[0, 0, i] = u[:nb].astype(o_ref.dtype)
        o_ref[0, 1, i] = u[nb:].astype(o_ref.dtype)


def dft_mid(g2, g2_inv, y1, h, order):
    npair, _, na, nb, c = y1.shape
    kg = HY_KG
    return pl.pallas_call(
        _dft_mid_kernel,
        grid=(npair, na // kg),
        in_specs=[pl.BlockSpec((kg, 2 * nb, 2 * nb), lambda p, g: (g, 0, 0)),
                  pl.BlockSpec((kg, 2 * nb, 2 * nb), lambda p, g: (g, 0, 0)),
                  pl.BlockSpec((1, 2, kg, nb, c), lambda p, g: (p, 0, g, 0, 0)),
                  pl.BlockSpec((kg, 2 * nb, c), lambda p, g: (g, 0, order))],
        out_specs=pl.BlockSpec((1, 2, kg, nb, c), lambda p, g: (p, 0, g, 0, 0)),
        out_shape=jax.ShapeDtypeStruct(y1.shape, BF16),
        compiler_params=_cparams(("parallel", "parallel")),
        name="dft_mid",
    )(g2, g2_inv, y1, h)


def _dft_last_kernel(g_ref, u_ref, z_ref, gate_ref, skip_ref, o_ref):
    conv = jnp.dot(g_ref[...].astype(BF16), u_ref[0], preferred_element_type=F32)
    o_ref[0] = gate_ref[0] * (conv + skip_ref[...] * z_ref[0])


def dft_last(g1_inv, u, z, z_off, gate, gate_off, skip_lanes, *, tn=8192):
    npair, m2, lanes = u.shape
    r = g1_inv.shape[0]
    return pl.pallas_call(
        _dft_last_kernel,
        grid=(npair, lanes // tn),
        in_specs=[pl.BlockSpec((r, m2), lambda p, j: (0, 0)),
                  pl.BlockSpec((1, m2, tn), lambda p, j: (p, 0, j)),
                  pl.BlockSpec((1, r, tn), lambda p, j: (z_off + p, 0, j)),
                  pl.BlockSpec((1, r, tn), lambda p, j: (gate_off + p, 0, j)),
                  pl.BlockSpec((1, tn), lambda p, j: (0, 0))],
        out_specs=pl.BlockSpec((1, r, tn), lambda p, j: (p, 0, j)),
        out_shape=jax.ShapeDtypeStruct((npair, r, lanes), F32),
        compiler_params=_cparams(("parallel", "parallel")),
        name="dft_last",
    )(g1_inv, u, z, gate, skip_lanes)


def hyena_long(proj, p):
    bsz, l, _ = proj.shape
    assert 2 * l == HY_N and bsz % 2 == 0
    c = HY_WIDTH
    npair = bsz // 2
    lanes = HY_NB * c
    g1, g1_inv, g1_real, g2, g2_inv = _dft_tables()
    g1_bf, g1_inv_bf = jnp.asarray(g1), jnp.asarray(g1_inv)
    g2_bf, g2_inv_bf = jnp.asarray(g2), jnp.asarray(g2_inv)

    buf = hyena_filter_buffer(l, p)
    y1h = dft_first(jnp.asarray(g1_real), buf.reshape(1, HY_NA, HY_NB * HY_ORDER * c), n_out=1, precise=True)
    hspec = dft_filter_mid(jnp.asarray(g2), y1h.reshape(2, HY_NA, HY_NB, HY_ORDER * c))

    vx = short_conv_groups(proj, P_HY, p['hy_conv_w'], p['hy_conv_b'], group=c, tc=128, silu=False,
                           name="hyena_short_conv")
    vx = vx.reshape(3 * npair, HY_NA, lanes)
    z, z_off = vx, 0
    for n in range(HY_ORDER):
        y1 = dft_first(g1_bf, z, n_out=npair, offset=z_off)
        u = dft_mid(g2_bf, g2_inv_bf, y1.reshape(npair, 2, HY_NA, HY_NB, c), hspec, n)
        skip_lanes = jnp.tile(p['hy_skip'][n], 8192 // c).reshape(1, 8192)
        z = dft_last(g1_inv_bf, u.reshape(npair, 2 * HY_NA, lanes), z, z_off, vx, (n + 1) * npair, skip_lanes)
        z_off = 0
    return z.reshape(bsz, l, c)


def _log_sigmoid(x):
    return jnp.minimum(x, 0.0) - jnp.log1p(jnp.exp(-jnp.abs(x)))


def _mlstm_kernel(qc_ref, kc_ref, vc_ref, gc_ref, gtc_ref, ql_ref, kl_ref, vl_ref, gl_ref, gtl_ref,
                  bias_ref, biast_ref, oc_ref, ol_ref, c_ref, m_ref, *, nc_ctx):
    d = ML_HEAD_DIM
    t = ML_CHUNK
    nh = ML_HEADS
    direction = pl.program_id(0)
    step = pl.program_id(2)

    @pl.when(step == 0)
    def _():
        c_ref[...] = jnp.zeros_like(c_ref)
        m_ref[...] = jnp.zeros_like(m_ref)

    def chunk(q_ref, k_ref, v_ref, g_ref, gt_ref, o_ref):
        g_cols = g_ref[0, 0] + bias_ref[0]
        g_rows = gt_ref[0, 0] + biast_ref[0]
        lf_cols = _log_sigmoid(g_cols[:, nh:])
        li_rows = g_rows[:nh]
        lf_rows = _log_sigmoid(g_rows[nh:])
        r = lax.broadcasted_iota(jnp.int32, (t, t), 0)
        cidx = lax.broadcasted_iota(jnp.int32, (t, t), 1)
        sign = 1 - 2 * direction
        mask = (cidx - r) * sign <= 0
        mask_t = (r - cidx) * sign <= 0
        tri = mask.astype(F32)
        tri_t = mask_t.astype(F32)
        lane = lax.broadcasted_iota(jnp.int32, (t, d), 1)
        ones_col = jnp.where(lane == 0, 1.0, 0.0).astype(BF16)
        for h in range(nh):
            sl = slice(h * d, (h + 1) * d)
            bc = jnp.sum(tri * lf_rows[h:h + 1, :], axis=1, keepdims=True)
            br = jnp.sum(tri_t * lf_cols[:, h:h + 1], axis=0, keepdims=True)
            lir = li_rows[h:h + 1, :]
            m_prev = m_ref[h, 0:1, 0:1]
            dmat = jnp.where(mask, bc - br + lir, -jnp.inf)
            inter = bc + m_prev
            m_t = jnp.maximum(inter, jnp.max(dmat, axis=1, keepdims=True))
            dexp = jnp.exp(dmat - m_t)
            inter_w = jnp.exp(inter - m_t)
            qh = (q_ref[0, 0, :, sl] * (d ** -0.5)).astype(BF16)
            kt = k_ref[0, 0, :, sl].T
            v_ext = jnp.concatenate([v_ref[0, :, sl].astype(BF16), ones_col], axis=1)
            s = jnp.dot(qh, kt.astype(BF16), preferred_element_type=F32) * dexp
            c_ext = c_ref[h]
            acc = inter_w * jnp.dot(qh, c_ext.astype(BF16), preferred_element_type=F32)
            acc = acc + jnp.dot(s.astype(BF16), v_ext, preferred_element_type=F32)
            den = jnp.maximum(jnp.abs(acc[:, d:d + 1]), jnp.exp(-m_t))
            o_ref[0, 0, :, sl] = acc[:, :d] / den
            b_last = jnp.sum(lf_rows[h:h + 1, :], axis=1, keepdims=True)
            g_row = b_last - br + lir
            m_new = jnp.maximum(b_last + m_prev, jnp.max(g_row, axis=1, keepdims=True))
            a = jnp.exp(b_last + m_prev - m_new)
            w_row = jnp.exp(g_row - m_new)
            c_ref[h] = a * c_ext + jnp.dot((kt * w_row).astype(BF16), v_ext, preferred_element_type=F32)
            m_ref[h] = jnp.broadcast_to(m_new, m_ref.shape[1:])

    @pl.when(step < nc_ctx)
    def _():
        chunk(qc_ref, kc_ref, vc_ref, gc_ref, gtc_ref, oc_ref)

    @pl.when(step >= nc_ctx)
    def _():
        chunk(ql_ref, kl_ref, vl_ref, gl_ref, gtl_ref, ol_ref)


def mlstm_bidir(qk_c, proj_c, gates_c, qk_l, proj_l, gates_l, gate_bias):
    _, bsz, lc, w = qk_c.shape
    ll = qk_l.shape[2]
    t, nh = ML_CHUNK, ML_HEADS
    nc_c, nc_l = lc // t, ll // t

    def by_dir(g):
        g4 = g.reshape(g.shape[:-1] + (4, nh))
        return jnp.stack([jnp.concatenate([g4[..., dd, :], g4[..., 2 + dd, :]], axis=-1) for dd in range(2)])

    gd_c, gd_l = by_dir(gates_c), by_dir(gates_l)
    gt_c, gt_l = gd_c.transpose(0, 1, 3, 2), gd_l.transpose(0, 1, 3, 2)
    bias = by_dir(gate_bias.reshape(1, ML_GATES))
    bias_t = bias.transpose(0, 2, 1)

    def idx(step_off, nc):
        def f(dd, g):
            cc = jnp.clip(g - step_off, 0, nc - 1)
            return cc + dd * (nc - 1 - 2 * cc)
        return f

    ic, il = idx(0, nc_c), idx(nc_c, nc_l)
    vcol = P_MLV // w

    def seg_specs(ix):
        return [pl.BlockSpec((1, 1, t, w), lambda dd, b, g: (0, b, ix(dd, g), 0)),
                pl.BlockSpec((1, 1, t, w), lambda dd, b, g: (1, b, ix(dd, g), 0)),
                pl.BlockSpec((1, t, w), lambda dd, b, g: (b, ix(dd, g), vcol)),
                pl.BlockSpec((1, 1, t, 2 * nh), lambda dd, b, g: (dd, b, ix(dd, g), 0)),
                pl.BlockSpec((1, 1, 2 * nh, t), lambda dd, b, g: (dd, b, 0, ix(dd, g)))]

    return pl.pallas_call(
        functools.partial(_mlstm_kernel, nc_ctx=nc_c),
        grid=(2, bsz, nc_c + nc_l),
        in_specs=seg_specs(ic) + seg_specs(il) + [
            pl.BlockSpec((1, 1, 2 * nh), lambda dd, b, g: (dd, 0, 0)),
            pl.BlockSpec((1, 2 * nh, 1), lambda dd, b, g: (dd, 0, 0))],
        out_specs=[pl.BlockSpec((1, 1, t, w), lambda dd, b, g: (dd, b, ic(dd, g), 0)),
                   pl.BlockSpec((1, 1, t, w), lambda dd, b, g: (dd, b, il(dd, g), 0))],
        out_shape=[jax.ShapeDtypeStruct((2, bsz, lc, w), F32), jax.ShapeDtypeStruct((2, bsz, ll, w), F32)],
        scratch_shapes=[pltpu.VMEM((nh, ML_HEAD_DIM, 2 * ML_HEAD_DIM), F32), pltpu.VMEM((nh, 8, 128), F32)],
        compiler_params=_cparams(("parallel", "parallel", "arbitrary")),
        name="mlstm",
    )(qk_c, qk_c, proj_c, gd_c, gt_c, qk_l, qk_l, proj_l, gd_l, gt_l, bias, bias_t)


def _rms(x):
    return x * lax.rsqrt(jnp.mean(x * x, axis=-1, keepdims=True) + EPS)


def _dot_split(x, e):
    hi = x.astype(BF16)
    lo = (x - hi.astype(F32)).astype(BF16)
    return jnp.dot(hi, e, preferred_element_type=F32) + jnp.dot(lo, e, preferred_element_type=F32)


def _swap_rope_halves(x):
    w = x.shape[1]
    lane = lax.broadcasted_iota(jnp.int32, x.shape, 1) % MLA_ROPE
    return jnp.where(lane < MLA_ROPE // 2, pltpu.roll(x, w - MLA_ROPE // 2, axis=1),
                     pltpu.roll(x, MLA_ROPE // 2, axis=1))


def _mla_prep_kernel(pq_ref, pkv_ref, sm_ref, gqa_ref, gkva_ref, wq_ref, wkv_ref, gqn_ref, gqr_ref, gkn_ref,
                     gkr_ref, en_ref, er_ref, ent_ref, ert_ref, *rest, use_rope):
    if use_rope:
        ccq_ref, ssq_ref, cck_ref, ssk_ref, qt_ref, k_ref, vt_ref = rest
    else:
        qt_ref, k_ref, vt_ref = rest
    nh, dn, dr = MLA_HEADS, MLA_NOPE, MLA_ROPE
    qa = (_rms(pq_ref[0].astype(F32)) * gqa_ref[...]).astype(BF16)
    kva = (_rms(pkv_ref[0].astype(F32)) * gkva_ref[...]).astype(BF16)
    q_raw = jnp.dot(qa, wq_ref[...], preferred_element_type=F32)
    kv_raw = jnp.dot(kva, wkv_ref[...], preferred_element_type=F32)
    inv_d = 1.0 / MLA_QK

    qn, qr = q_raw[:, :nh * dn], q_raw[:, nh * dn:]
    ss = _dot_split(qn * qn, en_ref[...]) + _dot_split(qr * qr, er_ref[...])
    rs = lax.rsqrt(ss * inv_d + EPS)
    qn = qn * _dot_split(rs, ent_ref[...]) * gqn_ref[...]
    qr = qr * _dot_split(rs, ert_ref[...]) * gqr_ref[...]
    if use_rope:
        qr = qr * ccq_ref[...] + _swap_rope_halves(qr) * ssq_ref[...]
    qrt = qr.T
    tl = qn.shape[0]
    for h in range(nh):
        qt_ref[0, h, 0:dn, :] = qn[:, h * dn:(h + 1) * dn].T.astype(BF16)
        qt_ref[0, h, dn:dn + dr, :] = qrt[h * dr:(h + 1) * dr].astype(BF16)
        qt_ref[0, h, dn + dr:, :] = jnp.zeros((QK_PAD - dn - dr, tl), BF16)

    kn, vv = kv_raw[:, :nh * dn], kv_raw[:, nh * dn:]
    lane = lax.broadcasted_iota(jnp.int32, sm_ref.shape[1:], 1)
    kr = jnp.where(lane < dr, sm_ref[0], 0.0)
    ssk = _dot_split(kn * kn, en_ref[...]) + jnp.sum(kr * kr, axis=-1, keepdims=True)
    rsk = lax.rsqrt(ssk * inv_d + EPS)
    kn = kn * _dot_split(rsk, ent_ref[...]) * gkn_ref[...]
    kr = kr * gkr_ref[...]
    if use_rope:
        kr = kr * cck_ref[...] + _swap_rope_halves(kr) * ssk_ref[...]
    for h in range(nh):
        k_ref[0, h, :, 0:dn] = kn[:, h * dn:(h + 1) * dn].astype(BF16)
        k_ref[0, h, :, dn:] = (kr * rsk[:, h:h + 1]).astype(BF16)
        vt_ref[0, h] = vv[:, h * MLA_V:(h + 1) * MLA_V].T.astype(BF16)


@functools.lru_cache(maxsize=None)
def _head_selectors():
    en = np.kron(np.eye(MLA_HEADS), np.ones((MLA_NOPE, 1)))
    er = np.kron(np.eye(MLA_HEADS), np.ones((MLA_ROPE, 1)))
    pad = lambda m: np.pad(m, ((0, 0), (0, 128 - m.shape[1]))).astype(np.float32)
    return pad(en), pad(er)


def _rope_tables(n_tokens):
    rows = n_tokens // GRID_W
    row = jnp.repeat(jnp.arange(rows, dtype=F32), GRID_W)
    col = jnp.tile(jnp.arange(GRID_W, dtype=F32), rows)
    freqs = ROPE_BASE ** (-jnp.arange(ROPE_PAIRS_PER_AXIS, dtype=F32) / ROPE_PAIRS_PER_AXIS)
    ang = jnp.concatenate([row[:, None] * freqs, col[:, None] * freqs], axis=-1)
    cos, sin = jnp.cos(ang), jnp.sin(ang)
    cc = jnp.concatenate([cos, cos], axis=-1)
    ss = jnp.concatenate([-sin, sin], axis=-1)
    zero = jnp.zeros((n_tokens, 128 - MLA_ROPE), F32)
    return (jnp.tile(cc, (1, MLA_HEADS)), jnp.tile(ss, (1, MLA_HEADS)),
            jnp.concatenate([cc, zero], axis=-1), jnp.concatenate([ss, zero], axis=-1))


def mla_prep(proj, small, p, rope_tables, *, tl):
    b, l, _ = proj.shape
    nh = MLA_HEADS
    en, er = _head_selectors()
    en_b, er_b = jnp.asarray(en, BF16), jnp.asarray(er, BF16)
    ent_b, ert_b = jnp.asarray(en.T.copy(), BF16), jnp.asarray(er.T.copy(), BF16)
    const = lambda shape: pl.BlockSpec(shape, lambda bi, i: (0,) * len(shape))
    nq = nh * MLA_QK
    nkv = nh * (MLA_NOPE + MLA_V)
    in_specs = [pl.BlockSpec((1, tl, MLA_Q_LORA), lambda bi, i: (bi, i, P_CQ // MLA_Q_LORA)),
                pl.BlockSpec((1, tl, MLA_KV_LORA), lambda bi, i: (bi, i, P_CKV // MLA_KV_LORA)),
                pl.BlockSpec((1, tl, N_SMALL), lambda bi, i: (bi, i, 0)),
                const((1, MLA_Q_LORA)), const((1, MLA_KV_LORA)), const((MLA_Q_LORA, nq)), const((MLA_KV_LORA, nkv)),
                const((1, nh * MLA_NOPE)), const((1, nh * MLA_ROPE)), const((1, nh * MLA_NOPE)), const((1, 128)),
                const(en.shape), const(er.shape), const(en.T.shape), const(er.T.shape)]
    operands = [proj, proj, small, p['gqa'], p['gkva'], p['wq'], p['wkv'], p['gqn'], p['gqr'], p['gkn'], p['gkr'],
                en_b, er_b, ent_b, ert_b]
    use_rope = rope_tables is not None
    if use_rope:
        in_specs += [pl.BlockSpec((tl, nh * MLA_ROPE), lambda bi, i: (i, 0)),
                     pl.BlockSpec((tl, nh * MLA_ROPE), lambda bi, i: (i, 0)),
                     pl.BlockSpec((tl, 128), lambda bi, i: (i, 0)),
                     pl.BlockSpec((tl, 128), lambda bi, i: (i, 0))]
        operands += list(rope_tables)
    return pl.pallas_call(
        functools.partial(_mla_prep_kernel, use_rope=use_rope),
        grid=(b, l // tl),
        in_specs=in_specs,
        out_specs=[pl.BlockSpec((1, nh, QK_PAD, tl), lambda bi, i: (bi, 0, 0, i)),
                   pl.BlockSpec((1, nh, tl, QK_PAD), lambda bi, i: (bi, 0, i, 0)),
                   pl.BlockSpec((1, nh, MLA_V, tl), lambda bi, i: (bi, 0, 0, i))],
        out_shape=[jax.ShapeDtypeStruct((b, nh, QK_PAD, l), BF16),
                   jax.ShapeDtypeStruct((b, nh, l, QK_PAD), BF16),
                   jax.ShapeDtypeStruct((b, nh, MLA_V, l), BF16)],
        compiler_params=_cparams(("parallel", "parallel")),
        name="mla_prep",
    )(*operands)


def _combine_out_kernel(a_ref, y_ref, h_ref, o_ref, g_ref, w_ref, r_ref, gt_ref, out_ref, cat_ref):
    @pl.when(pl.program_id(2) == 0)
    def _():
        g = g_ref[...]
        cat_ref[:, :MLA_WIDTH] = (_rms(a_ref[0]) * g[:, :MLA_WIDTH]).astype(BF16)
        hy0 = MLA_WIDTH
        cat_ref[:, hy0:hy0 + HY_WIDTH] = (_rms(y_ref[0]) * g[:, hy0:hy0 + HY_WIDTH]).astype(BF16)
        ml0 = MLA_WIDTH + HY_WIDTH
        hh = h_ref[0, 0] + h_ref[1, 0]
        og = jax.nn.sigmoid(o_ref[0].astype(F32))
        for hd in range(ML_HEADS):
            sl = slice(hd * ML_HEAD_DIM, (hd + 1) * ML_HEAD_DIM)
            gl = g[:, ml0 + hd * ML_HEAD_DIM:ml0 + (hd + 1) * ML_HEAD_DIM]
            cat_ref[:, ml0 + hd * ML_HEAD_DIM:ml0 + (hd + 1) * ML_HEAD_DIM] = (
                _rms(hh[:, sl]) * gl * og[:, sl]).astype(BF16)

    y = jnp.dot(cat_ref[...], w_ref[0], preferred_element_type=F32)
    out_ref[0] = r_ref[0] + gt_ref[0] * y


def combine_out_proj(a, y, h2, proj, g_mix, w_out, res, gate, *, tm, tn, name="out_proj"):
    b, s, d = res.shape
    return pl.pallas_call(
        _combine_out_kernel,
        grid=(b, s // tm, d // tn),
        in_specs=[pl.BlockSpec((1, tm, MLA_WIDTH), lambda bi, i, j: (bi, i, 0)),
                  pl.BlockSpec((1, tm, HY_WIDTH), lambda bi, i, j: (bi, i, 0)),
                  pl.BlockSpec((2, 1, tm, ML_WIDTH), lambda bi, i, j: (0, bi, i, 0)),
                  pl.BlockSpec((1, tm, ML_WIDTH), lambda bi, i, j: (bi, i, P_MLO // ML_WIDTH)),
                  pl.BlockSpec((1, MIX_WIDTH), lambda bi, i, j: (0, 0)),
                  pl.BlockSpec((1, MIX_WIDTH, tn), lambda bi, i, j: (j, 0, 0)),
                  pl.BlockSpec((1, tm, tn), lambda bi, i, j: (bi, i, j)),
                  pl.BlockSpec((1, 1, tn), lambda bi, i, j: (bi, 0, j))],
        out_specs=pl.BlockSpec((1, tm, tn), lambda bi, i, j: (bi, i, j)),
        out_shape=jax.ShapeDtypeStruct((b, s, d), F32),
        scratch_shapes=[pltpu.VMEM((tm, MIX_WIDTH), BF16)],
        compiler_params=_cparams(("parallel", "parallel", "arbitrary")),
        name=name,
    )(a, y, h2, proj, g_mix.reshape(1, MIX_WIDTH), w_out, res, gate)


@functools.lru_cache(maxsize=None)
def _direct_dft_tables(l):
    n = 2 * l
    k = np.arange(n)[:, None]
    t = np.arange(l)[None, :]
    ang = 2.0 * np.pi * ((k * t) % n) / n
    c, s = np.cos(ang), np.sin(ang)
    gf = np.block([[c, s], [-s, c]])
    gi = gf.T.copy()
    t_full = np.arange(n)[None, :]
    ang = 2.0 * np.pi * ((k * t_full) % n) / n
    g_real = np.concatenate([np.cos(ang), -np.sin(ang)], axis=0)
    return gf.astype(np.float32), gi.astype(np.float32), g_real.astype(np.float32)


def _direct_conv_kernel(gf_ref, gi_ref, z_ref, gate_ref, h_ref, skip_ref, o_ref, *, scale):
    z = z_ref[0]
    x = jnp.dot(gf_ref[...].astype(BF16), z.astype(BF16), preferred_element_type=F32)
    n = x.shape[0] // 2
    xr, xi = x[:n], x[n:]
    hr, hi = h_ref[:n] * scale, h_ref[n:] * scale
    prod = jnp.concatenate([xr * hr - xi * hi, xr * hi + xi * hr], axis=0).astype(BF16)
    conv = jnp.dot(gi_ref[...].astype(BF16), prod, preferred_element_type=F32)
    o_ref[0] = gate_ref[0] * (conv + skip_ref[...] * z)


def hyena_short(proj, p):
    bsz, l, _ = proj.shape
    c = HY_WIDTH
    npair = bsz // 2
    n = 2 * l
    gf, gi, g_real = _direct_dft_tables(l)
    buf = hyena_filter_buffer(l, p, tl=l)
    hspec = dft_first(jnp.asarray(g_real), buf.reshape(1, n, HY_ORDER * c), n_out=1, tn=HY_ORDER * c,
                      precise=True)[0]
    vx = short_conv_groups(proj, P_HY, p['hy_conv_w'], p['hy_conv_b'], group=c, tc=128, silu=False,
                           name="hyena_short_conv_ctx")
    vx = vx.reshape(3 * npair, 2 * l, c)
    z = vx

    def rows_at(off):
        return lambda q: (off + q, 0, 0)

    def cols_at(col):
        return lambda q: (0, col)

    for order in range(HY_ORDER):
        z = pl.pallas_call(
            functools.partial(_direct_conv_kernel, scale=1.0 / n),
            grid=(npair,),
            in_specs=[pl.BlockSpec(gf.shape, lambda q: (0, 0)),
                      pl.BlockSpec(gi.shape, lambda q: (0, 0)),
                      pl.BlockSpec((1, 2 * l, c), rows_at(0)),
                      pl.BlockSpec((1, 2 * l, c), rows_at((order + 1) * npair)),
                      pl.BlockSpec((2 * n, c), cols_at(order)),
                      pl.BlockSpec((1, c), lambda q: (0, 0))],
            out_specs=pl.BlockSpec((1, 2 * l, c), lambda q: (q, 0, 0)),
            out_shape=jax.ShapeDtypeStruct((npair, 2 * l, c), F32),
            compiler_params=_cparams(("parallel",)),
            name="hyena_direct_conv",
        )(jnp.asarray(gf), jnp.asarray(gi), z, vx, hspec, p['hy_skip'][order].reshape(1, c))
    return z.reshape(bsz, l, c)


def _split_cols(w, sizes):
    out, start = [], 0
    for s in sizes:
        out.append(w[:, start:start + s])
        start += s
    return out


def _prep_layer_weights(l, w_in, mla_qa_norm, mla_kva_norm, mla_w_uq, mla_w_ukv, mla_q_norm, mla_k_norm):
    cq, ckv, kr, hy, mqk, mv, mo, gt = _split_cols(w_in[l], IN_SIZES)
    w_main = column_tiles(jnp.concatenate([cq, ckv, hy, mqk, mv, mo], axis=1).astype(BF16), MM_TN)
    pad = jnp.zeros((D_MODEL, N_SMALL - MLA_ROPE - ML_GATES), F32)
    w_small = jnp.concatenate([kr, gt, pad], axis=1).astype(BF16)
    nh = MLA_HEADS
    wq = mla_w_uq[l].reshape(MLA_Q_LORA, nh, MLA_QK)
    wq = jnp.concatenate([wq[:, :, :MLA_NOPE].reshape(MLA_Q_LORA, -1), wq[:, :, MLA_NOPE:].reshape(MLA_Q_LORA, -1)],
                         axis=1).astype(BF16)
    wkv = mla_w_ukv[l].reshape(MLA_KV_LORA, nh, MLA_NOPE + MLA_V)
    wkv = jnp.concatenate([wkv[:, :, :MLA_NOPE].reshape(MLA_KV_LORA, -1),
                           wkv[:, :, MLA_NOPE:].reshape(MLA_KV_LORA, -1)], axis=1).astype(BF16)
    q_scale = (MLA_QK ** -0.5) * math.log2(math.e)
    gq, gk = mla_q_norm[l] * q_scale, mla_k_norm[l]
    mla = {'gqa': mla_qa_norm[l].reshape(1, -1), 'gkva': mla_kva_norm[l].reshape(1, -1), 'wq': wq, 'wkv': wkv,
           'gqn': jnp.tile(gq[:MLA_NOPE], nh).reshape(1, -1), 'gqr': jnp.tile(gq[MLA_NOPE:], nh).reshape(1, -1),
           'gkn': jnp.tile(gk[:MLA_NOPE], nh).reshape(1, -1),
           'gkr': jnp.concatenate([gk[MLA_NOPE:], jnp.zeros((128 - MLA_ROPE,), F32)]).reshape(1, -1)}
    return w_main, w_small, mla


def kernel(x, c, ctx, c_ctx, ada_w, ada_b, norm1_g, norm2_g, w_in, mla_qa_norm, mla_kva_norm, mla_w_uq,
           mla_w_ukv, mla_q_norm, mla_k_norm, hy_conv_w, hy_conv_b, hy_w1, hy_b1, hy_w2, hy_b2, hy_w3,
           hy_decay, hy_skip, ml_conv_w, ml_conv_b, ml_gate_b, mix_norm_g, w_out, ffn_w1, ffn_w2):
    B, S, D = x.shape
    LC = ctx.shape[1]
    rope_tables = _rope_tables(S)

    silu_rows = jnp.concatenate([jax.nn.silu(c), jax.nn.silu(c_ctx)[None], jnp.zeros((8 - B - 1, D), F32)], axis=0)
    mods = ada_modulation(silu_rows.astype(BF16), ada_w, ada_b)

    for l in range(DEPTH):
        need_ctx = l < DEPTH - 1
        p = {'hy_conv_w': hy_conv_w[l], 'hy_conv_b': hy_conv_b[l], 'hy_w1': hy_w1[l], 'hy_b1': hy_b1[l],
             'hy_w2': hy_w2[l], 'hy_b2': hy_b2[l], 'hy_w3': hy_w3[l], 'hy_decay': hy_decay[l],
             'hy_skip': hy_skip[l], 'ml_conv_w': ml_conv_w[l], 'ml_conv_b': ml_conv_b[l]}
        w_main, w_small, mla = _prep_layer_weights(l, w_in, mla_qa_norm, mla_kva_norm, mla_w_uq, mla_w_ukv,
                                                   mla_q_norm, mla_k_norm)
        w_out_l = column_tiles(w_out[l].astype(BF16), MM_TN)
        w1_l = column_tiles(ffn_w1[l].astype(BF16), MM_TN)
        w2_l = column_tiles(ffn_w2[l].astype(BF16), MM_TN)

        mod_l = [m[:, None, :] for m in jnp.split(mods[l, :B], 6, axis=-1)]
        mod_c = [jnp.broadcast_to(m[:, None, :], (B, 1, D)) for m in jnp.split(mods[l, B:B + 1], 6, axis=-1)]

        proj_l, small_l = in_proj(x, norm1_g[l], mod_l[1], mod_l[0], w_main, w_small, tm=1024, tn=MM_TN)
        proj_c, small_c = in_proj(ctx, norm1_g[l], mod_c[1], mod_c[0], w_main, w_small, tm=LC, tn=MM_TN,
                                  name="in_proj_ctx")

        qt_l, k_l, vt_l = mla_prep(proj_l, small_l, mla, rope_tables, tl=512)
        qt_c, k_c, vt_c = mla_prep(proj_c, small_c, mla, None, tl=LC)
        a_l = attention(qt_l, [(k_c, vt_c), (k_l, vt_l)])
        y_l = hyena_long(proj_l, p)

        def mlstm_in(proj, small):
            qk = short_conv_groups(proj, P_MLQK, p['ml_conv_w'], p['ml_conv_b'], group=ML_WIDTH, tc=128,
                                   silu=True, name="mlstm_short_conv")
            return qk, small[..., S_GATES:S_GATES + ML_GATES]

        qk_c, gates_c = mlstm_in(proj_c, small_c)
        qk_l, gates_l = mlstm_in(proj_l, small_l)
        h_c, h_l = mlstm_bidir(qk_c, proj_c, gates_c, qk_l, proj_l, gates_l, ml_gate_b[l])
        x = combine_out_proj(a_l, y_l, h_l, proj_l, mix_norm_g[l], w_out_l, x, mod_l[2], tm=1024, tn=MM_TN)
        hid = norm_swiglu(x, norm2_g[l], mod_l[4], mod_l[3], w1_l, tm=1024, tn=MM_TN, name="ffn_up")
        x = mm_residual(hid, w2_l, x, mod_l[5], tm=1024, tn=MM_TN, name="ffn_down")

        if need_ctx:
            a_c = attention(qt_c, [(k_c, vt_c)], tq=LC)
            y_c = hyena_short(proj_c, p)
            ctx = combine_out_proj(a_c, y_c, h_c, proj_c, mix_norm_g[l], w_out_l, ctx, mod_c[2], tm=LC, tn=MM_TN,
                                   name="out_proj_ctx")
            hid = norm_swiglu(ctx, norm2_g[l], mod_c[4], mod_c[3], w1_l, tm=LC, tn=MM_TN, name="ffn_up_ctx")
            ctx = mm_residual(hid, w2_l, ctx, mod_c[5], tm=LC, tn=MM_TN, name="ffn_down_ctx")
    return x
```

```python
import functools
import math

import jax
import jax.numpy as jnp
import numpy as np
from jax import lax
from jax.experimental import pallas as pl
from jax.experimental.pallas import tpu as pltpu

D_MODEL = 2048
DEPTH = 2
GRID_W = 64
EPS = 1e-6

MLA_HEADS = 8
MLA_NOPE = 128
MLA_ROPE = 64
MLA_QK = MLA_NOPE + MLA_ROPE
MLA_V = 128
MLA_Q_LORA = 512
MLA_KV_LORA = 512
MLA_WIDTH = MLA_HEADS * MLA_V
ROPE_BASE = 10000.0
ROPE_PAIRS_PER_AXIS = MLA_ROPE // 4
QK_PAD = 256

HY_WIDTH = 512
HY_ORDER = 2
HY_BANDS = 16
HY_SHIFT = 0.05
SHORT_CONV = 3

ML_HEADS = 4
ML_HEAD_DIM = 128
ML_WIDTH = ML_HEADS * ML_HEAD_DIM
ML_CHUNK = 128
ML_GATES = 4 * ML_HEADS

MIX_WIDTH = MLA_WIDTH + HY_WIDTH + ML_WIDTH
FFN_HIDDEN = ((8 * D_MODEL // 3 + 255) // 256) * 256

IN_SIZES = (MLA_Q_LORA, MLA_KV_LORA, MLA_ROPE, 3 * HY_WIDTH, 2 * ML_WIDTH, ML_WIDTH, ML_WIDTH, ML_GATES)
N_IN = sum(IN_SIZES)

P_CQ, P_CKV, P_HY, P_MLQK, P_MLV, P_MLO = 0, 512, 1024, 2560, 3584, 4096
N_MAIN = 4608
N_SMALL = 128
S_GATES = MLA_ROPE

VMEM_LIMIT_BYTES = 56 * 1024 * 1024

BF16 = jnp.bfloat16
F32 = jnp.float32


def _cparams(sem):
    return pltpu.CompilerParams(dimension_semantics=sem, vmem_limit_bytes=VMEM_LIMIT_BYTES)


def _ada_kernel(s_ref, w_ref, b_ref, o_ref):
    w = w_ref[0].astype(BF16)
    o_ref[0] = jnp.dot(s_ref[...], w, preferred_element_type=F32) + b_ref[0]


def ada_modulation(silu_rows, ada_w, ada_b, tn=1024):
    depth, d, n = ada_w.shape
    return pl.pallas_call(
        _ada_kernel,
        grid=(depth, n // tn),
        in_specs=[pl.BlockSpec((8, d), lambda l, j: (0, 0)),
                  pl.BlockSpec((1, d, tn), lambda l, j: (l, 0, j)),
                  pl.BlockSpec((1, 1, tn), lambda l, j: (l, 0, j))],
        out_specs=pl.BlockSpec((1, 8, tn), lambda l, j: (l, 0, j)),
        out_shape=jax.ShapeDtypeStruct((depth, 8, n), F32),
        compiler_params=_cparams(("parallel", "parallel")),
        name="ada_modulation",
    )(silu_rows, ada_w, ada_b.reshape(depth, 1, n))


MM_TN = 512
IN_PROJ_TN = 1536


def _normed(x, g, sc, sh):
    y = x * lax.rsqrt(jnp.mean(x * x, axis=-1, keepdims=True) + EPS) * g
    return y * (1.0 + sc) + sh


def _in_proj_kernel(x_ref, g_ref, sc_ref, sh_ref, w_ref, ws_ref, o_ref, os_ref, xn_ref):
    @pl.when(pl.program_id(2) == 0)
    def _():
        xn = _normed(x_ref[0], g_ref[...], sc_ref[0], sh_ref[0]).astype(BF16)
        xn_ref[...] = xn
        os_ref[0] = jnp.dot(xn, ws_ref[...], preferred_element_type=F32)

    o_ref[0] = jnp.dot(xn_ref[...], w_ref[...], preferred_element_type=F32).astype(o_ref.dtype)


def in_proj(x, g, sc, sh, w_main, w_small, *, tm, tn, name="in_proj"):
    b, s, k = x.shape
    n = w_main.shape[1]
    ns = w_small.shape[1]
    return pl.pallas_call(
        _in_proj_kernel,
        grid=(b, s // tm, n // tn),
        in_specs=[pl.BlockSpec((1, tm, k), lambda bi, i, j: (bi, i, 0)),
                  pl.BlockSpec((1, k), lambda bi, i, j: (0, 0)),
                  pl.BlockSpec((1, 1, k), lambda bi, i, j: (bi, 0, 0)),
                  pl.BlockSpec((1, 1, k), lambda bi, i, j: (bi, 0, 0)),
                  pl.BlockSpec((k, tn), lambda bi, i, j: (0, j)),
                  pl.BlockSpec((k, ns), lambda bi, i, j: (0, 0))],
        out_specs=[pl.BlockSpec((1, tm, tn), lambda bi, i, j: (bi, i, j)),
                   pl.BlockSpec((1, tm, ns), lambda bi, i, j: (bi, i, 0))],
        out_shape=[jax.ShapeDtypeStruct((b, s, n), BF16), jax.ShapeDtypeStruct((b, s, ns), F32)],
        scratch_shapes=[pltpu.VMEM((tm, k), BF16)],
        compiler_params=_cparams(("parallel", "parallel", "arbitrary")),
        name=name,
    )(x, g.reshape(1, k), sc, sh, w_main, w_small)


def _norm_swiglu_kernel(x_ref, g_ref, sc_ref, sh_ref, wg_ref, wu_ref, o_ref, xn_ref):
    @pl.when(pl.program_id(2) == 0)
    def _():
        xn_ref[...] = _normed(x_ref[0], g_ref[...], sc_ref[0], sh_ref[0]).astype(BF16)

    xn = xn_ref[...]
    gate = jnp.dot(xn, wg_ref[...], preferred_element_type=F32)
    up = jnp.dot(xn, wu_ref[...], preferred_element_type=F32)
    o_ref[0] = (gate * (0.5 * jnp.tanh(0.5 * gate) + 0.5) * up).astype(o_ref.dtype)


def norm_swiglu(x, g, sc, sh, w1, *, tm, tn, name="norm_swiglu"):
    b, s, d = x.shape
    h = w1.shape[1] // 2
    nj = h // tn
    return pl.pallas_call(
        _norm_swiglu_kernel,
        grid=(b, s // tm, nj),
        in_specs=[pl.BlockSpec((1, tm, d), lambda bi, i, j: (bi, i, 0)),
                  pl.BlockSpec((1, d), lambda bi, i, j: (0, 0)),
                  pl.BlockSpec((1, 1, d), lambda bi, i, j: (bi, 0, 0)),
                  pl.BlockSpec((1, 1, d), lambda bi, i, j: (bi, 0, 0)),
                  pl.BlockSpec((d, tn), lambda bi, i, j: (0, j)),
                  pl.BlockSpec((d, tn), lambda bi, i, j: (0, j + nj))],
        out_specs=pl.BlockSpec((1, tm, tn), lambda bi, i, j: (bi, i, j)),
        out_shape=jax.ShapeDtypeStruct((b, s, h), BF16),
        scratch_shapes=[pltpu.VMEM((tm, d), BF16)],
        compiler_params=_cparams(("parallel", "parallel", "arbitrary")),
        name=name,
    )(x, g.reshape(1, d), sc, sh, w1, w1)


def _mm_res_kernel(a_ref, w_ref, r_ref, gt_ref, o_ref):
    y = jnp.dot(a_ref[0], w_ref[...], preferred_element_type=F32)
    o_ref[0] = r_ref[0] + gt_ref[0] * y


def mm_residual(a, w, res, gate, *, tm, tn, name="mm_residual"):
    b, s, k = a.shape
    n = w.shape[1]
    return pl.pallas_call(
        _mm_res_kernel,
        grid=(b, s // tm, n // tn),
        in_specs=[pl.BlockSpec((1, tm, k), lambda bi, i, j: (bi, i, 0)),
                  pl.BlockSpec((k, tn), lambda bi, i, j: (0, j)),
                  pl.BlockSpec((1, tm, tn), lambda bi, i, j: (bi, i, j)),
                  pl.BlockSpec((1, 1, tn), lambda bi, i, j: (bi, 0, j))],
        out_specs=pl.BlockSpec((1, tm, tn), lambda bi, i, j: (bi, i, j)),
        out_shape=jax.ShapeDtypeStruct((b, s, n), F32),
        compiler_params=_cparams(("parallel", "parallel", "parallel")),
        name=name,
    )(a, w, res, gate)


ATT_KEY_CHUNK = 512
ATT_TQ = 512


def _attn_kernel(qt_ref, *refs, n_seg):
    o_ref, s_ref = refs[2 * n_seg], refs[2 * n_seg + 1]
    qt = qt_ref[0, 0]
    tq = qt.shape[1]
    chunks, off = [], 0
    for sgi in range(n_seg):
        k_ref, vt_ref = refs[2 * sgi], refs[2 * sgi + 1]
        lk = k_ref.shape[2]
        ch = min(ATT_KEY_CHUNK, lk)
        for c in range(lk // ch):
            chunks.append((k_ref, vt_ref, c * ch, ch, off))
            off += ch

    m8 = jnp.full((8, tq), -jnp.inf, F32)
    for k_ref, _, start, ch, off in chunks:
        s = jnp.dot(k_ref[0, 0, start:start + ch, :], qt, preferred_element_type=F32)
        s_ref[off:off + ch, :] = s
        m8 = jnp.maximum(m8, jnp.max(s.reshape(ch // 8, 8, tq), axis=0))
    m = jnp.max(m8, axis=0, keepdims=True)
    l8 = jnp.zeros((8, tq), F32)
    acc = jnp.zeros((vt_ref.shape[2], tq), F32)
    for _, vt_ref, start, ch, off in chunks:
        p = jnp.exp2(s_ref[off:off + ch, :] - m)
        l8 = l8 + jnp.sum(p.reshape(ch // 8, 8, tq), axis=0)
        acc = acc + jnp.dot(vt_ref[0, 0, :, start:start + ch], p.astype(BF16), preferred_element_type=F32)
    l = jnp.sum(l8, axis=0, keepdims=True)
    o_ref[0] = (acc / l).T


def attention(qt, segments, *, tq=256):
    b, h, dk, lq = qt.shape
    dv = segments[0][1].shape[2]
    in_specs = [pl.BlockSpec((1, 1, dk, tq), lambda bi, hi, i: (bi, hi, 0, i))]
    operands = [qt]
    for k, vt in segments:
        lk = k.shape[2]
        in_specs += [pl.BlockSpec((1, 1, lk, dk), lambda bi, hi, i: (bi, hi, 0, 0)),
                     pl.BlockSpec((1, 1, dv, lk), lambda bi, hi, i: (bi, hi, 0, 0))]
        operands += [k, vt]
    return pl.pallas_call(
        functools.partial(_attn_kernel, n_seg=len(segments)),
        grid=(b, h, lq // tq),
        in_specs=in_specs,
        out_specs=pl.BlockSpec((1, tq, dv), lambda bi, hi, i: (bi, i, hi)),
        out_shape=jax.ShapeDtypeStruct((b, lq, h * dv), F32),
        scratch_shapes=[pltpu.VMEM((sum(k.shape[2] for k, _ in segments), tq), F32)],
        compiler_params=_cparams(("parallel", "parallel", "parallel")),
        name="attention",
    )(*operands)


HY_NA = 64
HY_NB = 128
HY_N = HY_NA * HY_NB
HY_KG = 8


@functools.lru_cache(maxsize=None)
def _dft_tables():
    na, nb, n = HY_NA, HY_NB, HY_N
    a = np.arange(na // 2)[None, :]
    ka = np.arange(na)[:, None]
    ang = 2.0 * np.pi * ((a * ka) % na) / na
    c, s = np.cos(ang), np.sin(ang)
    g1 = np.block([[c, s], [-s, c]])
    g1_inv = g1.T.copy()
    a_full = np.arange(na)[None, :]
    ang = 2.0 * np.pi * ((a_full * ka) % na) / na
    g1_real = np.concatenate([np.cos(ang), -np.sin(ang)], axis=0)
    b = np.arange(nb)[None, None, :]
    k = np.arange(na)[:, None, None] + na * np.arange(nb)[None, :, None]
    ang = 2.0 * np.pi * ((b * k) % n) / n
    cr, si = np.cos(ang), np.sin(ang)
    g2 = np.concatenate([np.concatenate([cr, si], axis=2), np.concatenate([-si, cr], axis=2)], axis=1)
    g2_inv = np.transpose(g2, (0, 2, 1)).copy()
    return (g1.astype(np.float32), g1_inv.astype(np.float32), g1_real.astype(np.float32),
            g2.astype(np.float32), g2_inv.astype(np.float32))


def _short_conv_kernel(u_ref, w_ref, b_ref, o_ref, *, silu):
    u = u_ref[0].astype(F32)
    n = u.shape[0]
    rows = lax.broadcasted_iota(jnp.int32, u.shape, 0)
    prev = jnp.where(rows == 0, 0.0, pltpu.roll(u, 1, axis=0))
    nxt = jnp.where(rows == n - 1, 0.0, pltpu.roll(u, n - 1, axis=0))
    y = b_ref[...] + prev * w_ref[0:1, :] + u * w_ref[1:2, :] + nxt * w_ref[2:3, :]
    if silu:
        y = y * jax.nn.sigmoid(y)
    o_ref[0, 0] = y


def short_conv_groups(proj, col0, w, b, *, group, tc, silu, name):
    bsz, l, _ = proj.shape
    width = w.shape[1]
    per = group // tc
    return pl.pallas_call(
        functools.partial(_short_conv_kernel, silu=silu),
        grid=(bsz, width // tc),
        in_specs=[pl.BlockSpec((1, l, tc), lambda bi, j: (bi, 0, col0 // tc + j)),
                  pl.BlockSpec((SHORT_CONV, tc), lambda bi, j: (0, j)),
                  pl.BlockSpec((1, tc), lambda bi, j: (0, j))],
        out_specs=pl.BlockSpec((1, 1, l, tc), lambda bi, j: (j // per, bi, 0, j % per)),
        out_shape=jax.ShapeDtypeStruct((width // group, bsz, l, group), F32),
        compiler_params=_cparams(("parallel", "parallel")),
        name=name,
    )(proj, w, b.reshape(1, width))


def _hy_filter_kernel(f_ref, w1_ref, b1_ref, w2_ref, b2_ref, w3_ref, dec_ref, o_ref, *, zero_row):
    hp = lax.Precision.HIGHEST
    f = f_ref[...]
    tl = f.shape[0]
    h = jnp.sin(jnp.dot(f, w1_ref[...], precision=hp, preferred_element_type=F32) + b1_ref[...])
    h = jnp.sin(jnp.dot(h, w2_ref[...], precision=hp, preferred_element_type=F32) + b2_ref[...])
    h = jnp.dot(h, w3_ref[0], precision=hp, preferred_element_type=F32)
    h = h * (jnp.exp(-f[:, 0:1] * dec_ref[0]) + HY_SHIFT)
    rows = lax.broadcasted_iota(jnp.int32, h.shape, 0) + pl.program_id(0) * tl
    o_ref[...] = jnp.where(rows == zero_row, 0.0, h)


def hyena_filter_buffer(L, p, tl=512):
    n = 2 * L
    r = jnp.arange(n, dtype=F32)
    t = jnp.where(r < L, r, n - r) / L
    bands = jnp.arange(1, HY_BANDS + 1, dtype=F32)
    ang = 2.0 * math.pi * t[:, None] * bands
    emb = 1 + 2 * HY_BANDS
    feats = jnp.concatenate([t[:, None], jnp.cos(ang), jnp.sin(ang), jnp.zeros((n, 128 - emb), F32)], axis=-1)
    hid = p['hy_w1'].shape[1]
    w1 = jnp.concatenate([p['hy_w1'], jnp.zeros((128 - emb, hid), F32)], axis=0)
    oc = HY_ORDER * HY_WIDTH
    w3 = p['hy_w3'].reshape(hid, HY_ORDER, 2, HY_WIDTH).transpose(2, 0, 1, 3).reshape(2, hid, oc)
    dec = p['hy_decay'].reshape(HY_ORDER, 2, HY_WIDTH).transpose(1, 0, 2).reshape(2, 1, oc)
    half = L // tl
    return pl.pallas_call(
        functools.partial(_hy_filter_kernel, zero_row=L),
        grid=(n // tl,),
        in_specs=[pl.BlockSpec((tl, 128), lambda i: (i, 0)),
                  pl.BlockSpec((128, hid), lambda i: (0, 0)),
                  pl.BlockSpec((1, hid), lambda i: (0, 0)),
                  pl.BlockSpec((hid, hid), lambda i: (0, 0)),
                  pl.BlockSpec((1, hid), lambda i: (0, 0)),
                  pl.BlockSpec((1, hid, oc), lambda i: (i // half, 0, 0)),
                  pl.BlockSpec((1, 1, oc), lambda i: (i // half, 0, 0))],
        out_specs=pl.BlockSpec((tl, oc), lambda i: (i, 0)),
        out_shape=jax.ShapeDtypeStruct((n, oc), F32),
        compiler_params=_cparams(("parallel",)),
        name="hyena_filter",
    )(feats, w1, p['hy_b1'].reshape(1, hid), p['hy_w2'], p['hy_b2'].reshape(1, hid), w3, dec)


def _dft_first_kernel(g_ref, x_ref, o_ref, *, precise):
    if precise:
        o_ref[0] = jnp.dot(g_ref[...], x_ref[0], precision=lax.Precision.HIGHEST, preferred_element_type=F32)
    else:
        y = jnp.dot(g_ref[...].astype(BF16), x_ref[0].astype(BF16), preferred_element_type=F32)
        o_ref[0] = y.astype(o_ref.dtype)


def dft_first(g, x, *, n_out, offset=0, tn=8192, precise=False):
    _, r, lanes = x.shape
    m = g.shape[0]
    return pl.pallas_call(
        functools.partial(_dft_first_kernel, precise=precise),
        grid=(n_out, lanes // tn),
        in_specs=[pl.BlockSpec((m, r), lambda p, j: (0, 0)),
                  pl.BlockSpec((1, r, tn), lambda p, j: (offset + p, 0, j))],
        out_specs=pl.BlockSpec((1, m, tn), lambda p, j: (p, 0, j)),
        out_shape=jax.ShapeDtypeStruct((n_out, m, lanes), F32 if precise else BF16),
        compiler_params=_cparams(("parallel", "parallel")),
        name="dft_first",
    )(g, x)


def _dft_filter_mid_kernel(g_ref, y_ref, o_ref, *, scale):
    hp = lax.Precision.HIGHEST
    for i in range(g_ref.shape[0]):
        y = jnp.concatenate([y_ref[0, i], y_ref[1, i]], axis=0)
        o_ref[i] = scale * jnp.dot(g_ref[i], y, precision=hp, preferred_element_type=F32)


def dft_filter_mid(g2, y1):
    _, na, nb, oc = y1.shape
    kg = HY_KG // 2
    return pl.pallas_call(
        functools.partial(_dft_filter_mid_kernel, scale=1.0 / HY_N),
        grid=(na // kg,),
        in_specs=[pl.BlockSpec((kg, 2 * nb, 2 * nb), lambda g: (g, 0, 0)),
                  pl.BlockSpec((2, kg, nb, oc), lambda g: (0, g, 0, 0))],
        out_specs=pl.BlockSpec((kg, 2 * nb, oc), lambda g: (g, 0, 0)),
        out_shape=jax.ShapeDtypeStruct((na, 2 * nb, oc), F32),
        compiler_params=_cparams(("parallel",)),
        name="dft_filter_mid",
    )(g2, y1)


def _dft_mid_kernel(gf_ref, gi_ref, y_ref, h_ref, o_ref):
    nb = y_ref.shape[3]
    for i in range(gf_ref.shape[0]):
        y = jnp.concatenate([y_ref[0, 0, i], y_ref[0, 1, i]], axis=0)
        x = jnp.dot(gf_ref[i].astype(BF16), y, preferred_element_type=F32)
        xr, xi = x[:nb], x[nb:]
        hr, hi = h_ref[i, :nb], h_ref[i, nb:]
        prod = jnp.concatenate([xr * hr - xi * hi, xr * hi + xi * hr], axis=0).astype(BF16)
        u = jnp.dot(gi_ref[i].astype(BF16), prod, preferred_element_type=F32)
        o_ref[0, 0, i] = u[:nb].astype(o_ref.dtype)
        o_ref[0, 1, i] = u[nb:].astype(o_ref.dtype)


def dft_mid(g2, g2_inv, y1, h, order):
    npair, _, na, nb, c = y1.shape
    kg = HY_KG
    return pl.pallas_call(
        _dft_mid_kernel,
        grid=(npair, na // kg),
        in_specs=[pl.BlockSpec((kg, 2 * nb, 2 * nb), lambda p, g: (g, 0, 0)),
                  pl.BlockSpec((kg, 2 * nb, 2 * nb), lambda p, g: (g, 0, 0)),
                  pl.BlockSpec((1, 2, kg, nb, c), lambda p, g: (p, 0, g, 0, 0)),
                  pl.BlockSpec((kg, 2 * nb, c), lambda p, g: (g, 0, order))],
        out_specs=pl.BlockSpec((1, 2, kg, nb, c), lambda p, g: (p, 0, g, 0, 0)),
        out_shape=jax.ShapeDtypeStruct(y1.shape, BF16),
        compiler_params=_cparams(("parallel", "parallel")),
        name="dft_mid",
    )(g2, g2_inv, y1, h)


def _dft_last_kernel(g_ref, u_ref, z_ref, gate_ref, skip_ref, o_ref):
    conv = jnp.dot(g_ref[...].astype(BF16), u_ref[0], preferred_element_type=F32)
    o_ref[0] = gate_ref[0] * (conv + skip_ref[...] * z_ref[0])


def dft_last(g1_inv, u, z, z_off, gate, gate_off, skip_lanes, *, tn=8192):
    npair, m2, lanes = u.shape
    r = g1_inv.shape[0]
    return pl.pallas_call(
        _dft_last_kernel,
        grid=(npair, lanes // tn),
        in_specs=[pl.BlockSpec((r, m2), lambda p, j: (0, 0)),
                  pl.BlockSpec((1, m2, tn), lambda p, j: (p, 0, j)),
                  pl.BlockSpec((1, r, tn), lambda p, j: (z_off + p, 0, j)),
                  pl.BlockSpec((1, r, tn), lambda p, j: (gate_off + p, 0, j)),
                  pl.BlockSpec((1, tn), lambda p, j: (0, 0))],
        out_specs=pl.BlockSpec((1, r, tn), lambda p, j: (p, 0, j)),
        out_shape=jax.ShapeDtypeStruct((npair, r, lanes), F32),
        compiler_params=_cparams(("parallel", "parallel")),
        name="dft_last",
    )(g1_inv, u, z, gate, skip_lanes)


def hyena_long(proj, p):
    bsz, l, _ = proj.shape
    assert 2 * l == HY_N and bsz % 2 == 0
    c = HY_WIDTH
    npair = bsz // 2
    lanes = HY_NB * c
    g1, g1_inv, g1_real, g2, g2_inv = _dft_tables()
    g1_bf, g1_inv_bf = jnp.asarray(g1), jnp.asarray(g1_inv)
    g2_bf, g2_inv_bf = jnp.asarray(g2), jnp.asarray(g2_inv)

    buf = hyena_filter_buffer(l, p)
    y1h = dft_first(jnp.asarray(g1_real), buf.reshape(1, HY_NA, HY_NB * HY_ORDER * c), n_out=1, precise=True)
    hspec = dft_filter_mid(jnp.asarray(g2), y1h.reshape(2, HY_NA, HY_NB, HY_ORDER * c))

    vx = short_conv_groups(proj, P_HY, p['hy_conv_w'], p['hy_conv_b'], group=c, tc=128, silu=False,
                           name="hyena_short_conv")
    vx = vx.reshape(3 * npair, HY_NA, lanes)
    z, z_off = vx, 0
    for n in range(HY_ORDER):
        y1 = dft_first(g1_bf, z, n_out=npair, offset=z_off)
        u = dft_mid(g2_bf, g2_inv_bf, y1.reshape(npair, 2, HY_NA, HY_NB, c), hspec, n)
        skip_lanes = jnp.tile(p['hy_skip'][n], 8192 // c).reshape(1, 8192)
        z = dft_last(g1_inv_bf, u.reshape(npair, 2 * HY_NA, lanes), z, z_off, vx, (n + 1) * npair, skip_lanes)
        z_off = 0
    return z.reshape(bsz, l, c)


def _log_sigmoid(x):
    return jnp.minimum(x, 0.0) - jnp.log1p(jnp.exp(-jnp.abs(x)))


def _mlstm_kernel(*refs, nc_ctx):
    ins, (bias_ref, biast_ref), outs, (c_ref, m_ref) = refs[:20], refs[20:22], refs[22:26], refs[26:28]
    d = ML_HEAD_DIM
    t = ML_CHUNK
    nh = ML_HEADS
    step = pl.program_id(1)

    @pl.when(step == 0)
    def _():
        c_ref[...] = jnp.zeros_like(c_ref)
        m_ref[...] = jnp.zeros_like(m_ref)

    def chunk(direction, q_ref, k_ref, v_ref, g_ref, gt_ref, o_ref):
        g_cols = g_ref[0, 0] + bias_ref[direction]
        g_rows = gt_ref[0, 0] + biast_ref[direction]
        lf_cols = _log_sigmoid(g_cols[:, nh:])
        li_rows = g_rows[:nh]
        lf_rows = _log_sigmoid(g_rows[nh:])
        r = lax.broadcasted_iota(jnp.int32, (t, t), 0)
        cidx = lax.broadcasted_iota(jnp.int32, (t, t), 1)
        mask = cidx <= r if direction == 0 else cidx >= r
        mask_t = r <= cidx if direction == 0 else r >= cidx
        tri = mask.astype(F32)
        tri_t = mask_t.astype(F32)
        lane = lax.broadcasted_iota(jnp.int32, (t, d), 1)
        ones_col = jnp.where(lane == 0, 1.0, 0.0).astype(BF16)
        for h in range(nh):
            sl = slice(h * d, (h + 1) * d)
            bc = jnp.sum(tri * lf_rows[h:h + 1, :], axis=1, keepdims=True)
            br = jnp.sum(tri_t * lf_cols[:, h:h + 1], axis=0, keepdims=True)
            lir = li_rows[h:h + 1, :]
            m_prev = m_ref[direction, h, 0:1, 0:1]
            dmat = jnp.where(mask, bc - br + lir, -jnp.inf)
            inter = bc + m_prev
            m_t = jnp.maximum(inter, jnp.max(dmat, axis=1, keepdims=True))
            dexp = jnp.exp(dmat - m_t)
            inter_w = jnp.exp(inter - m_t)
            qh = (q_ref[0, 0, :, sl] * (d ** -0.5)).astype(BF16)
            kt = k_ref[0, 0, :, sl].T
            v_ext = jnp.concatenate([v_ref[0, :, sl].astype(BF16), ones_col], axis=1)
            s = jnp.dot(qh, kt.astype(BF16), preferred_element_type=F32) * dexp
            c_ext = c_ref[direction, h]
            acc = inter_w * jnp.dot(qh, c_ext.astype(BF16), preferred_element_type=F32)
            acc = acc + jnp.dot(s.astype(BF16), v_ext, preferred_element_type=F32)
            den = jnp.maximum(jnp.abs(acc[:, d:d + 1]), jnp.exp(-m_t))
            o_ref[0, :, sl] = acc[:, :d] / den
            b_last = jnp.sum(lf_rows[h:h + 1, :], axis=1, keepdims=True)
            g_row = b_last - br + lir
            m_new = jnp.maximum(b_last + m_prev, jnp.max(g_row, axis=1, keepdims=True))
            a = jnp.exp(b_last + m_prev - m_new)
            w_row = jnp.exp(g_row - m_new)
            c_ref[direction, h] = a * c_ext + jnp.dot((kt * w_row).astype(BF16), v_ext,
                                                      preferred_element_type=F32)
            m_ref[direction, h] = jnp.broadcast_to(m_new, m_ref.shape[2:])

    @pl.when(step < nc_ctx)
    def _():
        for dd in range(2):
            chunk(dd, *ins[10 * dd:10 * dd + 5], outs[2 * dd])

    @pl.when(step >= nc_ctx)
    def _():
        for dd in range(2):
            chunk(dd, *ins[10 * dd + 5:10 * dd + 10], outs[2 * dd + 1])


def mlstm_bidir(qk_c, proj_c, gates_c, qk_l, proj_l, gates_l, gate_bias):
    _, bsz, lc, w = qk_c.shape
    ll = qk_l.shape[2]
    t, nh = ML_CHUNK, ML_HEADS
    nc_c, nc_l = lc // t, ll // t

    def by_dir(g):
        g4 = g.reshape(g.shape[:-1] + (4, nh))
        return jnp.stack([jnp.concatenate([g4[..., dd, :], g4[..., 2 + dd, :]], axis=-1) for dd in range(2)])

    gd_c, gd_l = by_dir(gates_c), by_dir(gates_l)
    gt_c, gt_l = gd_c.transpose(0, 1, 3, 2), gd_l.transpose(0, 1, 3, 2)
    bias = by_dir(gate_bias.reshape(1, ML_GATES))
    bias_t = bias.transpose(0, 2, 1)

    def idx(dd, step_off, nc):
        def f(g):
            cc = jnp.clip(g - step_off, 0, nc - 1)
            return cc + dd * (nc - 1 - 2 * cc)
        return f

    vcol = P_MLV // w

    def seg_specs(dd, ix):
        return [pl.BlockSpec((1, 1, t, w), lambda b, g: (0, b, ix(g), 0)),
                pl.BlockSpec((1, 1, t, w), lambda b, g: (1, b, ix(g), 0)),
                pl.BlockSpec((1, t, w), lambda b, g: (b, ix(g), vcol)),
                pl.BlockSpec((1, 1, t, 2 * nh), lambda b, g: (dd, b, ix(g), 0)),
                pl.BlockSpec((1, 1, 2 * nh, t), lambda b, g: (dd, b, 0, ix(g)))]

    def out_spec(ix):
        return pl.BlockSpec((1, t, w), lambda b, g: (b, ix(g), 0))

    in_specs, out_specs, operands = [], [], []
    for dd in range(2):
        ic, il = idx(dd, 0, nc_c), idx(dd, nc_c, nc_l)
        in_specs += seg_specs(dd, ic) + seg_specs(dd, il)
        operands += [qk_c, qk_c, proj_c, gd_c, gt_c, qk_l, qk_l, proj_l, gd_l, gt_l]
        out_specs += [out_spec(ic), out_spec(il)]
    in_specs += [pl.BlockSpec((2, 1, 2 * nh), lambda b, g: (0, 0, 0)),
                 pl.BlockSpec((2, 2 * nh, 1), lambda b, g: (0, 0, 0))]
    h_cf, h_lf, h_cb, h_lb = pl.pallas_call(
        functools.partial(_mlstm_kernel, nc_ctx=nc_c),
        grid=(bsz, nc_c + nc_l),
        in_specs=in_specs,
        out_specs=out_specs,
        out_shape=[jax.ShapeDtypeStruct((bsz, lc, w), F32), jax.ShapeDtypeStruct((bsz, ll, w), F32)] * 2,
        scratch_shapes=[pltpu.VMEM((2, nh, ML_HEAD_DIM, 2 * ML_HEAD_DIM), F32), pltpu.VMEM((2, nh, 8, 128), F32)],
        compiler_params=_cparams(("parallel", "arbitrary")),
        name="mlstm",
    )(*operands, bias, bias_t)
    return (h_cf, h_cb), (h_lf, h_lb)


def _rms(x):
    return x * lax.rsqrt(jnp.mean(x * x, axis=-1, keepdims=True) + EPS)


def _dot_split(x, e):
    hi = x.astype(BF16)
    lo = (x - hi.astype(F32)).astype(BF16)
    return jnp.dot(hi, e, preferred_element_type=F32) + jnp.dot(lo, e, preferred_element_type=F32)


def _swap_rope_halves(x):
    w = x.shape[1]
    lane = lax.broadcasted_iota(jnp.int32, x.shape, 1) % MLA_ROPE
    return jnp.where(lane < MLA_ROPE // 2, pltpu.roll(x, w - MLA_ROPE // 2, axis=1),
                     pltpu.roll(x, MLA_ROPE // 2, axis=1))


def _mla_prep_kernel(pq_ref, pkv_ref, sm_ref, gqa_ref, gkva_ref, wq_ref, wkv_ref, gqn_ref, gqr_ref, gkn_ref,
                     gkr_ref, en_ref, er_ref, ent_ref, ert_ref, *rest, use_rope):
    if use_rope:
        ccq_ref, ssq_ref, cck_ref, ssk_ref, qt_ref, k_ref, vt_ref = rest
    else:
        qt_ref, k_ref, vt_ref = rest
    nh, dn, dr = MLA_HEADS, MLA_NOPE, MLA_ROPE
    qa = (_rms(pq_ref[0].astype(F32)) * gqa_ref[...]).astype(BF16)
    kva = (_rms(pkv_ref[0].astype(F32)) * gkva_ref[...]).astype(BF16)
    q_raw = jnp.dot(qa, wq_ref[...], preferred_element_type=F32)
    kv_raw = jnp.dot(kva, wkv_ref[...], preferred_element_type=F32)
    inv_d = 1.0 / MLA_QK

    qn, qr = q_raw[:, :nh * dn], q_raw[:, nh * dn:]
    ss = _dot_split(qn * qn, en_ref[...]) + _dot_split(qr * qr, er_ref[...])
    rs = lax.rsqrt(ss * inv_d + EPS)
    qn = qn * _dot_split(rs, ent_ref[...]) * gqn_ref[...]
    qr = qr * _dot_split(rs, ert_ref[...]) * gqr_ref[...]
    if use_rope:
        qr = qr * ccq_ref[...] + _swap_rope_halves(qr) * ssq_ref[...]
    qrt = qr.T
    tl = qn.shape[0]
    for h in range(nh):
        qt_ref[0, h, 0:dn, :] = qn[:, h * dn:(h + 1) * dn].T.astype(BF16)
        qt_ref[0, h, dn:dn + dr, :] = qrt[h * dr:(h + 1) * dr].astype(BF16)
        qt_ref[0, h, dn + dr:, :] = jnp.zeros((QK_PAD - dn - dr, tl), BF16)

    kn, vv = kv_raw[:, :nh * dn], kv_raw[:, nh * dn:]
    lane = lax.broadcasted_iota(jnp.int32, sm_ref.shape[1:], 1)
    kr = jnp.where(lane < dr, sm_ref[0], 0.0)
    ssk = _dot_split(kn * kn, en_ref[...]) + jnp.sum(kr * kr, axis=-1, keepdims=True)
    rsk = lax.rsqrt(ssk * inv_d + EPS)
    kn = kn * _dot_split(rsk, ent_ref[...]) * gkn_ref[...]
    kr = kr * gkr_ref[...]
    if use_rope:
        kr = kr * cck_ref[...] + _swap_rope_halves(kr) * ssk_ref[...]
    for h in range(nh):
        k_ref[0, h, :, 0:dn] = kn[:, h * dn:(h + 1) * dn].astype(BF16)
        k_ref[0, h, :, dn:] = (kr * rsk[:, h:h + 1]).astype(BF16)
        vt_ref[0, h] = vv[:, h * MLA_V:(h + 1) * MLA_V].T.astype(BF16)


@functools.lru_cache(maxsize=None)
def _head_selectors():
    en = np.kron(np.eye(MLA_HEADS), np.ones((MLA_NOPE, 1)))
    er = np.kron(np.eye(MLA_HEADS), np.ones((MLA_ROPE, 1)))
    pad = lambda m: np.pad(m, ((0, 0), (0, 128 - m.shape[1]))).astype(np.float32)
    return pad(en), pad(er)


def _rope_tables(n_tokens):
    rows = n_tokens // GRID_W
    row = jnp.repeat(jnp.arange(rows, dtype=F32), GRID_W)
    col = jnp.tile(jnp.arange(GRID_W, dtype=F32), rows)
    freqs = ROPE_BASE ** (-jnp.arange(ROPE_PAIRS_PER_AXIS, dtype=F32) / ROPE_PAIRS_PER_AXIS)
    ang = jnp.concatenate([row[:, None] * freqs, col[:, None] * freqs], axis=-1)
    cos, sin = jnp.cos(ang), jnp.sin(ang)
    cc = jnp.concatenate([cos, cos], axis=-1)
    ss = jnp.concatenate([-sin, sin], axis=-1)
    zero = jnp.zeros((n_tokens, 128 - MLA_ROPE), F32)
    return (jnp.tile(cc, (1, MLA_HEADS)), jnp.tile(ss, (1, MLA_HEADS)),
            jnp.concatenate([cc, zero], axis=-1), jnp.concatenate([ss, zero], axis=-1))


def mla_prep(proj, small, p, rope_tables, *, tl):
    b, l, _ = proj.shape
    nh = MLA_HEADS
    en, er = _head_selectors()
    en_b, er_b = jnp.asarray(en, BF16), jnp.asarray(er, BF16)
    ent_b, ert_b = jnp.asarray(en.T.copy(), BF16), jnp.asarray(er.T.copy(), BF16)
    const = lambda shape: pl.BlockSpec(shape, lambda bi, i: (0,) * len(shape))
    nq = nh * MLA_QK
    nkv = nh * (MLA_NOPE + MLA_V)
    in_specs = [pl.BlockSpec((1, tl, MLA_Q_LORA), lambda bi, i: (bi, i, P_CQ // MLA_Q_LORA)),
                pl.BlockSpec((1, tl, MLA_KV_LORA), lambda bi, i: (bi, i, P_CKV // MLA_KV_LORA)),
                pl.BlockSpec((1, tl, N_SMALL), lambda bi, i: (bi, i, 0)),
                const((1, MLA_Q_LORA)), const((1, MLA_KV_LORA)), const((MLA_Q_LORA, nq)), const((MLA_KV_LORA, nkv)),
                const((1, nh * MLA_NOPE)), const((1, nh * MLA_ROPE)), const((1, nh * MLA_NOPE)), const((1, 128)),
                const(en.shape), const(er.shape), const(en.T.shape), const(er.T.shape)]
    operands = [proj, proj, small, p['gqa'], p['gkva'], p['wq'], p['wkv'], p['gqn'], p['gqr'], p['gkn'], p['gkr'],
                en_b, er_b, ent_b, ert_b]
    use_rope = rope_tables is not None
    if use_rope:
        in_specs += [pl.BlockSpec((tl, nh * MLA_ROPE), lambda bi, i: (i, 0)),
                     pl.BlockSpec((tl, nh * MLA_ROPE), lambda bi, i: (i, 0)),
                     pl.BlockSpec((tl, 128), lambda bi, i: (i, 0)),
                     pl.BlockSpec((tl, 128), lambda bi, i: (i, 0))]
        operands += list(rope_tables)
    return pl.pallas_call(
        functools.partial(_mla_prep_kernel, use_rope=use_rope),
        grid=(b, l // tl),
        in_specs=in_specs,
        out_specs=[pl.BlockSpec((1, nh, QK_PAD, tl), lambda bi, i: (bi, 0, 0, i)),
                   pl.BlockSpec((1, nh, tl, QK_PAD), lambda bi, i: (bi, 0, i, 0)),
                   pl.BlockSpec((1, nh, MLA_V, tl), lambda bi, i: (bi, 0, 0, i))],
        out_shape=[jax.ShapeDtypeStruct((b, nh, QK_PAD, l), BF16),
                   jax.ShapeDtypeStruct((b, nh, l, QK_PAD), BF16),
                   jax.ShapeDtypeStruct((b, nh, MLA_V, l), BF16)],
        compiler_params=_cparams(("parallel", "parallel")),
        name="mla_prep",
    )(*operands)


def _combine_out_kernel(a_ref, y_ref, hf_ref, hb_ref, o_ref, g_ref, w_ref, r_ref, gt_ref, out_ref, cat_ref):
    @pl.when(pl.program_id(2) == 0)
    def _():
        g = g_ref[...]
        cat_ref[:, :MLA_WIDTH] = (_rms(a_ref[0]) * g[:, :MLA_WIDTH]).astype(BF16)
        hy0 = MLA_WIDTH
        cat_ref[:, hy0:hy0 + HY_WIDTH] = (_rms(y_ref[0]) * g[:, hy0:hy0 + HY_WIDTH]).astype(BF16)
        ml0 = MLA_WIDTH + HY_WIDTH
        hh = hf_ref[0] + hb_ref[0]
        og = jax.nn.sigmoid(o_ref[0].astype(F32))
        for hd in range(ML_HEADS):
            sl = slice(hd * ML_HEAD_DIM, (hd + 1) * ML_HEAD_DIM)
            gl = g[:, ml0 + hd * ML_HEAD_DIM:ml0 + (hd + 1) * ML_HEAD_DIM]
            cat_ref[:, ml0 + hd * ML_HEAD_DIM:ml0 + (hd + 1) * ML_HEAD_DIM] = (
                _rms(hh[:, sl]) * gl * og[:, sl]).astype(BF16)

    y = jnp.dot(cat_ref[...], w_ref[...], preferred_element_type=F32)
    out_ref[0] = r_ref[0] + gt_ref[0] * y


def combine_out_proj(a, y, h2, proj, g_mix, w_out, res, gate, *, tm, tn, name="out_proj"):
    b, s, d = res.shape
    return pl.pallas_call(
        _combine_out_kernel,
        grid=(b, s // tm, d // tn),
        in_specs=[pl.BlockSpec((1, tm, MLA_WIDTH), lambda bi, i, j: (bi, i, 0)),
                  pl.BlockSpec((1, tm, HY_WIDTH), lambda bi, i, j: (bi, i, 0)),
                  pl.BlockSpec((1, tm, ML_WIDTH), lambda bi, i, j: (bi, i, 0)),
                  pl.BlockSpec((1, tm, ML_WIDTH), lambda bi, i, j: (bi, i, 0)),
                  pl.BlockSpec((1, tm, ML_WIDTH), lambda bi, i, j: (bi, i, P_MLO // ML_WIDTH)),
                  pl.BlockSpec((1, MIX_WIDTH), lambda bi, i, j: (0, 0)),
                  pl.BlockSpec((MIX_WIDTH, tn), lambda bi, i, j: (0, j)),
                  pl.BlockSpec((1, tm, tn), lambda bi, i, j: (bi, i, j)),
                  pl.BlockSpec((1, 1, tn), lambda bi, i, j: (bi, 0, j))],
        out_specs=pl.BlockSpec((1, tm, tn), lambda bi, i, j: (bi, i, j)),
        out_shape=jax.ShapeDtypeStruct((b, s, d), F32),
        scratch_shapes=[pltpu.VMEM((tm, MIX_WIDTH), BF16)],
        compiler_params=_cparams(("parallel", "parallel", "arbitrary")),
        name=name,
    )(a, y, h2[0], h2[1], proj, g_mix.reshape(1, MIX_WIDTH), w_out, res, gate)


@functools.lru_cache(maxsize=None)
def _direct_dft_tables(l):
    n = 2 * l
    k = np.arange(n)[:, None]
    t = np.arange(l)[None, :]
    ang = 2.0 * np.pi * ((k * t) % n) / n
    c, s = np.cos(ang), np.sin(ang)
    gf = np.block([[c, s], [-s, c]])
    gi = gf.T.copy()
    t_full = np.arange(n)[None, :]
    ang = 2.0 * np.pi * ((k * t_full) % n) / n
    g_real = np.concatenate([np.cos(ang), -np.sin(ang)], axis=0)
    return gf.astype(np.float32), gi.astype(np.float32), g_real.astype(np.float32)


def _direct_conv_kernel(gf_ref, gi_ref, z_ref, gate_ref, h_ref, skip_ref, o_ref, *, scale):
    z = z_ref[0]
    x = jnp.dot(gf_ref[...].astype(BF16), z.astype(BF16), preferred_element_type=F32)
    n = x.shape[0] // 2
    xr, xi = x[:n], x[n:]
    hr, hi = h_ref[:n] * scale, h_ref[n:] * scale
    prod = jnp.concatenate([xr * hr - xi * hi, xr * hi + xi * hr], axis=0).astype(BF16)
    conv = jnp.dot(gi_ref[...].astype(BF16), prod, preferred_element_type=F32)
    o_ref[0] = gate_ref[0] * (conv + skip_ref[...] * z)


def hyena_short(proj, p):
    bsz, l, _ = proj.shape
    c = HY_WIDTH
    npair = bsz // 2
    n = 2 * l
    gf, gi, g_real = _direct_dft_tables(l)
    buf = hyena_filter_buffer(l, p, tl=l)
    hspec = dft_first(jnp.asarray(g_real), buf.reshape(1, n, HY_ORDER * c), n_out=1, tn=HY_ORDER * c,
                      precise=True)[0]
    vx = short_conv_groups(proj, P_HY, p['hy_conv_w'], p['hy_conv_b'], group=c, tc=128, silu=False,
                           name="hyena_short_conv_ctx")
    vx = vx.reshape(3 * npair, 2 * l, c)
    z = vx

    def rows_at(off):
        return lambda q: (off + q, 0, 0)

    def cols_at(col):
        return lambda q: (0, col)

    for order in range(HY_ORDER):
        z = pl.pallas_call(
            functools.partial(_direct_conv_kernel, scale=1.0 / n),
            grid=(npair,),
            in_specs=[pl.BlockSpec(gf.shape, lambda q: (0, 0)),
                      pl.BlockSpec(gi.shape, lambda q: (0, 0)),
                      pl.BlockSpec((1, 2 * l, c), rows_at(0)),
                      pl.BlockSpec((1, 2 * l, c), rows_at((order + 1) * npair)),
                      pl.BlockSpec((2 * n, c), cols_at(order)),
                      pl.BlockSpec((1, c), lambda q: (0, 0))],
            out_specs=pl.BlockSpec((1, 2 * l, c), lambda q: (q, 0, 0)),
            out_shape=jax.ShapeDtypeStruct((npair, 2 * l, c), F32),
            compiler_params=_cparams(("parallel",)),
            name="hyena_direct_conv",
        )(jnp.asarray(gf), jnp.asarray(gi), z, vx, hspec, p['hy_skip'][order].reshape(1, c))
    return z.reshape(bsz, l, c)


def _split_cols(w, sizes):
    out, start = [], 0
    for s in sizes:
        out.append(w[:, start:start + s])
        start += s
    return out


def _prep_layer_weights(l, w_in, mla_qa_norm, mla_kva_norm, mla_w_uq, mla_w_ukv, mla_q_norm, mla_k_norm):
    cq, ckv, kr, hy, mqk, mv, mo, gt = _split_cols(w_in[l], IN_SIZES)
    w_main = jnp.concatenate([cq, ckv, hy, mqk, mv, mo], axis=1).astype(BF16)
    pad = jnp.zeros((D_MODEL, N_SMALL - MLA_ROPE - ML_GATES), F32)
    w_small = jnp.concatenate([kr, gt, pad], axis=1).astype(BF16)
    nh = MLA_HEADS
    wq = mla_w_uq[l].reshape(MLA_Q_LORA, nh, MLA_QK)
    wq = jnp.concatenate([wq[:, :, :MLA_NOPE].reshape(MLA_Q_LORA, -1), wq[:, :, MLA_NOPE:].reshape(MLA_Q_LORA, -1)],
                         axis=1).astype(BF16)
    wkv = mla_w_ukv[l].reshape(MLA_KV_LORA, nh, MLA_NOPE + MLA_V)
    wkv = jnp.concatenate([wkv[:, :, :MLA_NOPE].reshape(MLA_KV_LORA, -1),
                           wkv[:, :, MLA_NOPE:].reshape(MLA_KV_LORA, -1)], axis=1).astype(BF16)
    q_scale = (MLA_QK ** -0.5) * math.log2(math.e)
    gq, gk = mla_q_norm[l] * q_scale, mla_k_norm[l]
    mla = {'gqa': mla_qa_norm[l].reshape(1, -1), 'gkva': mla_kva_norm[l].reshape(1, -1), 'wq': wq, 'wkv': wkv,
           'gqn': jnp.tile(gq[:MLA_NOPE], nh).reshape(1, -1), 'gqr': jnp.tile(gq[MLA_NOPE:], nh).reshape(1, -1),
           'gkn': jnp.tile(gk[:MLA_NOPE], nh).reshape(1, -1),
           'gkr': jnp.concatenate([gk[MLA_NOPE:], jnp.zeros((128 - MLA_ROPE,), F32)]).reshape(1, -1)}
    return w_main, w_small, mla


def kernel(x, c, ctx, c_ctx, ada_w, ada_b, norm1_g, norm2_g, w_in, mla_qa_norm, mla_kva_norm, mla_w_uq,
           mla_w_ukv, mla_q_norm, mla_k_norm, hy_conv_w, hy_conv_b, hy_w1, hy_b1, hy_w2, hy_b2, hy_w3,
           hy_decay, hy_skip, ml_conv_w, ml_conv_b, ml_gate_b, mix_norm_g, w_out, ffn_w1, ffn_w2):
    B, S, D = x.shape
    LC = ctx.shape[1]
    rope_tables = _rope_tables(S)

    silu_rows = jnp.concatenate([jax.nn.silu(c), jax.nn.silu(c_ctx)[None], jnp.zeros((8 - B - 1, D), F32)], axis=0)
    mods = ada_modulation(silu_rows.astype(BF16), ada_w, ada_b)

    for l in range(DEPTH):
        need_ctx = l < DEPTH - 1
        p = {'hy_conv_w': hy_conv_w[l], 'hy_conv_b': hy_conv_b[l], 'hy_w1': hy_w1[l], 'hy_b1': hy_b1[l],
             'hy_w2': hy_w2[l], 'hy_b2': hy_b2[l], 'hy_w3': hy_w3[l], 'hy_decay': hy_decay[l],
             'hy_skip': hy_skip[l], 'ml_conv_w': ml_conv_w[l], 'ml_conv_b': ml_conv_b[l]}
        w_main, w_small, mla = _prep_layer_weights(l, w_in, mla_qa_norm, mla_kva_norm, mla_w_uq, mla_w_ukv,
                                                   mla_q_norm, mla_k_norm)
        w_out_l = w_out[l].astype(BF16)
        w1_l = ffn_w1[l].astype(BF16)
        w2_l = ffn_w2[l].astype(BF16)

        mod_l = [m[:, None, :] for m in jnp.split(mods[l, :B], 6, axis=-1)]
        mod_c = [jnp.broadcast_to(m[:, None, :], (B, 1, D)) for m in jnp.split(mods[l, B:B + 1], 6, axis=-1)]

        proj_l, small_l = in_proj(x, norm1_g[l], mod_l[1], mod_l[0], w_main, w_small, tm=1024, tn=IN_PROJ_TN)
        proj_c, small_c = in_proj(ctx, norm1_g[l], mod_c[1], mod_c[0], w_main, w_small, tm=LC, tn=IN_PROJ_TN,
                                  name="in_proj_ctx")

        qt_l, k_l, vt_l = mla_prep(proj_l, small_l, mla, rope_tables, tl=512)
        qt_c, k_c, vt_c = mla_prep(proj_c, small_c, mla, None, tl=LC)
        a_l = attention(qt_l, [(k_c, vt_c), (k_l, vt_l)], tq=ATT_TQ)
        y_l = hyena_long(proj_l, p)

        def mlstm_in(proj, small):
            qk = short_conv_groups(proj, P_MLQK, p['ml_conv_w'], p['ml_conv_b'], group=ML_WIDTH, tc=128,
                                   silu=True, name="mlstm_short_conv")
            return qk, small[..., S_GATES:S_GATES + ML_GATES]

        qk_c, gates_c = mlstm_in(proj_c, small_c)
        qk_l, gates_l = mlstm_in(proj_l, small_l)
        h_c, h_l = mlstm_bidir(qk_c, proj_c, gates_c, qk_l, proj_l, gates_l, ml_gate_b[l])
        x = combine_out_proj(a_l, y_l, h_l, proj_l, mix_norm_g[l], w_out_l, x, mod_l[2], tm=1024, tn=MM_TN)
        hid = norm_swiglu(x, norm2_g[l], mod_l[4], mod_l[3], w1_l, tm=1024, tn=MM_TN, name="ffn_up")
        x = mm_residual(hid, w2_l, x, mod_l[5], tm=1024, tn=MM_TN, name="ffn_down")

        if need_ctx:
            a_c = attention(qt_c, [(k_c, vt_c)], tq=LC)
            y_c = hyena_short(proj_c, p)
            ctx = combine_out_proj(a_c, y_c, h_c, proj_c, mix_norm_g[l], w_out_l, ctx, mod_c[2], tm=LC, tn=MM_TN,
                                   name="out_proj_ctx")
            hid = norm_swiglu(ctx, norm2_g[l], mod_c[4], mod_c[3], w1_l, tm=LC, tn=MM_TN, name="ffn_up_ctx")
            ctx = mm_residual(hid, w2_l, ctx, mod_c[5], tm=LC, tn=MM_TN, name="ffn_down_ctx")
    return x
```

```python
import functools
import math

import jax
import jax.numpy as jnp
import numpy as np
from jax import lax
from jax.experimental import pallas as pl
from jax.experimental.pallas import tpu as pltpu

D_MODEL = 2048
DEPTH = 2
GRID_W = 64
EPS = 1e-6

MLA_HEADS = 8
MLA_NOPE = 128
MLA_ROPE = 64
MLA_QK = MLA_NOPE + MLA_ROPE
MLA_V = 128
MLA_Q_LORA = 512
MLA_KV_LORA = 512
MLA_WIDTH = MLA_HEADS * MLA_V
ROPE_BASE = 10000.0
ROPE_PAIRS_PER_AXIS = MLA_ROPE // 4
QK_PAD = 256

HY_WIDTH = 512
HY_ORDER = 2
HY_BANDS = 16
HY_SHIFT = 0.05
SHORT_CONV = 3

ML_HEADS = 4
ML_HEAD_DIM = 128
ML_WIDTH = ML_HEADS * ML_HEAD_DIM
ML_CHUNK = 128
ML_GATES = 4 * ML_HEADS

MIX_WIDTH = MLA_WIDTH + HY_WIDTH + ML_WIDTH
FFN_HIDDEN = ((8 * D_MODEL // 3 + 255) // 256) * 256

IN_SIZES = (MLA_Q_LORA, MLA_KV_LORA, MLA_ROPE, 3 * HY_WIDTH, 2 * ML_WIDTH, ML_WIDTH, ML_WIDTH, ML_GATES)
N_IN = sum(IN_SIZES)

P_CQ, P_CKV, P_HY, P_MLQK, P_MLV, P_MLO = 0, 512, 1024, 2560, 3584, 4096
N_MAIN = 4608
N_SMALL = 128
S_GATES = MLA_ROPE

VMEM_LIMIT_BYTES = 56 * 1024 * 1024

BF16 = jnp.bfloat16
F32 = jnp.float32


def _cparams(sem):
    return pltpu.CompilerParams(dimension_semantics=sem, vmem_limit_bytes=VMEM_LIMIT_BYTES)


def _ada_kernel(s_ref, w_ref, b_ref, o_ref):
    w = w_ref[0].astype(BF16)
    o_ref[0] = jnp.dot(s_ref[...], w, preferred_element_type=F32) + b_ref[0]


def ada_modulation(silu_rows, ada_w, ada_b, tn=1024):
    depth, d, n = ada_w.shape
    return pl.pallas_call(
        _ada_kernel,
        grid=(depth, n // tn),
        in_specs=[pl.BlockSpec((8, d), lambda l, j: (0, 0)),
                  pl.BlockSpec((1, d, tn), lambda l, j: (l, 0, j)),
                  pl.BlockSpec((1, 1, tn), lambda l, j: (l, 0, j))],
        out_specs=pl.BlockSpec((1, 8, tn), lambda l, j: (l, 0, j)),
        out_shape=jax.ShapeDtypeStruct((depth, 8, n), F32),
        compiler_params=_cparams(("parallel", "parallel")),
        name="ada_modulation",
    )(silu_rows, ada_w, ada_b.reshape(depth, 1, n))


MM_TN = 512
IN_PROJ_TN = 1536


def _normed(x, g, sc, sh):
    y = x * lax.rsqrt(jnp.mean(x * x, axis=-1, keepdims=True) + EPS) * g
    return y * (1.0 + sc) + sh


def _in_proj_kernel(x_ref, g_ref, sc_ref, sh_ref, w_ref, ws_ref, o_ref, os_ref, xn_ref):
    @pl.when(pl.program_id(2) == 0)
    def _():
        xn = _normed(x_ref[0], g_ref[...], sc_ref[0], sh_ref[0]).astype(BF16)
        xn_ref[...] = xn
        os_ref[0] = jnp.dot(xn, ws_ref[...], preferred_element_type=F32)

    o_ref[0] = jnp.dot(xn_ref[...], w_ref[...], preferred_element_type=F32).astype(o_ref.dtype)


def in_proj(x, g, sc, sh, w_main, w_small, *, tm, tn, name="in_proj"):
    b, s, k = x.shape
    n = w_main.shape[1]
    ns = w_small.shape[1]
    return pl.pallas_call(
        _in_proj_kernel,
        grid=(b, s // tm, n // tn),
        in_specs=[pl.BlockSpec((1, tm, k), lambda bi, i, j: (bi, i, 0)),
                  pl.BlockSpec((1, k), lambda bi, i, j: (0, 0)),
                  pl.BlockSpec((1, 1, k), lambda bi, i, j: (bi, 0, 0)),
                  pl.BlockSpec((1, 1, k), lambda bi, i, j: (bi, 0, 0)),
                  pl.BlockSpec((k, tn), lambda bi, i, j: (0, j)),
                  pl.BlockSpec((k, ns), lambda bi, i, j: (0, 0))],
        out_specs=[pl.BlockSpec((1, tm, tn), lambda bi, i, j: (bi, i, j)),
                   pl.BlockSpec((1, tm, ns), lambda bi, i, j: (bi, i, 0))],
        out_shape=[jax.ShapeDtypeStruct((b, s, n), BF16), jax.ShapeDtypeStruct((b, s, ns), F32)],
        scratch_shapes=[pltpu.VMEM((tm, k), BF16)],
        compiler_params=_cparams(("parallel", "parallel", "arbitrary")),
        name=name,
    )(x, g.reshape(1, k), sc, sh, w_main, w_small)


def _norm_swiglu_kernel(x_ref, g_ref, sc_ref, sh_ref, wg_ref, wu_ref, o_ref, xn_ref):
    @pl.when(pl.program_id(2) == 0)
    def _():
        xn_ref[...] = _normed(x_ref[0], g_ref[...], sc_ref[0], sh_ref[0]).astype(BF16)

    xn = xn_ref[...]
    gate = jnp.dot(xn, wg_ref[...], preferred_element_type=F32)
    up = jnp.dot(xn, wu_ref[...], preferred_element_type=F32)
    o_ref[0] = (gate * (0.5 * jnp.tanh(0.5 * gate) + 0.5) * up).astype(o_ref.dtype)


def norm_swiglu(x, g, sc, sh, w1, *, tm, tn, name="norm_swiglu"):
    b, s, d = x.shape
    h = w1.shape[1] // 2
    nj = h // tn
    return pl.pallas_call(
        _norm_swiglu_kernel,
        grid=(b, s // tm, nj),
        in_specs=[pl.BlockSpec((1, tm, d), lambda bi, i, j: (bi, i, 0)),
                  pl.BlockSpec((1, d), lambda bi, i, j: (0, 0)),
                  pl.BlockSpec((1, 1, d), lambda bi, i, j: (bi, 0, 0)),
                  pl.BlockSpec((1, 1, d), lambda bi, i, j: (bi, 0, 0)),
                  pl.BlockSpec((d, tn), lambda bi, i, j: (0, j)),
                  pl.BlockSpec((d, tn), lambda bi, i, j: (0, j + nj))],
        out_specs=pl.BlockSpec((1, tm, tn), lambda bi, i, j: (bi, i, j)),
        out_shape=jax.ShapeDtypeStruct((b, s, h), BF16),
        scratch_shapes=[pltpu.VMEM((tm, d), BF16)],
        compiler_params=_cparams(("parallel", "parallel", "arbitrary")),
        name=name,
    )(x, g.reshape(1, d), sc, sh, w1, w1)


def _mm_res_kernel(a_ref, w_ref, r_ref, gt_ref, o_ref):
    y = jnp.dot(a_ref[0], w_ref[...], preferred_element_type=F32)
    o_ref[0] = r_ref[0] + gt_ref[0] * y


def mm_residual(a, w, res, gate, *, tm, tn, name="mm_residual"):
    b, s, k = a.shape
    n = w.shape[1]
    return pl.pallas_call(
        _mm_res_kernel,
        grid=(b, s // tm, n // tn),
        in_specs=[pl.BlockSpec((1, tm, k), lambda bi, i, j: (bi, i, 0)),
                  pl.BlockSpec((k, tn), lambda bi, i, j: (0, j)),
                  pl.BlockSpec((1, tm, tn), lambda bi, i, j: (bi, i, j)),
                  pl.BlockSpec((1, 1, tn), lambda bi, i, j: (bi, 0, j))],
        out_specs=pl.BlockSpec((1, tm, tn), lambda bi, i, j: (bi, i, j)),
        out_shape=jax.ShapeDtypeStruct((b, s, n), F32),
        compiler_params=_cparams(("parallel", "parallel", "parallel")),
        name=name,
    )(a, w, res, gate)


ATT_KEY_CHUNK = 512
ATT_TQ = 1024


def _attn_kernel(qt_ref, *refs, n_seg):
    o_ref, s_ref = refs[2 * n_seg], refs[2 * n_seg + 1]
    qt = qt_ref[0, 0]
    tq = qt.shape[1]
    chunks, off = [], 0
    for sgi in range(n_seg):
        k_ref, vt_ref = refs[2 * sgi], refs[2 * sgi + 1]
        lk = k_ref.shape[2]
        ch = min(ATT_KEY_CHUNK, lk)
        for c in range(lk // ch):
            chunks.append((k_ref, vt_ref, c * ch, ch, off))
            off += ch

    m8 = jnp.full((8, tq), -jnp.inf, F32)
    for k_ref, _, start, ch, off in chunks:
        s = jnp.dot(k_ref[0, 0, start:start + ch, :], qt, preferred_element_type=F32)
        s_ref[off:off + ch, :] = s
        m8 = jnp.maximum(m8, jnp.max(s.reshape(ch // 8, 8, tq), axis=0))
    m = jnp.max(m8, axis=0, keepdims=True)
    l8 = jnp.zeros((8, tq), F32)
    acc = jnp.zeros((vt_ref.shape[2], tq), F32)
    for _, vt_ref, start, ch, off in chunks:
        p = jnp.exp2(s_ref[off:off + ch, :] - m)
        l8 = l8 + jnp.sum(p.reshape(ch // 8, 8, tq), axis=0)
        acc = acc + jnp.dot(vt_ref[0, 0, :, start:start + ch], p.astype(BF16), preferred_element_type=F32)
    l = jnp.sum(l8, axis=0, keepdims=True)
    o_ref[0] = (acc / l).T


def attention(qt, segments, *, tq=256):
    b, h, dk, lq = qt.shape
    dv = segments[0][1].shape[2]
    in_specs = [pl.BlockSpec((1, 1, dk, tq), lambda bi, hi, i: (bi, hi, 0, i))]
    operands = [qt]
    for k, vt in segments:
        lk = k.shape[2]
        in_specs += [pl.BlockSpec((1, 1, lk, dk), lambda bi, hi, i: (bi, hi, 0, 0)),
                     pl.BlockSpec((1, 1, dv, lk), lambda bi, hi, i: (bi, hi, 0, 0))]
        operands += [k, vt]
    return pl.pallas_call(
        functools.partial(_attn_kernel, n_seg=len(segments)),
        grid=(b, h, lq // tq),
        in_specs=in_specs,
        out_specs=pl.BlockSpec((1, tq, dv), lambda bi, hi, i: (bi, i, hi)),
        out_shape=jax.ShapeDtypeStruct((b, lq, h * dv), F32),
        scratch_shapes=[pltpu.VMEM((sum(k.shape[2] for k, _ in segments), tq), F32)],
        compiler_params=_cparams(("parallel", "parallel", "parallel")),
        name="attention",
    )(*operands)


HY_NA = 64
HY_NB = 128
HY_N = HY_NA * HY_NB
HY_KG = 8


@functools.lru_cache(maxsize=None)
def _dft_tables():
    na, nb, n = HY_NA, HY_NB, HY_N
    a = np.arange(na // 2)[None, :]
    ka = np.arange(na)[:, None]
    ang = 2.0 * np.pi * ((a * ka) % na) / na
    c, s = np.cos(ang), np.sin(ang)
    g1 = np.block([[c, s], [-s, c]])
    g1_inv = g1.T.copy()
    a_full = np.arange(na)[None, :]
    ang = 2.0 * np.pi * ((a_full * ka) % na) / na
    g1_real = np.concatenate([np.cos(ang), -np.sin(ang)], axis=0)
    b = np.arange(nb)[None, None, :]
    k = np.arange(na)[:, None, None] + na * np.arange(nb)[None, :, None]
    ang = 2.0 * np.pi * ((b * k) % n) / n
    cr, si = np.cos(ang), np.sin(ang)
    g2 = np.concatenate([np.concatenate([cr, si], axis=2), np.concatenate([-si, cr], axis=2)], axis=1)
    g2_inv = np.transpose(g2, (0, 2, 1)).copy()
    return (g1.astype(np.float32), g1_inv.astype(np.float32), g1_real.astype(np.float32),
            g2.astype(np.float32), g2_inv.astype(np.float32))


def _dot_3pass(a, b):
    a_hi, b_hi = a.astype(BF16), b.astype(BF16)
    a_lo = (a - a_hi.astype(F32)).astype(BF16)
    b_lo = (b - b_hi.astype(F32)).astype(BF16)
    dot = functools.partial(jnp.dot, preferred_element_type=F32)
    return dot(a_hi, b_hi) + (dot(a_hi, b_lo) + dot(a_lo, b_hi))


def _short_conv_kernel(u_ref, w_ref, b_ref, o_ref, *, silu):
    u = u_ref[0].astype(F32)
    n = u.shape[0]
    rows = lax.broadcasted_iota(jnp.int32, u.shape, 0)
    prev = jnp.where(rows == 0, 0.0, pltpu.roll(u, 1, axis=0))
    nxt = jnp.where(rows == n - 1, 0.0, pltpu.roll(u, n - 1, axis=0))
    y = b_ref[...] + prev * w_ref[0:1, :] + u * w_ref[1:2, :] + nxt * w_ref[2:3, :]
    if silu:
        y = y * jax.nn.sigmoid(y)
    o_ref[0, 0] = y


def short_conv_groups(proj, col0, w, b, *, group, tc, silu, name):
    bsz, l, _ = proj.shape
    width = w.shape[1]
    per = group // tc
    return pl.pallas_call(
        functools.partial(_short_conv_kernel, silu=silu),
        grid=(bsz, width // tc),
        in_specs=[pl.BlockSpec((1, l, tc), lambda bi, j: (bi, 0, col0 // tc + j)),
                  pl.BlockSpec((SHORT_CONV, tc), lambda bi, j: (0, j)),
                  pl.BlockSpec((1, tc), lambda bi, j: (0, j))],
        out_specs=pl.BlockSpec((1, 1, l, tc), lambda bi, j: (j // per, bi, 0, j % per)),
        out_shape=jax.ShapeDtypeStruct((width // group, bsz, l, group), F32),
        compiler_params=_cparams(("parallel", "parallel")),
        name=name,
    )(proj, w, b.reshape(1, width))


def _hy_filter_kernel(f_ref, w1_ref, b1_ref, w2_ref, b2_ref, w3_ref, dec_ref, o_ref, *, zero_row):
    hp = lax.Precision.HIGHEST
    f = f_ref[...]
    tl = f.shape[0]
    h = jnp.sin(jnp.dot(f, w1_ref[...], precision=hp, preferred_element_type=F32) + b1_ref[...])
    h = jnp.sin(jnp.dot(h, w2_ref[...], precision=hp, preferred_element_type=F32) + b2_ref[...])
    h = _dot_3pass(h, w3_ref[0])
    h = h * (jnp.exp(-f[:, 0:1] * dec_ref[0]) + HY_SHIFT)
    rows = lax.broadcasted_iota(jnp.int32, h.shape, 0) + pl.program_id(0) * tl
    o_ref[...] = jnp.where(rows == zero_row, 0.0, h)


def hyena_filter_buffer(L, p, tl=512):
    n = 2 * L
    r = jnp.arange(n, dtype=F32)
    t = jnp.where(r < L, r, n - r) / L
    bands = jnp.arange(1, HY_BANDS + 1, dtype=F32)
    ang = 2.0 * math.pi * t[:, None] * bands
    emb = 1 + 2 * HY_BANDS
    feats = jnp.concatenate([t[:, None], jnp.cos(ang), jnp.sin(ang), jnp.zeros((n, 128 - emb), F32)], axis=-1)
    hid = p['hy_w1'].shape[1]
    w1 = jnp.concatenate([p['hy_w1'], jnp.zeros((128 - emb, hid), F32)], axis=0)
    oc = HY_ORDER * HY_WIDTH
    w3 = p['hy_w3'].reshape(hid, HY_ORDER, 2, HY_WIDTH).transpose(2, 0, 1, 3).reshape(2, hid, oc)
    dec = p['hy_decay'].reshape(HY_ORDER, 2, HY_WIDTH).transpose(1, 0, 2).reshape(2, 1, oc)
    half = L // tl
    return pl.pallas_call(
        functools.partial(_hy_filter_kernel, zero_row=L),
        grid=(n // tl,),
        in_specs=[pl.BlockSpec((tl, 128), lambda i: (i, 0)),
                  pl.BlockSpec((128, hid), lambda i: (0, 0)),
                  pl.BlockSpec((1, hid), lambda i: (0, 0)),
                  pl.BlockSpec((hid, hid), lambda i: (0, 0)),
                  pl.BlockSpec((1, hid), lambda i: (0, 0)),
                  pl.BlockSpec((1, hid, oc), lambda i: (i // half, 0, 0)),
                  pl.BlockSpec((1, 1, oc), lambda i: (i // half, 0, 0))],
        out_specs=pl.BlockSpec((tl, oc), lambda i: (i, 0)),
        out_shape=jax.ShapeDtypeStruct((n, oc), F32),
        compiler_params=_cparams(("parallel",)),
        name="hyena_filter",
    )(feats, w1, p['hy_b1'].reshape(1, hid), p['hy_w2'], p['hy_b2'].reshape(1, hid), w3, dec)


def _dft_first_kernel(g_ref, x_ref, o_ref, *, precise):
    if precise:
        o_ref[0] = _dot_3pass(g_ref[...], x_ref[0])
    else:
        y = jnp.dot(g_ref[...].astype(BF16), x_ref[0].astype(BF16), preferred_element_type=F32)
        o_ref[0] = y.astype(o_ref.dtype)


def dft_first(g, x, *, n_out, offset=0, tn=8192, precise=False):
    _, r, lanes = x.shape
    m = g.shape[0]
    return pl.pallas_call(
        functools.partial(_dft_first_kernel, precise=precise),
        grid=(n_out, lanes // tn),
        in_specs=[pl.BlockSpec((m, r), lambda p, j: (0, 0)),
                  pl.BlockSpec((1, r, tn), lambda p, j: (offset + p, 0, j))],
        out_specs=pl.BlockSpec((1, m, tn), lambda p, j: (p, 0, j)),
        out_shape=jax.ShapeDtypeStruct((n_out, m, lanes), F32 if precise else BF16),
        compiler_params=_cparams(("parallel", "parallel")),
        name="dft_first",
    )(g, x)


def _dft_filter_mid_kernel(g_ref, y_ref, o_ref, *, scale):
    for i in range(g_ref.shape[0]):
        y = jnp.concatenate([y_ref[0, i], y_ref[1, i]], axis=0)
        o_ref[i] = scale * _dot_3pass(g_ref[i], y)


def dft_filter_mid(g2, y1):
    _, na, nb, oc = y1.shape
    kg = HY_KG // 2
    return pl.pallas_call(
        functools.partial(_dft_filter_mid_kernel, scale=1.0 / HY_N),
        grid=(na // kg,),
        in_specs=[pl.BlockSpec((kg, 2 * nb, 2 * nb), lambda g: (g, 0, 0)),
                  pl.BlockSpec((2, kg, nb, oc), lambda g: (0, g, 0, 0))],
        out_specs=pl.BlockSpec((kg, 2 * nb, oc), lambda g: (g, 0, 0)),
        out_shape=jax.ShapeDtypeStruct((na, 2 * nb, oc), F32),
        compiler_params=_cparams(("parallel",)),
        name="dft_filter_mid",
    )(g2, y1)


def _dft_mid_kernel(gf_ref, gi_ref, y_ref, h_ref, o_ref):
    nb = y_ref.shape[3]
    for i in range(gf_ref.shape[0]):
        y = jnp.concatenate([y_ref[0, 0, i], y_ref[0, 1, i]], axis=0)
        x = jnp.dot(gf_ref[i].astype(BF16), y, preferred_element_type=F32)
        xr, xi = x[:nb], x[nb:]
        hr, hi = h_ref[i, :nb], h_ref[i, nb:]
        prod = jnp.concatenate([xr * hr - xi * hi, xr * hi + xi * hr], axis=0).astype(BF16)
        u = jnp.dot(gi_ref[i].astype(BF16), prod, preferred_element_type=F32)
        o_ref[0, 0, i] = u[:nb].astype(o_ref.dtype)
        o_ref[0, 1, i] = u[nb:].astype(o_ref.dtype)


def dft_mid(g2, g2_inv, y1, h, order):
    npair, _, na, nb, c = y1.shape
    kg = HY_KG
    return pl.pallas_call(
        _dft_mid_kernel,
        grid=(npair, na // kg),
        in_specs=[pl.BlockSpec((kg, 2 * nb, 2 * nb), lambda p, g: (g, 0, 0)),
                  pl.BlockSpec((kg, 2 * nb, 2 * nb), lambda p, g: (g, 0, 0)),
                  pl.BlockSpec((1, 2, kg, nb, c), lambda p, g: (p, 0, g, 0, 0)),
                  pl.BlockSpec((kg, 2 * nb, c), lambda p, g: (g, 0, order))],
        out_specs=pl.BlockSpec((1, 2, kg, nb, c), lambda p, g: (p, 0, g, 0, 0)),
        out_shape=jax.ShapeDtypeStruct(y1.shape, BF16),
        compiler_params=_cparams(("parallel", "parallel")),
        name="dft_mid",
    )(g2, g2_inv, y1, h)


def _dft_last_kernel(g_ref, u_ref, z_ref, gate_ref, skip_ref, o_ref):
    conv = jnp.dot(g_ref[...].astype(BF16), u_ref[0], preferred_element_type=F32)
    o_ref[0] = gate_ref[0] * (conv + skip_ref[...] * z_ref[0])


def dft_last(g1_inv, u, z, z_off, gate, gate_off, skip_lanes, *, tn=8192):
    npair, m2, lanes = u.shape
    r = g1_inv.shape[0]
    return pl.pallas_call(
        _dft_last_kernel,
        grid=(npair, lanes // tn),
        in_specs=[pl.BlockSpec((r, m2), lambda p, j: (0, 0)),
                  pl.BlockSpec((1, m2, tn), lambda p, j: (p, 0, j)),
                  pl.BlockSpec((1, r, tn), lambda p, j: (z_off + p, 0, j)),
                  pl.BlockSpec((1, r, tn), lambda p, j: (gate_off + p, 0, j)),
                  pl.BlockSpec((1, tn), lambda p, j: (0, 0))],
        out_specs=pl.BlockSpec((1, r, tn), lambda p, j: (p, 0, j)),
        out_shape=jax.ShapeDtypeStruct((npair, r, lanes), F32),
        compiler_params=_cparams(("parallel", "parallel")),
        name="dft_last",
    )(g1_inv, u, z, gate, skip_lanes)


def hyena_long(proj, p):
    bsz, l, _ = proj.shape
    assert 2 * l == HY_N and bsz % 2 == 0
    c = HY_WIDTH
    npair = bsz // 2
    lanes = HY_NB * c
    g1, g1_inv, g1_real, g2, g2_inv = _dft_tables()
    g1_bf, g1_inv_bf = jnp.asarray(g1), jnp.asarray(g1_inv)
    g2_bf, g2_inv_bf = jnp.asarray(g2), jnp.asarray(g2_inv)

    buf = hyena_filter_buffer(l, p)
    y1h = dft_first(jnp.asarray(g1_real), buf.reshape(1, HY_NA, HY_NB * HY_ORDER * c), n_out=1, precise=True)
    hspec = dft_filter_mid(jnp.asarray(g2), y1h.reshape(2, HY_NA, HY_NB, HY_ORDER * c))

    vx = short_conv_groups(proj, P_HY, p['hy_conv_w'], p['hy_conv_b'], group=c, tc=128, silu=False,
                           name="hyena_short_conv")
    vx = vx.reshape(3 * npair, HY_NA, lanes)
    z, z_off = vx, 0
    for n in range(HY_ORDER):
        y1 = dft_first(g1_bf, z, n_out=npair, offset=z_off)
        u = dft_mid(g2_bf, g2_inv_bf, y1.reshape(npair, 2, HY_NA, HY_NB, c), hspec, n)
        skip_lanes = jnp.tile(p['hy_skip'][n], 8192 // c).reshape(1, 8192)
        z = dft_last(g1_inv_bf, u.reshape(npair, 2 * HY_NA, lanes), z, z_off, vx, (n + 1) * npair, skip_lanes)
        z_off = 0
    return z.reshape(bsz, l, c)


def _log_sigmoid(x):
    return jnp.minimum(x, 0.0) - jnp.log1p(jnp.exp(-jnp.abs(x)))


def _mlstm_kernel(*refs, nc_ctx):
    ins, (bias_ref, biast_ref), outs, (c_ref, m_ref) = refs[:20], refs[20:22], refs[22:26], refs[26:28]
    d = ML_HEAD_DIM
    t = ML_CHUNK
    nh = ML_HEADS
    step = pl.program_id(1)

    @pl.when(step == 0)
    def _():
        c_ref[...] = jnp.zeros_like(c_ref)
        m_ref[...] = jnp.zeros_like(m_ref)

    def chunk(direction, q_ref, k_ref, v_ref, g_ref, gt_ref, o_ref):
        g_cols = g_ref[0, 0] + bias_ref[direction]
        g_rows = gt_ref[0, 0] + biast_ref[direction]
        lf_cols = _log_sigmoid(g_cols[:, nh:])
        li_rows = g_rows[:nh]
        lf_rows = _log_sigmoid(g_rows[nh:])
        r = lax.broadcasted_iota(jnp.int32, (t, t), 0)
        cidx = lax.broadcasted_iota(jnp.int32, (t, t), 1)
        mask = cidx <= r if direction == 0 else cidx >= r
        mask_t = r <= cidx if direction == 0 else r >= cidx
        tri = mask.astype(F32)
        tri_t = mask_t.astype(F32)
        lane = lax.broadcasted_iota(jnp.int32, (t, d), 1)
        ones_col = jnp.where(lane == 0, 1.0, 0.0).astype(BF16)
        for h in range(nh):
            sl = slice(h * d, (h + 1) * d)
            bc = jnp.sum(tri * lf_rows[h:h + 1, :], axis=1, keepdims=True)
            br = jnp.sum(tri_t * lf_cols[:, h:h + 1], axis=0, keepdims=True)
            lir = li_rows[h:h + 1, :]
            m_prev = m_ref[direction, h, 0:1, 0:1]
            dmat = jnp.where(mask, bc - br + lir, -jnp.inf)
            inter = bc + m_prev
            m_t = jnp.maximum(inter, jnp.max(dmat, axis=1, keepdims=True))
            dexp = jnp.exp(dmat - m_t)
            inter_w = jnp.exp(inter - m_t)
            qh = (q_ref[0, 0, :, sl] * (d ** -0.5)).astype(BF16)
            kt = k_ref[0, 0, :, sl].T
            v_ext = jnp.concatenate([v_ref[0, :, sl].astype(BF16), ones_col], axis=1)
            s = jnp.dot(qh, kt.astype(BF16), preferred_element_type=F32) * dexp
            c_ext = c_ref[direction, h]
            acc = inter_w * jnp.dot(qh, c_ext.astype(BF16), preferred_element_type=F32)
            acc = acc + jnp.dot(s.astype(BF16), v_ext, preferred_element_type=F32)
            den = jnp.maximum(jnp.abs(acc[:, d:d + 1]), jnp.exp(-m_t))
            o_ref[0, :, sl] = acc[:, :d] / den
            b_last = jnp.sum(lf_rows[h:h + 1, :], axis=1, keepdims=True)
            g_row = b_last - br + lir
            m_new = jnp.maximum(b_last + m_prev, jnp.max(g_row, axis=1, keepdims=True))
            a = jnp.exp(b_last + m_prev - m_new)
            w_row = jnp.exp(g_row - m_new)
            c_ref[direction, h] = a * c_ext + jnp.dot((kt * w_row).astype(BF16), v_ext,
                                                      preferred_element_type=F32)
            m_ref[direction, h] = jnp.broadcast_to(m_new, m_ref.shape[2:])

    @pl.when(step < nc_ctx)
    def _():
        for dd in range(2):
            chunk(dd, *ins[10 * dd:10 * dd + 5], outs[2 * dd])

    @pl.when(step >= nc_ctx)
    def _():
        for dd in range(2):
            chunk(dd, *ins[10 * dd + 5:10 * dd + 10], outs[2 * dd + 1])


def mlstm_bidir(qk_c, proj_c, gates_c, qk_l, proj_l, gates_l, gate_bias):
    _, bsz, lc, w = qk_c.shape
    ll = qk_l.shape[2]
    t, nh = ML_CHUNK, ML_HEADS
    nc_c, nc_l = lc // t, ll // t

    def by_dir(g):
        g4 = g.reshape(g.shape[:-1] + (4, nh))
        return jnp.stack([jnp.concatenate([g4[..., dd, :], g4[..., 2 + dd, :]], axis=-1) for dd in range(2)])

    gd_c, gd_l = by_dir(gates_c), by_dir(gates_l)
    gt_c, gt_l = gd_c.transpose(0, 1, 3, 2), gd_l.transpose(0, 1, 3, 2)
    bias = by_dir(gate_bias.reshape(1, ML_GATES))
    bias_t = bias.transpose(0, 2, 1)

    def idx(dd, step_off, nc):
        def f(g):
            cc = jnp.clip(g - step_off, 0, nc - 1)
            return cc + dd * (nc - 1 - 2 * cc)
        return f

    vcol = P_MLV // w

    def seg_specs(dd, ix):
        return [pl.BlockSpec((1, 1, t, w), lambda b, g: (0, b, ix(g), 0)),
                pl.BlockSpec((1, 1, t, w), lambda b, g: (1, b, ix(g), 0)),
                pl.BlockSpec((1, t, w), lambda b, g: (b, ix(g), vcol)),
                pl.BlockSpec((1, 1, t, 2 * nh), lambda b, g: (dd, b, ix(g), 0)),
                pl.BlockSpec((1, 1, 2 * nh, t), lambda b, g: (dd, b, 0, ix(g)))]

    def out_spec(ix):
        return pl.BlockSpec((1, t, w), lambda b, g: (b, ix(g), 0))

    in_specs, out_specs, operands = [], [], []
    for dd in range(2):
        ic, il = idx(dd, 0, nc_c), idx(dd, nc_c, nc_l)
        in_specs += seg_specs(dd, ic) + seg_specs(dd, il)
        operands += [qk_c, qk_c, proj_c, gd_c, gt_c, qk_l, qk_l, proj_l, gd_l, gt_l]
        out_specs += [out_spec(ic), out_spec(il)]
    in_specs += [pl.BlockSpec((2, 1, 2 * nh), lambda b, g: (0, 0, 0)),
                 pl.BlockSpec((2, 2 * nh, 1), lambda b, g: (0, 0, 0))]
    h_cf, h_lf, h_cb, h_lb = pl.pallas_call(
        functools.partial(_mlstm_kernel, nc_ctx=nc_c),
        grid=(bsz, nc_c + nc_l),
        in_specs=in_specs,
        out_specs=out_specs,
        out_shape=[jax.ShapeDtypeStruct((bsz, lc, w), F32), jax.ShapeDtypeStruct((bsz, ll, w), F32)] * 2,
        scratch_shapes=[pltpu.VMEM((2, nh, ML_HEAD_DIM, 2 * ML_HEAD_DIM), F32), pltpu.VMEM((2, nh, 8, 128), F32)],
        compiler_params=_cparams(("parallel", "arbitrary")),
        name="mlstm",
    )(*operands, bias, bias_t)
    return (h_cf, h_cb), (h_lf, h_lb)


def _rms(x):
    return x * lax.rsqrt(jnp.mean(x * x, axis=-1, keepdims=True) + EPS)


def _dot_split(x, e):
    hi = x.astype(BF16)
    lo = (x - hi.astype(F32)).astype(BF16)
    return jnp.dot(hi, e, preferred_element_type=F32) + jnp.dot(lo, e, preferred_element_type=F32)


def _swap_rope_halves(x):
    w = x.shape[1]
    lane = lax.broadcasted_iota(jnp.int32, x.shape, 1) % MLA_ROPE
    return jnp.where(lane < MLA_ROPE // 2, pltpu.roll(x, w - MLA_ROPE // 2, axis=1),
                     pltpu.roll(x, MLA_ROPE // 2, axis=1))


def _mla_prep_kernel(pq_ref, pkv_ref, sm_ref, gqa_ref, gkva_ref, wq_ref, wkv_ref, gqn_ref, gqr_ref, gkn_ref,
                     gkr_ref, en_ref, er_ref, ent_ref, ert_ref, *rest, use_rope):
    if use_rope:
        ccq_ref, ssq_ref, cck_ref, ssk_ref, qt_ref, k_ref, vt_ref = rest
    else:
        qt_ref, k_ref, vt_ref = rest
    nh, dn, dr = MLA_HEADS, MLA_NOPE, MLA_ROPE
    qa = (_rms(pq_ref[0].astype(F32)) * gqa_ref[...]).astype(BF16)
    kva = (_rms(pkv_ref[0].astype(F32)) * gkva_ref[...]).astype(BF16)
    q_raw = jnp.dot(qa, wq_ref[...], preferred_element_type=F32)
    kv_raw = jnp.dot(kva, wkv_ref[...], preferred_element_type=F32)
    inv_d = 1.0 / MLA_QK

    qn, qr = q_raw[:, :nh * dn], q_raw[:, nh * dn:]
    ss = _dot_split(qn * qn, en_ref[...]) + _dot_split(qr * qr, er_ref[...])
    rs = lax.rsqrt(ss * inv_d + EPS)
    qn = qn * _dot_split(rs, ent_ref[...]) * gqn_ref[...]
    qr = qr * _dot_split(rs, ert_ref[...]) * gqr_ref[...]
    if use_rope:
        qr = qr * ccq_ref[...] + _swap_rope_halves(qr) * ssq_ref[...]
    qrt = qr.T
    tl = qn.shape[0]
    for h in range(nh):
        qt_ref[0, h, 0:dn, :] = qn[:, h * dn:(h + 1) * dn].T.astype(BF16)
        qt_ref[0, h, dn:dn + dr, :] = qrt[h * dr:(h + 1) * dr].astype(BF16)
        qt_ref[0, h, dn + dr:, :] = jnp.zeros((QK_PAD - dn - dr, tl), BF16)

    kn, vv = kv_raw[:, :nh * dn], kv_raw[:, nh * dn:]
    lane = lax.broadcasted_iota(jnp.int32, sm_ref.shape[1:], 1)
    kr = jnp.where(lane < dr, sm_ref[0], 0.0)
    ssk = _dot_split(kn * kn, en_ref[...]) + jnp.sum(kr * kr, axis=-1, keepdims=True)
    rsk = lax.rsqrt(ssk * inv_d + EPS)
    kn = kn * _dot_split(rsk, ent_ref[...]) * gkn_ref[...]
    kr = kr * gkr_ref[...]
    if use_rope:
        kr = kr * cck_ref[...] + _swap_rope_halves(kr) * ssk_ref[...]
    for h in range(nh):
        k_ref[0, h, :, 0:dn] = kn[:, h * dn:(h + 1) * dn].astype(BF16)
        k_ref[0, h, :, dn:] = (kr * rsk[:, h:h + 1]).astype(BF16)
        vt_ref[0, h] = vv[:, h * MLA_V:(h + 1) * MLA_V].T.astype(BF16)


@functools.lru_cache(maxsize=None)
def _head_selectors():
    en = np.kron(np.eye(MLA_HEADS), np.ones((MLA_NOPE, 1)))
    er = np.kron(np.eye(MLA_HEADS), np.ones((MLA_ROPE, 1)))
    pad = lambda m: np.pad(m, ((0, 0), (0, 128 - m.shape[1]))).astype(np.float32)
    return pad(en), pad(er)


def _rope_tables(n_tokens):
    rows = n_tokens // GRID_W
    row = jnp.repeat(jnp.arange(rows, dtype=F32), GRID_W)
    col = jnp.tile(jnp.arange(GRID_W, dtype=F32), rows)
    freqs = ROPE_BASE ** (-jnp.arange(ROPE_PAIRS_PER_AXIS, dtype=F32) / ROPE_PAIRS_PER_AXIS)
    ang = jnp.concatenate([row[:, None] * freqs, col[:, None] * freqs], axis=-1)
    cos, sin = jnp.cos(ang), jnp.sin(ang)
    cc = jnp.concatenate([cos, cos], axis=-1)
    ss = jnp.concatenate([-sin, sin], axis=-1)
    zero = jnp.zeros((n_tokens, 128 - MLA_ROPE), F32)
    return (jnp.tile(cc, (1, MLA_HEADS)), jnp.tile(ss, (1, MLA_HEADS)),
            jnp.concatenate([cc, zero], axis=-1), jnp.concatenate([ss, zero], axis=-1))


def mla_prep(proj, small, p, rope_tables, *, tl):
    b, l, _ = proj.shape
    nh = MLA_HEADS
    en, er = _head_selectors()
    en_b, er_b = jnp.asarray(en, BF16), jnp.asarray(er, BF16)
    ent_b, ert_b = jnp.asarray(en.T.copy(), BF16), jnp.asarray(er.T.copy(), BF16)
    const = lambda shape: pl.BlockSpec(shape, lambda bi, i: (0,) * len(shape))
    nq = nh * MLA_QK
    nkv = nh * (MLA_NOPE + MLA_V)
    in_specs = [pl.BlockSpec((1, tl, MLA_Q_LORA), lambda bi, i: (bi, i, P_CQ // MLA_Q_LORA)),
                pl.BlockSpec((1, tl, MLA_KV_LORA), lambda bi, i: (bi, i, P_CKV // MLA_KV_LORA)),
                pl.BlockSpec((1, tl, N_SMALL), lambda bi, i: (bi, i, 0)),
                const((1, MLA_Q_LORA)), const((1, MLA_KV_LORA)), const((MLA_Q_LORA, nq)), const((MLA_KV_LORA, nkv)),
                const((1, nh * MLA_NOPE)), const((1, nh * MLA_ROPE)), const((1, nh * MLA_NOPE)), const((1, 128)),
                const(en.shape), const(er.shape), const(en.T.shape), const(er.T.shape)]
    operands = [proj, proj, small, p['gqa'], p['gkva'], p['wq'], p['wkv'], p['gqn'], p['gqr'], p['gkn'], p['gkr'],
                en_b, er_b, ent_b, ert_b]
    use_rope = rope_tables is not None
    if use_rope:
        in_specs += [pl.BlockSpec((tl, nh * MLA_ROPE), lambda bi, i: (i, 0)),
                     pl.BlockSpec((tl, nh * MLA_ROPE), lambda bi, i: (i, 0)),
                     pl.BlockSpec((tl, 128), lambda bi, i: (i, 0)),
                     pl.BlockSpec((tl, 128), lambda bi, i: (i, 0))]
        operands += list(rope_tables)
    return pl.pallas_call(
        functools.partial(_mla_prep_kernel, use_rope=use_rope),
        grid=(b, l // tl),
        in_specs=in_specs,
        out_specs=[pl.BlockSpec((1, nh, QK_PAD, tl), lambda bi, i: (bi, 0, 0, i)),
                   pl.BlockSpec((1, nh, tl, QK_PAD), lambda bi, i: (bi, 0, i, 0)),
                   pl.BlockSpec((1, nh, MLA_V, tl), lambda bi, i: (bi, 0, 0, i))],
        out_shape=[jax.ShapeDtypeStruct((b, nh, QK_PAD, l), BF16),
                   jax.ShapeDtypeStruct((b, nh, l, QK_PAD), BF16),
                   jax.ShapeDtypeStruct((b, nh, MLA_V, l), BF16)],
        compiler_params=_cparams(("parallel", "parallel")),
        name="mla_prep",
    )(*operands)


def _combine_out_kernel(a_ref, y_ref, hf_ref, hb_ref, o_ref, g_ref, w_ref, r_ref, gt_ref, out_ref, cat_ref):
    @pl.when(pl.program_id(2) == 0)
    def _():
        g = g_ref[...]
        cat_ref[:, :MLA_WIDTH] = (_rms(a_ref[0]) * g[:, :MLA_WIDTH]).astype(BF16)
        hy0 = MLA_WIDTH
        cat_ref[:, hy0:hy0 + HY_WIDTH] = (_rms(y_ref[0]) * g[:, hy0:hy0 + HY_WIDTH]).astype(BF16)
        ml0 = MLA_WIDTH + HY_WIDTH
        hh = hf_ref[0] + hb_ref[0]
        og = jax.nn.sigmoid(o_ref[0].astype(F32))
        for hd in range(ML_HEADS):
            sl = slice(hd * ML_HEAD_DIM, (hd + 1) * ML_HEAD_DIM)
            gl = g[:, ml0 + hd * ML_HEAD_DIM:ml0 + (hd + 1) * ML_HEAD_DIM]
            cat_ref[:, ml0 + hd * ML_HEAD_DIM:ml0 + (hd + 1) * ML_HEAD_DIM] = (
                _rms(hh[:, sl]) * gl * og[:, sl]).astype(BF16)

    y = jnp.dot(cat_ref[...], w_ref[...], preferred_element_type=F32)
    out_ref[0] = r_ref[0] + gt_ref[0] * y


def combine_out_proj(a, y, h2, proj, g_mix, w_out, res, gate, *, tm, tn, name="out_proj"):
    b, s, d = res.shape
    return pl.pallas_call(
        _combine_out_kernel,
        grid=(b, s // tm, d // tn),
        in_specs=[pl.BlockSpec((1, tm, MLA_WIDTH), lambda bi, i, j: (bi, i, 0)),
                  pl.BlockSpec((1, tm, HY_WIDTH), lambda bi, i, j: (bi, i, 0)),
                  pl.BlockSpec((1, tm, ML_WIDTH), lambda bi, i, j: (bi, i, 0)),
                  pl.BlockSpec((1, tm, ML_WIDTH), lambda bi, i, j: (bi, i, 0)),
                  pl.BlockSpec((1, tm, ML_WIDTH), lambda bi, i, j: (bi, i, P_MLO // ML_WIDTH)),
                  pl.BlockSpec((1, MIX_WIDTH), lambda bi, i, j: (0, 0)),
                  pl.BlockSpec((MIX_WIDTH, tn), lambda bi, i, j: (0, j)),
                  pl.BlockSpec((1, tm, tn), lambda bi, i, j: (bi, i, j)),
                  pl.BlockSpec((1, 1, tn), lambda bi, i, j: (bi, 0, j))],
        out_specs=pl.BlockSpec((1, tm, tn), lambda bi, i, j: (bi, i, j)),
        out_shape=jax.ShapeDtypeStruct((b, s, d), F32),
        scratch_shapes=[pltpu.VMEM((tm, MIX_WIDTH), BF16)],
        compiler_params=_cparams(("parallel", "parallel", "arbitrary")),
        name=name,
    )(a, y, h2[0], h2[1], proj, g_mix.reshape(1, MIX_WIDTH), w_out, res, gate)


@functools.lru_cache(maxsize=None)
def _direct_dft_tables(l):
    n = 2 * l
    k = np.arange(n)[:, None]
    t = np.arange(l)[None, :]
    ang = 2.0 * np.pi * ((k * t) % n) / n
    c, s = np.cos(ang), np.sin(ang)
    gf = np.block([[c, s], [-s, c]])
    gi = gf.T.copy()
    t_full = np.arange(n)[None, :]
    ang = 2.0 * np.pi * ((k * t_full) % n) / n
    g_real = np.concatenate([np.cos(ang), -np.sin(ang)], axis=0)
    return gf.astype(np.float32), gi.astype(np.float32), g_real.astype(np.float32)


def _direct_conv_kernel(gf_ref, gi_ref, z_ref, gate_ref, h_ref, skip_ref, o_ref, *, scale):
    z = z_ref[0]
    x = jnp.dot(gf_ref[...].astype(BF16), z.astype(BF16), preferred_element_type=F32)
    n = x.shape[0] // 2
    xr, xi = x[:n], x[n:]
    hr, hi = h_ref[:n] * scale, h_ref[n:] * scale
    prod = jnp.concatenate([xr * hr - xi * hi, xr * hi + xi * hr], axis=0).astype(BF16)
    conv = jnp.dot(gi_ref[...].astype(BF16), prod, preferred_element_type=F32)
    o_ref[0] = gate_ref[0] * (conv + skip_ref[...] * z)


def hyena_short(proj, p):
    bsz, l, _ = proj.shape
    c = HY_WIDTH
    npair = bsz // 2
    n = 2 * l
    gf, gi, g_real = _direct_dft_tables(l)
    buf = hyena_filter_buffer(l, p, tl=l)
    hspec = dft_first(jnp.asarray(g_real), buf.reshape(1, n, HY_ORDER * c), n_out=1, tn=HY_ORDER * c,
                      precise=True)[0]
    vx = short_conv_groups(proj, P_HY, p['hy_conv_w'], p['hy_conv_b'], group=c, tc=128, silu=False,
                           name="hyena_short_conv_ctx")
    vx = vx.reshape(3 * npair, 2 * l, c)
    z = vx

    def rows_at(off):
        return lambda q: (off + q, 0, 0)

    def cols_at(col):
        return lambda q: (0, col)

    for order in range(HY_ORDER):
        z = pl.pallas_call(
            functools.partial(_direct_conv_kernel, scale=1.0 / n),
            grid=(npair,),
            in_specs=[pl.BlockSpec(gf.shape, lambda q: (0, 0)),
                      pl.BlockSpec(gi.shape, lambda q: (0, 0)),
                      pl.BlockSpec((1, 2 * l, c), rows_at(0)),
                      pl.BlockSpec((1, 2 * l, c), rows_at((order + 1) * npair)),
                      pl.BlockSpec((2 * n, c), cols_at(order)),
                      pl.BlockSpec((1, c), lambda q: (0, 0))],
            out_specs=pl.BlockSpec((1, 2 * l, c), lambda q: (q, 0, 0)),
            out_shape=jax.ShapeDtypeStruct((npair, 2 * l, c), F32),
            compiler_params=_cparams(("parallel",)),
            name="hyena_direct_conv",
        )(jnp.asarray(gf), jnp.asarray(gi), z, vx, hspec, p['hy_skip'][order].reshape(1, c))
    return z.reshape(bsz, l, c)


def _split_cols(w, sizes):
    out, start = [], 0
    for s in sizes:
        out.append(w[:, start:start + s])
        start += s
    return out


def _prep_layer_weights(l, w_in, mla_qa_norm, mla_kva_norm, mla_w_uq, mla_w_ukv, mla_q_norm, mla_k_norm):
    cq, ckv, kr, hy, mqk, mv, mo, gt = _split_cols(w_in[l], IN_SIZES)
    w_main = jnp.concatenate([cq, ckv, hy, mqk, mv, mo], axis=1).astype(BF16)
    pad = jnp.zeros((D_MODEL, N_SMALL - MLA_ROPE - ML_GATES), F32)
    w_small = jnp.concatenate([kr, gt, pad], axis=1).astype(BF16)
    nh = MLA_HEADS
    wq = mla_w_uq[l].reshape(MLA_Q_LORA, nh, MLA_QK)
    wq = jnp.concatenate([wq[:, :, :MLA_NOPE].reshape(MLA_Q_LORA, -1), wq[:, :, MLA_NOPE:].reshape(MLA_Q_LORA, -1)],
                         axis=1).astype(BF16)
    wkv = mla_w_ukv[l].reshape(MLA_KV_LORA, nh, MLA_NOPE + MLA_V)
    wkv = jnp.concatenate([wkv[:, :, :MLA_NOPE].reshape(MLA_KV_LORA, -1),
                           wkv[:, :, MLA_NOPE:].reshape(MLA_KV_LORA, -1)], axis=1).astype(BF16)
    q_scale = (MLA_QK ** -0.5) * math.log2(math.e)
    gq, gk = mla_q_norm[l] * q_scale, mla_k_norm[l]
    mla = {'gqa': mla_qa_norm[l].reshape(1, -1), 'gkva': mla_kva_norm[l].reshape(1, -1), 'wq': wq, 'wkv': wkv,
           'gqn': jnp.tile(gq[:MLA_NOPE], nh).reshape(1, -1), 'gqr': jnp.tile(gq[MLA_NOPE:], nh).reshape(1, -1),
           'gkn': jnp.tile(gk[:MLA_NOPE], nh).reshape(1, -1),
           'gkr': jnp.concatenate([gk[MLA_NOPE:], jnp.zeros((128 - MLA_ROPE,), F32)]).reshape(1, -1)}
    return w_main, w_small, mla


def kernel(x, c, ctx, c_ctx, ada_w, ada_b, norm1_g, norm2_g, w_in, mla_qa_norm, mla_kva_norm, mla_w_uq,
           mla_w_ukv, mla_q_norm, mla_k_norm, hy_conv_w, hy_conv_b, hy_w1, hy_b1, hy_w2, hy_b2, hy_w3,
           hy_decay, hy_skip, ml_conv_w, ml_conv_b, ml_gate_b, mix_norm_g, w_out, ffn_w1, ffn_w2):
    B, S, D = x.shape
    LC = ctx.shape[1]
    rope_tables = _rope_tables(S)

    silu_rows = jnp.concatenate([jax.nn.silu(c), jax.nn.silu(c_ctx)[None], jnp.zeros((8 - B - 1, D), F32)], axis=0)
    mods = ada_modulation(silu_rows.astype(BF16), ada_w, ada_b)

    for l in range(DEPTH):
        need_ctx = l < DEPTH - 1
        p = {'hy_conv_w': hy_conv_w[l], 'hy_conv_b': hy_conv_b[l], 'hy_w1': hy_w1[l], 'hy_b1': hy_b1[l],
             'hy_w2': hy_w2[l], 'hy_b2': hy_b2[l], 'hy_w3': hy_w3[l], 'hy_decay': hy_decay[l],
             'hy_skip': hy_skip[l], 'ml_conv_w': ml_conv_w[l], 'ml_conv_b': ml_conv_b[l]}
        w_main, w_small, mla = _prep_layer_weights(l, w_in, mla_qa_norm, mla_kva_norm, mla_w_uq, mla_w_ukv,
                                                   mla_q_norm, mla_k_norm)
        w_out_l = w_out[l].astype(BF16)
        w1_l = ffn_w1[l].astype(BF16)
        w2_l = ffn_w2[l].astype(BF16)

        mod_l = [m[:, None, :] for m in jnp.split(mods[l, :B], 6, axis=-1)]
        mod_c = [jnp.broadcast_to(m[:, None, :], (B, 1, D)) for m in jnp.split(mods[l, B:B + 1], 6, axis=-1)]

        proj_l, small_l = in_proj(x, norm1_g[l], mod_l[1], mod_l[0], w_main, w_small, tm=1024, tn=IN_PROJ_TN)
        proj_c, small_c = in_proj(ctx, norm1_g[l], mod_c[1], mod_c[0], w_main, w_small, tm=LC, tn=IN_PROJ_TN,
                                  name="in_proj_ctx")

        qt_l, k_l, vt_l = mla_prep(proj_l, small_l, mla, rope_tables, tl=512)
        qt_c, k_c, vt_c = mla_prep(proj_c, small_c, mla, None, tl=LC)
        a_l = attention(qt_l, [(k_c, vt_c), (k_l, vt_l)], tq=ATT_TQ)
        y_l = hyena_long(proj_l, p)

        def mlstm_in(proj, small):
            qk = short_conv_groups(proj, P_MLQK, p['ml_conv_w'], p['ml_conv_b'], group=ML_WIDTH, tc=128,
                                   silu=True, name="mlstm_short_conv")
            return qk, small[..., S_GATES:S_GATES + ML_GATES]

        qk_c, gates_c = mlstm_in(proj_c, small_c)
        qk_l, gates_l = mlstm_in(proj_l, small_l)
        h_c, h_l = mlstm_bidir(qk_c, proj_c, gates_c, qk_l, proj_l, gates_l, ml_gate_b[l])
        x = combine_out_proj(a_l, y_l, h_l, proj_l, mix_norm_g[l], w_out_l, x, mod_l[2], tm=1024, tn=MM_TN)
        hid = norm_swiglu(x, norm2_g[l], mod_l[4], mod_l[3], w1_l, tm=1024, tn=MM_TN, name="ffn_up")
        x = mm_residual(hid, w2_l, x, mod_l[5], tm=1024, tn=MM_TN, name="ffn_down")

        if need_ctx:
            a_c = attention(qt_c, [(k_c, vt_c)], tq=LC)
            y_c = hyena_short(proj_c, p)
            ctx = combine_out_proj(a_c, y_c, h_c, proj_c, mix_norm_g[l], w_out_l, ctx, mod_c[2], tm=LC, tn=MM_TN,
                                   name="out_proj_ctx")
            hid = norm_swiglu(ctx, norm2_g[l], mod_c[4], mod_c[3], w1_l, tm=LC, tn=MM_TN, name="ffn_up_ctx")
            ctx = mm_residual(hid, w2_l, ctx, mod_c[5], tm=LC, tn=MM_TN, name="ffn_down_ctx")
    return x
```

```python
import functools
import math

import jax
import jax.numpy as jnp
import numpy as np
from jax import lax
from jax.experimental import pallas as pl
from jax.experimental.pallas import tpu as pltpu

D_MODEL = 2048
DEPTH = 2
GRID_W = 64
EPS = 1e-6

MLA_HEADS = 8
MLA_NOPE = 128
MLA_ROPE = 64
MLA_QK = MLA_NOPE + MLA_ROPE
MLA_V = 128
MLA_Q_LORA = 512
MLA_KV_LORA = 512
MLA_WIDTH = MLA_HEADS * MLA_V
ROPE_BASE = 10000.0
ROPE_PAIRS_PER_AXIS = MLA_ROPE // 4
QK_PAD = 256

HY_WIDTH = 512
HY_ORDER = 2
HY_BANDS = 16
HY_SHIFT = 0.05
SHORT_CONV = 3

ML_HEADS = 4
ML_HEAD_DIM = 128
ML_WIDTH = ML_HEADS * ML_HEAD_DIM
ML_CHUNK = 128
ML_GATES = 4 * ML_HEADS

MIX_WIDTH = MLA_WIDTH + HY_WIDTH + ML_WIDTH
FFN_HIDDEN = ((8 * D_MODEL // 3 + 255) // 256) * 256

IN_SIZES = (MLA_Q_LORA, MLA_KV_LORA, MLA_ROPE, 3 * HY_WIDTH, 2 * ML_WIDTH, ML_WIDTH, ML_WIDTH, ML_GATES)
N_IN = sum(IN_SIZES)

P_CQ, P_CKV, P_HY, P_MLQK, P_MLV, P_MLO = 0, 512, 1024, 2560, 3584, 4096
N_MAIN = 4608
N_SMALL = 128
S_GATES = MLA_ROPE

VMEM_LIMIT_BYTES = 56 * 1024 * 1024

BF16 = jnp.bfloat16
F32 = jnp.float32


def _cparams(sem):
    return pltpu.CompilerParams(dimension_semantics=sem, vmem_limit_bytes=VMEM_LIMIT_BYTES)


def _ada_kernel(s_ref, w_ref, b_ref, o_ref):
    w = w_ref[0].astype(BF16)
    o_ref[0] = jnp.dot(s_ref[...], w, preferred_element_type=F32) + b_ref[0]


def ada_modulation(silu_rows, ada_w, ada_b, tn=1024):
    depth, d, n = ada_w.shape
    return pl.pallas_call(
        _ada_kernel,
        grid=(depth, n // tn),
        in_specs=[pl.BlockSpec((8, d), lambda l, j: (0, 0)),
                  pl.BlockSpec((1, d, tn), lambda l, j: (l, 0, j)),
                  pl.BlockSpec((1, 1, tn), lambda l, j: (l, 0, j))],
        out_specs=pl.BlockSpec((1, 8, tn), lambda l, j: (l, 0, j)),
        out_shape=jax.ShapeDtypeStruct((depth, 8, n), F32),
        compiler_params=_cparams(("parallel", "parallel")),
        name="ada_modulation",
    )(silu_rows, ada_w, ada_b.reshape(depth, 1, n))


MM_TN = 512
IN_PROJ_TN = 1536


def _normed(x, g, sc, sh):
    y = x * lax.rsqrt(jnp.mean(x * x, axis=-1, keepdims=True) + EPS) * g
    return y * (1.0 + sc) + sh


def _in_proj_kernel(x_ref, g_ref, sc_ref, sh_ref, w_ref, ws_ref, o_ref, os_ref, xn_ref):
    @pl.when(pl.program_id(2) == 0)
    def _():
        xn = _normed(x_ref[0], g_ref[...], sc_ref[0], sh_ref[0]).astype(BF16)
        xn_ref[...] = xn
        os_ref[0] = jnp.dot(xn, ws_ref[...], preferred_element_type=F32)

    o_ref[0] = jnp.dot(xn_ref[...], w_ref[...], preferred_element_type=F32).astype(o_ref.dtype)


def in_proj(x, g, sc, sh, w_main, w_small, *, tm, tn, name="in_proj"):
    b, s, k = x.shape
    n = w_main.shape[1]
    ns = w_small.shape[1]
    return pl.pallas_call(
        _in_proj_kernel,
        grid=(b, s // tm, n // tn),
        in_specs=[pl.BlockSpec((1, tm, k), lambda bi, i, j: (bi, i, 0)),
                  pl.BlockSpec((1, k), lambda bi, i, j: (0, 0)),
                  pl.BlockSpec((1, 1, k), lambda bi, i, j: (bi, 0, 0)),
                  pl.BlockSpec((1, 1, k), lambda bi, i, j: (bi, 0, 0)),
                  pl.BlockSpec((k, tn), lambda bi, i, j: (0, j)),
                  pl.BlockSpec((k, ns), lambda bi, i, j: (0, 0))],
        out_specs=[pl.BlockSpec((1, tm, tn), lambda bi, i, j: (bi, i, j)),
                   pl.BlockSpec((1, tm, ns), lambda bi, i, j: (bi, i, 0))],
        out_shape=[jax.ShapeDtypeStruct((b, s, n), BF16), jax.ShapeDtypeStruct((b, s, ns), F32)],
        scratch_shapes=[pltpu.VMEM((tm, k), BF16)],
        compiler_params=_cparams(("parallel", "parallel", "arbitrary")),
        name=name,
    )(x, g.reshape(1, k), sc, sh, w_main, w_small)


def _norm_swiglu_kernel(x_ref, g_ref, sc_ref, sh_ref, wg_ref, wu_ref, o_ref, xn_ref):
    @pl.when(pl.program_id(2) == 0)
    def _():
        xn_ref[...] = _normed(x_ref[0], g_ref[...], sc_ref[0], sh_ref[0]).astype(BF16)

    xn = xn_ref[...]
    gate = jnp.dot(xn, wg_ref[...], preferred_element_type=F32)
    up = jnp.dot(xn, wu_ref[...], preferred_element_type=F32)
    o_ref[0] = (gate * (0.5 * jnp.tanh(0.5 * gate) + 0.5) * up).astype(o_ref.dtype)


def norm_swiglu(x, g, sc, sh, w1, *, tm, tn, name="norm_swiglu"):
    b, s, d = x.shape
    h = w1.shape[1] // 2
    nj = h // tn
    return pl.pallas_call(
        _norm_swiglu_kernel,
        grid=(b, s // tm, nj),
        in_specs=[pl.BlockSpec((1, tm, d), lambda bi, i, j: (bi, i, 0)),
                  pl.BlockSpec((1, d), lambda bi, i, j: (0, 0)),
                  pl.BlockSpec((1, 1, d), lambda bi, i, j: (bi, 0, 0)),
                  pl.BlockSpec((1, 1, d), lambda bi, i, j: (bi, 0, 0)),
                  pl.BlockSpec((d, tn), lambda bi, i, j: (0, j)),
                  pl.BlockSpec((d, tn), lambda bi, i, j: (0, j + nj))],
        out_specs=pl.BlockSpec((1, tm, tn), lambda bi, i, j: (bi, i, j)),
        out_shape=jax.ShapeDtypeStruct((b, s, h), BF16),
        scratch_shapes=[pltpu.VMEM((tm, d), BF16)],
        compiler_params=_cparams(("parallel", "parallel", "arbitrary")),
        name=name,
    )(x, g.reshape(1, d), sc, sh, w1, w1)


def _mm_res_kernel(a_ref, w_ref, r_ref, gt_ref, o_ref):
    y = jnp.dot(a_ref[0], w_ref[...], preferred_element_type=F32)
    o_ref[0] = r_ref[0] + gt_ref[0] * y


def mm_residual(a, w, res, gate, *, tm, tn, name="mm_residual"):
    b, s, k = a.shape
    n = w.shape[1]
    return pl.pallas_call(
        _mm_res_kernel,
        grid=(b, s // tm, n // tn),
        in_specs=[pl.BlockSpec((1, tm, k), lambda bi, i, j: (bi, i, 0)),
                  pl.BlockSpec((k, tn), lambda bi, i, j: (0, j)),
                  pl.BlockSpec((1, tm, tn), lambda bi, i, j: (bi, i, j)),
                  pl.BlockSpec((1, 1, tn), lambda bi, i, j: (bi, 0, j))],
        out_specs=pl.BlockSpec((1, tm, tn), lambda bi, i, j: (bi, i, j)),
        out_shape=jax.ShapeDtypeStruct((b, s, n), F32),
        compiler_params=_cparams(("parallel", "parallel", "parallel")),
        name=name,
    )(a, w, res, gate)


ATT_KEY_CHUNK = 512
ATT_TQ = 1024


def _attn_kernel(qt_ref, *refs, n_seg):
    o_ref, s_ref = refs[2 * n_seg], refs[2 * n_seg + 1]
    qt = qt_ref[0, 0]
    tq = qt.shape[1]
    chunks, off = [], 0
    for sgi in range(n_seg):
        k_ref, vt_ref = refs[2 * sgi], refs[2 * sgi + 1]
        lk = k_ref.shape[2]
        ch = min(ATT_KEY_CHUNK, lk)
        for c in range(lk // ch):
            chunks.append((k_ref, vt_ref, c * ch, ch, off))
            off += ch

    m8 = jnp.full((8, tq), -jnp.inf, F32)
    for k_ref, _, start, ch, off in chunks:
        s = jnp.dot(k_ref[0, 0, start:start + ch, :], qt, preferred_element_type=F32)
        s_ref[off:off + ch, :] = s
        m8 = jnp.maximum(m8, jnp.max(s.reshape(ch // 8, 8, tq), axis=0))
    m = jnp.max(m8, axis=0, keepdims=True)
    l8 = jnp.zeros((8, tq), F32)
    acc = jnp.zeros((vt_ref.shape[2], tq), F32)
    for _, vt_ref, start, ch, off in chunks:
        p = jnp.exp2(s_ref[off:off + ch, :] - m)
        l8 = l8 + jnp.sum(p.reshape(ch // 8, 8, tq), axis=0)
        acc = acc + jnp.dot(vt_ref[0, 0, :, start:start + ch], p.astype(BF16), preferred_element_type=F32)
    l = jnp.sum(l8, axis=0, keepdims=True)
    o_ref[0] = (acc / l).T


def attention(qt, segments, *, tq=256):
    b, h, dk, lq = qt.shape
    dv = segments[0][1].shape[2]
    in_specs = [pl.BlockSpec((1, 1, dk, tq), lambda bi, hi, i: (bi, hi, 0, i))]
    operands = [qt]
    for k, vt in segments:
        lk = k.shape[2]
        in_specs += [pl.BlockSpec((1, 1, lk, dk), lambda bi, hi, i: (bi, hi, 0, 0)),
                     pl.BlockSpec((1, 1, dv, lk), lambda bi, hi, i: (bi, hi, 0, 0))]
        operands += [k, vt]
    return pl.pallas_call(
        functools.partial(_attn_kernel, n_seg=len(segments)),
        grid=(b, h, lq // tq),
        in_specs=in_specs,
        out_specs=pl.BlockSpec((1, tq, dv), lambda bi, hi, i: (bi, i, hi)),
        out_shape=jax.ShapeDtypeStruct((b, lq, h * dv), F32),
        scratch_shapes=[pltpu.VMEM((sum(k.shape[2] for k, _ in segments), tq), F32)],
        compiler_params=_cparams(("parallel", "parallel", "parallel")),
        name="attention",
    )(*operands)


HY_NA = 64
HY_NB = 128
HY_N = HY_NA * HY_NB
HY_KG = 8


@functools.lru_cache(maxsize=None)
def _dft_tables():
    na, nb, n = HY_NA, HY_NB, HY_N
    a = np.arange(na // 2)[None, :]
    ka = np.arange(na)[:, None]
    ang = 2.0 * np.pi * ((a * ka) % na) / na
    c, s = np.cos(ang), np.sin(ang)
    g1 = np.block([[c, s], [-s, c]])
    g1_inv = g1.T.copy()
    a_full = np.arange(na)[None, :]
    ang = 2.0 * np.pi * ((a_full * ka) % na) / na
    g1_real = np.concatenate([np.cos(ang), -np.sin(ang)], axis=0)
    b = np.arange(nb)[None, None, :]
    k = np.arange(na)[:, None, None] + na * np.arange(nb)[None, :, None]
    ang = 2.0 * np.pi * ((b * k) % n) / n
    cr, si = np.cos(ang), np.sin(ang)
    g2 = np.concatenate([np.concatenate([cr, si], axis=2), np.concatenate([-si, cr], axis=2)], axis=1)
    g2_inv = np.transpose(g2, (0, 2, 1)).copy()
    return (g1.astype(np.float32), g1_inv.astype(np.float32), g1_real.astype(np.float32),
            g2.astype(np.float32), g2_inv.astype(np.float32))


def _dot_3pass(a, b):
    a_hi, b_hi = a.astype(BF16), b.astype(BF16)
    a_lo = (a - a_hi.astype(F32)).astype(BF16)
    b_lo = (b - b_hi.astype(F32)).astype(BF16)
    dot = functools.partial(jnp.dot, preferred_element_type=F32)
    return dot(a_hi, b_hi) + (dot(a_hi, b_lo) + dot(a_lo, b_hi))


def _short_conv_kernel(u_ref, w_ref, b_ref, o_ref, *, silu):
    u = u_ref[0].astype(F32)
    n = u.shape[0]
    rows = lax.broadcasted_iota(jnp.int32, u.shape, 0)
    prev = jnp.where(rows == 0, 0.0, pltpu.roll(u, 1, axis=0))
    nxt = jnp.where(rows == n - 1, 0.0, pltpu.roll(u, n - 1, axis=0))
    y = b_ref[...] + prev * w_ref[0:1, :] + u * w_ref[1:2, :] + nxt * w_ref[2:3, :]
    if silu:
        y = y * jax.nn.sigmoid(y)
    o_ref[0, 0] = y


def short_conv_groups(proj, col0, w, b, *, group, tc, silu, name):
    bsz, l, _ = proj.shape
    width = w.shape[1]
    per = group // tc
    return pl.pallas_call(
        functools.partial(_short_conv_kernel, silu=silu),
        grid=(bsz, width // tc),
        in_specs=[pl.BlockSpec((1, l, tc), lambda bi, j: (bi, 0, col0 // tc + j)),
                  pl.BlockSpec((SHORT_CONV, tc), lambda bi, j: (0, j)),
                  pl.BlockSpec((1, tc), lambda bi, j: (0, j))],
        out_specs=pl.BlockSpec((1, 1, l, tc), lambda bi, j: (j // per, bi, 0, j % per)),
        out_shape=jax.ShapeDtypeStruct((width // group, bsz, l, group), F32),
        compiler_params=_cparams(("parallel", "parallel")),
        name=name,
    )(proj, w, b.reshape(1, width))


def _hy_filter_kernel(f_ref, w1_ref, b1_ref, w2_ref, b2_ref, w3_ref, dec_ref, o_ref, *, zero_row):
    hp = lax.Precision.HIGHEST
    f = f_ref[...]
    tl = f.shape[0]
    h = jnp.sin(jnp.dot(f, w1_ref[...], precision=hp, preferred_element_type=F32) + b1_ref[...])
    h = jnp.sin(jnp.dot(h, w2_ref[...], precision=hp, preferred_element_type=F32) + b2_ref[...])
    h = _dot_3pass(h, w3_ref[0])
    h = h * (jnp.exp(-f[:, 0:1] * dec_ref[0]) + HY_SHIFT)
    rows = lax.broadcasted_iota(jnp.int32, h.shape, 0) + pl.program_id(0) * tl
    o_ref[...] = jnp.where(rows == zero_row, 0.0, h)


def hyena_filter_buffer(L, p, tl=512):
    n = 2 * L
    r = jnp.arange(n, dtype=F32)
    t = jnp.where(r < L, r, n - r) / L
    bands = jnp.arange(1, HY_BANDS + 1, dtype=F32)
    ang = 2.0 * math.pi * t[:, None] * bands
    emb = 1 + 2 * HY_BANDS
    feats = jnp.concatenate([t[:, None], jnp.cos(ang), jnp.sin(ang), jnp.zeros((n, 128 - emb), F32)], axis=-1)
    hid = p['hy_w1'].shape[1]
    w1 = jnp.concatenate([p['hy_w1'], jnp.zeros((128 - emb, hid), F32)], axis=0)
    oc = HY_ORDER * HY_WIDTH
    w3 = p['hy_w3'].reshape(hid, HY_ORDER, 2, HY_WIDTH).transpose(2, 0, 1, 3).reshape(2, hid, oc)
    dec = p['hy_decay'].reshape(HY_ORDER, 2, HY_WIDTH).transpose(1, 0, 2).reshape(2, 1, oc)
    half = L // tl
    return pl.pallas_call(
        functools.partial(_hy_filter_kernel, zero_row=L),
        grid=(n // tl,),
        in_specs=[pl.BlockSpec((tl, 128), lambda i: (i, 0)),
                  pl.BlockSpec((128, hid), lambda i: (0, 0)),
                  pl.BlockSpec((1, hid), lambda i: (0, 0)),
                  pl.BlockSpec((hid, hid), lambda i: (0, 0)),
                  pl.BlockSpec((1, hid), lambda i: (0, 0)),
                  pl.BlockSpec((1, hid, oc), lambda i: (i // half, 0, 0)),
                  pl.BlockSpec((1, 1, oc), lambda i: (i // half, 0, 0))],
        out_specs=pl.BlockSpec((tl, oc), lambda i: (i, 0)),
        out_shape=jax.ShapeDtypeStruct((n, oc), F32),
        compiler_params=_cparams(("parallel",)),
        name="hyena_filter",
    )(feats, w1, p['hy_b1'].reshape(1, hid), p['hy_w2'], p['hy_b2'].reshape(1, hid), w3, dec)


def _dft_first_kernel(g_ref, x_ref, o_ref, *, precise):
    if precise:
        o_ref[0] = _dot_3pass(g_ref[...], x_ref[0])
    else:
        y = jnp.dot(g_ref[...].astype(BF16), x_ref[0].astype(BF16), preferred_element_type=F32)
        o_ref[0] = y.astype(o_ref.dtype)


def dft_first(g, x, *, n_out, offset=0, tn=8192, precise=False):
    _, r, lanes = x.shape
    m = g.shape[0]
    return pl.pallas_call(
        functools.partial(_dft_first_kernel, precise=precise),
        grid=(n_out, lanes // tn),
        in_specs=[pl.BlockSpec((m, r), lambda p, j: (0, 0)),
                  pl.BlockSpec((1, r, tn), lambda p, j: (offset + p, 0, j))],
        out_specs=pl.BlockSpec((1, m, tn), lambda p, j: (p, 0, j)),
        out_shape=jax.ShapeDtypeStruct((n_out, m, lanes), F32 if precise else BF16),
        compiler_params=_cparams(("parallel", "parallel")),
        name="dft_first",
    )(g, x)


def _dft_filter_mid_kernel(g_ref, y_ref, o_ref, *, scale):
    for i in range(g_ref.shape[0]):
        y = jnp.concatenate([y_ref[0, i], y_ref[1, i]], axis=0)
        o_ref[i] = scale * _dot_3pass(g_ref[i], y)


def dft_filter_mid(g2, y1):
    _, na, nb, oc = y1.shape
    kg = HY_KG // 2
    return pl.pallas_call(
        functools.partial(_dft_filter_mid_kernel, scale=1.0 / HY_N),
        grid=(na // kg,),
        in_specs=[pl.BlockSpec((kg, 2 * nb, 2 * nb), lambda g: (g, 0, 0)),
                  pl.BlockSpec((2, kg, nb, oc), lambda g: (0, g, 0, 0))],
        out_specs=pl.BlockSpec((kg, 2 * nb, oc), lambda g: (g, 0, 0)),
        out_shape=jax.ShapeDtypeStruct((na, 2 * nb, oc), F32),
        compiler_params=_cparams(("parallel",)),
        name="dft_filter_mid",
    )(g2, y1)


def _dft_mid_kernel(gf_ref, gi_ref, y_ref, h_ref, o_ref):
    nb = y_ref.shape[3]
    for i in range(gf_ref.shape[0]):
        y = jnp.concatenate([y_ref[0, 0, i], y_ref[0, 1, i]], axis=0)
        x = jnp.dot(gf_ref[i].astype(BF16), y, preferred_element_type=F32)
        xr, xi = x[:nb], x[nb:]
        hr, hi = h_ref[i, :nb], h_ref[i, nb:]
        prod = jnp.concatenate([xr * hr - xi * hi, xr * hi + xi * hr], axis=0).astype(BF16)
        u = jnp.dot(gi_ref[i].astype(BF16), prod, preferred_element_type=F32)
        o_ref[0, 0, i] = u[:nb].astype(o_ref.dtype)
        o_ref[0, 1, i] = u[nb:].astype(o_ref.dtype)


def dft_mid(g2, g2_inv, y1, h, order):
    npair, _, na, nb, c = y1.shape
    kg = HY_KG
    return pl.pallas_call(
        _dft_mid_kernel,
        grid=(npair, na // kg),
        in_specs=[pl.BlockSpec((kg, 2 * nb, 2 * nb), lambda p, g: (g, 0, 0)),
                  pl.BlockSpec((kg, 2 * nb, 2 * nb), lambda p, g: (g, 0, 0)),
                  pl.BlockSpec((1, 2, kg, nb, c), lambda p, g: (p, 0, g, 0, 0)),
                  pl.BlockSpec((kg, 2 * nb, c), lambda p, g: (g, 0, order))],
        out_specs=pl.BlockSpec((1, 2, kg, nb, c), lambda p, g: (p, 0, g, 0, 0)),
        out_shape=jax.ShapeDtypeStruct(y1.shape, BF16),
        compiler_params=_cparams(("parallel", "parallel")),
        name="dft_mid",
    )(g2, g2_inv, y1, h)


def _dft_last_kernel(g_ref, u_ref, z_ref, gate_ref, skip_ref, o_ref):
    conv = jnp.dot(g_ref[...].astype(BF16), u_ref[0], preferred_element_type=F32)
    o_ref[0] = gate_ref[0] * (conv + skip_ref[...] * z_ref[0])


def dft_last(g1_inv, u, z, z_off, gate, gate_off, skip_lanes, *, tn=8192):
    npair, m2, lanes = u.shape
    r = g1_inv.shape[0]
    return pl.pallas_call(
        _dft_last_kernel,
        grid=(npair, lanes // tn),
        in_specs=[pl.BlockSpec((r, m2), lambda p, j: (0, 0)),
                  pl.BlockSpec((1, m2, tn), lambda p, j: (p, 0, j)),
                  pl.BlockSpec((1, r, tn), lambda p, j: (z_off + p, 0, j)),
                  pl.BlockSpec((1, r, tn), lambda p, j: (gate_off + p, 0, j)),
                  pl.BlockSpec((1, tn), lambda p, j: (0, 0))],
        out_specs=pl.BlockSpec((1, r, tn), lambda p, j: (p, 0, j)),
        out_shape=jax.ShapeDtypeStruct((npair, r, lanes), F32),
        compiler_params=_cparams(("parallel", "parallel")),
        name="dft_last",
    )(g1_inv, u, z, gate, skip_lanes)


def hyena_long(proj, p):
    bsz, l, _ = proj.shape
    assert 2 * l == HY_N and bsz % 2 == 0
    c = HY_WIDTH
    npair = bsz // 2
    lanes = HY_NB * c
    g1, g1_inv, g1_real, g2, g2_inv = _dft_tables()
    g1_bf, g1_inv_bf = jnp.asarray(g1), jnp.asarray(g1_inv)
    g2_bf, g2_inv_bf = jnp.asarray(g2), jnp.asarray(g2_inv)

    buf = hyena_filter_buffer(l, p)
    y1h = dft_first(jnp.asarray(g1_real), buf.reshape(1, HY_NA, HY_NB * HY_ORDER * c), n_out=1, precise=True)
    hspec = dft_filter_mid(jnp.asarray(g2), y1h.reshape(2, HY_NA, HY_NB, HY_ORDER * c))

    vx = short_conv_groups(proj, P_HY, p['hy_conv_w'], p['hy_conv_b'], group=c, tc=128, silu=False,
                           name="hyena_short_conv")
    vx = vx.reshape(3 * npair, HY_NA, lanes)
    z, z_off = vx, 0
    for n in range(HY_ORDER):
        y1 = dft_first(g1_bf, z, n_out=npair, offset=z_off)
        u = dft_mid(g2_bf, g2_inv_bf, y1.reshape(npair, 2, HY_NA, HY_NB, c), hspec, n)
        skip_lanes = jnp.tile(p['hy_skip'][n], 8192 // c).reshape(1, 8192)
        z = dft_last(g1_inv_bf, u.reshape(npair, 2 * HY_NA, lanes), z, z_off, vx, (n + 1) * npair, skip_lanes)
        z_off = 0
    return z.reshape(bsz, l, c)


def _log_sigmoid(x):
    return jnp.minimum(x, 0.0) - jnp.log1p(jnp.exp(-jnp.abs(x)))


def _mlstm_kernel(*refs, nc_ctx, direction):
    ins, bias_ref, outs, (c_ref, m_ref) = refs[:8], refs[8], refs[9:11], refs[11:13]
    d = ML_HEAD_DIM
    t = ML_CHUNK
    nh = ML_HEADS
    step = pl.program_id(1)

    @pl.when(step == 0)
    def _():
        c_ref[...] = jnp.zeros_like(c_ref)
        m_ref[...] = jnp.zeros_like(m_ref)

    i0 = S_GATES + direction * nh
    f0 = S_GATES + (2 + direction) * nh

    def chunk(q_ref, k_ref, v_ref, sm_ref, o_ref):
        g = sm_ref[0] + bias_ref[...]
        gt = g.T
        lf_cols = _log_sigmoid(g)
        lf_rows = _log_sigmoid(gt)
        r = lax.broadcasted_iota(jnp.int32, (t, t), 0)
        cidx = lax.broadcasted_iota(jnp.int32, (t, t), 1)
        mask = cidx <= r if direction == 0 else cidx >= r
        mask_t = r <= cidx if direction == 0 else r >= cidx
        tri = mask.astype(F32)
        tri_t = mask_t.astype(F32)
        lane = lax.broadcasted_iota(jnp.int32, (t, d), 1)
        ones_col = jnp.where(lane == 0, 1.0, 0.0).astype(BF16)
        for h in range(nh):
            sl = slice(h * d, (h + 1) * d)
            lf_row = lf_rows[f0 + h:f0 + h + 1, :]
            bc = jnp.sum(tri * lf_row, axis=1, keepdims=True)
            br = jnp.sum(tri_t * lf_cols[:, f0 + h:f0 + h + 1], axis=0, keepdims=True)
            lir = gt[i0 + h:i0 + h + 1, :]
            m_prev = m_ref[h, 0:1, 0:1]
            dmat = jnp.where(mask, bc - br + lir, -jnp.inf)
            inter = bc + m_prev
            m_t = jnp.maximum(inter, jnp.max(dmat, axis=1, keepdims=True))
            dexp = jnp.exp(dmat - m_t)
            inter_w = jnp.exp(inter - m_t)
            qh = (q_ref[0, 0, :, sl] * (d ** -0.5)).astype(BF16)
            kt = k_ref[0, 0, :, sl].T
            v_ext = jnp.concatenate([v_ref[0, :, sl].astype(BF16), ones_col], axis=1)
            s = jnp.dot(qh, kt.astype(BF16), preferred_element_type=F32) * dexp
            c_ext = c_ref[h]
            acc = inter_w * jnp.dot(qh, c_ext.astype(BF16), preferred_element_type=F32)
            acc = acc + jnp.dot(s.astype(BF16), v_ext, preferred_element_type=F32)
            den = jnp.maximum(jnp.abs(acc[:, d:d + 1]), jnp.exp(-m_t))
            o_ref[0, :, sl] = acc[:, :d] / den
            b_last = jnp.sum(lf_row, axis=1, keepdims=True)
            g_row = b_last - br + lir
            m_new = jnp.maximum(b_last + m_prev, jnp.max(g_row, axis=1, keepdims=True))
            a = jnp.exp(b_last + m_prev - m_new)
            w_row = jnp.exp(g_row - m_new)
            c_ref[h] = a * c_ext + jnp.dot((kt * w_row).astype(BF16), v_ext, preferred_element_type=F32)
            m_ref[h] = jnp.broadcast_to(m_new, m_ref.shape[1:])

    @pl.when(step < nc_ctx)
    def _():
        chunk(*ins[:4], outs[0])

    @pl.when(step >= nc_ctx)
    def _():
        chunk(*ins[4:], outs[1])


def mlstm_bidir(qk_c, proj_c, small_c, qk_l, proj_l, small_l, gate_bias):
    _, bsz, lc, w = qk_c.shape
    ll = qk_l.shape[2]
    t, nh = ML_CHUNK, ML_HEADS
    nc_c, nc_l = lc // t, ll // t
    bias = jnp.zeros((1, N_SMALL), F32).at[0, S_GATES:S_GATES + ML_GATES].set(gate_bias)

    def idx(dd, step_off, nc):
        def f(g):
            cc = jnp.clip(g - step_off, 0, nc - 1)
            return cc + dd * (nc - 1 - 2 * cc)
        return f

    vcol = P_MLV // w

    def seg_specs(ix):
        return [pl.BlockSpec((1, 1, t, w), lambda b, g: (0, b, ix(g), 0)),
                pl.BlockSpec((1, 1, t, w), lambda b, g: (1, b, ix(g), 0)),
                pl.BlockSpec((1, t, w), lambda b, g: (b, ix(g), vcol)),
                pl.BlockSpec((1, t, N_SMALL), lambda b, g: (b, ix(g), 0))]

    def out_spec(ix):
        return pl.BlockSpec((1, t, w), lambda b, g: (b, ix(g), 0))

    h_c, h_l = [], []
    for dd in range(2):
        ic, il = idx(dd, 0, nc_c), idx(dd, nc_c, nc_l)
        hc, hl = pl.pallas_call(
            functools.partial(_mlstm_kernel, nc_ctx=nc_c, direction=dd),
            grid=(bsz, nc_c + nc_l),
            in_specs=seg_specs(ic) + seg_specs(il) + [pl.BlockSpec((1, N_SMALL), lambda b, g: (0, 0))],
            out_specs=[out_spec(ic), out_spec(il)],
            out_shape=[jax.ShapeDtypeStruct((bsz, lc, w), F32), jax.ShapeDtypeStruct((bsz, ll, w), F32)],
            scratch_shapes=[pltpu.VMEM((nh, ML_HEAD_DIM, 2 * ML_HEAD_DIM), F32), pltpu.VMEM((nh, 8, 128), F32)],
            compiler_params=_cparams(("parallel", "arbitrary")),
            name="mlstm_fwd" if dd == 0 else "mlstm_bwd",
        )(qk_c, qk_c, proj_c, small_c, qk_l, qk_l, proj_l, small_l, bias)
        h_c.append(hc)
        h_l.append(hl)
    return tuple(h_c), tuple(h_l)


def _rms(x):
    return x * lax.rsqrt(jnp.mean(x * x, axis=-1, keepdims=True) + EPS)


def _dot_split(x, e):
    hi = x.astype(BF16)
    lo = (x - hi.astype(F32)).astype(BF16)
    return jnp.dot(hi, e, preferred_element_type=F32) + jnp.dot(lo, e, preferred_element_type=F32)


def _swap_rope_halves(x):
    w = x.shape[1]
    lane = lax.broadcasted_iota(jnp.int32, x.shape, 1) % MLA_ROPE
    return jnp.where(lane < MLA_ROPE // 2, pltpu.roll(x, w - MLA_ROPE // 2, axis=1),
                     pltpu.roll(x, MLA_ROPE // 2, axis=1))


def _mla_prep_kernel(pq_ref, pkv_ref, sm_ref, gqa_ref, gkva_ref, wq_ref, wkv_ref, gqn_ref, gqr_ref, gkn_ref,
                     gkr_ref, en_ref, er_ref, ent_ref, ert_ref, *rest, use_rope):
    if use_rope:
        ccq_ref, ssq_ref, cck_ref, ssk_ref, qt_ref, k_ref, vt_ref = rest
    else:
        qt_ref, k_ref, vt_ref = rest
    nh, dn, dr = MLA_HEADS, MLA_NOPE, MLA_ROPE
    qa = (_rms(pq_ref[0].astype(F32)) * gqa_ref[...]).astype(BF16)
    kva = (_rms(pkv_ref[0].astype(F32)) * gkva_ref[...]).astype(BF16)
    q_raw = jnp.dot(qa, wq_ref[...], preferred_element_type=F32)
    kv_raw = jnp.dot(kva, wkv_ref[...], preferred_element_type=F32)
    inv_d = 1.0 / MLA_QK

    qn, qr = q_raw[:, :nh * dn], q_raw[:, nh * dn:]
    ss = _dot_split(qn * qn, en_ref[...]) + _dot_split(qr * qr, er_ref[...])
    rs = lax.rsqrt(ss * inv_d + EPS)
    qn = qn * _dot_split(rs, ent_ref[...]) * gqn_ref[...]
    qr = qr * _dot_split(rs, ert_ref[...]) * gqr_ref[...]
    if use_rope:
        qr = qr * ccq_ref[...] + _swap_rope_halves(qr) * ssq_ref[...]
    qrt = qr.T
    tl = qn.shape[0]
    for h in range(nh):
        qt_ref[0, h, 0:dn, :] = qn[:, h * dn:(h + 1) * dn].T.astype(BF16)
        qt_ref[0, h, dn:dn + dr, :] = qrt[h * dr:(h + 1) * dr].astype(BF16)
        qt_ref[0, h, dn + dr:, :] = jnp.zeros((QK_PAD - dn - dr, tl), BF16)

    kn, vv = kv_raw[:, :nh * dn], kv_raw[:, nh * dn:]
    lane = lax.broadcasted_iota(jnp.int32, sm_ref.shape[1:], 1)
    kr = jnp.where(lane < dr, sm_ref[0], 0.0)
    ssk = _dot_split(kn * kn, en_ref[...]) + jnp.sum(kr * kr, axis=-1, keepdims=True)
    rsk = lax.rsqrt(ssk * inv_d + EPS)
    kn = kn * _dot_split(rsk, ent_ref[...]) * gkn_ref[...]
    kr = kr * gkr_ref[...]
    if use_rope:
        kr = kr * cck_ref[...] + _swap_rope_halves(kr) * ssk_ref[...]
    for h in range(nh):
        k_ref[0, h, :, 0:dn] = kn[:, h * dn:(h + 1) * dn].astype(BF16)
        k_ref[0, h, :, dn:] = (kr * rsk[:, h:h + 1]).astype(BF16)
        vt_ref[0, h] = vv[:, h * MLA_V:(h + 1) * MLA_V].T.astype(BF16)


@functools.lru_cache(maxsize=None)
def _head_selectors():
    en = np.kron(np.eye(MLA_HEADS), np.ones((MLA_NOPE, 1)))
    er = np.kron(np.eye(MLA_HEADS), np.ones((MLA_ROPE, 1)))
    pad = lambda m: np.pad(m, ((0, 0), (0, 128 - m.shape[1]))).astype(np.float32)
    return pad(en), pad(er)


def _rope_tables(n_tokens):
    rows = n_tokens // GRID_W
    row = jnp.repeat(jnp.arange(rows, dtype=F32), GRID_W)
    col = jnp.tile(jnp.arange(GRID_W, dtype=F32), rows)
    freqs = ROPE_BASE ** (-jnp.arange(ROPE_PAIRS_PER_AXIS, dtype=F32) / ROPE_PAIRS_PER_AXIS)
    ang = jnp.concatenate([row[:, None] * freqs, col[:, None] * freqs], axis=-1)
    cos, sin = jnp.cos(ang), jnp.sin(ang)
    cc = jnp.concatenate([cos, cos], axis=-1)
    ss = jnp.concatenate([-sin, sin], axis=-1)
    zero = jnp.zeros((n_tokens, 128 - MLA_ROPE), F32)
    return (jnp.tile(cc, (1, MLA_HEADS)), jnp.tile(ss, (1, MLA_HEADS)),
            jnp.concatenate([cc, zero], axis=-1), jnp.concatenate([ss, zero], axis=-1))


def mla_prep(proj, small, p, rope_tables, *, tl):
    b, l, _ = proj.shape
    nh = MLA_HEADS
    en, er = _head_selectors()
    en_b, er_b = jnp.asarray(en, BF16), jnp.asarray(er, BF16)
    ent_b, ert_b = jnp.asarray(en.T.copy(), BF16), jnp.asarray(er.T.copy(), BF16)
    const = lambda shape: pl.BlockSpec(shape, lambda bi, i: (0,) * len(shape))
    nq = nh * MLA_QK
    nkv = nh * (MLA_NOPE + MLA_V)
    in_specs = [pl.BlockSpec((1, tl, MLA_Q_LORA), lambda bi, i: (bi, i, P_CQ // MLA_Q_LORA)),
                pl.BlockSpec((1, tl, MLA_KV_LORA), lambda bi, i: (bi, i, P_CKV // MLA_KV_LORA)),
                pl.BlockSpec((1, tl, N_SMALL), lambda bi, i: (bi, i, 0)),
                const((1, MLA_Q_LORA)), const((1, MLA_KV_LORA)), const((MLA_Q_LORA, nq)), const((MLA_KV_LORA, nkv)),
                const((1, nh * MLA_NOPE)), const((1, nh * MLA_ROPE)), const((1, nh * MLA_NOPE)), const((1, 128)),
                const(en.shape), const(er.shape), const(en.T.shape), const(er.T.shape)]
    operands = [proj, proj, small, p['gqa'], p['gkva'], p['wq'], p['wkv'], p['gqn'], p['gqr'], p['gkn'], p['gkr'],
                en_b, er_b, ent_b, ert_b]
    use_rope = rope_tables is not None
    if use_rope:
        in_specs += [pl.BlockSpec((tl, nh * MLA_ROPE), lambda bi, i: (i, 0)),
                     pl.BlockSpec((tl, nh * MLA_ROPE), lambda bi, i: (i, 0)),
                     pl.BlockSpec((tl, 128), lambda bi, i: (i, 0)),
                     pl.BlockSpec((tl, 128), lambda bi, i: (i, 0))]
        operands += list(rope_tables)
    return pl.pallas_call(
        functools.partial(_mla_prep_kernel, use_rope=use_rope),
        grid=(b, l // tl),
        in_specs=in_specs,
        out_specs=[pl.BlockSpec((1, nh, QK_PAD, tl), lambda bi, i: (bi, 0, 0, i)),
                   pl.BlockSpec((1, nh, tl, QK_PAD), lambda bi, i: (bi, 0, i, 0)),
                   pl.BlockSpec((1, nh, MLA_V, tl), lambda bi, i: (bi, 0, 0, i))],
        out_shape=[jax.ShapeDtypeStruct((b, nh, QK_PAD, l), BF16),
                   jax.ShapeDtypeStruct((b, nh, l, QK_PAD), BF16),
                   jax.ShapeDtypeStruct((b, nh, MLA_V, l), BF16)],
        compiler_params=_cparams(("parallel", "parallel")),
        name="mla_prep",
    )(*operands)


def _combine_out_kernel(a_ref, y_ref, hf_ref, hb_ref, o_ref, g_ref, w_ref, r_ref, gt_ref, out_ref, cat_ref):
    @pl.when(pl.program_id(2) == 0)
    def _():
        g = g_ref[...]
        cat_ref[:, :MLA_WIDTH] = (_rms(a_ref[0]) * g[:, :MLA_WIDTH]).astype(BF16)
        hy0 = MLA_WIDTH
        cat_ref[:, hy0:hy0 + HY_WIDTH] = (_rms(y_ref[0]) * g[:, hy0:hy0 + HY_WIDTH]).astype(BF16)
        ml0 = MLA_WIDTH + HY_WIDTH
        hh = hf_ref[0] + hb_ref[0]
        og = 0.5 * jnp.tanh(0.5 * o_ref[0].astype(F32)) + 0.5
        for hd in range(ML_HEADS):
            sl = slice(hd * ML_HEAD_DIM, (hd + 1) * ML_HEAD_DIM)
            gl = g[:, ml0 + hd * ML_HEAD_DIM:ml0 + (hd + 1) * ML_HEAD_DIM]
            cat_ref[:, ml0 + hd * ML_HEAD_DIM:ml0 + (hd + 1) * ML_HEAD_DIM] = (
                _rms(hh[:, sl]) * gl * og[:, sl]).astype(BF16)

    y = jnp.dot(cat_ref[...], w_ref[...], preferred_element_type=F32)
    out_ref[0] = r_ref[0] + gt_ref[0] * y


def combine_out_proj(a, y, h2, proj, g_mix, w_out, res, gate, *, tm, tn, name="out_proj"):
    b, s, d = res.shape
    return pl.pallas_call(
        _combine_out_kernel,
        grid=(b, s // tm, d // tn),
        in_specs=[pl.BlockSpec((1, tm, MLA_WIDTH), lambda bi, i, j: (bi, i, 0)),
                  pl.BlockSpec((1, tm, HY_WIDTH), lambda bi, i, j: (bi, i, 0)),
                  pl.BlockSpec((1, tm, ML_WIDTH), lambda bi, i, j: (bi, i, 0)),
                  pl.BlockSpec((1, tm, ML_WIDTH), lambda bi, i, j: (bi, i, 0)),
                  pl.BlockSpec((1, tm, ML_WIDTH), lambda bi, i, j: (bi, i, P_MLO // ML_WIDTH)),
                  pl.BlockSpec((1, MIX_WIDTH), lambda bi, i, j: (0, 0)),
                  pl.BlockSpec((MIX_WIDTH, tn), lambda bi, i, j: (0, j)),
                  pl.BlockSpec((1, tm, tn), lambda bi, i, j: (bi, i, j)),
                  pl.BlockSpec((1, 1, tn), lambda bi, i, j: (bi, 0, j))],
        out_specs=pl.BlockSpec((1, tm, tn), lambda bi, i, j: (bi, i, j)),
        out_shape=jax.ShapeDtypeStruct((b, s, d), F32),
        scratch_shapes=[pltpu.VMEM((tm, MIX_WIDTH), BF16)],
        compiler_params=_cparams(("parallel", "parallel", "arbitrary")),
        name=name,
    )(a, y, h2[0], h2[1], proj, g_mix.reshape(1, MIX_WIDTH), w_out, res, gate)


@functools.lru_cache(maxsize=None)
def _direct_dft_tables(l):
    n = 2 * l
    k = np.arange(n)[:, None]
    t = np.arange(l)[None, :]
    ang = 2.0 * np.pi * ((k * t) % n) / n
    c, s = np.cos(ang), np.sin(ang)
    gf = np.block([[c, s], [-s, c]])
    gi = gf.T.copy()
    t_full = np.arange(n)[None, :]
    ang = 2.0 * np.pi * ((k * t_full) % n) / n
    g_real = np.concatenate([np.cos(ang), -np.sin(ang)], axis=0)
    return gf.astype(np.float32), gi.astype(np.float32), g_real.astype(np.float32)


def _direct_conv_kernel(gf_ref, gi_ref, z_ref, gate_ref, h_ref, skip_ref, o_ref, *, scale):
    z = z_ref[0]
    x = jnp.dot(gf_ref[...].astype(BF16), z.astype(BF16), preferred_element_type=F32)
    n = x.shape[0] // 2
    xr, xi = x[:n], x[n:]
    hr, hi = h_ref[:n] * scale, h_ref[n:] * scale
    prod = jnp.concatenate([xr * hr - xi * hi, xr * hi + xi * hr], axis=0).astype(BF16)
    conv = jnp.dot(gi_ref[...].astype(BF16), prod, preferred_element_type=F32)
    o_ref[0] = gate_ref[0] * (conv + skip_ref[...] * z)


def hyena_short(proj, p):
    bsz, l, _ = proj.shape
    c = HY_WIDTH
    npair = bsz // 2
    n = 2 * l
    gf, gi, g_real = _direct_dft_tables(l)
    buf = hyena_filter_buffer(l, p, tl=l)
    hspec = dft_first(jnp.asarray(g_real), buf.reshape(1, n, HY_ORDER * c), n_out=1, tn=HY_ORDER * c,
                      precise=True)[0]
    vx = short_conv_groups(proj, P_HY, p['hy_conv_w'], p['hy_conv_b'], group=c, tc=128, silu=False,
                           name="hyena_short_conv_ctx")
    vx = vx.reshape(3 * npair, 2 * l, c)
    z = vx

    def rows_at(off):
        return lambda q: (off + q, 0, 0)

    def cols_at(col):
        return lambda q: (0, col)

    for order in range(HY_ORDER):
        z = pl.pallas_call(
            functools.partial(_direct_conv_kernel, scale=1.0 / n),
            grid=(npair,),
            in_specs=[pl.BlockSpec(gf.shape, lambda q: (0, 0)),
                      pl.BlockSpec(gi.shape, lambda q: (0, 0)),
                      pl.BlockSpec((1, 2 * l, c), rows_at(0)),
                      pl.BlockSpec((1, 2 * l, c), rows_at((order + 1) * npair)),
                      pl.BlockSpec((2 * n, c), cols_at(order)),
                      pl.BlockSpec((1, c), lambda q: (0, 0))],
            out_specs=pl.BlockSpec((1, 2 * l, c), lambda q: (q, 0, 0)),
            out_shape=jax.ShapeDtypeStruct((npair, 2 * l, c), F32),
            compiler_params=_cparams(("parallel",)),
            name="hyena_direct_conv",
        )(jnp.asarray(gf), jnp.asarray(gi), z, vx, hspec, p['hy_skip'][order].reshape(1, c))
    return z.reshape(bsz, l, c)


def _split_cols(w, sizes):
    out, start = [], 0
    for s in sizes:
        out.append(w[:, start:start + s])
        start += s
    return out


def _prep_layer_weights(l, w_in, mla_qa_norm, mla_kva_norm, mla_w_uq, mla_w_ukv, mla_q_norm, mla_k_norm):
    cq, ckv, kr, hy, mqk, mv, mo, gt = _split_cols(w_in[l], IN_SIZES)
    w_main = jnp.concatenate([cq, ckv, hy, mqk, mv, mo], axis=1).astype(BF16)
    pad = jnp.zeros((D_MODEL, N_SMALL - MLA_ROPE - ML_GATES), F32)
    w_small = jnp.concatenate([kr, gt, pad], axis=1).astype(BF16)
    nh = MLA_HEADS
    wq = mla_w_uq[l].reshape(MLA_Q_LORA, nh, MLA_QK)
    wq = jnp.concatenate([wq[:, :, :MLA_NOPE].reshape(MLA_Q_LORA, -1), wq[:, :, MLA_NOPE:].reshape(MLA_Q_LORA, -1)],
                         axis=1).astype(BF16)
    wkv = mla_w_ukv[l].reshape(MLA_KV_LORA, nh, MLA_NOPE + MLA_V)
    wkv = jnp.concatenate([wkv[:, :, :MLA_NOPE].reshape(MLA_KV_LORA, -1),
                           wkv[:, :, MLA_NOPE:].reshape(MLA_KV_LORA, -1)], axis=1).astype(BF16)
    q_scale = (MLA_QK ** -0.5) * math.log2(math.e)
    gq, gk = mla_q_norm[l] * q_scale, mla_k_norm[l]
    mla = {'gqa': mla_qa_norm[l].reshape(1, -1), 'gkva': mla_kva_norm[l].reshape(1, -1), 'wq': wq, 'wkv': wkv,
           'gqn': jnp.tile(gq[:MLA_NOPE], nh).reshape(1, -1), 'gqr': jnp.tile(gq[MLA_NOPE:], nh).reshape(1, -1),
           'gkn': jnp.tile(gk[:MLA_NOPE], nh).reshape(1, -1),
           'gkr': jnp.concatenate([gk[MLA_NOPE:], jnp.zeros((128 - MLA_ROPE,), F32)]).reshape(1, -1)}
    return w_main, w_small, mla


def kernel(x, c, ctx, c_ctx, ada_w, ada_b, norm1_g, norm2_g, w_in, mla_qa_norm, mla_kva_norm, mla_w_uq,
           mla_w_ukv, mla_q_norm, mla_k_norm, hy_conv_w, hy_conv_b, hy_w1, hy_b1, hy_w2, hy_b2, hy_w3,
           hy_decay, hy_skip, ml_conv_w, ml_conv_b, ml_gate_b, mix_norm_g, w_out, ffn_w1, ffn_w2):
    B, S, D = x.shape
    LC = ctx.shape[1]
    rope_tables = _rope_tables(S)

    silu_rows = jnp.concatenate([jax.nn.silu(c), jax.nn.silu(c_ctx)[None], jnp.zeros((8 - B - 1, D), F32)], axis=0)
    mods = ada_modulation(silu_rows.astype(BF16), ada_w, ada_b)

    for l in range(DEPTH):
        need_ctx = l < DEPTH - 1
        p = {'hy_conv_w': hy_conv_w[l], 'hy_conv_b': hy_conv_b[l], 'hy_w1': hy_w1[l], 'hy_b1': hy_b1[l],
             'hy_w2': hy_w2[l], 'hy_b2': hy_b2[l], 'hy_w3': hy_w3[l], 'hy_decay': hy_decay[l],
             'hy_skip': hy_skip[l], 'ml_conv_w': ml_conv_w[l], 'ml_conv_b': ml_conv_b[l]}
        w_main, w_small, mla = _prep_layer_weights(l, w_in, mla_qa_norm, mla_kva_norm, mla_w_uq, mla_w_ukv,
                                                   mla_q_norm, mla_k_norm)
        w_out_l = w_out[l].astype(BF16)
        w1_l = ffn_w1[l].astype(BF16)
        w2_l = ffn_w2[l].astype(BF16)

        mod_l = [m[:, None, :] for m in jnp.split(mods[l, :B], 6, axis=-1)]
        mod_c = [jnp.broadcast_to(m[:, None, :], (B, 1, D)) for m in jnp.split(mods[l, B:B + 1], 6, axis=-1)]

        proj_l, small_l = in_proj(x, norm1_g[l], mod_l[1], mod_l[0], w_main, w_small, tm=1024, tn=IN_PROJ_TN)
        proj_c, small_c = in_proj(ctx, norm1_g[l], mod_c[1], mod_c[0], w_main, w_small, tm=LC, tn=IN_PROJ_TN,
                                  name="in_proj_ctx")

        qt_l, k_l, vt_l = mla_prep(proj_l, small_l, mla, rope_tables, tl=512)
        qt_c, k_c, vt_c = mla_prep(proj_c, small_c, mla, None, tl=LC)
        a_l = attention(qt_l, [(k_c, vt_c), (k_l, vt_l)], tq=ATT_TQ)
        y_l = hyena_long(proj_l, p)

        def mlstm_qk(proj):
            return short_conv_groups(proj, P_MLQK, p['ml_conv_w'], p['ml_conv_b'], group=ML_WIDTH, tc=128,
                                     silu=True, name="mlstm_short_conv")

        h_c, h_l = mlstm_bidir(mlstm_qk(proj_c), proj_c, small_c, mlstm_qk(proj_l), proj_l, small_l, ml_gate_b[l])
        x = combine_out_proj(a_l, y_l, h_l, proj_l, mix_norm_g[l], w_out_l, x, mod_l[2], tm=1024, tn=MM_TN)
        hid = norm_swiglu(x, norm2_g[l], mod_l[4], mod_l[3], w1_l, tm=1024, tn=MM_TN, name="ffn_up")
        x = mm_residual(hid, w2_l, x, mod_l[5], tm=1024, tn=MM_TN, name="ffn_down")

        if need_ctx:
            a_c = attention(qt_c, [(k_c, vt_c)], tq=LC)
            y_c = hyena_short(proj_c, p)
            ctx = combine_out_proj(a_c, y_c, h_c, proj_c, mix_norm_g[l], w_out_l, ctx, mod_c[2], tm=LC, tn=MM_TN,
                                   name="out_proj_ctx")
            hid = norm_swiglu(ctx, norm2_g[l], mod_c[4], mod_c[3], w1_l, tm=LC, tn=MM_TN, name="ffn_up_ctx")
            ctx = mm_residual(hid, w2_l, ctx, mod_c[5], tm=LC, tn=MM_TN, name="ffn_down_ctx")
    return x
```

```python
import functools
import math

import jax
import jax.numpy as jnp
import numpy as np
from jax import lax
from jax.experimental import pallas as pl
from jax.experimental.pallas import tpu as pltpu

D_MODEL = 2048
DEPTH = 2
GRID_W = 64
EPS = 1e-6

MLA_HEADS = 8
MLA_NOPE = 128
MLA_ROPE = 64
MLA_QK = MLA_NOPE + MLA_ROPE
MLA_V = 128
MLA_Q_LORA = 512
MLA_KV_LORA = 512
MLA_WIDTH = MLA_HEADS * MLA_V
ROPE_BASE = 10000.0
ROPE_PAIRS_PER_AXIS = MLA_ROPE // 4
QK_PAD = 256

HY_WIDTH = 512
HY_ORDER = 2
HY_BANDS = 16
HY_SHIFT = 0.05
SHORT_CONV = 3

ML_HEADS = 4
ML_HEAD_DIM = 128
ML_WIDTH = ML_HEADS * ML_HEAD_DIM
ML_CHUNK = 128
ML_GATES = 4 * ML_HEADS

MIX_WIDTH = MLA_WIDTH + HY_WIDTH + ML_WIDTH
FFN_HIDDEN = ((8 * D_MODEL // 3 + 255) // 256) * 256

IN_SIZES = (MLA_Q_LORA, MLA_KV_LORA, MLA_ROPE, 3 * HY_WIDTH, 2 * ML_WIDTH, ML_WIDTH, ML_WIDTH, ML_GATES)
N_IN = sum(IN_SIZES)

P_CQ, P_CKV, P_HY, P_MLQK, P_MLV, P_MLO = 0, 512, 1024, 2560, 3584, 4096
N_MAIN = 4608
N_SMALL = 128
S_GATES = MLA_ROPE

VMEM_LIMIT_BYTES = 56 * 1024 * 1024

BF16 = jnp.bfloat16
F32 = jnp.float32


def _cparams(sem):
    return pltpu.CompilerParams(dimension_semantics=sem, vmem_limit_bytes=VMEM_LIMIT_BYTES)


def _ada_kernel(s_ref, w_ref, b_ref, o_ref):
    w = w_ref[0].astype(BF16)
    o_ref[0] = jnp.dot(s_ref[...], w, preferred_element_type=F32) + b_ref[0]


def ada_modulation(silu_rows, ada_w, ada_b, tn=1024):
    depth, d, n = ada_w.shape
    return pl.pallas_call(
        _ada_kernel,
        grid=(depth, n // tn),
        in_specs=[pl.BlockSpec((8, d), lambda l, j: (0, 0)),
                  pl.BlockSpec((1, d, tn), lambda l, j: (l, 0, j)),
                  pl.BlockSpec((1, 1, tn), lambda l, j: (l, 0, j))],
        out_specs=pl.BlockSpec((1, 8, tn), lambda l, j: (l, 0, j)),
        out_shape=jax.ShapeDtypeStruct((depth, 8, n), F32),
        compiler_params=_cparams(("parallel", "parallel")),
        name="ada_modulation",
    )(silu_rows, ada_w, ada_b.reshape(depth, 1, n))


MM_TN = 512
IN_PROJ_TN = 1536


def _normed(x, g, sc, sh):
    y = x * lax.rsqrt(jnp.mean(x * x, axis=-1, keepdims=True) + EPS) * g
    return y * (1.0 + sc) + sh


def _in_proj_kernel(x_ref, g_ref, sc_ref, sh_ref, w_ref, ws_ref, o_ref, os_ref, xn_ref):
    @pl.when(pl.program_id(2) == 0)
    def _():
        xn = _normed(x_ref[0], g_ref[...], sc_ref[0], sh_ref[0]).astype(BF16)
        xn_ref[...] = xn
        os_ref[0] = jnp.dot(xn, ws_ref[...], preferred_element_type=F32)

    o_ref[0] = jnp.dot(xn_ref[...], w_ref[...], preferred_element_type=F32).astype(o_ref.dtype)


def in_proj(x, g, sc, sh, w_main, w_small, *, tm, tn, name="in_proj"):
    b, s, k = x.shape
    n = w_main.shape[1]
    ns = w_small.shape[1]
    return pl.pallas_call(
        _in_proj_kernel,
        grid=(b, s // tm, n // tn),
        in_specs=[pl.BlockSpec((1, tm, k), lambda bi, i, j: (bi, i, 0)),
                  pl.BlockSpec((1, k), lambda bi, i, j: (0, 0)),
                  pl.BlockSpec((1, 1, k), lambda bi, i, j: (bi, 0, 0)),
                  pl.BlockSpec((1, 1, k), lambda bi, i, j: (bi, 0, 0)),
                  pl.BlockSpec((k, tn), lambda bi, i, j: (0, j)),
                  pl.BlockSpec((k, ns), lambda bi, i, j: (0, 0))],
        out_specs=[pl.BlockSpec((1, tm, tn), lambda bi, i, j: (bi, i, j)),
                   pl.BlockSpec((1, tm, ns), lambda bi, i, j: (bi, i, 0))],
        out_shape=[jax.ShapeDtypeStruct((b, s, n), BF16), jax.ShapeDtypeStruct((b, s, ns), F32)],
        scratch_shapes=[pltpu.VMEM((tm, k), BF16)],
        compiler_params=_cparams(("parallel", "parallel", "arbitrary")),
        name=name,
    )(x, g.reshape(1, k), sc, sh, w_main, w_small)


def _norm_swiglu_kernel(x_ref, g_ref, sc_ref, sh_ref, wg_ref, wu_ref, o_ref, xn_ref):
    @pl.when(pl.program_id(2) == 0)
    def _():
        xn_ref[...] = _normed(x_ref[0], g_ref[...], sc_ref[0], sh_ref[0]).astype(BF16)

    xn = xn_ref[...]
    gate = jnp.dot(xn, wg_ref[...], preferred_element_type=F32)
    up = jnp.dot(xn, wu_ref[...], preferred_element_type=F32)
    o_ref[0] = (gate * (0.5 * jnp.tanh(0.5 * gate) + 0.5) * up).astype(o_ref.dtype)


def norm_swiglu(x, g, sc, sh, w1, *, tm, tn, name="norm_swiglu"):
    b, s, d = x.shape
    h = w1.shape[1] // 2
    nj = h // tn
    return pl.pallas_call(
        _norm_swiglu_kernel,
        grid=(b, s // tm, nj),
        in_specs=[pl.BlockSpec((1, tm, d), lambda bi, i, j: (bi, i, 0)),
                  pl.BlockSpec((1, d), lambda bi, i, j: (0, 0)),
                  pl.BlockSpec((1, 1, d), lambda bi, i, j: (bi, 0, 0)),
                  pl.BlockSpec((1, 1, d), lambda bi, i, j: (bi, 0, 0)),
                  pl.BlockSpec((d, tn), lambda bi, i, j: (0, j)),
                  pl.BlockSpec((d, tn), lambda bi, i, j: (0, j + nj))],
        out_specs=pl.BlockSpec((1, tm, tn), lambda bi, i, j: (bi, i, j)),
        out_shape=jax.ShapeDtypeStruct((b, s, h), BF16),
        scratch_shapes=[pltpu.VMEM((tm, d), BF16)],
        compiler_params=_cparams(("parallel", "parallel", "arbitrary")),
        name=name,
    )(x, g.reshape(1, d), sc, sh, w1, w1)


def _mm_res_kernel(a_ref, w_ref, r_ref, gt_ref, o_ref):
    y = jnp.dot(a_ref[0], w_ref[...], preferred_element_type=F32)
    o_ref[0] = r_ref[0] + gt_ref[0] * y


def mm_residual(a, w, res, gate, *, tm, tn, name="mm_residual"):
    b, s, k = a.shape
    n = w.shape[1]
    return pl.pallas_call(
        _mm_res_kernel,
        grid=(b, s // tm, n // tn),
        in_specs=[pl.BlockSpec((1, tm, k), lambda bi, i, j: (bi, i, 0)),
                  pl.BlockSpec((k, tn), lambda bi, i, j: (0, j)),
                  pl.BlockSpec((1, tm, tn), lambda bi, i, j: (bi, i, j)),
                  pl.BlockSpec((1, 1, tn), lambda bi, i, j: (bi, 0, j))],
        out_specs=pl.BlockSpec((1, tm, tn), lambda bi, i, j: (bi, i, j)),
        out_shape=jax.ShapeDtypeStruct((b, s, n), F32),
        compiler_params=_cparams(("parallel", "parallel", "parallel")),
        name=name,
    )(a, w, res, gate)


ATT_KEY_CHUNK = 512
ATT_TQ = 1024


def _attn_kernel(qt_ref, *refs, n_seg):
    o_ref, s_ref = refs[2 * n_seg], refs[2 * n_seg + 1]
    qt = qt_ref[0, 0]
    tq = qt.shape[1]
    chunks, off = [], 0
    for sgi in range(n_seg):
        k_ref, vt_ref = refs[2 * sgi], refs[2 * sgi + 1]
        lk = k_ref.shape[2]
        ch = min(ATT_KEY_CHUNK, lk)
        for c in range(lk // ch):
            chunks.append((k_ref, vt_ref, c * ch, ch, off))
            off += ch

    m8 = jnp.full((8, tq), -jnp.inf, F32)
    for k_ref, _, start, ch, off in chunks:
        s = jnp.dot(k_ref[0, 0, start:start + ch, :], qt, preferred_element_type=F32)
        s_ref[off:off + ch, :] = s
        m8 = jnp.maximum(m8, jnp.max(s.reshape(ch // 8, 8, tq), axis=0))
    m = jnp.max(m8, axis=0, keepdims=True)
    l8 = jnp.zeros((8, tq), F32)
    acc = jnp.zeros((vt_ref.shape[2], tq), F32)
    for _, vt_ref, start, ch, off in chunks:
        p = jnp.exp2(s_ref[off:off + ch, :] - m)
        l8 = l8 + jnp.sum(p.reshape(ch // 8, 8, tq), axis=0)
        acc = acc + jnp.dot(vt_ref[0, 0, :, start:start + ch], p.astype(BF16), preferred_element_type=F32)
    l = jnp.sum(l8, axis=0, keepdims=True)
    o_ref[0] = (acc / l).T


def attention(qt, segments, *, tq=256):
    b, h, dk, lq = qt.shape
    dv = segments[0][1].shape[2]
    in_specs = [pl.BlockSpec((1, 1, dk, tq), lambda bi, hi, i: (bi, hi, 0, i))]
    operands = [qt]
    for k, vt in segments:
        lk = k.shape[2]
        in_specs += [pl.BlockSpec((1, 1, lk, dk), lambda bi, hi, i: (bi, hi, 0, 0)),
                     pl.BlockSpec((1, 1, dv, lk), lambda bi, hi, i: (bi, hi, 0, 0))]
        operands += [k, vt]
    return pl.pallas_call(
        functools.partial(_attn_kernel, n_seg=len(segments)),
        grid=(b, h, lq // tq),
        in_specs=in_specs,
        out_specs=pl.BlockSpec((1, tq, dv), lambda bi, hi, i: (bi, i, hi)),
        out_shape=jax.ShapeDtypeStruct((b, lq, h * dv), F32),
        scratch_shapes=[pltpu.VMEM((sum(k.shape[2] for k, _ in segments), tq), F32)],
        compiler_params=_cparams(("parallel", "parallel", "parallel")),
        name="attention",
    )(*operands)


HY_NA = 64
HY_NB = 128
HY_N = HY_NA * HY_NB
HY_KG = 8


@functools.lru_cache(maxsize=None)
def _dft_tables():
    na, nb, n = HY_NA, HY_NB, HY_N
    a = np.arange(na // 2)[None, :]
    ka = np.arange(na)[:, None]
    ang = 2.0 * np.pi * ((a * ka) % na) / na
    c, s = np.cos(ang), np.sin(ang)
    g1 = np.block([[c, s], [-s, c]])
    g1_inv = g1.T.copy()
    a_full = np.arange(na)[None, :]
    ang = 2.0 * np.pi * ((a_full * ka) % na) / na
    g1_real = np.concatenate([np.cos(ang), -np.sin(ang)], axis=0)
    b = np.arange(nb)[None, None, :]
    k = np.arange(na)[:, None, None] + na * np.arange(nb)[None, :, None]
    ang = 2.0 * np.pi * ((b * k) % n) / n
    cr, si = np.cos(ang), np.sin(ang)
    g2 = np.concatenate([np.concatenate([cr, si], axis=2), np.concatenate([-si, cr], axis=2)], axis=1)
    g2_inv = np.transpose(g2, (0, 2, 1)).copy()
    return (g1.astype(np.float32), g1_inv.astype(np.float32), g1_real.astype(np.float32),
            g2.astype(np.float32), g2_inv.astype(np.float32))


def _dot_3pass(a, b):
    a_hi, b_hi = a.astype(BF16), b.astype(BF16)
    a_lo = (a - a_hi.astype(F32)).astype(BF16)
    b_lo = (b - b_hi.astype(F32)).astype(BF16)
    dot = functools.partial(jnp.dot, preferred_element_type=F32)
    return dot(a_hi, b_hi) + (dot(a_hi, b_lo) + dot(a_lo, b_hi))


def _short_conv_kernel(u_ref, w_ref, b_ref, o_ref, *, silu):
    u = u_ref[0].astype(F32)
    n = u.shape[0]
    rows = lax.broadcasted_iota(jnp.int32, u.shape, 0)
    prev = jnp.where(rows == 0, 0.0, pltpu.roll(u, 1, axis=0))
    nxt = jnp.where(rows == n - 1, 0.0, pltpu.roll(u, n - 1, axis=0))
    y = b_ref[...] + prev * w_ref[0:1, :] + u * w_ref[1:2, :] + nxt * w_ref[2:3, :]
    if silu:
        y = y * jax.nn.sigmoid(y)
    o_ref[0, 0] = y.astype(o_ref.dtype)


def short_conv_groups(proj, col0, w, b, *, group, tc, silu, name, out_dtype=F32):
    bsz, l, _ = proj.shape
    width = w.shape[1]
    per = group // tc
    return pl.pallas_call(
        functools.partial(_short_conv_kernel, silu=silu),
        grid=(bsz, width // tc),
        in_specs=[pl.BlockSpec((1, l, tc), lambda bi, j: (bi, 0, col0 // tc + j)),
                  pl.BlockSpec((SHORT_CONV, tc), lambda bi, j: (0, j)),
                  pl.BlockSpec((1, tc), lambda bi, j: (0, j))],
        out_specs=pl.BlockSpec((1, 1, l, tc), lambda bi, j: (j // per, bi, 0, j % per)),
        out_shape=jax.ShapeDtypeStruct((width // group, bsz, l, group), out_dtype),
        compiler_params=_cparams(("parallel", "parallel")),
        name=name,
    )(proj, w, b.reshape(1, width))


def _hy_filter_kernel(f_ref, w1_ref, b1_ref, w2_ref, b2_ref, w3_ref, dec_ref, o_ref, *, zero_row):
    hp = lax.Precision.HIGHEST
    f = f_ref[...]
    tl = f.shape[0]
    h = jnp.sin(jnp.dot(f, w1_ref[...], precision=hp, preferred_element_type=F32) + b1_ref[...])
    h = jnp.sin(jnp.dot(h, w2_ref[...], precision=hp, preferred_element_type=F32) + b2_ref[...])
    h = _dot_3pass(h, w3_ref[0])
    h = h * (jnp.exp(-f[:, 0:1] * dec_ref[0]) + HY_SHIFT)
    rows = lax.broadcasted_iota(jnp.int32, h.shape, 0) + pl.program_id(0) * tl
    o_ref[...] = jnp.where(rows == zero_row, 0.0, h)


def hyena_filter_buffer(L, p, tl=512):
    n = 2 * L
    r = jnp.arange(n, dtype=F32)
    t = jnp.where(r < L, r, n - r) / L
    bands = jnp.arange(1, HY_BANDS + 1, dtype=F32)
    ang = 2.0 * math.pi * t[:, None] * bands
    emb = 1 + 2 * HY_BANDS
    feats = jnp.concatenate([t[:, None], jnp.cos(ang), jnp.sin(ang), jnp.zeros((n, 128 - emb), F32)], axis=-1)
    hid = p['hy_w1'].shape[1]
    w1 = jnp.concatenate([p['hy_w1'], jnp.zeros((128 - emb, hid), F32)], axis=0)
    oc = HY_ORDER * HY_WIDTH
    w3 = p['hy_w3'].reshape(hid, HY_ORDER, 2, HY_WIDTH).transpose(2, 0, 1, 3).reshape(2, hid, oc)
    dec = p['hy_decay'].reshape(HY_ORDER, 2, HY_WIDTH).transpose(1, 0, 2).reshape(2, 1, oc)
    half = L // tl
    return pl.pallas_call(
        functools.partial(_hy_filter_kernel, zero_row=L),
        grid=(n // tl,),
        in_specs=[pl.BlockSpec((tl, 128), lambda i: (i, 0)),
                  pl.BlockSpec((128, hid), lambda i: (0, 0)),
                  pl.BlockSpec((1, hid), lambda i: (0, 0)),
                  pl.BlockSpec((hid, hid), lambda i: (0, 0)),
                  pl.BlockSpec((1, hid), lambda i: (0, 0)),
                  pl.BlockSpec((1, hid, oc), lambda i: (i // half, 0, 0)),
                  pl.BlockSpec((1, 1, oc), lambda i: (i // half, 0, 0))],
        out_specs=pl.BlockSpec((tl, oc), lambda i: (i, 0)),
        out_shape=jax.ShapeDtypeStruct((n, oc), F32),
        compiler_params=_cparams(("parallel",)),
        name="hyena_filter",
    )(feats, w1, p['hy_b1'].reshape(1, hid), p['hy_w2'], p['hy_b2'].reshape(1, hid), w3, dec)


def _dft_first_kernel(g_ref, x_ref, o_ref, *, precise):
    if precise:
        o_ref[0] = _dot_3pass(g_ref[...], x_ref[0])
    else:
        y = jnp.dot(g_ref[...].astype(BF16), x_ref[0].astype(BF16), preferred_element_type=F32)
        o_ref[0] = y.astype(o_ref.dtype)


def dft_first(g, x, *, n_out, offset=0, tn=8192, precise=False):
    _, r, lanes = x.shape
    m = g.shape[0]
    return pl.pallas_call(
        functools.partial(_dft_first_kernel, precise=precise),
        grid=(n_out, lanes // tn),
        in_specs=[pl.BlockSpec((m, r), lambda p, j: (0, 0)),
                  pl.BlockSpec((1, r, tn), lambda p, j: (offset + p, 0, j))],
        out_specs=pl.BlockSpec((1, m, tn), lambda p, j: (p, 0, j)),
        out_shape=jax.ShapeDtypeStruct((n_out, m, lanes), F32 if precise else BF16),
        compiler_params=_cparams(("parallel", "parallel")),
        name="dft_first",
    )(g, x)


def _dft_filter_mid_kernel(g_ref, y_ref, o_ref, *, scale):
    for i in range(g_ref.shape[0]):
        y = jnp.concatenate([y_ref[0, i], y_ref[1, i]], axis=0)
        o_ref[i] = scale * _dot_3pass(g_ref[i], y)


def dft_filter_mid(g2, y1):
    _, na, nb, oc = y1.shape
    kg = HY_KG // 2
    return pl.pallas_call(
        functools.partial(_dft_filter_mid_kernel, scale=1.0 / HY_N),
        grid=(na // kg,),
        in_specs=[pl.BlockSpec((kg, 2 * nb, 2 * nb), lambda g: (g, 0, 0)),
                  pl.BlockSpec((2, kg, nb, oc), lambda g: (0, g, 0, 0))],
        out_specs=pl.BlockSpec((kg, 2 * nb, oc), lambda g: (g, 0, 0)),
        out_shape=jax.ShapeDtypeStruct((na, 2 * nb, oc), F32),
        compiler_params=_cparams(("parallel",)),
        name="dft_filter_mid",
    )(g2, y1)


def _dft_mid_kernel(gf_ref, gi_ref, y_ref, h_ref, o_ref):
    nb = y_ref.shape[3]
    for i in range(gf_ref.shape[0]):
        y = jnp.concatenate([y_ref[0, 0, i], y_ref[0, 1, i]], axis=0)
        x = jnp.dot(gf_ref[i].astype(BF16), y, preferred_element_type=F32)
        xr, xi = x[:nb], x[nb:]
        hr, hi = h_ref[i, :nb], h_ref[i, nb:]
        prod = jnp.concatenate([xr * hr - xi * hi, xr * hi + xi * hr], axis=0).astype(BF16)
        u = jnp.dot(gi_ref[i].astype(BF16), prod, preferred_element_type=F32)
        o_ref[0, 0, i] = u[:nb].astype(o_ref.dtype)
        o_ref[0, 1, i] = u[nb:].astype(o_ref.dtype)


def dft_mid(g2, g2_inv, y1, h, order):
    npair, _, na, nb, c = y1.shape
    kg = HY_KG
    return pl.pallas_call(
        _dft_mid_kernel,
        grid=(npair, na // kg),
        in_specs=[pl.BlockSpec((kg, 2 * nb, 2 * nb), lambda p, g: (g, 0, 0)),
                  pl.BlockSpec((kg, 2 * nb, 2 * nb), lambda p, g: (g, 0, 0)),
                  pl.BlockSpec((1, 2, kg, nb, c), lambda p, g: (p, 0, g, 0, 0)),
                  pl.BlockSpec((kg, 2 * nb, c), lambda p, g: (g, 0, order))],
        out_specs=pl.BlockSpec((1, 2, kg, nb, c), lambda p, g: (p, 0, g, 0, 0)),
        out_shape=jax.ShapeDtypeStruct(y1.shape, BF16),
        compiler_params=_cparams(("parallel", "parallel")),
        name="dft_mid",
    )(g2, g2_inv, y1, h)


def _dft_last_kernel(g_ref, u_ref, z_ref, gate_ref, skip_ref, o_ref):
    conv = jnp.dot(g_ref[...].astype(BF16), u_ref[0], preferred_element_type=F32)
    y = gate_ref[0].astype(F32) * (conv + skip_ref[...] * z_ref[0].astype(F32))
    o_ref[0] = y.astype(o_ref.dtype)


def dft_last(g1_inv, u, z, z_off, gate, gate_off, skip_lanes, *, out_dtype, tn=8192):
    npair, m2, lanes = u.shape
    r = g1_inv.shape[0]
    return pl.pallas_call(
        _dft_last_kernel,
        grid=(npair, lanes // tn),
        in_specs=[pl.BlockSpec((r, m2), lambda p, j: (0, 0)),
                  pl.BlockSpec((1, m2, tn), lambda p, j: (p, 0, j)),
                  pl.BlockSpec((1, r, tn), lambda p, j: (z_off + p, 0, j)),
                  pl.BlockSpec((1, r, tn), lambda p, j: (gate_off + p, 0, j)),
                  pl.BlockSpec((1, tn), lambda p, j: (0, 0))],
        out_specs=pl.BlockSpec((1, r, tn), lambda p, j: (p, 0, j)),
        out_shape=jax.ShapeDtypeStruct((npair, r, lanes), out_dtype),
        compiler_params=_cparams(("parallel", "parallel")),
        name="dft_last",
    )(g1_inv, u, z, gate, skip_lanes)


def hyena_long(proj, p):
    bsz, l, _ = proj.shape
    assert 2 * l == HY_N and bsz % 2 == 0
    c = HY_WIDTH
    npair = bsz // 2
    lanes = HY_NB * c
    g1, g1_inv, g1_real, g2, g2_inv = _dft_tables()
    g1_bf, g1_inv_bf = jnp.asarray(g1), jnp.asarray(g1_inv)
    g2_bf, g2_inv_bf = jnp.asarray(g2), jnp.asarray(g2_inv)

    buf = hyena_filter_buffer(l, p)
    y1h = dft_first(jnp.asarray(g1_real), buf.reshape(1, HY_NA, HY_NB * HY_ORDER * c), n_out=1, precise=True)
    hspec = dft_filter_mid(jnp.asarray(g2), y1h.reshape(2, HY_NA, HY_NB, HY_ORDER * c))

    vx = short_conv_groups(proj, P_HY, p['hy_conv_w'], p['hy_conv_b'], group=c, tc=128, silu=False,
                           name="hyena_short_conv", out_dtype=BF16)
    vx = vx.reshape(3 * npair, HY_NA, lanes)
    z, z_off = vx, 0
    for n in range(HY_ORDER):
        y1 = dft_first(g1_bf, z, n_out=npair, offset=z_off)
        u = dft_mid(g2_bf, g2_inv_bf, y1.reshape(npair, 2, HY_NA, HY_NB, c), hspec, n)
        skip_lanes = jnp.tile(p['hy_skip'][n], 8192 // c).reshape(1, 8192)
        z = dft_last(g1_inv_bf, u.reshape(npair, 2 * HY_NA, lanes), z, z_off, vx, (n + 1) * npair, skip_lanes,
                     out_dtype=BF16 if n + 1 < HY_ORDER else F32)
        z_off = 0
    return z.reshape(bsz, l, c)


def _log_sigmoid(x):
    return jnp.minimum(x, 0.0) - jnp.log1p(jnp.exp(-jnp.abs(x)))


def _mlstm_kernel(*refs, nc_ctx, direction):
    ins, bias_ref, outs, (c_ref, m_ref) = refs[:8], refs[8], refs[9:11], refs[11:13]
    d = ML_HEAD_DIM
    t = ML_CHUNK
    nh = ML_HEADS
    step = pl.program_id(1)

    @pl.when(step == 0)
    def _():
        c_ref[...] = jnp.zeros_like(c_ref)
        m_ref[...] = jnp.zeros_like(m_ref)

    i0 = S_GATES + direction * nh
    f0 = S_GATES + (2 + direction) * nh

    def chunk(q_ref, k_ref, v_ref, sm_ref, o_ref):
        g = sm_ref[0] + bias_ref[...]
        gt = g.T
        lf_cols = _log_sigmoid(g)
        lf_rows = _log_sigmoid(gt)
        r = lax.broadcasted_iota(jnp.int32, (t, t), 0)
        cidx = lax.broadcasted_iota(jnp.int32, (t, t), 1)
        mask = cidx <= r if direction == 0 else cidx >= r
        mask_t = r <= cidx if direction == 0 else r >= cidx
        tri = mask.astype(F32)
        tri_t = mask_t.astype(F32)
        lane = lax.broadcasted_iota(jnp.int32, (t, d), 1)
        ones_col = jnp.where(lane == 0, 1.0, 0.0).astype(BF16)
        for h in range(nh):
            sl = slice(h * d, (h + 1) * d)
            lf_row = lf_rows[f0 + h:f0 + h + 1, :]
            bc = jnp.sum(tri * lf_row, axis=1, keepdims=True)
            br = jnp.sum(tri_t * lf_cols[:, f0 + h:f0 + h + 1], axis=0, keepdims=True)
            lir = gt[i0 + h:i0 + h + 1, :]
            m_prev = m_ref[h, 0:1, 0:1]
            dmat = jnp.where(mask, bc - br + lir, -jnp.inf)
            inter = bc + m_prev
            m_t = jnp.maximum(inter, jnp.max(dmat, axis=1, keepdims=True))
            dexp = jnp.exp(dmat - m_t)
            inter_w = jnp.exp(inter - m_t)
            qh = (q_ref[0, 0, :, sl] * (d ** -0.5)).astype(BF16)
            kt = k_ref[0, 0, :, sl].T
            v_ext = jnp.concatenate([v_ref[0, :, sl].astype(BF16), ones_col], axis=1)
            s = jnp.dot(qh, kt.astype(BF16), preferred_element_type=F32) * dexp
            c_ext = c_ref[h]
            acc = inter_w * jnp.dot(qh, c_ext.astype(BF16), preferred_element_type=F32)
            acc = acc + jnp.dot(s.astype(BF16), v_ext, preferred_element_type=F32)
            den = jnp.maximum(jnp.abs(acc[:, d:d + 1]), jnp.exp(-m_t))
            o_ref[0, :, sl] = acc[:, :d] / den
            b_last = jnp.sum(lf_row, axis=1, keepdims=True)
            g_row = b_last - br + lir
            m_new = jnp.maximum(b_last + m_prev, jnp.max(g_row, axis=1, keepdims=True))
            a = jnp.exp(b_last + m_prev - m_new)
            w_row = jnp.exp(g_row - m_new)
            c_ref[h] = a * c_ext + jnp.dot((kt * w_row).astype(BF16), v_ext, preferred_element_type=F32)
            m_ref[h] = jnp.broadcast_to(m_new, m_ref.shape[1:])

    @pl.when(step < nc_ctx)
    def _():
        chunk(*ins[:4], outs[0])

    @pl.when(step >= nc_ctx)
    def _():
        chunk(*ins[4:], outs[1])


def mlstm_bidir(qk_c, proj_c, small_c, qk_l, proj_l, small_l, gate_bias):
    _, bsz, lc, w = qk_c.shape
    ll = qk_l.shape[2]
    t, nh = ML_CHUNK, ML_HEADS
    nc_c, nc_l = lc // t, ll // t
    bias = jnp.zeros((1, N_SMALL), F32).at[0, S_GATES:S_GATES + ML_GATES].set(gate_bias)

    def idx(dd, step_off, nc):
        def f(g):
            cc = jnp.clip(g - step_off, 0, nc - 1)
            return cc + dd * (nc - 1 - 2 * cc)
        return f

    vcol = P_MLV // w

    def seg_specs(ix):
        return [pl.BlockSpec((1, 1, t, w), lambda b, g: (0, b, ix(g), 0)),
                pl.BlockSpec((1, 1, t, w), lambda b, g: (1, b, ix(g), 0)),
                pl.BlockSpec((1, t, w), lambda b, g: (b, ix(g), vcol)),
                pl.BlockSpec((1, t, N_SMALL), lambda b, g: (b, ix(g), 0))]

    def out_spec(ix):
        return pl.BlockSpec((1, t, w), lambda b, g: (b, ix(g), 0))

    h_c, h_l = [], []
    for dd in range(2):
        ic, il = idx(dd, 0, nc_c), idx(dd, nc_c, nc_l)
        hc, hl = pl.pallas_call(
            functools.partial(_mlstm_kernel, nc_ctx=nc_c, direction=dd),
            grid=(bsz, nc_c + nc_l),
            in_specs=seg_specs(ic) + seg_specs(il) + [pl.BlockSpec((1, N_SMALL), lambda b, g: (0, 0))],
            out_specs=[out_spec(ic), out_spec(il)],
            out_shape=[jax.ShapeDtypeStruct((bsz, lc, w), F32), jax.ShapeDtypeStruct((bsz, ll, w), F32)],
            scratch_shapes=[pltpu.VMEM((nh, ML_HEAD_DIM, 2 * ML_HEAD_DIM), F32), pltpu.VMEM((nh, 8, 128), F32)],
            compiler_params=_cparams(("parallel", "arbitrary")),
            name="mlstm_fwd" if dd == 0 else "mlstm_bwd",
        )(qk_c, qk_c, proj_c, small_c, qk_l, qk_l, proj_l, small_l, bias)
        h_c.append(hc)
        h_l.append(hl)
    return tuple(h_c), tuple(h_l)


def _rms(x):
    return x * lax.rsqrt(jnp.mean(x * x, axis=-1, keepdims=True) + EPS)


def _dot_split(x, e):
    hi = x.astype(BF16)
    lo = (x - hi.astype(F32)).astype(BF16)
    return jnp.dot(hi, e, preferred_element_type=F32) + jnp.dot(lo, e, preferred_element_type=F32)


def _swap_rope_halves(x):
    w = x.shape[1]
    lane = lax.broadcasted_iota(jnp.int32, x.shape, 1) % MLA_ROPE
    return jnp.where(lane < MLA_ROPE // 2, pltpu.roll(x, w - MLA_ROPE // 2, axis=1),
                     pltpu.roll(x, MLA_ROPE // 2, axis=1))


def _mla_prep_kernel(pq_ref, pkv_ref, sm_ref, gqa_ref, gkva_ref, wq_ref, wkv_ref, gqn_ref, gqr_ref, gkn_ref,
                     gkr_ref, en_ref, er_ref, ent_ref, ert_ref, *rest, use_rope):
    if use_rope:
        ccq_ref, ssq_ref, cck_ref, ssk_ref, qt_ref, k_ref, vt_ref = rest
    else:
        qt_ref, k_ref, vt_ref = rest
    nh, dn, dr = MLA_HEADS, MLA_NOPE, MLA_ROPE
    qa = (_rms(pq_ref[0].astype(F32)) * gqa_ref[...]).astype(BF16)
    kva = (_rms(pkv_ref[0].astype(F32)) * gkva_ref[...]).astype(BF16)
    q_raw = jnp.dot(qa, wq_ref[...], preferred_element_type=F32)
    kv_raw = jnp.dot(kva, wkv_ref[...], preferred_element_type=F32)
    inv_d = 1.0 / MLA_QK

    qn, qr = q_raw[:, :nh * dn], q_raw[:, nh * dn:]
    ss = _dot_split(qn * qn, en_ref[...]) + _dot_split(qr * qr, er_ref[...])
    rs = lax.rsqrt(ss * inv_d + EPS)
    qn = qn * _dot_split(rs, ent_ref[...]) * gqn_ref[...]
    qr = qr * _dot_split(rs, ert_ref[...]) * gqr_ref[...]
    if use_rope:
        qr = qr * ccq_ref[...] + _swap_rope_halves(qr) * ssq_ref[...]
    qrt = qr.T
    tl = qn.shape[0]
    for h in range(nh):
        qt_ref[0, h, 0:dn, :] = qn[:, h * dn:(h + 1) * dn].T.astype(BF16)
        qt_ref[0, h, dn:dn + dr, :] = qrt[h * dr:(h + 1) * dr].astype(BF16)
        qt_ref[0, h, dn + dr:, :] = jnp.zeros((QK_PAD - dn - dr, tl), BF16)

    kn, vv = kv_raw[:, :nh * dn], kv_raw[:, nh * dn:]
    lane = lax.broadcasted_iota(jnp.int32, sm_ref.shape[1:], 1)
    kr = jnp.where(lane < dr, sm_ref[0], 0.0)
    ssk = _dot_split(kn * kn, en_ref[...]) + jnp.sum(kr * kr, axis=-1, keepdims=True)
    rsk = lax.rsqrt(ssk * inv_d + EPS)
    kn = kn * _dot_split(rsk, ent_ref[...]) * gkn_ref[...]
    kr = kr * gkr_ref[...]
    if use_rope:
        kr = kr * cck_ref[...] + _swap_rope_halves(kr) * ssk_ref[...]
    for h in range(nh):
        k_ref[0, h, :, 0:dn] = kn[:, h * dn:(h + 1) * dn].astype(BF16)
        k_ref[0, h, :, dn:] = (kr * rsk[:, h:h + 1]).astype(BF16)
        vt_ref[0, h] = vv[:, h * MLA_V:(h + 1) * MLA_V].T.astype(BF16)


@functools.lru_cache(maxsize=None)
def _head_selectors():
    en = np.kron(np.eye(MLA_HEADS), np.ones((MLA_NOPE, 1)))
    er = np.kron(np.eye(MLA_HEADS), np.ones((MLA_ROPE, 1)))
    pad = lambda m: np.pad(m, ((0, 0), (0, 128 - m.shape[1]))).astype(np.float32)
    return pad(en), pad(er)


def _rope_tables(n_tokens):
    rows = n_tokens // GRID_W
    row = jnp.repeat(jnp.arange(rows, dtype=F32), GRID_W)
    col = jnp.tile(jnp.arange(GRID_W, dtype=F32), rows)
    freqs = ROPE_BASE ** (-jnp.arange(ROPE_PAIRS_PER_AXIS, dtype=F32) / ROPE_PAIRS_PER_AXIS)
    ang = jnp.concatenate([row[:, None] * freqs, col[:, None] * freqs], axis=-1)
    cos, sin = jnp.cos(ang), jnp.sin(ang)
    cc = jnp.concatenate([cos, cos], axis=-1)
    ss = jnp.concatenate([-sin, sin], axis=-1)
    zero = jnp.zeros((n_tokens, 128 - MLA_ROPE), F32)
    return (jnp.tile(cc, (1, MLA_HEADS)), jnp.tile(ss, (1, MLA_HEADS)),
            jnp.concatenate([cc, zero], axis=-1), jnp.concatenate([ss, zero], axis=-1))


def mla_prep(proj, small, p, rope_tables, *, tl):
    b, l, _ = proj.shape
    nh = MLA_HEADS
    en, er = _head_selectors()
    en_b, er_b = jnp.asarray(en, BF16), jnp.asarray(er, BF16)
    ent_b, ert_b = jnp.asarray(en.T.copy(), BF16), jnp.asarray(er.T.copy(), BF16)
    const = lambda shape: pl.BlockSpec(shape, lambda bi, i: (0,) * len(shape))
    nq = nh * MLA_QK
    nkv = nh * (MLA_NOPE + MLA_V)
    in_specs = [pl.BlockSpec((1, tl, MLA_Q_LORA), lambda bi, i: (bi, i, P_CQ // MLA_Q_LORA)),
                pl.BlockSpec((1, tl, MLA_KV_LORA), lambda bi, i: (bi, i, P_CKV // MLA_KV_LORA)),
                pl.BlockSpec((1, tl, N_SMALL), lambda bi, i: (bi, i, 0)),
                const((1, MLA_Q_LORA)), const((1, MLA_KV_LORA)), const((MLA_Q_LORA, nq)), const((MLA_KV_LORA, nkv)),
                const((1, nh * MLA_NOPE)), const((1, nh * MLA_ROPE)), const((1, nh * MLA_NOPE)), const((1, 128)),
                const(en.shape), const(er.shape), const(en.T.shape), const(er.T.shape)]
    operands = [proj, proj, small, p['gqa'], p['gkva'], p['wq'], p['wkv'], p['gqn'], p['gqr'], p['gkn'], p['gkr'],
                en_b, er_b, ent_b, ert_b]
    use_rope = rope_tables is not None
    if use_rope:
        in_specs += [pl.BlockSpec((tl, nh * MLA_ROPE), lambda bi, i: (i, 0)),
                     pl.BlockSpec((tl, nh * MLA_ROPE), lambda bi, i: (i, 0)),
                     pl.BlockSpec((tl, 128), lambda bi, i: (i, 0)),
                     pl.BlockSpec((tl, 128), lambda bi, i: (i, 0))]
        operands += list(rope_tables)
    return pl.pallas_call(
        functools.partial(_mla_prep_kernel, use_rope=use_rope),
        grid=(b, l // tl),
        in_specs=in_specs,
        out_specs=[pl.BlockSpec((1, nh, QK_PAD, tl), lambda bi, i: (bi, 0, 0, i)),
                   pl.BlockSpec((1, nh, tl, QK_PAD), lambda bi, i: (bi, 0, i, 0)),
                   pl.BlockSpec((1, nh, MLA_V, tl), lambda bi, i: (bi, 0, 0, i))],
        out_shape=[jax.ShapeDtypeStruct((b, nh, QK_PAD, l), BF16),
                   jax.ShapeDtypeStruct((b, nh, l, QK_PAD), BF16),
                   jax.ShapeDtypeStruct((b, nh, MLA_V, l), BF16)],
        compiler_params=_cparams(("parallel", "parallel")),
        name="mla_prep",
    )(*operands)


def _combine_out_kernel(a_ref, y_ref, hf_ref, hb_ref, o_ref, g_ref, w_ref, r_ref, gt_ref, out_ref, cat_ref):
    @pl.when(pl.program_id(2) == 0)
    def _():
        g = g_ref[...]
        cat_ref[:, :MLA_WIDTH] = (_rms(a_ref[0]) * g[:, :MLA_WIDTH]).astype(BF16)
        hy0 = MLA_WIDTH
        cat_ref[:, hy0:hy0 + HY_WIDTH] = (_rms(y_ref[0]) * g[:, hy0:hy0 + HY_WIDTH]).astype(BF16)
        ml0 = MLA_WIDTH + HY_WIDTH
        hh = hf_ref[0] + hb_ref[0]
        og = 0.5 * jnp.tanh(0.5 * o_ref[0].astype(F32)) + 0.5
        for hd in range(ML_HEADS):
            sl = slice(hd * ML_HEAD_DIM, (hd + 1) * ML_HEAD_DIM)
            gl = g[:, ml0 + hd * ML_HEAD_DIM:ml0 + (hd + 1) * ML_HEAD_DIM]
            cat_ref[:, ml0 + hd * ML_HEAD_DIM:ml0 + (hd + 1) * ML_HEAD_DIM] = (
                _rms(hh[:, sl]) * gl * og[:, sl]).astype(BF16)

    y = jnp.dot(cat_ref[...], w_ref[...], preferred_element_type=F32)
    out_ref[0] = r_ref[0] + gt_ref[0] * y


def combine_out_proj(a, y, h2, proj, g_mix, w_out, res, gate, *, tm, tn, name="out_proj"):
    b, s, d = res.shape
    return pl.pallas_call(
        _combine_out_kernel,
        grid=(b, s // tm, d // tn),
        in_specs=[pl.BlockSpec((1, tm, MLA_WIDTH), lambda bi, i, j: (bi, i, 0)),
                  pl.BlockSpec((1, tm, HY_WIDTH), lambda bi, i, j: (bi, i, 0)),
                  pl.BlockSpec((1, tm, ML_WIDTH), lambda bi, i, j: (bi, i, 0)),
                  pl.BlockSpec((1, tm, ML_WIDTH), lambda bi, i, j: (bi, i, 0)),
                  pl.BlockSpec((1, tm, ML_WIDTH), lambda bi, i, j: (bi, i, P_MLO // ML_WIDTH)),
                  pl.BlockSpec((1, MIX_WIDTH), lambda bi, i, j: (0, 0)),
                  pl.BlockSpec((MIX_WIDTH, tn), lambda bi, i, j: (0, j)),
                  pl.BlockSpec((1, tm, tn), lambda bi, i, j: (bi, i, j)),
                  pl.BlockSpec((1, 1, tn), lambda bi, i, j: (bi, 0, j))],
        out_specs=pl.BlockSpec((1, tm, tn), lambda bi, i, j: (bi, i, j)),
        out_shape=jax.ShapeDtypeStruct((b, s, d), F32),
        scratch_shapes=[pltpu.VMEM((tm, MIX_WIDTH), BF16)],
        compiler_params=_cparams(("parallel", "parallel", "arbitrary")),
        name=name,
    )(a, y, h2[0], h2[1], proj, g_mix.reshape(1, MIX_WIDTH), w_out, res, gate)


@functools.lru_cache(maxsize=None)
def _direct_dft_tables(l):
    n = 2 * l
    k = np.arange(n)[:, None]
    t = np.arange(l)[None, :]
    ang = 2.0 * np.pi * ((k * t) % n) / n
    c, s = np.cos(ang), np.sin(ang)
    gf = np.block([[c, s], [-s, c]])
    gi = gf.T.copy()
    t_full = np.arange(n)[None, :]
    ang = 2.0 * np.pi * ((k * t_full) % n) / n
    g_real = np.concatenate([np.cos(ang), -np.sin(ang)], axis=0)
    return gf.astype(np.float32), gi.astype(np.float32), g_real.astype(np.float32)


def _direct_conv_kernel(gf_ref, gi_ref, z_ref, gate_ref, h_ref, skip_ref, o_ref, *, scale):
    z = z_ref[0]
    x = jnp.dot(gf_ref[...].astype(BF16), z.astype(BF16), preferred_element_type=F32)
    n = x.shape[0] // 2
    xr, xi = x[:n], x[n:]
    hr, hi = h_ref[:n] * scale, h_ref[n:] * scale
    prod = jnp.concatenate([xr * hr - xi * hi, xr * hi + xi * hr], axis=0).astype(BF16)
    conv = jnp.dot(gi_ref[...].astype(BF16), prod, preferred_element_type=F32)
    o_ref[0] = gate_ref[0] * (conv + skip_ref[...] * z)


def hyena_short(proj, p):
    bsz, l, _ = proj.shape
    c = HY_WIDTH
    npair = bsz // 2
    n = 2 * l
    gf, gi, g_real = _direct_dft_tables(l)
    buf = hyena_filter_buffer(l, p, tl=l)
    hspec = dft_first(jnp.asarray(g_real), buf.reshape(1, n, HY_ORDER * c), n_out=1, tn=HY_ORDER * c,
                      precise=True)[0]
    vx = short_conv_groups(proj, P_HY, p['hy_conv_w'], p['hy_conv_b'], group=c, tc=128, silu=False,
                           name="hyena_short_conv_ctx")
    vx = vx.reshape(3 * npair, 2 * l, c)
    z = vx

    def rows_at(off):
        return lambda q: (off + q, 0, 0)

    def cols_at(col):
        return lambda q: (0, col)

    for order in range(HY_ORDER):
        z = pl.pallas_call(
            functools.partial(_direct_conv_kernel, scale=1.0 / n),
            grid=(npair,),
            in_specs=[pl.BlockSpec(gf.shape, lambda q: (0, 0)),
                      pl.BlockSpec(gi.shape, lambda q: (0, 0)),
                      pl.BlockSpec((1, 2 * l, c), rows_at(0)),
                      pl.BlockSpec((1, 2 * l, c), rows_at((order + 1) * npair)),
                      pl.BlockSpec((2 * n, c), cols_at(order)),
                      pl.BlockSpec((1, c), lambda q: (0, 0))],
            out_specs=pl.BlockSpec((1, 2 * l, c), lambda q: (q, 0, 0)),
            out_shape=jax.ShapeDtypeStruct((npair, 2 * l, c), F32),
            compiler_params=_cparams(("parallel",)),
            name="hyena_direct_conv",
        )(jnp.asarray(gf), jnp.asarray(gi), z, vx, hspec, p['hy_skip'][order].reshape(1, c))
    return z.reshape(bsz, l, c)


def _split_cols(w, sizes):
    out, start = [], 0
    for s in sizes:
        out.append(w[:, start:start + s])
        start += s
    return out


def _prep_layer_weights(l, w_in, mla_qa_norm, mla_kva_norm, mla_w_uq, mla_w_ukv, mla_q_norm, mla_k_norm):
    cq, ckv, kr, hy, mqk, mv, mo, gt = _split_cols(w_in[l], IN_SIZES)
    w_main = jnp.concatenate([cq, ckv, hy, mqk, mv, mo], axis=1).astype(BF16)
    pad = jnp.zeros((D_MODEL, N_SMALL - MLA_ROPE - ML_GATES), F32)
    w_small = jnp.concatenate([kr, gt, pad], axis=1).astype(BF16)
    nh = MLA_HEADS
    wq = mla_w_uq[l].reshape(MLA_Q_LORA, nh, MLA_QK)
    wq = jnp.concatenate([wq[:, :, :MLA_NOPE].reshape(MLA_Q_LORA, -1), wq[:, :, MLA_NOPE:].reshape(MLA_Q_LORA, -1)],
                         axis=1).astype(BF16)
    wkv = mla_w_ukv[l].reshape(MLA_KV_LORA, nh, MLA_NOPE + MLA_V)
    wkv = jnp.concatenate([wkv[:, :, :MLA_NOPE].reshape(MLA_KV_LORA, -1),
                           wkv[:, :, MLA_NOPE:].reshape(MLA_KV_LORA, -1)], axis=1).astype(BF16)
    q_scale = (MLA_QK ** -0.5) * math.log2(math.e)
    gq, gk = mla_q_norm[l] * q_scale, mla_k_norm[l]
    mla = {'gqa': mla_qa_norm[l].reshape(1, -1), 'gkva': mla_kva_norm[l].reshape(1, -1), 'wq': wq, 'wkv': wkv,
           'gqn': jnp.tile(gq[:MLA_NOPE], nh).reshape(1, -1), 'gqr': jnp.tile(gq[MLA_NOPE:], nh).reshape(1, -1),
           'gkn': jnp.tile(gk[:MLA_NOPE], nh).reshape(1, -1),
           'gkr': jnp.concatenate([gk[MLA_NOPE:], jnp.zeros((128 - MLA_ROPE,), F32)]).reshape(1, -1)}
    return w_main, w_small, mla


def kernel(x, c, ctx, c_ctx, ada_w, ada_b, norm1_g, norm2_g, w_in, mla_qa_norm, mla_kva_norm, mla_w_uq,
           mla_w_ukv, mla_q_norm, mla_k_norm, hy_conv_w, hy_conv_b, hy_w1, hy_b1, hy_w2, hy_b2, hy_w3,
           hy_decay, hy_skip, ml_conv_w, ml_conv_b, ml_gate_b, mix_norm_g, w_out, ffn_w1, ffn_w2):
    B, S, D = x.shape
    LC = ctx.shape[1]
    rope_tables = _rope_tables(S)

    silu_rows = jnp.concatenate([jax.nn.silu(c), jax.nn.silu(c_ctx)[None], jnp.zeros((8 - B - 1, D), F32)], axis=0)
    mods = ada_modulation(silu_rows.astype(BF16), ada_w, ada_b)

    for l in range(DEPTH):
        need_ctx = l < DEPTH - 1
        p = {'hy_conv_w': hy_conv_w[l], 'hy_conv_b': hy_conv_b[l], 'hy_w1': hy_w1[l], 'hy_b1': hy_b1[l],
             'hy_w2': hy_w2[l], 'hy_b2': hy_b2[l], 'hy_w3': hy_w3[l], 'hy_decay': hy_decay[l],
             'hy_skip': hy_skip[l], 'ml_conv_w': ml_conv_w[l], 'ml_conv_b': ml_conv_b[l]}
        w_main, w_small, mla = _prep_layer_weights(l, w_in, mla_qa_norm, mla_kva_norm, mla_w_uq, mla_w_ukv,
                                                   mla_q_norm, mla_k_norm)
        w_out_l = w_out[l].astype(BF16)
        w1_l = ffn_w1[l].astype(BF16)
        w2_l = ffn_w2[l].astype(BF16)

        mod_l = [m[:, None, :] for m in jnp.split(mods[l, :B], 6, axis=-1)]
        mod_c = [jnp.broadcast_to(m[:, None, :], (B, 1, D)) for m in jnp.split(mods[l, B:B + 1], 6, axis=-1)]

        proj_l, small_l = in_proj(x, norm1_g[l], mod_l[1], mod_l[0], w_main, w_small, tm=1024, tn=IN_PROJ_TN)
        proj_c, small_c = in_proj(ctx, norm1_g[l], mod_c[1], mod_c[0], w_main, w_small, tm=LC, tn=IN_PROJ_TN,
                                  name="in_proj_ctx")

        qt_l, k_l, vt_l = mla_prep(proj_l, small_l, mla, rope_tables, tl=512)
        qt_c, k_c, vt_c = mla_prep(proj_c, small_c, mla, None, tl=LC)
        a_l = attention(qt_l, [(k_c, vt_c), (k_l, vt_l)], tq=ATT_TQ)
        y_l = hyena_long(proj_l, p)

        def mlstm_qk(proj):
            return short_conv_groups(proj, P_MLQK, p['ml_conv_w'], p['ml_conv_b'], group=ML_WIDTH, tc=128,
                                     silu=True, name="mlstm_short_conv")

        h_c, h_l = mlstm_bidir(mlstm_qk(proj_c), proj_c, small_c, mlstm_qk(proj_l), proj_l, small_l, ml_gate_b[l])
        x = combine_out_proj(a_l, y_l, h_l, proj_l, mix_norm_g[l], w_out_l, x, mod_l[2], tm=1024, tn=MM_TN)
        hid = norm_swiglu(x, norm2_g[l], mod_l[4], mod_l[3], w1_l, tm=1024, tn=MM_TN, name="ffn_up")
        x = mm_residual(hid, w2_l, x, mod_l[5], tm=1024, tn=MM_TN, name="ffn_down")

        if need_ctx:
            a_c = attention(qt_c, [(k_c, vt_c)], tq=LC)
            y_c = hyena_short(proj_c, p)
            ctx = combine_out_proj(a_c, y_c, h_c, proj_c, mix_norm_g[l], w_out_l, ctx, mod_c[2], tm=LC, tn=MM_TN,
                                   name="out_proj_ctx")
            hid = norm_swiglu(ctx, norm2_g[l], mod_c[4], mod_c[3], w1_l, tm=LC, tn=MM_TN, name="ffn_up_ctx")
            ctx = mm_residual(hid, w2_l, ctx, mod_c[5], tm=LC, tn=MM_TN, name="ffn_down_ctx")
    return x
```

```python
import functools
import math

import jax
import jax.numpy as jnp
import numpy as np
from jax import lax
from jax.experimental import pallas as pl
from jax.experimental.pallas import tpu as pltpu

D_MODEL = 2048
DEPTH = 2
GRID_W = 64
EPS = 1e-6

MLA_HEADS = 8
MLA_NOPE = 128
MLA_ROPE = 64
MLA_QK = MLA_NOPE + MLA_ROPE
MLA_V = 128
MLA_Q_LORA = 512
MLA_KV_LORA = 512
MLA_WIDTH = MLA_HEADS * MLA_V
ROPE_BASE = 10000.0
ROPE_PAIRS_PER_AXIS = MLA_ROPE // 4
QK_PAD = 256

HY_WIDTH = 512
HY_ORDER = 2
HY_BANDS = 16
HY_SHIFT = 0.05
SHORT_CONV = 3
CONV_TC = 256

ML_HEADS = 4
ML_HEAD_DIM = 128
ML_WIDTH = ML_HEADS * ML_HEAD_DIM
ML_CHUNK = 128
ML_GATES = 4 * ML_HEADS

MIX_WIDTH = MLA_WIDTH + HY_WIDTH + ML_WIDTH
FFN_HIDDEN = ((8 * D_MODEL // 3 + 255) // 256) * 256

IN_SIZES = (MLA_Q_LORA, MLA_KV_LORA, MLA_ROPE, 3 * HY_WIDTH, 2 * ML_WIDTH, ML_WIDTH, ML_WIDTH, ML_GATES)
N_IN = sum(IN_SIZES)

P_CQ, P_CKV, P_HY, P_MLQK, P_MLV, P_MLO = 0, 512, 1024, 2560, 3584, 4096
N_MAIN = 4608
N_SMALL = 128
S_GATES = MLA_ROPE

VMEM_LIMIT_BYTES = 56 * 1024 * 1024

BF16 = jnp.bfloat16
F32 = jnp.float32


def _cparams(sem):
    return pltpu.CompilerParams(dimension_semantics=sem, vmem_limit_bytes=VMEM_LIMIT_BYTES)


def _ada_kernel(s_ref, w_ref, b_ref, o_ref):
    w = w_ref[0].astype(BF16)
    o_ref[0] = jnp.dot(s_ref[...], w, preferred_element_type=F32) + b_ref[0]


def ada_modulation(silu_rows, ada_w, ada_b, tn=1024):
    depth, d, n = ada_w.shape
    return pl.pallas_call(
        _ada_kernel,
        grid=(depth, n // tn),
        in_specs=[pl.BlockSpec((8, d), lambda l, j: (0, 0)),
                  pl.BlockSpec((1, d, tn), lambda l, j: (l, 0, j)),
                  pl.BlockSpec((1, 1, tn), lambda l, j: (l, 0, j))],
        out_specs=pl.BlockSpec((1, 8, tn), lambda l, j: (l, 0, j)),
        out_shape=jax.ShapeDtypeStruct((depth, 8, n), F32),
        compiler_params=_cparams(("parallel", "parallel")),
        name="ada_modulation",
    )(silu_rows, ada_w, ada_b.reshape(depth, 1, n))


MM_TN = 512
IN_PROJ_TN = 1536


def _normed(x, g, sc, sh):
    y = x * lax.rsqrt(jnp.mean(x * x, axis=-1, keepdims=True) + EPS) * g
    return y * (1.0 + sc) + sh


def _in_proj_kernel(x_ref, g_ref, sc_ref, sh_ref, w_ref, ws_ref, o_ref, os_ref, xn_ref):
    @pl.when(pl.program_id(2) == 0)
    def _():
        xn = _normed(x_ref[0], g_ref[...], sc_ref[0], sh_ref[0]).astype(BF16)
        xn_ref[...] = xn
        os_ref[0] = jnp.dot(xn, ws_ref[...], preferred_element_type=F32)

    o_ref[0] = jnp.dot(xn_ref[...], w_ref[...], preferred_element_type=F32).astype(o_ref.dtype)


def in_proj(x, g, sc, sh, w_main, w_small, *, tm, tn, name="in_proj"):
    b, s, k = x.shape
    n = w_main.shape[1]
    ns = w_small.shape[1]
    return pl.pallas_call(
        _in_proj_kernel,
        grid=(b, s // tm, n // tn),
        in_specs=[pl.BlockSpec((1, tm, k), lambda bi, i, j: (bi, i, 0)),
                  pl.BlockSpec((1, k), lambda bi, i, j: (0, 0)),
                  pl.BlockSpec((1, 1, k), lambda bi, i, j: (bi, 0, 0)),
                  pl.BlockSpec((1, 1, k), lambda bi, i, j: (bi, 0, 0)),
                  pl.BlockSpec((k, tn), lambda bi, i, j: (0, j)),
                  pl.BlockSpec((k, ns), lambda bi, i, j: (0, 0))],
        out_specs=[pl.BlockSpec((1, tm, tn), lambda bi, i, j: (bi, i, j)),
                   pl.BlockSpec((1, tm, ns), lambda bi, i, j: (bi, i, 0))],
        out_shape=[jax.ShapeDtypeStruct((b, s, n), BF16), jax.ShapeDtypeStruct((b, s, ns), F32)],
        scratch_shapes=[pltpu.VMEM((tm, k), BF16)],
        compiler_params=_cparams(("parallel", "parallel", "arbitrary")),
        name=name,
    )(x, g.reshape(1, k), sc, sh, w_main, w_small)


def _norm_swiglu_kernel(x_ref, g_ref, sc_ref, sh_ref, wg_ref, wu_ref, o_ref, xn_ref):
    @pl.when(pl.program_id(2) == 0)
    def _():
        xn_ref[...] = _normed(x_ref[0], g_ref[...], sc_ref[0], sh_ref[0]).astype(BF16)

    xn = xn_ref[...]
    gate = jnp.dot(xn, wg_ref[...], preferred_element_type=F32)
    up = jnp.dot(xn, wu_ref[...], preferred_element_type=F32)
    o_ref[0] = (gate * (0.5 * jnp.tanh(0.5 * gate) + 0.5) * up).astype(o_ref.dtype)


def norm_swiglu(x, g, sc, sh, w1, *, tm, tn, name="norm_swiglu"):
    b, s, d = x.shape
    h = w1.shape[1] // 2
    nj = h // tn
    return pl.pallas_call(
        _norm_swiglu_kernel,
        grid=(b, s // tm, nj),
        in_specs=[pl.BlockSpec((1, tm, d), lambda bi, i, j: (bi, i, 0)),
                  pl.BlockSpec((1, d), lambda bi, i, j: (0, 0)),
                  pl.BlockSpec((1, 1, d), lambda bi, i, j: (bi, 0, 0)),
                  pl.BlockSpec((1, 1, d), lambda bi, i, j: (bi, 0, 0)),
                  pl.BlockSpec((d, tn), lambda bi, i, j: (0, j)),
                  pl.BlockSpec((d, tn), lambda bi, i, j: (0, j + nj))],
        out_specs=pl.BlockSpec((1, tm, tn), lambda bi, i, j: (bi, i, j)),
        out_shape=jax.ShapeDtypeStruct((b, s, h), BF16),
        scratch_shapes=[pltpu.VMEM((tm, d), BF16)],
        compiler_params=_cparams(("parallel", "parallel", "arbitrary")),
        name=name,
    )(x, g.reshape(1, d), sc, sh, w1, w1)


def _mm_res_kernel(a_ref, w_ref, r_ref, gt_ref, o_ref):
    y = jnp.dot(a_ref[0], w_ref[...], preferred_element_type=F32)
    o_ref[0] = r_ref[0] + gt_ref[0] * y


def mm_residual(a, w, res, gate, *, tm, tn, name="mm_residual"):
    b, s, k = a.shape
    n = w.shape[1]
    return pl.pallas_call(
        _mm_res_kernel,
        grid=(b, s // tm, n // tn),
        in_specs=[pl.BlockSpec((1, tm, k), lambda bi, i, j: (bi, i, 0)),
                  pl.BlockSpec((k, tn), lambda bi, i, j: (0, j)),
                  pl.BlockSpec((1, tm, tn), lambda bi, i, j: (bi, i, j)),
                  pl.BlockSpec((1, 1, tn), lambda bi, i, j: (bi, 0, j))],
        out_specs=pl.BlockSpec((1, tm, tn), lambda bi, i, j: (bi, i, j)),
        out_shape=jax.ShapeDtypeStruct((b, s, n), F32),
        compiler_params=_cparams(("parallel", "parallel", "parallel")),
        name=name,
    )(a, w, res, gate)


ATT_KEY_CHUNK = 512
ATT_TQ = 1024


def _attn_kernel(qt_ref, *refs, n_seg):
    o_ref, s_ref = refs[2 * n_seg], refs[2 * n_seg + 1]
    qt = qt_ref[0, 0]
    tq = qt.shape[1]
    chunks, off = [], 0
    for sgi in range(n_seg):
        k_ref, vt_ref = refs[2 * sgi], refs[2 * sgi + 1]
        lk = k_ref.shape[2]
        ch = min(ATT_KEY_CHUNK, lk)
        for c in range(lk // ch):
            chunks.append((k_ref, vt_ref, c * ch, ch, off))
            off += ch

    m8 = jnp.full((8, tq), -jnp.inf, F32)
    for k_ref, _, start, ch, off in chunks:
        s = jnp.dot(k_ref[0, 0, start:start + ch, :], qt, preferred_element_type=F32)
        s_ref[off:off + ch, :] = s
        m8 = jnp.maximum(m8, jnp.max(s.reshape(ch // 8, 8, tq), axis=0))
    m = jnp.max(m8, axis=0, keepdims=True)
    l8 = jnp.zeros((8, tq), F32)
    acc = jnp.zeros((vt_ref.shape[2], tq), F32)
    for _, vt_ref, start, ch, off in chunks:
        p = jnp.exp2(s_ref[off:off + ch, :] - m)
        l8 = l8 + jnp.sum(p.reshape(ch // 8, 8, tq), axis=0)
        acc = acc + jnp.dot(vt_ref[0, 0, :, start:start + ch], p.astype(BF16), preferred_element_type=F32)
    l = jnp.sum(l8, axis=0, keepdims=True)
    o_ref[0] = (acc / l).T


def attention(qt, segments, *, tq=256):
    b, h, dk, lq = qt.shape
    dv = segments[0][1].shape[2]
    in_specs = [pl.BlockSpec((1, 1, dk, tq), lambda bi, hi, i: (bi, hi, 0, i))]
    operands = [qt]
    for k, vt in segments:
        lk = k.shape[2]
        in_specs += [pl.BlockSpec((1, 1, lk, dk), lambda bi, hi, i: (bi, hi, 0, 0)),
                     pl.BlockSpec((1, 1, dv, lk), lambda bi, hi, i: (bi, hi, 0, 0))]
        operands += [k, vt]
    return pl.pallas_call(
        functools.partial(_attn_kernel, n_seg=len(segments)),
        grid=(b, h, lq // tq),
        in_specs=in_specs,
        out_specs=pl.BlockSpec((1, tq, dv), lambda bi, hi, i: (bi, i, hi)),
        out_shape=jax.ShapeDtypeStruct((b, lq, h * dv), F32),
        scratch_shapes=[pltpu.VMEM((sum(k.shape[2] for k, _ in segments), tq), F32)],
        compiler_params=_cparams(("parallel", "parallel", "parallel")),
        name="attention",
    )(*operands)


HY_NA = 64
HY_NB = 128
HY_N = HY_NA * HY_NB
HY_KG = 8


@functools.lru_cache(maxsize=None)
def _dft_tables():
    na, nb, n = HY_NA, HY_NB, HY_N
    a = np.arange(na // 2)[None, :]
    ka = np.arange(na)[:, None]
    ang = 2.0 * np.pi * ((a * ka) % na) / na
    c, s = np.cos(ang), np.sin(ang)
    g1 = np.block([[c, s], [-s, c]])
    g1_inv = g1.T.copy()
    a_full = np.arange(na)[None, :]
    ang = 2.0 * np.pi * ((a_full * ka) % na) / na
    g1_real = np.concatenate([np.cos(ang), -np.sin(ang)], axis=0)
    b = np.arange(nb)[None, None, :]
    k = np.arange(na)[:, None, None] + na * np.arange(nb)[None, :, None]
    ang = 2.0 * np.pi * ((b * k) % n) / n
    cr, si = np.cos(ang), np.sin(ang)
    g2 = np.concatenate([np.concatenate([cr, si], axis=2), np.concatenate([-si, cr], axis=2)], axis=1)
    g2_inv = np.transpose(g2, (0, 2, 1)).copy()
    return (g1.astype(np.float32), g1_inv.astype(np.float32), g1_real.astype(np.float32),
            g2.astype(np.float32), g2_inv.astype(np.float32))


def _dot_3pass(a, b):
    a_hi, b_hi = a.astype(BF16), b.astype(BF16)
    a_lo = (a - a_hi.astype(F32)).astype(BF16)
    b_lo = (b - b_hi.astype(F32)).astype(BF16)
    dot = functools.partial(jnp.dot, preferred_element_type=F32)
    return dot(a_hi, b_hi) + (dot(a_hi, b_lo) + dot(a_lo, b_hi))


def _short_conv_kernel(u_ref, w_ref, b_ref, o_ref, *, silu):
    u = u_ref[0].astype(F32)
    n = u.shape[0]
    rows = lax.broadcasted_iota(jnp.int32, u.shape, 0)
    prev = jnp.where(rows == 0, 0.0, pltpu.roll(u, 1, axis=0))
    nxt = jnp.where(rows == n - 1, 0.0, pltpu.roll(u, n - 1, axis=0))
    y = b_ref[...] + prev * w_ref[0:1, :] + u * w_ref[1:2, :] + nxt * w_ref[2:3, :]
    if silu:
        y = y * jax.nn.sigmoid(y)
    o_ref[0, 0] = y.astype(o_ref.dtype)


def short_conv_groups(proj, col0, w, b, *, group, tc, silu, name, out_dtype=F32):
    bsz, l, _ = proj.shape
    width = w.shape[1]
    per = group // tc
    return pl.pallas_call(
        functools.partial(_short_conv_kernel, silu=silu),
        grid=(bsz, width // tc),
        in_specs=[pl.BlockSpec((1, l, tc), lambda bi, j: (bi, 0, col0 // tc + j)),
                  pl.BlockSpec((SHORT_CONV, tc), lambda bi, j: (0, j)),
                  pl.BlockSpec((1, tc), lambda bi, j: (0, j))],
        out_specs=pl.BlockSpec((1, 1, l, tc), lambda bi, j: (j // per, bi, 0, j % per)),
        out_shape=jax.ShapeDtypeStruct((width // group, bsz, l, group), out_dtype),
        compiler_params=_cparams(("parallel", "parallel")),
        name=name,
    )(proj, w, b.reshape(1, width))


def _hy_filter_kernel(f_ref, w1_ref, b1_ref, w2_ref, b2_ref, w3_ref, dec_ref, o_ref, *, zero_row):
    hp = lax.Precision.HIGHEST
    f = f_ref[...]
    tl = f.shape[0]
    h = jnp.sin(jnp.dot(f, w1_ref[...], precision=hp, preferred_element_type=F32) + b1_ref[...])
    h = jnp.sin(jnp.dot(h, w2_ref[...], precision=hp, preferred_element_type=F32) + b2_ref[...])
    h = _dot_3pass(h, w3_ref[0])
    h = h * (jnp.exp(-f[:, 0:1] * dec_ref[0]) + HY_SHIFT)
    rows = lax.broadcasted_iota(jnp.int32, h.shape, 0) + pl.program_id(0) * tl
    o_ref[...] = jnp.where(rows == zero_row, 0.0, h)


def hyena_filter_buffer(L, p, tl=512):
    n = 2 * L
    r = jnp.arange(n, dtype=F32)
    t = jnp.where(r < L, r, n - r) / L
    bands = jnp.arange(1, HY_BANDS + 1, dtype=F32)
    ang = 2.0 * math.pi * t[:, None] * bands
    emb = 1 + 2 * HY_BANDS
    feats = jnp.concatenate([t[:, None], jnp.cos(ang), jnp.sin(ang), jnp.zeros((n, 128 - emb), F32)], axis=-1)
    hid = p['hy_w1'].shape[1]
    w1 = jnp.concatenate([p['hy_w1'], jnp.zeros((128 - emb, hid), F32)], axis=0)
    oc = HY_ORDER * HY_WIDTH
    w3 = p['hy_w3'].reshape(hid, HY_ORDER, 2, HY_WIDTH).transpose(2, 0, 1, 3).reshape(2, hid, oc)
    dec = p['hy_decay'].reshape(HY_ORDER, 2, HY_WIDTH).transpose(1, 0, 2).reshape(2, 1, oc)
    half = L // tl
    return pl.pallas_call(
        functools.partial(_hy_filter_kernel, zero_row=L),
        grid=(n // tl,),
        in_specs=[pl.BlockSpec((tl, 128), lambda i: (i, 0)),
                  pl.BlockSpec((128, hid), lambda i: (0, 0)),
                  pl.BlockSpec((1, hid), lambda i: (0, 0)),
                  pl.BlockSpec((hid, hid), lambda i: (0, 0)),
                  pl.BlockSpec((1, hid), lambda i: (0, 0)),
                  pl.BlockSpec((1, hid, oc), lambda i: (i // half, 0, 0)),
                  pl.BlockSpec((1, 1, oc), lambda i: (i // half, 0, 0))],
        out_specs=pl.BlockSpec((tl, oc), lambda i: (i, 0)),
        out_shape=jax.ShapeDtypeStruct((n, oc), F32),
        compiler_params=_cparams(("parallel",)),
        name="hyena_filter",
    )(feats, w1, p['hy_b1'].reshape(1, hid), p['hy_w2'], p['hy_b2'].reshape(1, hid), w3, dec)


def _dft_first_kernel(g_ref, x_ref, o_ref, *, precise):
    if precise:
        o_ref[0] = _dot_3pass(g_ref[...], x_ref[0])
    else:
        y = jnp.dot(g_ref[...].astype(BF16), x_ref[0].astype(BF16), preferred_element_type=F32)
        o_ref[0] = y.astype(o_ref.dtype)


def dft_first(g, x, *, n_out, offset=0, tn=8192, precise=False):
    _, r, lanes = x.shape
    m = g.shape[0]
    return pl.pallas_call(
        functools.partial(_dft_first_kernel, precise=precise),
        grid=(n_out, lanes // tn),
        in_specs=[pl.BlockSpec((m, r), lambda p, j: (0, 0)),
                  pl.BlockSpec((1, r, tn), lambda p, j: (offset + p, 0, j))],
        out_specs=pl.BlockSpec((1, m, tn), lambda p, j: (p, 0, j)),
        out_shape=jax.ShapeDtypeStruct((n_out, m, lanes), F32 if precise else BF16),
        compiler_params=_cparams(("parallel", "parallel")),
        name="dft_first",
    )(g, x)


def _dft_filter_mid_kernel(g_ref, y_ref, o_ref, *, scale):
    for i in range(g_ref.shape[0]):
        y = jnp.concatenate([y_ref[0, i], y_ref[1, i]], axis=0)
        o_ref[i] = scale * _dot_3pass(g_ref[i], y)


def dft_filter_mid(g2, y1):
    _, na, nb, oc = y1.shape
    kg = HY_KG // 2
    return pl.pallas_call(
        functools.partial(_dft_filter_mid_kernel, scale=1.0 / HY_N),
        grid=(na // kg,),
        in_specs=[pl.BlockSpec((kg, 2 * nb, 2 * nb), lambda g: (g, 0, 0)),
                  pl.BlockSpec((2, kg, nb, oc), lambda g: (0, g, 0, 0))],
        out_specs=pl.BlockSpec((kg, 2 * nb, oc), lambda g: (g, 0, 0)),
        out_shape=jax.ShapeDtypeStruct((na, 2 * nb, oc), F32),
        compiler_params=_cparams(("parallel",)),
        name="dft_filter_mid",
    )(g2, y1)


def _dft_mid_kernel(gf_ref, gi_ref, y_ref, h_ref, o_ref):
    nb = y_ref.shape[3]
    for i in range(gf_ref.shape[0]):
        y = jnp.concatenate([y_ref[0, 0, i], y_ref[0, 1, i]], axis=0)
        x = jnp.dot(gf_ref[i].astype(BF16), y, preferred_element_type=F32)
        xr, xi = x[:nb], x[nb:]
        hr, hi = h_ref[i, :nb], h_ref[i, nb:]
        prod = jnp.concatenate([xr * hr - xi * hi, xr * hi + xi * hr], axis=0).astype(BF16)
        u = jnp.dot(gi_ref[i].astype(BF16), prod, preferred_element_type=F32)
        o_ref[0, 0, i] = u[:nb].astype(o_ref.dtype)
        o_ref[0, 1, i] = u[nb:].astype(o_ref.dtype)


def dft_mid(g2, g2_inv, y1, h, order):
    npair, _, na, nb, c = y1.shape
    kg = HY_KG
    return pl.pallas_call(
        _dft_mid_kernel,
        grid=(na // kg, npair),
        in_specs=[pl.BlockSpec((kg, 2 * nb, 2 * nb), lambda g, p: (g, 0, 0)),
                  pl.BlockSpec((kg, 2 * nb, 2 * nb), lambda g, p: (g, 0, 0)),
                  pl.BlockSpec((1, 2, kg, nb, c), lambda g, p: (p, 0, g, 0, 0)),
                  pl.BlockSpec((kg, 2 * nb, c), lambda g, p: (g, 0, order))],
        out_specs=pl.BlockSpec((1, 2, kg, nb, c), lambda g, p: (p, 0, g, 0, 0)),
        out_shape=jax.ShapeDtypeStruct(y1.shape, BF16),
        compiler_params=_cparams(("parallel", "parallel")),
        name="dft_mid",
    )(g2, g2_inv, y1, h)


def _dft_last_kernel(g_ref, u_ref, z_ref, gate_ref, skip_ref, o_ref):
    conv = jnp.dot(g_ref[...].astype(BF16), u_ref[0], preferred_element_type=F32)
    y = gate_ref[0].astype(F32) * (conv + skip_ref[...] * z_ref[0].astype(F32))
    o_ref[0] = y.astype(o_ref.dtype)


def dft_last(g1_inv, u, z, z_off, gate, gate_off, skip_lanes, *, out_dtype, tn=8192):
    npair, m2, lanes = u.shape
    r = g1_inv.shape[0]
    return pl.pallas_call(
        _dft_last_kernel,
        grid=(npair, lanes // tn),
        in_specs=[pl.BlockSpec((r, m2), lambda p, j: (0, 0)),
                  pl.BlockSpec((1, m2, tn), lambda p, j: (p, 0, j)),
                  pl.BlockSpec((1, r, tn), lambda p, j: (z_off + p, 0, j)),
                  pl.BlockSpec((1, r, tn), lambda p, j: (gate_off + p, 0, j)),
                  pl.BlockSpec((1, tn), lambda p, j: (0, 0))],
        out_specs=pl.BlockSpec((1, r, tn), lambda p, j: (p, 0, j)),
        out_shape=jax.ShapeDtypeStruct((npair, r, lanes), out_dtype),
        compiler_params=_cparams(("parallel", "parallel")),
        name="dft_last",
    )(g1_inv, u, z, gate, skip_lanes)


def hyena_long(proj, p):
    bsz, l, _ = proj.shape
    assert 2 * l == HY_N and bsz % 2 == 0
    c = HY_WIDTH
    npair = bsz // 2
    lanes = HY_NB * c
    g1, g1_inv, g1_real, g2, g2_inv = _dft_tables()
    g1_bf, g1_inv_bf = jnp.asarray(g1), jnp.asarray(g1_inv)
    g2_bf, g2_inv_bf = jnp.asarray(g2), jnp.asarray(g2_inv)

    buf = hyena_filter_buffer(l, p)
    y1h = dft_first(jnp.asarray(g1_real), buf.reshape(1, HY_NA, HY_NB * HY_ORDER * c), n_out=1, precise=True)
    hspec = dft_filter_mid(jnp.asarray(g2), y1h.reshape(2, HY_NA, HY_NB, HY_ORDER * c))

    vx = short_conv_groups(proj, P_HY, p['hy_conv_w'], p['hy_conv_b'], group=c, tc=CONV_TC, silu=False,
                           name="hyena_short_conv", out_dtype=BF16)
    vx = vx.reshape(3 * npair, HY_NA, lanes)
    z, z_off = vx, 0
    for n in range(HY_ORDER):
        y1 = dft_first(g1_bf, z, n_out=npair, offset=z_off)
        u = dft_mid(g2_bf, g2_inv_bf, y1.reshape(npair, 2, HY_NA, HY_NB, c), hspec, n)
        skip_lanes = jnp.tile(p['hy_skip'][n], 8192 // c).reshape(1, 8192)
        z = dft_last(g1_inv_bf, u.reshape(npair, 2 * HY_NA, lanes), z, z_off, vx, (n + 1) * npair, skip_lanes,
                     out_dtype=BF16 if n + 1 < HY_ORDER else F32)
        z_off = 0
    return z.reshape(bsz, l, c)


def _log_sigmoid(x):
    return jnp.minimum(x, 0.0) - jnp.log1p(jnp.exp(-jnp.abs(x)))


def _mlstm_kernel(*refs, nc_ctx, direction):
    ins, bias_ref, outs, (c_ref, m_ref) = refs[:8], refs[8], refs[9:11], refs[11:13]
    d = ML_HEAD_DIM
    t = ML_CHUNK
    nh = ML_HEADS
    step = pl.program_id(1)

    @pl.when(step == 0)
    def _():
        c_ref[...] = jnp.zeros_like(c_ref)
        m_ref[...] = jnp.zeros_like(m_ref)

    i0 = S_GATES + direction * nh
    f0 = S_GATES + (2 + direction) * nh

    def chunk(q_ref, k_ref, v_ref, sm_ref, o_ref):
        g = sm_ref[0] + bias_ref[...]
        gt = g.T
        lf_cols = _log_sigmoid(g)
        lf_rows = _log_sigmoid(gt)
        r = lax.broadcasted_iota(jnp.int32, (t, t), 0)
        cidx = lax.broadcasted_iota(jnp.int32, (t, t), 1)
        mask = cidx <= r if direction == 0 else cidx >= r
        mask_t = r <= cidx if direction == 0 else r >= cidx
        tri = mask.astype(F32)
        tri_t = mask_t.astype(F32)
        lane = lax.broadcasted_iota(jnp.int32, (t, d), 1)
        ones_col = jnp.where(lane == 0, 1.0, 0.0).astype(BF16)
        for h in range(nh):
            sl = slice(h * d, (h + 1) * d)
            lf_row = lf_rows[f0 + h:f0 + h + 1, :]
            bc = jnp.sum(tri * lf_row, axis=1, keepdims=True)
            br = jnp.sum(tri_t * lf_cols[:, f0 + h:f0 + h + 1], axis=0, keepdims=True)
            lir = gt[i0 + h:i0 + h + 1, :]
            m_prev = m_ref[h, 0:1, 0:1]
            dmat = jnp.where(mask, bc - br + lir, -jnp.inf)
            inter = bc + m_prev
            m_t = jnp.maximum(inter, jnp.max(dmat, axis=1, keepdims=True))
            dexp = jnp.exp(dmat - m_t)
            inter_w = jnp.exp(inter - m_t)
            qh = (q_ref[0, 0, :, sl] * (d ** -0.5)).astype(BF16)
            kt = k_ref[0, 0, :, sl].T
            v_ext = jnp.concatenate([v_ref[0, :, sl].astype(BF16), ones_col], axis=1)
            s = jnp.dot(qh, kt.astype(BF16), preferred_element_type=F32) * dexp
            c_ext = c_ref[h]
            acc = inter_w * jnp.dot(qh, c_ext.astype(BF16), preferred_element_type=F32)
            acc = acc + jnp.dot(s.astype(BF16), v_ext, preferred_element_type=F32)
            den = jnp.maximum(jnp.abs(acc[:, d:d + 1]), jnp.exp(-m_t))
            o_ref[0, :, sl] = acc[:, :d] / den
            b_last = jnp.sum(lf_row, axis=1, keepdims=True)
            g_row = b_last - br + lir
            m_new = jnp.maximum(b_last + m_prev, jnp.max(g_row, axis=1, keepdims=True))
            a = jnp.exp(b_last + m_prev - m_new)
            w_row = jnp.exp(g_row - m_new)
            c_ref[h] = a * c_ext + jnp.dot((kt * w_row).astype(BF16), v_ext, preferred_element_type=F32)
            m_ref[h] = jnp.broadcast_to(m_new, m_ref.shape[1:])

    @pl.when(step < nc_ctx)
    def _():
        chunk(*ins[:4], outs[0])

    @pl.when(step >= nc_ctx)
    def _():
        chunk(*ins[4:], outs[1])


def mlstm_bidir(qk_c, proj_c, small_c, qk_l, proj_l, small_l, gate_bias):
    _, bsz, lc, w = qk_c.shape
    ll = qk_l.shape[2]
    t, nh = ML_CHUNK, ML_HEADS
    nc_c, nc_l = lc // t, ll // t
    bias = jnp.zeros((1, N_SMALL), F32).at[0, S_GATES:S_GATES + ML_GATES].set(gate_bias)

    def idx(dd, step_off, nc):
        def f(g):
            cc = jnp.clip(g - step_off, 0, nc - 1)
            return cc + dd * (nc - 1 - 2 * cc)
        return f

    vcol = P_MLV // w

    def seg_specs(ix):
        return [pl.BlockSpec((1, 1, t, w), lambda b, g: (0, b, ix(g), 0)),
                pl.BlockSpec((1, 1, t, w), lambda b, g: (1, b, ix(g), 0)),
                pl.BlockSpec((1, t, w), lambda b, g: (b, ix(g), vcol)),
                pl.BlockSpec((1, t, N_SMALL), lambda b, g: (b, ix(g), 0))]

    def out_spec(ix):
        return pl.BlockSpec((1, t, w), lambda b, g: (b, ix(g), 0))

    h_c, h_l = [], []
    for dd in range(2):
        ic, il = idx(dd, 0, nc_c), idx(dd, nc_c, nc_l)
        hc, hl = pl.pallas_call(
            functools.partial(_mlstm_kernel, nc_ctx=nc_c, direction=dd),
            grid=(bsz, nc_c + nc_l),
            in_specs=seg_specs(ic) + seg_specs(il) + [pl.BlockSpec((1, N_SMALL), lambda b, g: (0, 0))],
            out_specs=[out_spec(ic), out_spec(il)],
            out_shape=[jax.ShapeDtypeStruct((bsz, lc, w), F32), jax.ShapeDtypeStruct((bsz, ll, w), F32)],
            scratch_shapes=[pltpu.VMEM((nh, ML_HEAD_DIM, 2 * ML_HEAD_DIM), F32), pltpu.VMEM((nh, 8, 128), F32)],
            compiler_params=_cparams(("parallel", "arbitrary")),
            name="mlstm_fwd" if dd == 0 else "mlstm_bwd",
        )(qk_c, qk_c, proj_c, small_c, qk_l, qk_l, proj_l, small_l, bias)
        h_c.append(hc)
        h_l.append(hl)
    return tuple(h_c), tuple(h_l)


def _rms(x):
    return x * lax.rsqrt(jnp.mean(x * x, axis=-1, keepdims=True) + EPS)


def _swap_rope_halves(x):
    w = x.shape[1]
    lane = lax.broadcasted_iota(jnp.int32, x.shape, 1) % MLA_ROPE
    return jnp.where(lane < MLA_ROPE // 2, pltpu.roll(x, w - MLA_ROPE // 2, axis=1),
                     pltpu.roll(x, MLA_ROPE // 2, axis=1))


def _mla_prep_kernel(pq_ref, pkv_ref, sm_ref, gqa_ref, gkva_ref, wq_ref, wkv_ref, gqn_ref, gqr_ref, gkn_ref,
                     gkr_ref, en_ref, er_ref, *rest, use_rope):
    if use_rope:
        ccq_ref, ssq_ref, cck_ref, ssk_ref, qt_ref, k_ref, vt_ref = rest
    else:
        qt_ref, k_ref, vt_ref = rest
    nh, dn, dr = MLA_HEADS, MLA_NOPE, MLA_ROPE
    qa = (_rms(pq_ref[0].astype(F32)) * gqa_ref[...]).astype(BF16)
    kva = (_rms(pkv_ref[0].astype(F32)) * gkva_ref[...]).astype(BF16)
    q_raw = jnp.dot(qa, wq_ref[...], preferred_element_type=F32)
    kv_raw = jnp.dot(kva, wkv_ref[...], preferred_element_type=F32)
    inv_d = 1.0 / MLA_QK

    sel = functools.partial(jnp.dot, preferred_element_type=F32)
    qn, qr = q_raw[:, :nh * dn], q_raw[:, nh * dn:]
    ss = sel((qn * qn).astype(BF16), en_ref[...]) + sel((qr * qr).astype(BF16), er_ref[...])
    rs_rows = lax.rsqrt(ss * inv_d + EPS).T
    qn = qn * gqn_ref[...]
    qr = qr * gqr_ref[...]
    if use_rope:
        qr = qr * ccq_ref[...] + _swap_rope_halves(qr) * ssq_ref[...]
    qrt = qr.T
    tl = qn.shape[0]
    for h in range(nh):
        rs_h = rs_rows[h:h + 1, :]
        qt_ref[0, h, 0:dn, :] = (qn[:, h * dn:(h + 1) * dn].T * rs_h).astype(BF16)
        qt_ref[0, h, dn:dn + dr, :] = (qrt[h * dr:(h + 1) * dr] * rs_h).astype(BF16)
        qt_ref[0, h, dn + dr:, :] = jnp.zeros((QK_PAD - dn - dr, tl), BF16)

    kn, vv = kv_raw[:, :nh * dn], kv_raw[:, nh * dn:]
    lane = lax.broadcasted_iota(jnp.int32, sm_ref.shape[1:], 1)
    kr = jnp.where(lane < dr, sm_ref[0], 0.0)
    ssk = sel((kn * kn).astype(BF16), en_ref[...]) + jnp.sum(kr * kr, axis=-1, keepdims=True)
    rsk = lax.rsqrt(ssk * inv_d + EPS)
    kn = kn * gkn_ref[...]
    kr = kr * gkr_ref[...]
    if use_rope:
        kr = kr * cck_ref[...] + _swap_rope_halves(kr) * ssk_ref[...]
    for h in range(nh):
        rs_h = jnp.broadcast_to(rsk[:, h:h + 1], (tl, dn))
        k_ref[0, h, :, 0:dn] = (kn[:, h * dn:(h + 1) * dn] * rs_h).astype(BF16)
        k_ref[0, h, :, dn:] = (kr * rs_h).astype(BF16)
        vt_ref[0, h] = vv[:, h * MLA_V:(h + 1) * MLA_V].T.astype(BF16)


@functools.lru_cache(maxsize=None)
def _head_selectors():
    en = np.kron(np.eye(MLA_HEADS), np.ones((MLA_NOPE, 1)))
    er = np.kron(np.eye(MLA_HEADS), np.ones((MLA_ROPE, 1)))
    pad = lambda m: np.pad(m, ((0, 0), (0, 128 - m.shape[1]))).astype(np.float32)
    return pad(en), pad(er)


def _rope_tables(n_tokens):
    rows = n_tokens // GRID_W
    row = jnp.repeat(jnp.arange(rows, dtype=F32), GRID_W)
    col = jnp.tile(jnp.arange(GRID_W, dtype=F32), rows)
    freqs = ROPE_BASE ** (-jnp.arange(ROPE_PAIRS_PER_AXIS, dtype=F32) / ROPE_PAIRS_PER_AXIS)
    ang = jnp.concatenate([row[:, None] * freqs, col[:, None] * freqs], axis=-1)
    cos, sin = jnp.cos(ang), jnp.sin(ang)
    cc = jnp.concatenate([cos, cos], axis=-1)
    ss = jnp.concatenate([-sin, sin], axis=-1)
    zero = jnp.zeros((n_tokens, 128 - MLA_ROPE), F32)
    return (jnp.tile(cc, (1, MLA_HEADS)), jnp.tile(ss, (1, MLA_HEADS)),
            jnp.concatenate([cc, zero], axis=-1), jnp.concatenate([ss, zero], axis=-1))


def mla_prep(proj, small, p, rope_tables, *, tl):
    b, l, _ = proj.shape
    nh = MLA_HEADS
    en, er = _head_selectors()
    en_b, er_b = jnp.asarray(en, BF16), jnp.asarray(er, BF16)
    const = lambda shape: pl.BlockSpec(shape, lambda bi, i: (0,) * len(shape))
    nq = nh * MLA_QK
    nkv = nh * (MLA_NOPE + MLA_V)
    in_specs = [pl.BlockSpec((1, tl, MLA_Q_LORA), lambda bi, i: (bi, i, P_CQ // MLA_Q_LORA)),
                pl.BlockSpec((1, tl, MLA_KV_LORA), lambda bi, i: (bi, i, P_CKV // MLA_KV_LORA)),
                pl.BlockSpec((1, tl, N_SMALL), lambda bi, i: (bi, i, 0)),
                const((1, MLA_Q_LORA)), const((1, MLA_KV_LORA)), const((MLA_Q_LORA, nq)), const((MLA_KV_LORA, nkv)),
                const((1, nh * MLA_NOPE)), const((1, nh * MLA_ROPE)), const((1, nh * MLA_NOPE)), const((1, 128)),
                const(en.shape), const(er.shape)]
    operands = [proj, proj, small, p['gqa'], p['gkva'], p['wq'], p['wkv'], p['gqn'], p['gqr'], p['gkn'], p['gkr'],
                en_b, er_b]
    use_rope = rope_tables is not None
    if use_rope:
        in_specs += [pl.BlockSpec((tl, nh * MLA_ROPE), lambda bi, i: (i, 0)),
                     pl.BlockSpec((tl, nh * MLA_ROPE), lambda bi, i: (i, 0)),
                     pl.BlockSpec((tl, 128), lambda bi, i: (i, 0)),
                     pl.BlockSpec((tl, 128), lambda bi, i: (i, 0))]
        operands += list(rope_tables)
    return pl.pallas_call(
        functools.partial(_mla_prep_kernel, use_rope=use_rope),
        grid=(b, l // tl),
        in_specs=in_specs,
        out_specs=[pl.BlockSpec((1, nh, QK_PAD, tl), lambda bi, i: (bi, 0, 0, i)),
                   pl.BlockSpec((1, nh, tl, QK_PAD), lambda bi, i: (bi, 0, i, 0)),
                   pl.BlockSpec((1, nh, MLA_V, tl), lambda bi, i: (bi, 0, 0, i))],
        out_shape=[jax.ShapeDtypeStruct((b, nh, QK_PAD, l), BF16),
                   jax.ShapeDtypeStruct((b, nh, l, QK_PAD), BF16),
                   jax.ShapeDtypeStruct((b, nh, MLA_V, l), BF16)],
        compiler_params=_cparams(("parallel", "parallel")),
        name="mla_prep",
    )(*operands)


def _combine_out_kernel(a_ref, y_ref, hf_ref, hb_ref, o_ref, g_ref, w_ref, r_ref, gt_ref, out_ref, cat_ref):
    @pl.when(pl.program_id(2) == 0)
    def _():
        g = g_ref[...]
        cat_ref[:, :MLA_WIDTH] = (_rms(a_ref[0]) * g[:, :MLA_WIDTH]).astype(BF16)
        hy0 = MLA_WIDTH
        cat_ref[:, hy0:hy0 + HY_WIDTH] = (_rms(y_ref[0]) * g[:, hy0:hy0 + HY_WIDTH]).astype(BF16)
        ml0 = MLA_WIDTH + HY_WIDTH
        hh = hf_ref[0] + hb_ref[0]
        og = 0.5 * jnp.tanh(0.5 * o_ref[0].astype(F32)) + 0.5
        for hd in range(ML_HEADS):
            sl = slice(hd * ML_HEAD_DIM, (hd + 1) * ML_HEAD_DIM)
            gl = g[:, ml0 + hd * ML_HEAD_DIM:ml0 + (hd + 1) * ML_HEAD_DIM]
            cat_ref[:, ml0 + hd * ML_HEAD_DIM:ml0 + (hd + 1) * ML_HEAD_DIM] = (
                _rms(hh[:, sl]) * gl * og[:, sl]).astype(BF16)

    y = jnp.dot(cat_ref[...], w_ref[...], preferred_element_type=F32)
    out_ref[0] = r_ref[0] + gt_ref[0] * y


def combine_out_proj(a, y, h2, proj, g_mix, w_out, res, gate, *, tm, tn, name="out_proj"):
    b, s, d = res.shape
    return pl.pallas_call(
        _combine_out_kernel,
        grid=(b, s // tm, d // tn),
        in_specs=[pl.BlockSpec((1, tm, MLA_WIDTH), lambda bi, i, j: (bi, i, 0)),
                  pl.BlockSpec((1, tm, HY_WIDTH), lambda bi, i, j: (bi, i, 0)),
                  pl.BlockSpec((1, tm, ML_WIDTH), lambda bi, i, j: (bi, i, 0)),
                  pl.BlockSpec((1, tm, ML_WIDTH), lambda bi, i, j: (bi, i, 0)),
                  pl.BlockSpec((1, tm, ML_WIDTH), lambda bi, i, j: (bi, i, P_MLO // ML_WIDTH)),
                  pl.BlockSpec((1, MIX_WIDTH), lambda bi, i, j: (0, 0)),
                  pl.BlockSpec((MIX_WIDTH, tn), lambda bi, i, j: (0, j)),
                  pl.BlockSpec((1, tm, tn), lambda bi, i, j: (bi, i, j)),
                  pl.BlockSpec((1, 1, tn), lambda bi, i, j: (bi, 0, j))],
        out_specs=pl.BlockSpec((1, tm, tn), lambda bi, i, j: (bi, i, j)),
        out_shape=jax.ShapeDtypeStruct((b, s, d), F32),
        scratch_shapes=[pltpu.VMEM((tm, MIX_WIDTH), BF16)],
        compiler_params=_cparams(("parallel", "parallel", "arbitrary")),
        name=name,
    )(a, y, h2[0], h2[1], proj, g_mix.reshape(1, MIX_WIDTH), w_out, res, gate)


@functools.lru_cache(maxsize=None)
def _direct_dft_tables(l):
    n = 2 * l
    k = np.arange(n)[:, None]
    t = np.arange(l)[None, :]
    ang = 2.0 * np.pi * ((k * t) % n) / n
    c, s = np.cos(ang), np.sin(ang)
    gf = np.block([[c, s], [-s, c]])
    gi = gf.T.copy()
    t_full = np.arange(n)[None, :]
    ang = 2.0 * np.pi * ((k * t_full) % n) / n
    g_real = np.concatenate([np.cos(ang), -np.sin(ang)], axis=0)
    return gf.astype(np.float32), gi.astype(np.float32), g_real.astype(np.float32)


def _direct_conv_kernel(gf_ref, gi_ref, z_ref, gate_ref, h_ref, skip_ref, o_ref, *, scale):
    z = z_ref[0]
    x = jnp.dot(gf_ref[...].astype(BF16), z.astype(BF16), preferred_element_type=F32)
    n = x.shape[0] // 2
    xr, xi = x[:n], x[n:]
    hr, hi = h_ref[:n] * scale, h_ref[n:] * scale
    prod = jnp.concatenate([xr * hr - xi * hi, xr * hi + xi * hr], axis=0).astype(BF16)
    conv = jnp.dot(gi_ref[...].astype(BF16), prod, preferred_element_type=F32)
    o_ref[0] = gate_ref[0] * (conv + skip_ref[...] * z)


def hyena_short(proj, p):
    bsz, l, _ = proj.shape
    c = HY_WIDTH
    npair = bsz // 2
    n = 2 * l
    gf, gi, g_real = _direct_dft_tables(l)
    buf = hyena_filter_buffer(l, p, tl=l)
    hspec = dft_first(jnp.asarray(g_real), buf.reshape(1, n, HY_ORDER * c), n_out=1, tn=HY_ORDER * c,
                      precise=True)[0]
    vx = short_conv_groups(proj, P_HY, p['hy_conv_w'], p['hy_conv_b'], group=c, tc=CONV_TC, silu=False,
                           name="hyena_short_conv_ctx")
    vx = vx.reshape(3 * npair, 2 * l, c)
    z = vx

    def rows_at(off):
        return lambda q: (off + q, 0, 0)

    def cols_at(col):
        return lambda q: (0, col)

    for order in range(HY_ORDER):
        z = pl.pallas_call(
            functools.partial(_direct_conv_kernel, scale=1.0 / n),
            grid=(npair,),
            in_specs=[pl.BlockSpec(gf.shape, lambda q: (0, 0)),
                      pl.BlockSpec(gi.shape, lambda q: (0, 0)),
                      pl.BlockSpec((1, 2 * l, c), rows_at(0)),
                      pl.BlockSpec((1, 2 * l, c), rows_at((order + 1) * npair)),
                      pl.BlockSpec((2 * n, c), cols_at(order)),
                      pl.BlockSpec((1, c), lambda q: (0, 0))],
            out_specs=pl.BlockSpec((1, 2 * l, c), lambda q: (q, 0, 0)),
            out_shape=jax.ShapeDtypeStruct((npair, 2 * l, c), F32),
            compiler_params=_cparams(("parallel",)),
            name="hyena_direct_conv",
        )(jnp.asarray(gf), jnp.asarray(gi), z, vx, hspec, p['hy_skip'][order].reshape(1, c))
    return z.reshape(bsz, l, c)


def _split_cols(w, sizes):
    out, start = [], 0
    for s in sizes:
        out.append(w[:, start:start + s])
        start += s
    return out


def _prep_layer_weights(l, w_in, mla_qa_norm, mla_kva_norm, mla_w_uq, mla_w_ukv, mla_q_norm, mla_k_norm):
    cq, ckv, kr, hy, mqk, mv, mo, gt = _split_cols(w_in[l], IN_SIZES)
    w_main = jnp.concatenate([cq, ckv, hy, mqk, mv, mo], axis=1).astype(BF16)
    pad = jnp.zeros((D_MODEL, N_SMALL - MLA_ROPE - ML_GATES), F32)
    w_small = jnp.concatenate([kr, gt, pad], axis=1).astype(BF16)
    nh = MLA_HEADS
    wq = mla_w_uq[l].reshape(MLA_Q_LORA, nh, MLA_QK)
    wq = jnp.concatenate([wq[:, :, :MLA_NOPE].reshape(MLA_Q_LORA, -1), wq[:, :, MLA_NOPE:].reshape(MLA_Q_LORA, -1)],
                         axis=1).astype(BF16)
    wkv = mla_w_ukv[l].reshape(MLA_KV_LORA, nh, MLA_NOPE + MLA_V)
    wkv = jnp.concatenate([wkv[:, :, :MLA_NOPE].reshape(MLA_KV_LORA, -1),
                           wkv[:, :, MLA_NOPE:].reshape(MLA_KV_LORA, -1)], axis=1).astype(BF16)
    q_scale = (MLA_QK ** -0.5) * math.log2(math.e)
    gq, gk = mla_q_norm[l] * q_scale, mla_k_norm[l]
    mla = {'gqa': mla_qa_norm[l].reshape(1, -1), 'gkva': mla_kva_norm[l].reshape(1, -1), 'wq': wq, 'wkv': wkv,
           'gqn': jnp.tile(gq[:MLA_NOPE], nh).reshape(1, -1), 'gqr': jnp.tile(gq[MLA_NOPE:], nh).reshape(1, -1),
           'gkn': jnp.tile(gk[:MLA_NOPE], nh).reshape(1, -1),
           'gkr': jnp.concatenate([gk[MLA_NOPE:], jnp.zeros((128 - MLA_ROPE,), F32)]).reshape(1, -1)}
    return w_main, w_small, mla


def kernel(x, c, ctx, c_ctx, ada_w, ada_b, norm1_g, norm2_g, w_in, mla_qa_norm, mla_kva_norm, mla_w_uq,
           mla_w_ukv, mla_q_norm, mla_k_norm, hy_conv_w, hy_conv_b, hy_w1, hy_b1, hy_w2, hy_b2, hy_w3,
           hy_decay, hy_skip, ml_conv_w, ml_conv_b, ml_gate_b, mix_norm_g, w_out, ffn_w1, ffn_w2):
    B, S, D = x.shape
    LC = ctx.shape[1]
    rope_tables = _rope_tables(S)

    silu_rows = jnp.concatenate([jax.nn.silu(c), jax.nn.silu(c_ctx)[None], jnp.zeros((8 - B - 1, D), F32)], axis=0)
    mods = ada_modulation(silu_rows.astype(BF16), ada_w, ada_b)

    for l in range(DEPTH):
        need_ctx = l < DEPTH - 1
        p = {'hy_conv_w': hy_conv_w[l], 'hy_conv_b': hy_conv_b[l], 'hy_w1': hy_w1[l], 'hy_b1': hy_b1[l],
             'hy_w2': hy_w2[l], 'hy_b2': hy_b2[l], 'hy_w3': hy_w3[l], 'hy_decay': hy_decay[l],
             'hy_skip': hy_skip[l], 'ml_conv_w': ml_conv_w[l], 'ml_conv_b': ml_conv_b[l]}
        w_main, w_small, mla = _prep_layer_weights(l, w_in, mla_qa_norm, mla_kva_norm, mla_w_uq, mla_w_ukv,
                                                   mla_q_norm, mla_k_norm)
        w_out_l = w_out[l].astype(BF16)
        w1_l = ffn_w1[l].astype(BF16)
        w2_l = ffn_w2[l].astype(BF16)

        mod_l = [m[:, None, :] for m in jnp.split(mods[l, :B], 6, axis=-1)]
        mod_c = [m[:, None, :] for m in jnp.split(mods[l, B:B + 1], 6, axis=-1)]
        flat = lambda t: t.reshape(1, B * LC, t.shape[-1])

        proj_l, small_l = in_proj(x, norm1_g[l], mod_l[1], mod_l[0], w_main, w_small, tm=1024, tn=IN_PROJ_TN)
        proj_c, small_c = in_proj(flat(ctx), norm1_g[l], mod_c[1], mod_c[0], w_main, w_small, tm=B * LC,
                                  tn=IN_PROJ_TN, name="in_proj_ctx")
        proj_c, small_c = proj_c.reshape(B, LC, N_MAIN), small_c.reshape(B, LC, N_SMALL)

        qt_l, k_l, vt_l = mla_prep(proj_l, small_l, mla, rope_tables, tl=512)
        qt_c, k_c, vt_c = mla_prep(proj_c, small_c, mla, None, tl=LC)
        a_l = attention(qt_l, [(k_c, vt_c), (k_l, vt_l)], tq=ATT_TQ)
        y_l = hyena_long(proj_l, p)

        def mlstm_qk(proj):
            return short_conv_groups(proj, P_MLQK, p['ml_conv_w'], p['ml_conv_b'], group=ML_WIDTH, tc=CONV_TC,
                                     silu=True, name="mlstm_short_conv")

        h_c, h_l = mlstm_bidir(mlstm_qk(proj_c), proj_c, small_c, mlstm_qk(proj_l), proj_l, small_l, ml_gate_b[l])
        x = combine_out_proj(a_l, y_l, h_l, proj_l, mix_norm_g[l], w_out_l, x, mod_l[2], tm=1024, tn=MM_TN)
        hid = norm_swiglu(x, norm2_g[l], mod_l[4], mod_l[3], w1_l, tm=1024, tn=MM_TN, name="ffn_up")
        x = mm_residual(hid, w2_l, x, mod_l[5], tm=1024, tn=MM_TN, name="ffn_down")

        if need_ctx:
            a_c = attention(qt_c, [(k_c, vt_c)], tq=LC)
            y_c = hyena_short(proj_c, p)
            ctx = combine_out_proj(flat(a_c), flat(y_c), (flat(h_c[0]), flat(h_c[1])), flat(proj_c), mix_norm_g[l],
                                   w_out_l, flat(ctx), mod_c[2], tm=B * LC, tn=MM_TN, name="out_proj_ctx")
            hid = norm_swiglu(ctx, norm2_g[l], mod_c[4], mod_c[3], w1_l, tm=B * LC, tn=MM_TN, name="ffn_up_ctx")
            ctx = mm_residual(hid, w2_l, ctx, mod_c[5], tm=B * LC, tn=MM_TN, name="ffn_down_ctx")
            ctx = ctx.reshape(B, LC, D)
    return x
```

```python
import functools
import math

import jax
import jax.numpy as jnp
import numpy as np
from jax import lax
from jax.experimental import pallas as pl
from jax.experimental.pallas import tpu as pltpu

D_MODEL = 2048
DEPTH = 2
GRID_W = 64
EPS = 1e-6

MLA_HEADS = 8
MLA_NOPE = 128
MLA_ROPE = 64
MLA_QK = MLA_NOPE + MLA_ROPE
MLA_V = 128
MLA_Q_LORA = 512
MLA_KV_LORA = 512
MLA_WIDTH = MLA_HEADS * MLA_V
ROPE_BASE = 10000.0
ROPE_PAIRS_PER_AXIS = MLA_ROPE // 4
QK_PAD = 256

HY_WIDTH = 512
HY_ORDER = 2
HY_BANDS = 16
HY_SHIFT = 0.05
SHORT_CONV = 3
CONV_TC = 256

ML_HEADS = 4
ML_HEAD_DIM = 128
ML_WIDTH = ML_HEADS * ML_HEAD_DIM
ML_CHUNK = 128
ML_GATES = 4 * ML_HEADS

MIX_WIDTH = MLA_WIDTH + HY_WIDTH + ML_WIDTH
FFN_HIDDEN = ((8 * D_MODEL // 3 + 255) // 256) * 256

IN_SIZES = (MLA_Q_LORA, MLA_KV_LORA, MLA_ROPE, 3 * HY_WIDTH, 2 * ML_WIDTH, ML_WIDTH, ML_WIDTH, ML_GATES)
N_IN = sum(IN_SIZES)

P_CQ, P_CKV, P_HY, P_MLQK, P_MLV, P_MLO = 0, 512, 1024, 2560, 3584, 4096
N_MAIN = 4608
N_SMALL = 128
S_GATES = MLA_ROPE

VMEM_LIMIT_BYTES = 56 * 1024 * 1024

BF16 = jnp.bfloat16
F32 = jnp.float32


def _cparams(sem):
    return pltpu.CompilerParams(dimension_semantics=sem, vmem_limit_bytes=VMEM_LIMIT_BYTES)


def _ada_kernel(s_ref, w_ref, b_ref, o_ref):
    w = w_ref[0].astype(BF16)
    o_ref[0] = jnp.dot(s_ref[...], w, preferred_element_type=F32) + b_ref[0]


def ada_modulation(silu_rows, ada_w, ada_b, tn=1024):
    depth, d, n = ada_w.shape
    return pl.pallas_call(
        _ada_kernel,
        grid=(depth, n // tn),
        in_specs=[pl.BlockSpec((8, d), lambda l, j: (0, 0)),
                  pl.BlockSpec((1, d, tn), lambda l, j: (l, 0, j)),
                  pl.BlockSpec((1, 1, tn), lambda l, j: (l, 0, j))],
        out_specs=pl.BlockSpec((1, 8, tn), lambda l, j: (l, 0, j)),
        out_shape=jax.ShapeDtypeStruct((depth, 8, n), F32),
        compiler_params=_cparams(("parallel", "parallel")),
        name="ada_modulation",
    )(silu_rows, ada_w, ada_b.reshape(depth, 1, n))


MM_TN = 512
IN_PROJ_TN = 1536


def _normed(x, g, sc, sh):
    y = x * lax.rsqrt(jnp.mean(x * x, axis=-1, keepdims=True) + EPS) * g
    return y * (1.0 + sc) + sh


def _in_proj_kernel(x_ref, g_ref, sc_ref, sh_ref, w_ref, ws_ref, o_ref, os_ref, xn_ref):
    @pl.when(pl.program_id(2) == 0)
    def _():
        xn = _normed(x_ref[0], g_ref[...], sc_ref[0], sh_ref[0]).astype(BF16)
        xn_ref[...] = xn
        os_ref[0] = jnp.dot(xn, ws_ref[...], preferred_element_type=F32)

    o_ref[0] = jnp.dot(xn_ref[...], w_ref[...], preferred_element_type=F32).astype(o_ref.dtype)


def in_proj(x, g, sc, sh, w_main, w_small, *, tm, tn, name="in_proj"):
    b, s, k = x.shape
    n = w_main.shape[1]
    ns = w_small.shape[1]
    return pl.pallas_call(
        _in_proj_kernel,
        grid=(b, s // tm, n // tn),
        in_specs=[pl.BlockSpec((1, tm, k), lambda bi, i, j: (bi, i, 0)),
                  pl.BlockSpec((1, k), lambda bi, i, j: (0, 0)),
                  pl.BlockSpec((1, 1, k), lambda bi, i, j: (bi, 0, 0)),
                  pl.BlockSpec((1, 1, k), lambda bi, i, j: (bi, 0, 0)),
                  pl.BlockSpec((k, tn), lambda bi, i, j: (0, j)),
                  pl.BlockSpec((k, ns), lambda bi, i, j: (0, 0))],
        out_specs=[pl.BlockSpec((1, tm, tn), lambda bi, i, j: (bi, i, j)),
                   pl.BlockSpec((1, tm, ns), lambda bi, i, j: (bi, i, 0))],
        out_shape=[jax.ShapeDtypeStruct((b, s, n), BF16), jax.ShapeDtypeStruct((b, s, ns), F32)],
        scratch_shapes=[pltpu.VMEM((tm, k), BF16)],
        compiler_params=_cparams(("parallel", "parallel", "arbitrary")),
        name=name,
    )(x, g.reshape(1, k), sc, sh, w_main, w_small)


def _norm_swiglu_kernel(x_ref, g_ref, sc_ref, sh_ref, wg_ref, wu_ref, o_ref, xn_ref):
    @pl.when(pl.program_id(2) == 0)
    def _():
        xn_ref[...] = _normed(x_ref[0], g_ref[...], sc_ref[0], sh_ref[0]).astype(BF16)

    xn = xn_ref[...]
    gate = jnp.dot(xn, wg_ref[...], preferred_element_type=F32)
    up = jnp.dot(xn, wu_ref[...], preferred_element_type=F32)
    o_ref[0] = (gate * (0.5 * jnp.tanh(0.5 * gate) + 0.5) * up).astype(o_ref.dtype)


def norm_swiglu(x, g, sc, sh, w1, *, tm, tn, name="norm_swiglu"):
    b, s, d = x.shape
    h = w1.shape[1] // 2
    nj = h // tn
    return pl.pallas_call(
        _norm_swiglu_kernel,
        grid=(b, s // tm, nj),
        in_specs=[pl.BlockSpec((1, tm, d), lambda bi, i, j: (bi, i, 0)),
                  pl.BlockSpec((1, d), lambda bi, i, j: (0, 0)),
                  pl.BlockSpec((1, 1, d), lambda bi, i, j: (bi, 0, 0)),
                  pl.BlockSpec((1, 1, d), lambda bi, i, j: (bi, 0, 0)),
                  pl.BlockSpec((d, tn), lambda bi, i, j: (0, j)),
                  pl.BlockSpec((d, tn), lambda bi, i, j: (0, j + nj))],
        out_specs=pl.BlockSpec((1, tm, tn), lambda bi, i, j: (bi, i, j)),
        out_shape=jax.ShapeDtypeStruct((b, s, h), BF16),
        scratch_shapes=[pltpu.VMEM((tm, d), BF16)],
        compiler_params=_cparams(("parallel", "parallel", "arbitrary")),
        name=name,
    )(x, g.reshape(1, d), sc, sh, w1, w1)


def _mm_res_kernel(a_ref, w_ref, r_ref, gt_ref, o_ref):
    y = jnp.dot(a_ref[0], w_ref[...], preferred_element_type=F32)
    o_ref[0] = r_ref[0] + gt_ref[0] * y


def mm_residual(a, w, res, gate, *, tm, tn, name="mm_residual"):
    b, s, k = a.shape
    n = w.shape[1]
    return pl.pallas_call(
        _mm_res_kernel,
        grid=(b, s // tm, n // tn),
        in_specs=[pl.BlockSpec((1, tm, k), lambda bi, i, j: (bi, i, 0)),
                  pl.BlockSpec((k, tn), lambda bi, i, j: (0, j)),
                  pl.BlockSpec((1, tm, tn), lambda bi, i, j: (bi, i, j)),
                  pl.BlockSpec((1, 1, tn), lambda bi, i, j: (bi, 0, j))],
        out_specs=pl.BlockSpec((1, tm, tn), lambda bi, i, j: (bi, i, j)),
        out_shape=jax.ShapeDtypeStruct((b, s, n), F32),
        compiler_params=_cparams(("parallel", "parallel", "parallel")),
        name=name,
    )(a, w, res, gate)


ATT_KEY_CHUNK = 512
ATT_TQ = 1024


def _attn_kernel(qt_ref, *refs, n_seg):
    o_ref, s_ref = refs[2 * n_seg], refs[2 * n_seg + 1]
    qt = qt_ref[0, 0]
    tq = qt.shape[1]
    chunks, off = [], 0
    for sgi in range(n_seg):
        k_ref, vt_ref = refs[2 * sgi], refs[2 * sgi + 1]
        lk = k_ref.shape[2]
        ch = min(ATT_KEY_CHUNK, lk)
        for c in range(lk // ch):
            chunks.append((k_ref, vt_ref, c * ch, ch, off))
            off += ch

    m8 = jnp.full((8, tq), -jnp.inf, F32)
    for k_ref, _, start, ch, off in chunks:
        s = jnp.dot(k_ref[0, 0, start:start + ch, :], qt, preferred_element_type=F32)
        s_ref[off:off + ch, :] = s
        m8 = jnp.maximum(m8, jnp.max(s.reshape(ch // 8, 8, tq), axis=0))
    m = jnp.max(m8, axis=0, keepdims=True)
    l8 = jnp.zeros((8, tq), F32)
    acc = jnp.zeros((vt_ref.shape[2], tq), F32)
    for _, vt_ref, start, ch, off in chunks:
        p = jnp.exp2(s_ref[off:off + ch, :] - m)
        l8 = l8 + jnp.sum(p.reshape(ch // 8, 8, tq), axis=0)
        acc = acc + jnp.dot(vt_ref[0, 0, :, start:start + ch], p.astype(BF16), preferred_element_type=F32)
    l = jnp.sum(l8, axis=0, keepdims=True)
    o_ref[0] = (acc / l).T


def attention(qt, segments, *, tq=256):
    b, h, dk, lq = qt.shape
    dv = segments[0][1].shape[2]
    in_specs = [pl.BlockSpec((1, 1, dk, tq), lambda bi, hi, i: (bi, hi, 0, i))]
    operands = [qt]
    for k, vt in segments:
        lk = k.shape[2]
        in_specs += [pl.BlockSpec((1, 1, lk, dk), lambda bi, hi, i: (bi, hi, 0, 0)),
                     pl.BlockSpec((1, 1, dv, lk), lambda bi, hi, i: (bi, hi, 0, 0))]
        operands += [k, vt]
    return pl.pallas_call(
        functools.partial(_attn_kernel, n_seg=len(segments)),
        grid=(b, h, lq // tq),
        in_specs=in_specs,
        out_specs=pl.BlockSpec((1, tq, dv), lambda bi, hi, i: (bi, i, hi)),
        out_shape=jax.ShapeDtypeStruct((b, lq, h * dv), F32),
        scratch_shapes=[pltpu.VMEM((sum(k.shape[2] for k, _ in segments), tq), F32)],
        compiler_params=_cparams(("parallel", "parallel", "parallel")),
        name="attention",
    )(*operands)


HY_NA = 64
HY_NB = 128
HY_N = HY_NA * HY_NB
HY_KG = 8


@functools.lru_cache(maxsize=None)
def _dft_tables():
    na, nb, n = HY_NA, HY_NB, HY_N
    a = np.arange(na // 2)[None, :]
    ka = np.arange(na)[:, None]
    ang = 2.0 * np.pi * ((a * ka) % na) / na
    c, s = np.cos(ang), np.sin(ang)
    g1 = np.block([[c, s], [-s, c]])
    g1_inv = g1.T.copy()
    a_full = np.arange(na)[None, :]
    ang = 2.0 * np.pi * ((a_full * ka) % na) / na
    g1_real = np.concatenate([np.cos(ang), -np.sin(ang)], axis=0)
    b = np.arange(nb)[None, None, :]
    k = np.arange(na)[:, None, None] + na * np.arange(nb)[None, :, None]
    ang = 2.0 * np.pi * ((b * k) % n) / n
    cr, si = np.cos(ang), np.sin(ang)
    g2 = np.concatenate([np.concatenate([cr, si], axis=2), np.concatenate([-si, cr], axis=2)], axis=1)
    g2_inv = np.transpose(g2, (0, 2, 1)).copy()
    return (g1.astype(np.float32), g1_inv.astype(np.float32), g1_real.astype(np.float32),
            g2.astype(np.float32), g2_inv.astype(np.float32))


def _dot_3pass(a, b):
    a_hi, b_hi = a.astype(BF16), b.astype(BF16)
    a_lo = (a - a_hi.astype(F32)).astype(BF16)
    b_lo = (b - b_hi.astype(F32)).astype(BF16)
    dot = functools.partial(jnp.dot, preferred_element_type=F32)
    return dot(a_hi, b_hi) + (dot(a_hi, b_lo) + dot(a_lo, b_hi))


def _short_conv_kernel(u_ref, w_ref, b_ref, o_ref, *, silu):
    u = u_ref[0].astype(F32)
    n = u.shape[0]
    rows = lax.broadcasted_iota(jnp.int32, u.shape, 0)
    prev = jnp.where(rows == 0, 0.0, pltpu.roll(u, 1, axis=0))
    nxt = jnp.where(rows == n - 1, 0.0, pltpu.roll(u, n - 1, axis=0))
    y = b_ref[...] + prev * w_ref[0:1, :] + u * w_ref[1:2, :] + nxt * w_ref[2:3, :]
    if silu:
        y = y * jax.nn.sigmoid(y)
    o_ref[0, 0] = y.astype(o_ref.dtype)


def short_conv_groups(proj, col0, w, b, *, group, tc, silu, name, out_dtype=F32):
    bsz, l, _ = proj.shape
    width = w.shape[1]
    per = group // tc
    return pl.pallas_call(
        functools.partial(_short_conv_kernel, silu=silu),
        grid=(bsz, width // tc),
        in_specs=[pl.BlockSpec((1, l, tc), lambda bi, j: (bi, 0, col0 // tc + j)),
                  pl.BlockSpec((SHORT_CONV, tc), lambda bi, j: (0, j)),
                  pl.BlockSpec((1, tc), lambda bi, j: (0, j))],
        out_specs=pl.BlockSpec((1, 1, l, tc), lambda bi, j: (j // per, bi, 0, j % per)),
        out_shape=jax.ShapeDtypeStruct((width // group, bsz, l, group), out_dtype),
        compiler_params=_cparams(("parallel", "parallel")),
        name=name,
    )(proj, w, b.reshape(1, width))


def _hy_filter_kernel(f_ref, w1_ref, b1_ref, w2_ref, b2_ref, w3_ref, dec_ref, o_ref, *, zero_row):
    hp = lax.Precision.HIGHEST
    f = f_ref[...]
    tl = f.shape[0]
    h = jnp.sin(jnp.dot(f, w1_ref[...], precision=hp, preferred_element_type=F32) + b1_ref[...])
    h = jnp.sin(jnp.dot(h, w2_ref[...], precision=hp, preferred_element_type=F32) + b2_ref[...])
    h = _dot_3pass(h, w3_ref[0])
    h = h * (jnp.exp(-f[:, 0:1] * dec_ref[0]) + HY_SHIFT)
    rows = lax.broadcasted_iota(jnp.int32, h.shape, 0) + pl.program_id(0) * tl
    o_ref[...] = jnp.where(rows == zero_row, 0.0, h)


@functools.lru_cache(maxsize=None)
def _filter_features(L):
    f32 = np.float32
    n = 2 * L
    r = np.arange(n, dtype=f32)
    t = (np.where(r < L, r, n - r) / f32(L)).astype(f32)
    bands = np.arange(1, HY_BANDS + 1, dtype=f32)
    ang = (f32(2.0 * math.pi) * t[:, None] * bands).astype(f32)
    emb = 1 + 2 * HY_BANDS
    return np.concatenate([t[:, None], np.cos(ang).astype(f32), np.sin(ang).astype(f32),
                           np.zeros((n, 128 - emb), f32)], axis=-1)


def hyena_filter_buffer(L, p, tl=512):
    n = 2 * L
    emb = 1 + 2 * HY_BANDS
    feats = jnp.asarray(_filter_features(L))
    hid = p['hy_w1'].shape[1]
    w1 = jnp.concatenate([p['hy_w1'], jnp.zeros((128 - emb, hid), F32)], axis=0)
    oc = HY_ORDER * HY_WIDTH
    w3 = p['hy_w3'].reshape(hid, HY_ORDER, 2, HY_WIDTH).transpose(2, 0, 1, 3).reshape(2, hid, oc)
    dec = p['hy_decay'].reshape(HY_ORDER, 2, HY_WIDTH).transpose(1, 0, 2).reshape(2, 1, oc)
    half = L // tl
    return pl.pallas_call(
        functools.partial(_hy_filter_kernel, zero_row=L),
        grid=(n // tl,),
        in_specs=[pl.BlockSpec((tl, 128), lambda i: (i, 0)),
                  pl.BlockSpec((128, hid), lambda i: (0, 0)),
                  pl.BlockSpec((1, hid), lambda i: (0, 0)),
                  pl.BlockSpec((hid, hid), lambda i: (0, 0)),
                  pl.BlockSpec((1, hid), lambda i: (0, 0)),
                  pl.BlockSpec((1, hid, oc), lambda i: (i // half, 0, 0)),
                  pl.BlockSpec((1, 1, oc), lambda i: (i // half, 0, 0))],
        out_specs=pl.BlockSpec((tl, oc), lambda i: (i, 0)),
        out_shape=jax.ShapeDtypeStruct((n, oc), F32),
        compiler_params=_cparams(("parallel",)),
        name="hyena_filter",
    )(feats, w1, p['hy_b1'].reshape(1, hid), p['hy_w2'], p['hy_b2'].reshape(1, hid), w3, dec)


def _dft_first_kernel(g_ref, x_ref, o_ref, *, precise):
    if precise:
        o_ref[0] = _dot_3pass(g_ref[...], x_ref[0])
    else:
        y = jnp.dot(g_ref[...].astype(BF16), x_ref[0].astype(BF16), preferred_element_type=F32)
        o_ref[0] = y.astype(o_ref.dtype)


def dft_first(g, x, *, n_out, offset=0, tn=8192, precise=False):
    _, r, lanes = x.shape
    m = g.shape[0]
    return pl.pallas_call(
        functools.partial(_dft_first_kernel, precise=precise),
        grid=(n_out, lanes // tn),
        in_specs=[pl.BlockSpec((m, r), lambda p, j: (0, 0)),
                  pl.BlockSpec((1, r, tn), lambda p, j: (offset + p, 0, j))],
        out_specs=pl.BlockSpec((1, m, tn), lambda p, j: (p, 0, j)),
        out_shape=jax.ShapeDtypeStruct((n_out, m, lanes), F32 if precise else BF16),
        compiler_params=_cparams(("parallel", "parallel")),
        name="dft_first",
    )(g, x)


def _dft_filter_mid_kernel(g_ref, y_ref, o_ref, *, scale):
    for i in range(g_ref.shape[0]):
        y = jnp.concatenate([y_ref[0, i], y_ref[1, i]], axis=0)
        o_ref[i] = scale * _dot_3pass(g_ref[i], y)


def dft_filter_mid(g2, y1):
    _, na, nb, oc = y1.shape
    kg = HY_KG // 2
    return pl.pallas_call(
        functools.partial(_dft_filter_mid_kernel, scale=1.0 / HY_N),
        grid=(na // kg,),
        in_specs=[pl.BlockSpec((kg, 2 * nb, 2 * nb), lambda g: (g, 0, 0)),
                  pl.BlockSpec((2, kg, nb, oc), lambda g: (0, g, 0, 0))],
        out_specs=pl.BlockSpec((kg, 2 * nb, oc), lambda g: (g, 0, 0)),
        out_shape=jax.ShapeDtypeStruct((na, 2 * nb, oc), F32),
        compiler_params=_cparams(("parallel",)),
        name="dft_filter_mid",
    )(g2, y1)


def _dft_mid_kernel(gf_ref, gi_ref, y_ref, h_ref, o_ref):
    nb = y_ref.shape[3]
    for i in range(gf_ref.shape[0]):
        y = jnp.concatenate([y_ref[0, 0, i], y_ref[0, 1, i]], axis=0)
        x = jnp.dot(gf_ref[i].astype(BF16), y, preferred_element_type=F32)
        xr, xi = x[:nb], x[nb:]
        hr, hi = h_ref[i, :nb], h_ref[i, nb:]
        prod = jnp.concatenate([xr * hr - xi * hi, xr * hi + xi * hr], axis=0).astype(BF16)
        u = jnp.dot(gi_ref[i].astype(BF16), prod, preferred_element_type=F32)
        o_ref[0, 0, i] = u[:nb].astype(o_ref.dtype)
        o_ref[0, 1, i] = u[nb:].astype(o_ref.dtype)


def dft_mid(g2, g2_inv, y1, h, order):
    npair, _, na, nb, c = y1.shape
    kg = HY_KG
    return pl.pallas_call(
        _dft_mid_kernel,
        grid=(na // kg, npair),
        in_specs=[pl.BlockSpec((kg, 2 * nb, 2 * nb), lambda g, p: (g, 0, 0)),
                  pl.BlockSpec((kg, 2 * nb, 2 * nb), lambda g, p: (g, 0, 0)),
                  pl.BlockSpec((1, 2, kg, nb, c), lambda g, p: (p, 0, g, 0, 0)),
                  pl.BlockSpec((kg, 2 * nb, c), lambda g, p: (g, 0, order))],
        out_specs=pl.BlockSpec((1, 2, kg, nb, c), lambda g, p: (p, 0, g, 0, 0)),
        out_shape=jax.ShapeDtypeStruct(y1.shape, BF16),
        compiler_params=_cparams(("parallel", "parallel")),
        name="dft_mid",
    )(g2, g2_inv, y1, h)


def _dft_last_kernel(g_ref, u_ref, z_ref, gate_ref, skip_ref, o_ref):
    conv = jnp.dot(g_ref[...].astype(BF16), u_ref[0], preferred_element_type=F32)
    y = gate_ref[0].astype(F32) * (conv + skip_ref[...] * z_ref[0].astype(F32))
    o_ref[0] = y.astype(o_ref.dtype)


def dft_last(g1_inv, u, z, z_off, gate, gate_off, skip_lanes, *, out_dtype, tn=8192):
    npair, m2, lanes = u.shape
    r = g1_inv.shape[0]
    return pl.pallas_call(
        _dft_last_kernel,
        grid=(npair, lanes // tn),
        in_specs=[pl.BlockSpec((r, m2), lambda p, j: (0, 0)),
                  pl.BlockSpec((1, m2, tn), lambda p, j: (p, 0, j)),
                  pl.BlockSpec((1, r, tn), lambda p, j: (z_off + p, 0, j)),
                  pl.BlockSpec((1, r, tn), lambda p, j: (gate_off + p, 0, j)),
                  pl.BlockSpec((1, tn), lambda p, j: (0, 0))],
        out_specs=pl.BlockSpec((1, r, tn), lambda p, j: (p, 0, j)),
        out_shape=jax.ShapeDtypeStruct((npair, r, lanes), out_dtype),
        compiler_params=_cparams(("parallel", "parallel")),
        name="dft_last",
    )(g1_inv, u, z, gate, skip_lanes)


def hyena_long(proj, p):
    bsz, l, _ = proj.shape
    assert 2 * l == HY_N and bsz % 2 == 0
    c = HY_WIDTH
    npair = bsz // 2
    lanes = HY_NB * c
    g1, g1_inv, g1_real, g2, g2_inv = _dft_tables()
    g1_bf, g1_inv_bf = jnp.asarray(g1), jnp.asarray(g1_inv)
    g2_bf, g2_inv_bf = jnp.asarray(g2), jnp.asarray(g2_inv)

    buf = hyena_filter_buffer(l, p)
    y1h = dft_first(jnp.asarray(g1_real), buf.reshape(1, HY_NA, HY_NB * HY_ORDER * c), n_out=1, precise=True)
    hspec = dft_filter_mid(jnp.asarray(g2), y1h.reshape(2, HY_NA, HY_NB, HY_ORDER * c))

    vx = short_conv_groups(proj, P_HY, p['hy_conv_w'], p['hy_conv_b'], group=c, tc=CONV_TC, silu=False,
                           name="hyena_short_conv", out_dtype=BF16)
    vx = vx.reshape(3 * npair, HY_NA, lanes)
    z, z_off = vx, 0
    for n in range(HY_ORDER):
        y1 = dft_first(g1_bf, z, n_out=npair, offset=z_off)
        u = dft_mid(g2_bf, g2_inv_bf, y1.reshape(npair, 2, HY_NA, HY_NB, c), hspec, n)
        skip_lanes = jnp.tile(p['hy_skip'][n], 8192 // c).reshape(1, 8192)
        z = dft_last(g1_inv_bf, u.reshape(npair, 2 * HY_NA, lanes), z, z_off, vx, (n + 1) * npair, skip_lanes,
                     out_dtype=BF16 if n + 1 < HY_ORDER else F32)
        z_off = 0
    return z.reshape(bsz, l, c)


def _log_sigmoid(x):
    return jnp.minimum(x, 0.0) - jnp.log1p(jnp.exp(-jnp.abs(x)))


def _mlstm_kernel(*refs, nc_ctx, direction):
    ins, bias_ref, outs, (c_ref, m_ref) = refs[:8], refs[8], refs[9:11], refs[11:13]
    d = ML_HEAD_DIM
    t = ML_CHUNK
    nh = ML_HEADS
    step = pl.program_id(1)

    @pl.when(step == 0)
    def _():
        c_ref[...] = jnp.zeros_like(c_ref)
        m_ref[...] = jnp.zeros_like(m_ref)

    i0 = S_GATES + direction * nh
    f0 = S_GATES + (2 + direction) * nh

    def chunk(q_ref, k_ref, v_ref, sm_ref, o_ref):
        g = sm_ref[0] + bias_ref[...]
        gt = g.T
        lf_cols = _log_sigmoid(g)
        lf_rows = _log_sigmoid(gt)
        r = lax.broadcasted_iota(jnp.int32, (t, t), 0)
        cidx = lax.broadcasted_iota(jnp.int32, (t, t), 1)
        mask = cidx <= r if direction == 0 else cidx >= r
        mask_t = r <= cidx if direction == 0 else r >= cidx
        tri = mask.astype(F32)
        tri_t = mask_t.astype(F32)
        lane = lax.broadcasted_iota(jnp.int32, (t, d), 1)
        ones_col = jnp.where(lane == 0, 1.0, 0.0).astype(BF16)
        for h in range(nh):
            sl = slice(h * d, (h + 1) * d)
            lf_row = lf_rows[f0 + h:f0 + h + 1, :]
            bc = jnp.sum(tri * lf_row, axis=1, keepdims=True)
            br = jnp.sum(tri_t * lf_cols[:, f0 + h:f0 + h + 1], axis=0, keepdims=True)
            lir = gt[i0 + h:i0 + h + 1, :]
            m_prev = m_ref[h, 0:1, 0:1]
            dmat = jnp.where(mask, bc - br + lir, -jnp.inf)
            inter = bc + m_prev
            m_t = jnp.maximum(inter, jnp.max(dmat, axis=1, keepdims=True))
            dexp = jnp.exp(dmat - m_t)
            inter_w = jnp.exp(inter - m_t)
            qh = (q_ref[0, 0, :, sl] * (d ** -0.5)).astype(BF16)
            kt = k_ref[0, 0, :, sl].T
            v_ext = jnp.concatenate([v_ref[0, :, sl].astype(BF16), ones_col], axis=1)
            s = jnp.dot(qh, kt.astype(BF16), preferred_element_type=F32) * dexp
            c_ext = c_ref[h]
            acc = inter_w * jnp.dot(qh, c_ext.astype(BF16), preferred_element_type=F32)
            acc = acc + jnp.dot(s.astype(BF16), v_ext, preferred_element_type=F32)
            den = jnp.maximum(jnp.abs(acc[:, d:d + 1]), jnp.exp(-m_t))
            o_ref[0, :, sl] = acc[:, :d] / den
            b_last = jnp.sum(lf_row, axis=1, keepdims=True)
            g_row = b_last - br + lir
            m_new = jnp.maximum(b_last + m_prev, jnp.max(g_row, axis=1, keepdims=True))
            a = jnp.exp(b_last + m_prev - m_new)
            w_row = jnp.exp(g_row - m_new)
            c_ref[h] = a * c_ext + jnp.dot((kt * w_row).astype(BF16), v_ext, preferred_element_type=F32)
            m_ref[h] = jnp.broadcast_to(m_new, m_ref.shape[1:])

    @pl.when(step < nc_ctx)
    def _():
        chunk(*ins[:4], outs[0])

    @pl.when(step >= nc_ctx)
    def _():
        chunk(*ins[4:], outs[1])


def mlstm_bidir(qk_c, proj_c, small_c, qk_l, proj_l, small_l, gate_bias):
    _, bsz, lc, w = qk_c.shape
    ll = qk_l.shape[2]
    t, nh = ML_CHUNK, ML_HEADS
    nc_c, nc_l = lc // t, ll // t
    bias = jnp.zeros((1, N_SMALL), F32).at[0, S_GATES:S_GATES + ML_GATES].set(gate_bias)

    def idx(dd, step_off, nc):
        def f(g):
            cc = jnp.clip(g - step_off, 0, nc - 1)
            return cc + dd * (nc - 1 - 2 * cc)
        return f

    vcol = P_MLV // w

    def seg_specs(ix):
        return [pl.BlockSpec((1, 1, t, w), lambda b, g: (0, b, ix(g), 0)),
                pl.BlockSpec((1, 1, t, w), lambda b, g: (1, b, ix(g), 0)),
                pl.BlockSpec((1, t, w), lambda b, g: (b, ix(g), vcol)),
                pl.BlockSpec((1, t, N_SMALL), lambda b, g: (b, ix(g), 0))]

    def out_spec(ix):
        return pl.BlockSpec((1, t, w), lambda b, g: (b, ix(g), 0))

    h_c, h_l = [], []
    for dd in range(2):
        ic, il = idx(dd, 0, nc_c), idx(dd, nc_c, nc_l)
        hc, hl = pl.pallas_call(
            functools.partial(_mlstm_kernel, nc_ctx=nc_c, direction=dd),
            grid=(bsz, nc_c + nc_l),
            in_specs=seg_specs(ic) + seg_specs(il) + [pl.BlockSpec((1, N_SMALL), lambda b, g: (0, 0))],
            out_specs=[out_spec(ic), out_spec(il)],
            out_shape=[jax.ShapeDtypeStruct((bsz, lc, w), F32), jax.ShapeDtypeStruct((bsz, ll, w), F32)],
            scratch_shapes=[pltpu.VMEM((nh, ML_HEAD_DIM, 2 * ML_HEAD_DIM), F32), pltpu.VMEM((nh, 8, 128), F32)],
            compiler_params=_cparams(("parallel", "arbitrary")),
            name="mlstm_fwd" if dd == 0 else "mlstm_bwd",
        )(qk_c, qk_c, proj_c, small_c, qk_l, qk_l, proj_l, small_l, bias)
        h_c.append(hc)
        h_l.append(hl)
    return tuple(h_c), tuple(h_l)


def _rms(x):
    return x * lax.rsqrt(jnp.mean(x * x, axis=-1, keepdims=True) + EPS)


def _swap_rope_halves(x):
    w = x.shape[1]
    lane = lax.broadcasted_iota(jnp.int32, x.shape, 1) % MLA_ROPE
    return jnp.where(lane < MLA_ROPE // 2, pltpu.roll(x, w - MLA_ROPE // 2, axis=1),
                     pltpu.roll(x, MLA_ROPE // 2, axis=1))


def _mla_prep_kernel(pq_ref, pkv_ref, sm_ref, gqa_ref, gkva_ref, wq_ref, wkv_ref, gqn_ref, gqr_ref, gkn_ref,
                     gkr_ref, en_ref, er_ref, *rest, use_rope):
    if use_rope:
        ccq_ref, ssq_ref, cck_ref, ssk_ref, qt_ref, k_ref, vt_ref = rest
    else:
        qt_ref, k_ref, vt_ref = rest
    nh, dn, dr = MLA_HEADS, MLA_NOPE, MLA_ROPE
    qa = (_rms(pq_ref[0].astype(F32)) * gqa_ref[...]).astype(BF16)
    kva = (_rms(pkv_ref[0].astype(F32)) * gkva_ref[...]).astype(BF16)
    q_raw = jnp.dot(qa, wq_ref[...], preferred_element_type=F32)
    kv_raw = jnp.dot(kva, wkv_ref[...], preferred_element_type=F32)
    inv_d = 1.0 / MLA_QK

    sel = functools.partial(jnp.dot, preferred_element_type=F32)
    qn, qr = q_raw[:, :nh * dn], q_raw[:, nh * dn:]
    ss = sel((qn * qn).astype(BF16), en_ref[...]) + sel((qr * qr).astype(BF16), er_ref[...])
    rs_rows = lax.rsqrt(ss * inv_d + EPS).T
    qn = qn * gqn_ref[...]
    qr = qr * gqr_ref[...]
    if use_rope:
        qr = qr * ccq_ref[...] + _swap_rope_halves(qr) * ssq_ref[...]
    qrt = qr.T
    tl = qn.shape[0]
    for h in range(nh):
        rs_h = rs_rows[h:h + 1, :]
        qt_ref[0, h, 0:dn, :] = (qn[:, h * dn:(h + 1) * dn].T * rs_h).astype(BF16)
        qt_ref[0, h, dn:dn + dr, :] = (qrt[h * dr:(h + 1) * dr] * rs_h).astype(BF16)
        qt_ref[0, h, dn + dr:, :] = jnp.zeros((QK_PAD - dn - dr, tl), BF16)

    kn, vv = kv_raw[:, :nh * dn], kv_raw[:, nh * dn:]
    lane = lax.broadcasted_iota(jnp.int32, sm_ref.shape[1:], 1)
    kr = jnp.where(lane < dr, sm_ref[0], 0.0)
    ssk = sel((kn * kn).astype(BF16), en_ref[...]) + jnp.sum(kr * kr, axis=-1, keepdims=True)
    rsk = lax.rsqrt(ssk * inv_d + EPS)
    kn = kn * gkn_ref[...]
    kr = kr * gkr_ref[...]
    if use_rope:
        kr = kr * cck_ref[...] + _swap_rope_halves(kr) * ssk_ref[...]
    for h in range(nh):
        rs_h = jnp.broadcast_to(rsk[:, h:h + 1], (tl, dn))
        k_ref[0, h, :, 0:dn] = (kn[:, h * dn:(h + 1) * dn] * rs_h).astype(BF16)
        k_ref[0, h, :, dn:] = (kr * rs_h).astype(BF16)
        vt_ref[0, h] = vv[:, h * MLA_V:(h + 1) * MLA_V].T.astype(BF16)


@functools.lru_cache(maxsize=None)
def _head_selectors():
    en = np.kron(np.eye(MLA_HEADS), np.ones((MLA_NOPE, 1)))
    er = np.kron(np.eye(MLA_HEADS), np.ones((MLA_ROPE, 1)))
    pad = lambda m: np.pad(m, ((0, 0), (0, 128 - m.shape[1]))).astype(np.float32)
    return pad(en), pad(er)


@functools.lru_cache(maxsize=None)
def _rope_tables(n_tokens):
    f32 = np.float32
    rows = n_tokens // GRID_W
    row = np.repeat(np.arange(rows, dtype=f32), GRID_W)
    col = np.tile(np.arange(GRID_W, dtype=f32), rows)
    freqs = f32(ROPE_BASE) ** (-np.arange(ROPE_PAIRS_PER_AXIS, dtype=f32) / f32(ROPE_PAIRS_PER_AXIS))
    ang = np.concatenate([row[:, None] * freqs, col[:, None] * freqs], axis=-1).astype(f32)
    cos, sin = np.cos(ang).astype(f32), np.sin(ang).astype(f32)
    cc = np.concatenate([cos, cos], axis=-1)
    ss = np.concatenate([-sin, sin], axis=-1)
    zero = np.zeros((n_tokens, 128 - MLA_ROPE), f32)
    return (np.tile(cc, (1, MLA_HEADS)), np.tile(ss, (1, MLA_HEADS)),
            np.concatenate([cc, zero], axis=-1), np.concatenate([ss, zero], axis=-1))


def mla_prep(proj, small, p, rope_tables, *, tl):
    b, l, _ = proj.shape
    nh = MLA_HEADS
    en, er = _head_selectors()
    en_b, er_b = jnp.asarray(en, BF16), jnp.asarray(er, BF16)
    const = lambda shape: pl.BlockSpec(shape, lambda bi, i: (0,) * len(shape))
    nq = nh * MLA_QK
    nkv = nh * (MLA_NOPE + MLA_V)
    in_specs = [pl.BlockSpec((1, tl, MLA_Q_LORA), lambda bi, i: (bi, i, P_CQ // MLA_Q_LORA)),
                pl.BlockSpec((1, tl, MLA_KV_LORA), lambda bi, i: (bi, i, P_CKV // MLA_KV_LORA)),
                pl.BlockSpec((1, tl, N_SMALL), lambda bi, i: (bi, i, 0)),
                const((1, MLA_Q_LORA)), const((1, MLA_KV_LORA)), const((MLA_Q_LORA, nq)), const((MLA_KV_LORA, nkv)),
                const((1, nh * MLA_NOPE)), const((1, nh * MLA_ROPE)), const((1, nh * MLA_NOPE)), const((1, 128)),
                const(en.shape), const(er.shape)]
    operands = [proj, proj, small, p['gqa'], p['gkva'], p['wq'], p['wkv'], p['gqn'], p['gqr'], p['gkn'], p['gkr'],
                en_b, er_b]
    use_rope = rope_tables is not None
    if use_rope:
        in_specs += [pl.BlockSpec((tl, nh * MLA_ROPE), lambda bi, i: (i, 0)),
                     pl.BlockSpec((tl, nh * MLA_ROPE), lambda bi, i: (i, 0)),
                     pl.BlockSpec((tl, 128), lambda bi, i: (i, 0)),
                     pl.BlockSpec((tl, 128), lambda bi, i: (i, 0))]
        operands += list(rope_tables)
    return pl.pallas_call(
        functools.partial(_mla_prep_kernel, use_rope=use_rope),
        grid=(b, l // tl),
        in_specs=in_specs,
        out_specs=[pl.BlockSpec((1, nh, QK_PAD, tl), lambda bi, i: (bi, 0, 0, i)),
                   pl.BlockSpec((1, nh, tl, QK_PAD), lambda bi, i: (bi, 0, i, 0)),
                   pl.BlockSpec((1, nh, MLA_V, tl), lambda bi, i: (bi, 0, 0, i))],
        out_shape=[jax.ShapeDtypeStruct((b, nh, QK_PAD, l), BF16),
                   jax.ShapeDtypeStruct((b, nh, l, QK_PAD), BF16),
                   jax.ShapeDtypeStruct((b, nh, MLA_V, l), BF16)],
        compiler_params=_cparams(("parallel", "parallel")),
        name="mla_prep",
    )(*operands)


def _combine_out_kernel(a_ref, y_ref, hf_ref, hb_ref, o_ref, g_ref, w_ref, r_ref, gt_ref, out_ref, cat_ref):
    @pl.when(pl.program_id(2) == 0)
    def _():
        g = g_ref[...]
        cat_ref[:, :MLA_WIDTH] = (_rms(a_ref[0]) * g[:, :MLA_WIDTH]).astype(BF16)
        hy0 = MLA_WIDTH
        cat_ref[:, hy0:hy0 + HY_WIDTH] = (_rms(y_ref[0]) * g[:, hy0:hy0 + HY_WIDTH]).astype(BF16)
        ml0 = MLA_WIDTH + HY_WIDTH
        hh = hf_ref[0] + hb_ref[0]
        og = 0.5 * jnp.tanh(0.5 * o_ref[0].astype(F32)) + 0.5
        for hd in range(ML_HEADS):
            sl = slice(hd * ML_HEAD_DIM, (hd + 1) * ML_HEAD_DIM)
            gl = g[:, ml0 + hd * ML_HEAD_DIM:ml0 + (hd + 1) * ML_HEAD_DIM]
            cat_ref[:, ml0 + hd * ML_HEAD_DIM:ml0 + (hd + 1) * ML_HEAD_DIM] = (
                _rms(hh[:, sl]) * gl * og[:, sl]).astype(BF16)

    y = jnp.dot(cat_ref[...], w_ref[...], preferred_element_type=F32)
    out_ref[0] = r_ref[0] + gt_ref[0] * y


def combine_out_proj(a, y, h2, proj, g_mix, w_out, res, gate, *, tm, tn, name="out_proj"):
    b, s, d = res.shape
    return pl.pallas_call(
        _combine_out_kernel,
        grid=(b, s // tm, d // tn),
        in_specs=[pl.BlockSpec((1, tm, MLA_WIDTH), lambda bi, i, j: (bi, i, 0)),
                  pl.BlockSpec((1, tm, HY_WIDTH), lambda bi, i, j: (bi, i, 0)),
                  pl.BlockSpec((1, tm, ML_WIDTH), lambda bi, i, j: (bi, i, 0)),
                  pl.BlockSpec((1, tm, ML_WIDTH), lambda bi, i, j: (bi, i, 0)),
                  pl.BlockSpec((1, tm, ML_WIDTH), lambda bi, i, j: (bi, i, P_MLO // ML_WIDTH)),
                  pl.BlockSpec((1, MIX_WIDTH), lambda bi, i, j: (0, 0)),
                  pl.BlockSpec((MIX_WIDTH, tn), lambda bi, i, j: (0, j)),
                  pl.BlockSpec((1, tm, tn), lambda bi, i, j: (bi, i, j)),
                  pl.BlockSpec((1, 1, tn), lambda bi, i, j: (bi, 0, j))],
        out_specs=pl.BlockSpec((1, tm, tn), lambda bi, i, j: (bi, i, j)),
        out_shape=jax.ShapeDtypeStruct((b, s, d), F32),
        scratch_shapes=[pltpu.VMEM((tm, MIX_WIDTH), BF16)],
        compiler_params=_cparams(("parallel", "parallel", "arbitrary")),
        name=name,
    )(a, y, h2[0], h2[1], proj, g_mix.reshape(1, MIX_WIDTH), w_out, res, gate)


@functools.lru_cache(maxsize=None)
def _direct_dft_tables(l):
    n = 2 * l
    k = np.arange(n)[:, None]
    t = np.arange(l)[None, :]
    ang = 2.0 * np.pi * ((k * t) % n) / n
    c, s = np.cos(ang), np.sin(ang)
    gf = np.block([[c, s], [-s, c]])
    gi = gf.T.copy()
    t_full = np.arange(n)[None, :]
    ang = 2.0 * np.pi * ((k * t_full) % n) / n
    g_real = np.concatenate([np.cos(ang), -np.sin(ang)], axis=0)
    return gf.astype(np.float32), gi.astype(np.float32), g_real.astype(np.float32)


def _direct_conv_kernel(gf_ref, gi_ref, z_ref, gate_ref, h_ref, skip_ref, o_ref, *, scale):
    z = z_ref[0]
    x = jnp.dot(gf_ref[...].astype(BF16), z.astype(BF16), preferred_element_type=F32)
    n = x.shape[0] // 2
    xr, xi = x[:n], x[n:]
    hr, hi = h_ref[:n] * scale, h_ref[n:] * scale
    prod = jnp.concatenate([xr * hr - xi * hi, xr * hi + xi * hr], axis=0).astype(BF16)
    conv = jnp.dot(gi_ref[...].astype(BF16), prod, preferred_element_type=F32)
    o_ref[0] = gate_ref[0] * (conv + skip_ref[...] * z)


def hyena_short(proj, p):
    bsz, l, _ = proj.shape
    c = HY_WIDTH
    npair = bsz // 2
    n = 2 * l
    gf, gi, g_real = _direct_dft_tables(l)
    buf = hyena_filter_buffer(l, p, tl=l)
    hspec = dft_first(jnp.asarray(g_real), buf.reshape(1, n, HY_ORDER * c), n_out=1, tn=HY_ORDER * c,
                      precise=True)[0]
    vx = short_conv_groups(proj, P_HY, p['hy_conv_w'], p['hy_conv_b'], group=c, tc=CONV_TC, silu=False,
                           name="hyena_short_conv_ctx")
    vx = vx.reshape(3 * npair, 2 * l, c)
    z = vx

    def rows_at(off):
        return lambda q: (off + q, 0, 0)

    def cols_at(col):
        return lambda q: (0, col)

    for order in range(HY_ORDER):
        z = pl.pallas_call(
            functools.partial(_direct_conv_kernel, scale=1.0 / n),
            grid=(npair,),
            in_specs=[pl.BlockSpec(gf.shape, lambda q: (0, 0)),
                      pl.BlockSpec(gi.shape, lambda q: (0, 0)),
                      pl.BlockSpec((1, 2 * l, c), rows_at(0)),
                      pl.BlockSpec((1, 2 * l, c), rows_at((order + 1) * npair)),
                      pl.BlockSpec((2 * n, c), cols_at(order)),
                      pl.BlockSpec((1, c), lambda q: (0, 0))],
            out_specs=pl.BlockSpec((1, 2 * l, c), lambda q: (q, 0, 0)),
            out_shape=jax.ShapeDtypeStruct((npair, 2 * l, c), F32),
            compiler_params=_cparams(("parallel",)),
            name="hyena_direct_conv",
        )(jnp.asarray(gf), jnp.asarray(gi), z, vx, hspec, p['hy_skip'][order].reshape(1, c))
    return z.reshape(bsz, l, c)


def _split_cols(w, sizes):
    out, start = [], 0
    for s in sizes:
        out.append(w[:, start:start + s])
        start += s
    return out


def _prep_layer_weights(l, w_in, mla_qa_norm, mla_kva_norm, mla_w_uq, mla_w_ukv, mla_q_norm, mla_k_norm):
    cq, ckv, kr, hy, mqk, mv, mo, gt = _split_cols(w_in[l].astype(BF16), IN_SIZES)
    w_main = jnp.concatenate([cq, ckv, hy, mqk, mv, mo], axis=1)
    pad = jnp.zeros((D_MODEL, N_SMALL - MLA_ROPE - ML_GATES), BF16)
    w_small = jnp.concatenate([kr, gt, pad], axis=1)
    nh = MLA_HEADS
    wq = mla_w_uq[l].reshape(MLA_Q_LORA, nh, MLA_QK)
    wq = jnp.concatenate([wq[:, :, :MLA_NOPE].reshape(MLA_Q_LORA, -1), wq[:, :, MLA_NOPE:].reshape(MLA_Q_LORA, -1)],
                         axis=1).astype(BF16)
    wkv = mla_w_ukv[l].reshape(MLA_KV_LORA, nh, MLA_NOPE + MLA_V)
    wkv = jnp.concatenate([wkv[:, :, :MLA_NOPE].reshape(MLA_KV_LORA, -1),
                           wkv[:, :, MLA_NOPE:].reshape(MLA_KV_LORA, -1)], axis=1).astype(BF16)
    q_scale = (MLA_QK ** -0.5) * math.log2(math.e)
    gq, gk = mla_q_norm[l] * q_scale, mla_k_norm[l]
    mla = {'gqa': mla_qa_norm[l].reshape(1, -1), 'gkva': mla_kva_norm[l].reshape(1, -1), 'wq': wq, 'wkv': wkv,
           'gqn': jnp.tile(gq[:MLA_NOPE], nh).reshape(1, -1), 'gqr': jnp.tile(gq[MLA_NOPE:], nh).reshape(1, -1),
           'gkn': jnp.tile(gk[:MLA_NOPE], nh).reshape(1, -1),
           'gkr': jnp.concatenate([gk[MLA_NOPE:], jnp.zeros((128 - MLA_ROPE,), F32)]).reshape(1, -1)}
    return w_main, w_small, mla


def kernel(x, c, ctx, c_ctx, ada_w, ada_b, norm1_g, norm2_g, w_in, mla_qa_norm, mla_kva_norm, mla_w_uq,
           mla_w_ukv, mla_q_norm, mla_k_norm, hy_conv_w, hy_conv_b, hy_w1, hy_b1, hy_w2, hy_b2, hy_w3,
           hy_decay, hy_skip, ml_conv_w, ml_conv_b, ml_gate_b, mix_norm_g, w_out, ffn_w1, ffn_w2):
    B, S, D = x.shape
    LC = ctx.shape[1]
    rope_tables = _rope_tables(S)

    silu_rows = jnp.concatenate([jax.nn.silu(c), jax.nn.silu(c_ctx)[None], jnp.zeros((8 - B - 1, D), F32)], axis=0)
    mods = ada_modulation(silu_rows.astype(BF16), ada_w, ada_b)

    for l in range(DEPTH):
        need_ctx = l < DEPTH - 1
        p = {'hy_conv_w': hy_conv_w[l], 'hy_conv_b': hy_conv_b[l], 'hy_w1': hy_w1[l], 'hy_b1': hy_b1[l],
             'hy_w2': hy_w2[l], 'hy_b2': hy_b2[l], 'hy_w3': hy_w3[l], 'hy_decay': hy_decay[l],
             'hy_skip': hy_skip[l], 'ml_conv_w': ml_conv_w[l], 'ml_conv_b': ml_conv_b[l]}
        w_main, w_small, mla = _prep_layer_weights(l, w_in, mla_qa_norm, mla_kva_norm, mla_w_uq, mla_w_ukv,
                                                   mla_q_norm, mla_k_norm)
        w_out_l = w_out[l].astype(BF16)
        w1_l = ffn_w1[l].astype(BF16)
        w2_l = ffn_w2[l].astype(BF16)

        mod_l = [m[:, None, :] for m in jnp.split(mods[l, :B], 6, axis=-1)]
        mod_c = [m[:, None, :] for m in jnp.split(mods[l, B:B + 1], 6, axis=-1)]
        flat = lambda t: t.reshape(1, B * LC, t.shape[-1])

        proj_l, small_l = in_proj(x, norm1_g[l], mod_l[1], mod_l[0], w_main, w_small, tm=1024, tn=IN_PROJ_TN)
        proj_c, small_c = in_proj(flat(ctx), norm1_g[l], mod_c[1], mod_c[0], w_main, w_small, tm=B * LC,
                                  tn=IN_PROJ_TN, name="in_proj_ctx")
        proj_c, small_c = proj_c.reshape(B, LC, N_MAIN), small_c.reshape(B, LC, N_SMALL)

        qt_l, k_l, vt_l = mla_prep(proj_l, small_l, mla, rope_tables, tl=512)
        qt_c, k_c, vt_c = mla_prep(proj_c, small_c, mla, None, tl=LC)
        a_l = attention(qt_l, [(k_c, vt_c), (k_l, vt_l)], tq=ATT_TQ)
        y_l = hyena_long(proj_l, p)

        def mlstm_qk(proj):
            return short_conv_groups(proj, P_MLQK, p['ml_conv_w'], p['ml_conv_b'], group=ML_WIDTH, tc=CONV_TC,
                                     silu=True, name="mlstm_short_conv")

        h_c, h_l = mlstm_bidir(mlstm_qk(proj_c), proj_c, small_c, mlstm_qk(proj_l), proj_l, small_l, ml_gate_b[l])
        x = combine_out_proj(a_l, y_l, h_l, proj_l, mix_norm_g[l], w_out_l, x, mod_l[2], tm=1024, tn=MM_TN)
        hid = norm_swiglu(x, norm2_g[l], mod_l[4], mod_l[3], w1_l, tm=1024, tn=MM_TN, name="ffn_up")
        x = mm_residual(hid, w2_l, x, mod_l[5], tm=1024, tn=MM_TN, name="ffn_down")

        if need_ctx:
            a_c = attention(qt_c, [(k_c, vt_c)], tq=LC)
            y_c = hyena_short(proj_c, p)
            ctx = combine_out_proj(flat(a_c), flat(y_c), (flat(h_c[0]), flat(h_c[1])), flat(proj_c), mix_norm_g[l],
                                   w_out_l, flat(ctx), mod_c[2], tm=B * LC, tn=MM_TN, name="out_proj_ctx")
            hid = norm_swiglu(ctx, norm2_g[l], mod_c[4], mod_c[3], w1_l, tm=B * LC, tn=MM_TN, name="ffn_up_ctx")
            ctx = mm_residual(hid, w2_l, ctx, mod_c[5], tm=B * LC, tn=MM_TN, name="ffn_down_ctx")
            ctx = ctx.reshape(B, LC, D)
    return x
```

```python
import functools
import math

import jax
import jax.numpy as jnp
import numpy as np
from jax import lax
from jax.experimental import pallas as pl
from jax.experimental.pallas import tpu as pltpu

D_MODEL = 2048
DEPTH = 2
GRID_W = 64
EPS = 1e-6

MLA_HEADS = 8
MLA_NOPE = 128
MLA_ROPE = 64
MLA_QK = MLA_NOPE + MLA_ROPE
MLA_V = 128
MLA_Q_LORA = 512
MLA_KV_LORA = 512
MLA_WIDTH = MLA_HEADS * MLA_V
ROPE_BASE = 10000.0
ROPE_PAIRS_PER_AXIS = MLA_ROPE // 4
QK_PAD = 256

HY_WIDTH = 512
HY_ORDER = 2
HY_BANDS = 16
HY_SHIFT = 0.05
SHORT_CONV = 3
CONV_TC = 256

ML_HEADS = 4
ML_HEAD_DIM = 128
ML_WIDTH = ML_HEADS * ML_HEAD_DIM
ML_CHUNK = 128
ML_GATES = 4 * ML_HEADS

MIX_WIDTH = MLA_WIDTH + HY_WIDTH + ML_WIDTH
FFN_HIDDEN = ((8 * D_MODEL // 3 + 255) // 256) * 256

IN_SIZES = (MLA_Q_LORA, MLA_KV_LORA, MLA_ROPE, 3 * HY_WIDTH, 2 * ML_WIDTH, ML_WIDTH, ML_WIDTH, ML_GATES)
N_IN = sum(IN_SIZES)

P_CQ, P_CKV, P_HY, P_MLQK, P_MLV, P_MLO = 0, 512, 1024, 2560, 3584, 4096
N_MAIN = 4608
N_SMALL = 128
S_GATES = MLA_ROPE

VMEM_LIMIT_BYTES = 56 * 1024 * 1024

BF16 = jnp.bfloat16
F32 = jnp.float32


def _cparams(sem):
    return pltpu.CompilerParams(dimension_semantics=sem, vmem_limit_bytes=VMEM_LIMIT_BYTES)


def _ada_kernel(s_ref, w_ref, b_ref, o_ref):
    w = w_ref[0].astype(BF16)
    o_ref[0] = jnp.dot(s_ref[...], w, preferred_element_type=F32) + b_ref[0]


def ada_modulation(silu_rows, ada_w, ada_b, tn=1024):
    depth, d, n = ada_w.shape
    return pl.pallas_call(
        _ada_kernel,
        grid=(depth, n // tn),
        in_specs=[pl.BlockSpec((8, d), lambda l, j: (0, 0)),
                  pl.BlockSpec((1, d, tn), lambda l, j: (l, 0, j)),
                  pl.BlockSpec((1, 1, tn), lambda l, j: (l, 0, j))],
        out_specs=pl.BlockSpec((1, 8, tn), lambda l, j: (l, 0, j)),
        out_shape=jax.ShapeDtypeStruct((depth, 8, n), F32),
        compiler_params=_cparams(("parallel", "parallel")),
        name="ada_modulation",
    )(silu_rows, ada_w, ada_b.reshape(depth, 1, n))


MM_TN = 512
IN_PROJ_TN = 1536


def _normed(x, g, sc, sh):
    gain = g * (1.0 + sc)
    return x * lax.rsqrt(jnp.mean(x * x, axis=-1, keepdims=True) + EPS) * gain + sh


def _in_proj_kernel(x_ref, g_ref, sc_ref, sh_ref, w_ref, ws_ref, o_ref, os_ref, xn_ref):
    @pl.when(pl.program_id(2) == 0)
    def _():
        xn = _normed(x_ref[0], g_ref[...], sc_ref[0], sh_ref[0]).astype(BF16)
        xn_ref[...] = xn
        os_ref[0] = jnp.dot(xn, ws_ref[...], preferred_element_type=F32)

    o_ref[0] = jnp.dot(xn_ref[...], w_ref[...], preferred_element_type=F32).astype(o_ref.dtype)


def in_proj(x, g, sc, sh, w_main, w_small, *, tm, tn, name="in_proj"):
    b, s, k = x.shape
    n = w_main.shape[1]
    ns = w_small.shape[1]
    return pl.pallas_call(
        _in_proj_kernel,
        grid=(b, s // tm, n // tn),
        in_specs=[pl.BlockSpec((1, tm, k), lambda bi, i, j: (bi, i, 0)),
                  pl.BlockSpec((1, k), lambda bi, i, j: (0, 0)),
                  pl.BlockSpec((1, 1, k), lambda bi, i, j: (bi, 0, 0)),
                  pl.BlockSpec((1, 1, k), lambda bi, i, j: (bi, 0, 0)),
                  pl.BlockSpec((k, tn), lambda bi, i, j: (0, j)),
                  pl.BlockSpec((k, ns), lambda bi, i, j: (0, 0))],
        out_specs=[pl.BlockSpec((1, tm, tn), lambda bi, i, j: (bi, i, j)),
                   pl.BlockSpec((1, tm, ns), lambda bi, i, j: (bi, i, 0))],
        out_shape=[jax.ShapeDtypeStruct((b, s, n), BF16), jax.ShapeDtypeStruct((b, s, ns), F32)],
        scratch_shapes=[pltpu.VMEM((tm, k), BF16)],
        compiler_params=_cparams(("parallel", "parallel", "arbitrary")),
        name=name,
    )(x, g.reshape(1, k), sc, sh, w_main, w_small)


def _norm_swiglu_kernel(x_ref, g_ref, sc_ref, sh_ref, wg_ref, wu_ref, o_ref, xn_ref):
    @pl.when(pl.program_id(2) == 0)
    def _():
        xn_ref[...] = _normed(x_ref[0], g_ref[...], sc_ref[0], sh_ref[0]).astype(BF16)

    xn = xn_ref[...]
    gate = jnp.dot(xn, wg_ref[...], preferred_element_type=F32)
    up = jnp.dot(xn, wu_ref[...], preferred_element_type=F32)
    o_ref[0] = (gate * (0.5 * jnp.tanh(0.5 * gate) + 0.5) * up).astype(o_ref.dtype)


def norm_swiglu(x, g, sc, sh, w1, *, tm, tn, name="norm_swiglu"):
    b, s, d = x.shape
    h = w1.shape[1] // 2
    nj = h // tn
    return pl.pallas_call(
        _norm_swiglu_kernel,
        grid=(b, s // tm, nj),
        in_specs=[pl.BlockSpec((1, tm, d), lambda bi, i, j: (bi, i, 0)),
                  pl.BlockSpec((1, d), lambda bi, i, j: (0, 0)),
                  pl.BlockSpec((1, 1, d), lambda bi, i, j: (bi, 0, 0)),
                  pl.BlockSpec((1, 1, d), lambda bi, i, j: (bi, 0, 0)),
                  pl.BlockSpec((d, tn), lambda bi, i, j: (0, j)),
                  pl.BlockSpec((d, tn), lambda bi, i, j: (0, j + nj))],
        out_specs=pl.BlockSpec((1, tm, tn), lambda bi, i, j: (bi, i, j)),
        out_shape=jax.ShapeDtypeStruct((b, s, h), BF16),
        scratch_shapes=[pltpu.VMEM((tm, d), BF16)],
        compiler_params=_cparams(("parallel", "parallel", "arbitrary")),
        name=name,
    )(x, g.reshape(1, d), sc, sh, w1, w1)


def _mm_res_kernel(a_ref, w_ref, r_ref, gt_ref, o_ref):
    y = jnp.dot(a_ref[0], w_ref[...], preferred_element_type=F32)
    o_ref[0] = r_ref[0] + gt_ref[0] * y


def mm_residual(a, w, res, gate, *, tm, tn, name="mm_residual"):
    b, s, k = a.shape
    n = w.shape[1]
    return pl.pallas_call(
        _mm_res_kernel,
        grid=(b, s // tm, n // tn),
        in_specs=[pl.BlockSpec((1, tm, k), lambda bi, i, j: (bi, i, 0)),
                  pl.BlockSpec((k, tn), lambda bi, i, j: (0, j)),
                  pl.BlockSpec((1, tm, tn), lambda bi, i, j: (bi, i, j)),
                  pl.BlockSpec((1, 1, tn), lambda bi, i, j: (bi, 0, j))],
        out_specs=pl.BlockSpec((1, tm, tn), lambda bi, i, j: (bi, i, j)),
        out_shape=jax.ShapeDtypeStruct((b, s, n), F32),
        compiler_params=_cparams(("parallel", "parallel", "parallel")),
        name=name,
    )(a, w, res, gate)


ATT_KEY_CHUNK = 512
ATT_TQ = 1024


def _attn_kernel(qt_ref, *refs, n_seg):
    o_ref, s_ref = refs[2 * n_seg], refs[2 * n_seg + 1]
    qt = qt_ref[0, 0]
    tq = qt.shape[1]
    chunks, off = [], 0
    for sgi in range(n_seg):
        k_ref, vt_ref = refs[2 * sgi], refs[2 * sgi + 1]
        lk = k_ref.shape[2]
        ch = min(ATT_KEY_CHUNK, lk)
        for c in range(lk // ch):
            chunks.append((k_ref, vt_ref, c * ch, ch, off))
            off += ch

    m8 = jnp.full((8, tq), -jnp.inf, F32)
    for k_ref, _, start, ch, off in chunks:
        s = jnp.dot(k_ref[0, 0, start:start + ch, :], qt, preferred_element_type=F32)
        s_ref[off:off + ch, :] = s
        m8 = jnp.maximum(m8, jnp.max(s.reshape(ch // 8, 8, tq), axis=0))
    m = jnp.max(m8, axis=0, keepdims=True)
    l8 = jnp.zeros((8, tq), F32)
    acc = jnp.zeros((vt_ref.shape[2], tq), F32)
    for _, vt_ref, start, ch, off in chunks:
        p = jnp.exp2(s_ref[off:off + ch, :] - m)
        l8 = l8 + jnp.sum(p.reshape(ch // 8, 8, tq), axis=0)
        acc = acc + jnp.dot(vt_ref[0, 0, :, start:start + ch], p.astype(BF16), preferred_element_type=F32)
    l = jnp.sum(l8, axis=0, keepdims=True)
    o_ref[0] = (acc / l).T


def attention(qt, segments, *, tq=256):
    b, h, dk, lq = qt.shape
    dv = segments[0][1].shape[2]
    in_specs = [pl.BlockSpec((1, 1, dk, tq), lambda bi, hi, i: (bi, hi, 0, i))]
    operands = [qt]
    for k, vt in segments:
        lk = k.shape[2]
        in_specs += [pl.BlockSpec((1, 1, lk, dk), lambda bi, hi, i: (bi, hi, 0, 0)),
                     pl.BlockSpec((1, 1, dv, lk), lambda bi, hi, i: (bi, hi, 0, 0))]
        operands += [k, vt]
    return pl.pallas_call(
        functools.partial(_attn_kernel, n_seg=len(segments)),
        grid=(b, h, lq // tq),
        in_specs=in_specs,
        out_specs=pl.BlockSpec((1, tq, dv), lambda bi, hi, i: (bi, i, hi)),
        out_shape=jax.ShapeDtypeStruct((b, lq, h * dv), F32),
        scratch_shapes=[pltpu.VMEM((sum(k.shape[2] for k, _ in segments), tq), F32)],
        compiler_params=_cparams(("parallel", "parallel", "parallel")),
        name="attention",
    )(*operands)


HY_NA = 64
HY_NB = 128
HY_N = HY_NA * HY_NB
HY_KG = 8


@functools.lru_cache(maxsize=None)
def _dft_tables():
    na, nb, n = HY_NA, HY_NB, HY_N
    a = np.arange(na // 2)[None, :]
    ka = np.arange(na)[:, None]
    ang = 2.0 * np.pi * ((a * ka) % na) / na
    c, s = np.cos(ang), np.sin(ang)
    g1 = np.block([[c, s], [-s, c]])
    g1_inv = g1.T.copy()
    a_full = np.arange(na)[None, :]
    ang = 2.0 * np.pi * ((a_full * ka) % na) / na
    g1_real = np.concatenate([np.cos(ang), -np.sin(ang)], axis=0)
    b = np.arange(nb)[None, None, :]
    k = np.arange(na)[:, None, None] + na * np.arange(nb)[None, :, None]
    ang = 2.0 * np.pi * ((b * k) % n) / n
    cr, si = np.cos(ang), np.sin(ang)
    g2 = np.concatenate([np.concatenate([cr, si], axis=2), np.concatenate([-si, cr], axis=2)], axis=1)
    g2_inv = np.transpose(g2, (0, 2, 1)).copy()
    return (g1.astype(np.float32), g1_inv.astype(np.float32), g1_real.astype(np.float32),
            g2.astype(np.float32), g2_inv.astype(np.float32))


def _dot_3pass(a, b):
    a_hi, b_hi = a.astype(BF16), b.astype(BF16)
    a_lo = (a - a_hi.astype(F32)).astype(BF16)
    b_lo = (b - b_hi.astype(F32)).astype(BF16)
    dot = functools.partial(jnp.dot, preferred_element_type=F32)
    return dot(a_hi, b_hi) + (dot(a_hi, b_lo) + dot(a_lo, b_hi))


def _short_conv_kernel(u_ref, w_ref, b_ref, o_ref, *, silu):
    u = u_ref[0].astype(F32)
    n = u.shape[0]
    rows = lax.broadcasted_iota(jnp.int32, u.shape, 0)
    prev = jnp.where(rows == 0, 0.0, pltpu.roll(u, 1, axis=0))
    nxt = jnp.where(rows == n - 1, 0.0, pltpu.roll(u, n - 1, axis=0))
    y = b_ref[...] + prev * w_ref[0:1, :] + u * w_ref[1:2, :] + nxt * w_ref[2:3, :]
    if silu:
        y = y * jax.nn.sigmoid(y)
    o_ref[0, 0] = y.astype(o_ref.dtype)


def short_conv_groups(proj, col0, w, b, *, group, tc, silu, name, out_dtype=F32):
    bsz, l, _ = proj.shape
    width = w.shape[1]
    per = group // tc
    return pl.pallas_call(
        functools.partial(_short_conv_kernel, silu=silu),
        grid=(bsz, width // tc),
        in_specs=[pl.BlockSpec((1, l, tc), lambda bi, j: (bi, 0, col0 // tc + j)),
                  pl.BlockSpec((SHORT_CONV, tc), lambda bi, j: (0, j)),
                  pl.BlockSpec((1, tc), lambda bi, j: (0, j))],
        out_specs=pl.BlockSpec((1, 1, l, tc), lambda bi, j: (j // per, bi, 0, j % per)),
        out_shape=jax.ShapeDtypeStruct((width // group, bsz, l, group), out_dtype),
        compiler_params=_cparams(("parallel", "parallel")),
        name=name,
    )(proj, w, b.reshape(1, width))


def _hy_filter_kernel(f_ref, w1_ref, b1_ref, w2_ref, b2_ref, w3_ref, dec_ref, o_ref, *, zero_row):
    hp = lax.Precision.HIGHEST
    f = f_ref[...]
    tl = f.shape[0]
    h = jnp.sin(jnp.dot(f, w1_ref[...], precision=hp, preferred_element_type=F32) + b1_ref[...])
    h = jnp.sin(jnp.dot(h, w2_ref[...], precision=hp, preferred_element_type=F32) + b2_ref[...])
    h = _dot_3pass(h, w3_ref[0])
    h = h * (jnp.exp(-f[:, 0:1] * dec_ref[0]) + HY_SHIFT)
    rows = lax.broadcasted_iota(jnp.int32, h.shape, 0) + pl.program_id(0) * tl
    o_ref[...] = jnp.where(rows == zero_row, 0.0, h)


@functools.lru_cache(maxsize=None)
def _filter_features(L):
    f32 = np.float32
    n = 2 * L
    r = np.arange(n, dtype=f32)
    t = (np.where(r < L, r, n - r) / f32(L)).astype(f32)
    bands = np.arange(1, HY_BANDS + 1, dtype=f32)
    ang = (f32(2.0 * math.pi) * t[:, None] * bands).astype(f32)
    emb = 1 + 2 * HY_BANDS
    return np.concatenate([t[:, None], np.cos(ang).astype(f32), np.sin(ang).astype(f32),
                           np.zeros((n, 128 - emb), f32)], axis=-1)


def hyena_filter_buffer(L, p, tl=512):
    n = 2 * L
    emb = 1 + 2 * HY_BANDS
    feats = jnp.asarray(_filter_features(L))
    hid = p['hy_w1'].shape[1]
    w1 = jnp.concatenate([p['hy_w1'], jnp.zeros((128 - emb, hid), F32)], axis=0)
    oc = HY_ORDER * HY_WIDTH
    w3 = p['hy_w3'].reshape(hid, HY_ORDER, 2, HY_WIDTH).transpose(2, 0, 1, 3).reshape(2, hid, oc)
    dec = p['hy_decay'].reshape(HY_ORDER, 2, HY_WIDTH).transpose(1, 0, 2).reshape(2, 1, oc)
    half = L // tl
    return pl.pallas_call(
        functools.partial(_hy_filter_kernel, zero_row=L),
        grid=(n // tl,),
        in_specs=[pl.BlockSpec((tl, 128), lambda i: (i, 0)),
                  pl.BlockSpec((128, hid), lambda i: (0, 0)),
                  pl.BlockSpec((1, hid), lambda i: (0, 0)),
                  pl.BlockSpec((hid, hid), lambda i: (0, 0)),
                  pl.BlockSpec((1, hid), lambda i: (0, 0)),
                  pl.BlockSpec((1, hid, oc), lambda i: (i // half, 0, 0)),
                  pl.BlockSpec((1, 1, oc), lambda i: (i // half, 0, 0))],
        out_specs=pl.BlockSpec((tl, oc), lambda i: (i, 0)),
        out_shape=jax.ShapeDtypeStruct((n, oc), F32),
        compiler_params=_cparams(("parallel",)),
        name="hyena_filter",
    )(feats, w1, p['hy_b1'].reshape(1, hid), p['hy_w2'], p['hy_b2'].reshape(1, hid), w3, dec)


def _dft_first_kernel(g_ref, x_ref, o_ref, *, precise):
    if precise:
        o_ref[0] = _dot_3pass(g_ref[...], x_ref[0])
    else:
        y = jnp.dot(g_ref[...].astype(BF16), x_ref[0].astype(BF16), preferred_element_type=F32)
        o_ref[0] = y.astype(o_ref.dtype)


def dft_first(g, x, *, n_out, offset=0, tn=8192, precise=False):
    _, r, lanes = x.shape
    m = g.shape[0]
    return pl.pallas_call(
        functools.partial(_dft_first_kernel, precise=precise),
        grid=(n_out, lanes // tn),
        in_specs=[pl.BlockSpec((m, r), lambda p, j: (0, 0)),
                  pl.BlockSpec((1, r, tn), lambda p, j: (offset + p, 0, j))],
        out_specs=pl.BlockSpec((1, m, tn), lambda p, j: (p, 0, j)),
        out_shape=jax.ShapeDtypeStruct((n_out, m, lanes), F32 if precise else BF16),
        compiler_params=_cparams(("parallel", "parallel")),
        name="dft_first",
    )(g, x)


def _dft_filter_mid_kernel(g_ref, y_ref, o_ref, *, scale):
    for i in range(g_ref.shape[0]):
        y = jnp.concatenate([y_ref[0, i], y_ref[1, i]], axis=0)
        o_ref[i] = scale * _dot_3pass(g_ref[i], y)


def dft_filter_mid(g2, y1):
    _, na, nb, oc = y1.shape
    kg = HY_KG // 2
    return pl.pallas_call(
        functools.partial(_dft_filter_mid_kernel, scale=1.0 / HY_N),
        grid=(na // kg,),
        in_specs=[pl.BlockSpec((kg, 2 * nb, 2 * nb), lambda g: (g, 0, 0)),
                  pl.BlockSpec((2, kg, nb, oc), lambda g: (0, g, 0, 0))],
        out_specs=pl.BlockSpec((kg, 2 * nb, oc), lambda g: (g, 0, 0)),
        out_shape=jax.ShapeDtypeStruct((na, 2 * nb, oc), F32),
        compiler_params=_cparams(("parallel",)),
        name="dft_filter_mid",
    )(g2, y1)


def _dft_mid_kernel(gf_ref, gi_ref, y_ref, h_ref, o_ref):
    nb = y_ref.shape[3]
    for i in range(gf_ref.shape[0]):
        y = jnp.concatenate([y_ref[0, 0, i], y_ref[0, 1, i]], axis=0)
        x = jnp.dot(gf_ref[i].astype(BF16), y, preferred_element_type=F32)
        xr, xi = x[:nb], x[nb:]
        hr, hi = h_ref[i, :nb], h_ref[i, nb:]
        prod = jnp.concatenate([xr * hr - xi * hi, xr * hi + xi * hr], axis=0).astype(BF16)
        u = jnp.dot(gi_ref[i].astype(BF16), prod, preferred_element_type=F32)
        o_ref[0, 0, i] = u[:nb].astype(o_ref.dtype)
        o_ref[0, 1, i] = u[nb:].astype(o_ref.dtype)


def dft_mid(g2, g2_inv, y1, h, order):
    npair, _, na, nb, c = y1.shape
    kg = HY_KG
    return pl.pallas_call(
        _dft_mid_kernel,
        grid=(na // kg, npair),
        in_specs=[pl.BlockSpec((kg, 2 * nb, 2 * nb), lambda g, p: (g, 0, 0)),
                  pl.BlockSpec((kg, 2 * nb, 2 * nb), lambda g, p: (g, 0, 0)),
                  pl.BlockSpec((1, 2, kg, nb, c), lambda g, p: (p, 0, g, 0, 0)),
                  pl.BlockSpec((kg, 2 * nb, c), lambda g, p: (g, 0, order))],
        out_specs=pl.BlockSpec((1, 2, kg, nb, c), lambda g, p: (p, 0, g, 0, 0)),
        out_shape=jax.ShapeDtypeStruct(y1.shape, BF16),
        compiler_params=_cparams(("parallel", "parallel")),
        name="dft_mid",
    )(g2, g2_inv, y1, h)


def _dft_last_kernel(g_ref, u_ref, z_ref, gate_ref, skip_ref, o_ref):
    conv = jnp.dot(g_ref[...].astype(BF16), u_ref[0], preferred_element_type=F32)
    y = gate_ref[0].astype(F32) * (conv + skip_ref[...] * z_ref[0].astype(F32))
    o_ref[0] = y.astype(o_ref.dtype)


def dft_last(g1_inv, u, z, z_off, gate, gate_off, skip_lanes, *, out_dtype, tn=8192):
    npair, m2, lanes = u.shape
    r = g1_inv.shape[0]
    return pl.pallas_call(
        _dft_last_kernel,
        grid=(npair, lanes // tn),
        in_specs=[pl.BlockSpec((r, m2), lambda p, j: (0, 0)),
                  pl.BlockSpec((1, m2, tn), lambda p, j: (p, 0, j)),
                  pl.BlockSpec((1, r, tn), lambda p, j: (z_off + p, 0, j)),
                  pl.BlockSpec((1, r, tn), lambda p, j: (gate_off + p, 0, j)),
                  pl.BlockSpec((1, tn), lambda p, j: (0, 0))],
        out_specs=pl.BlockSpec((1, r, tn), lambda p, j: (p, 0, j)),
        out_shape=jax.ShapeDtypeStruct((npair, r, lanes), out_dtype),
        compiler_params=_cparams(("parallel", "parallel")),
        name="dft_last",
    )(g1_inv, u, z, gate, skip_lanes)


def hyena_long(proj, p):
    bsz, l, _ = proj.shape
    assert 2 * l == HY_N and bsz % 2 == 0
    c = HY_WIDTH
    npair = bsz // 2
    lanes = HY_NB * c
    g1, g1_inv, g1_real, g2, g2_inv = _dft_tables()
    g1_bf, g1_inv_bf = jnp.asarray(g1), jnp.asarray(g1_inv)
    g2_bf, g2_inv_bf = jnp.asarray(g2), jnp.asarray(g2_inv)

    buf = hyena_filter_buffer(l, p)
    y1h = dft_first(jnp.asarray(g1_real), buf.reshape(1, HY_NA, HY_NB * HY_ORDER * c), n_out=1, precise=True)
    hspec = dft_filter_mid(jnp.asarray(g2), y1h.reshape(2, HY_NA, HY_NB, HY_ORDER * c))

    vx = short_conv_groups(proj, P_HY, p['hy_conv_w'], p['hy_conv_b'], group=c, tc=CONV_TC, silu=False,
                           name="hyena_short_conv", out_dtype=BF16)
    vx = vx.reshape(3 * npair, HY_NA, lanes)
    z, z_off = vx, 0
    for n in range(HY_ORDER):
        y1 = dft_first(g1_bf, z, n_out=npair, offset=z_off)
        u = dft_mid(g2_bf, g2_inv_bf, y1.reshape(npair, 2, HY_NA, HY_NB, c), hspec, n)
        skip_lanes = jnp.tile(p['hy_skip'][n], 8192 // c).reshape(1, 8192)
        z = dft_last(g1_inv_bf, u.reshape(npair, 2 * HY_NA, lanes), z, z_off, vx, (n + 1) * npair, skip_lanes,
                     out_dtype=BF16 if n + 1 < HY_ORDER else F32)
        z_off = 0
    return z.reshape(bsz, l, c)


def _log_sigmoid(x):
    return jnp.minimum(x, 0.0) - jnp.log1p(jnp.exp(-jnp.abs(x)))


def _mlstm_kernel(*refs, nc_ctx, direction):
    ins, bias_ref, outs, (c_ref, m_ref) = refs[:8], refs[8], refs[9:11], refs[11:13]
    d = ML_HEAD_DIM
    t = ML_CHUNK
    nh = ML_HEADS
    step = pl.program_id(1)

    @pl.when(step == 0)
    def _():
        c_ref[...] = jnp.zeros_like(c_ref)
        m_ref[...] = jnp.zeros_like(m_ref)

    i0 = S_GATES + direction * nh
    f0 = S_GATES + (2 + direction) * nh

    def chunk(q_ref, k_ref, v_ref, sm_ref, o_ref):
        g = sm_ref[0] + bias_ref[...]
        gt = g.T
        lf_cols = _log_sigmoid(g)
        lf_rows = _log_sigmoid(gt)
        r = lax.broadcasted_iota(jnp.int32, (t, t), 0)
        cidx = lax.broadcasted_iota(jnp.int32, (t, t), 1)
        mask = cidx <= r if direction == 0 else cidx >= r
        mask_t = r <= cidx if direction == 0 else r >= cidx
        tri = mask.astype(F32)
        tri_t = mask_t.astype(F32)
        lane = lax.broadcasted_iota(jnp.int32, (t, d), 1)
        ones_col = jnp.where(lane == 0, 1.0, 0.0).astype(BF16)
        for h in range(nh):
            sl = slice(h * d, (h + 1) * d)
            lf_row = lf_rows[f0 + h:f0 + h + 1, :]
            bc = jnp.sum(tri * lf_row, axis=1, keepdims=True)
            br = jnp.sum(tri_t * lf_cols[:, f0 + h:f0 + h + 1], axis=0, keepdims=True)
            lir = gt[i0 + h:i0 + h + 1, :]
            m_prev = m_ref[h, 0:1, 0:1]
            dmat = jnp.where(mask, bc - br + lir, -jnp.inf)
            inter = bc + m_prev
            m_t = jnp.maximum(inter, jnp.max(dmat, axis=1, keepdims=True))
            dexp = jnp.exp(dmat - m_t)
            inter_w = jnp.exp(inter - m_t)
            qh = (q_ref[0, 0, :, sl] * (d ** -0.5)).astype(BF16)
            kt = k_ref[0, 0, :, sl].T
            v_ext = jnp.concatenate([v_ref[0, :, sl].astype(BF16), ones_col], axis=1)
            s = jnp.dot(qh, kt.astype(BF16), preferred_element_type=F32) * dexp
            c_ext = c_ref[h]
            acc = inter_w * jnp.dot(qh, c_ext.astype(BF16), preferred_element_type=F32)
            acc = acc + jnp.dot(s.astype(BF16), v_ext, preferred_element_type=F32)
            den = jnp.maximum(jnp.abs(acc[:, d:d + 1]), jnp.exp(-m_t))
            o_ref[0, :, sl] = acc[:, :d] / den
            b_last = jnp.sum(lf_row, axis=1, keepdims=True)
            g_row = b_last - br + lir
            m_new = jnp.maximum(b_last + m_prev, jnp.max(g_row, axis=1, keepdims=True))
            a = jnp.exp(b_last + m_prev - m_new)
            w_row = jnp.exp(g_row - m_new)
            c_ref[h] = a * c_ext + jnp.dot((kt * w_row).astype(BF16), v_ext, preferred_element_type=F32)
            m_ref[h] = jnp.broadcast_to(m_new, m_ref.shape[1:])

    @pl.when(step < nc_ctx)
    def _():
        chunk(*ins[:4], outs[0])

    @pl.when(step >= nc_ctx)
    def _():
        chunk(*ins[4:], outs[1])


def mlstm_bidir(qk_c, proj_c, small_c, qk_l, proj_l, small_l, gate_bias):
    _, bsz, lc, w = qk_c.shape
    ll = qk_l.shape[2]
    t, nh = ML_CHUNK, ML_HEADS
    nc_c, nc_l = lc // t, ll // t
    bias = jnp.zeros((1, N_SMALL), F32).at[0, S_GATES:S_GATES + ML_GATES].set(gate_bias)

    def idx(dd, step_off, nc):
        def f(g):
            cc = jnp.clip(g - step_off, 0, nc - 1)
            return cc + dd * (nc - 1 - 2 * cc)
        return f

    vcol = P_MLV // w

    def seg_specs(ix):
        return [pl.BlockSpec((1, 1, t, w), lambda b, g: (0, b, ix(g), 0)),
                pl.BlockSpec((1, 1, t, w), lambda b, g: (1, b, ix(g), 0)),
                pl.BlockSpec((1, t, w), lambda b, g: (b, ix(g), vcol)),
                pl.BlockSpec((1, t, N_SMALL), lambda b, g: (b, ix(g), 0))]

    def out_spec(ix):
        return pl.BlockSpec((1, t, w), lambda b, g: (b, ix(g), 0))

    h_c, h_l = [], []
    for dd in range(2):
        ic, il = idx(dd, 0, nc_c), idx(dd, nc_c, nc_l)
        hc, hl = pl.pallas_call(
            functools.partial(_mlstm_kernel, nc_ctx=nc_c, direction=dd),
            grid=(bsz, nc_c + nc_l),
            in_specs=seg_specs(ic) + seg_specs(il) + [pl.BlockSpec((1, N_SMALL), lambda b, g: (0, 0))],
            out_specs=[out_spec(ic), out_spec(il)],
            out_shape=[jax.ShapeDtypeStruct((bsz, lc, w), F32), jax.ShapeDtypeStruct((bsz, ll, w), F32)],
            scratch_shapes=[pltpu.VMEM((nh, ML_HEAD_DIM, 2 * ML_HEAD_DIM), F32), pltpu.VMEM((nh, 8, 128), F32)],
            compiler_params=_cparams(("parallel", "arbitrary")),
            name="mlstm_fwd" if dd == 0 else "mlstm_bwd",
        )(qk_c, qk_c, proj_c, small_c, qk_l, qk_l, proj_l, small_l, bias)
        h_c.append(hc)
        h_l.append(hl)
    return tuple(h_c), tuple(h_l)


def _rms(x):
    return x * lax.rsqrt(jnp.mean(x * x, axis=-1, keepdims=True) + EPS)


def _swap_rope_halves(x):
    w = x.shape[1]
    lane = lax.broadcasted_iota(jnp.int32, x.shape, 1) % MLA_ROPE
    return jnp.where(lane < MLA_ROPE // 2, pltpu.roll(x, w - MLA_ROPE // 2, axis=1),
                     pltpu.roll(x, MLA_ROPE // 2, axis=1))


def _mla_prep_kernel(pq_ref, pkv_ref, sm_ref, gqa_ref, gkva_ref, wq_ref, wkv_ref, gqn_ref, gqr_ref, gkn_ref,
                     gkr_ref, en_ref, er_ref, *rest, use_rope):
    if use_rope:
        ccq_ref, ssq_ref, cck_ref, ssk_ref, qt_ref, k_ref, vt_ref = rest
    else:
        qt_ref, k_ref, vt_ref = rest
    nh, dn, dr = MLA_HEADS, MLA_NOPE, MLA_ROPE
    qa = (_rms(pq_ref[0].astype(F32)) * gqa_ref[...]).astype(BF16)
    kva = (_rms(pkv_ref[0].astype(F32)) * gkva_ref[...]).astype(BF16)
    q_raw = jnp.dot(qa, wq_ref[...], preferred_element_type=F32)
    kv_raw = jnp.dot(kva, wkv_ref[...], preferred_element_type=F32)
    inv_d = 1.0 / MLA_QK

    sel = functools.partial(jnp.dot, preferred_element_type=F32)
    qn, qr = q_raw[:, :nh * dn], q_raw[:, nh * dn:]
    ss = sel((qn * qn).astype(BF16), en_ref[...]) + sel((qr * qr).astype(BF16), er_ref[...])
    rs_rows = lax.rsqrt(ss * inv_d + EPS).T
    qn = qn * gqn_ref[...]
    qr = qr * gqr_ref[...]
    if use_rope:
        qr = qr * ccq_ref[...] + _swap_rope_halves(qr) * ssq_ref[...]
    qrt = qr.T
    tl = qn.shape[0]
    for h in range(nh):
        rs_h = rs_rows[h:h + 1, :]
        qt_ref[0, h, 0:dn, :] = (qn[:, h * dn:(h + 1) * dn].T * rs_h).astype(BF16)
        qt_ref[0, h, dn:dn + dr, :] = (qrt[h * dr:(h + 1) * dr] * rs_h).astype(BF16)
        qt_ref[0, h, dn + dr:, :] = jnp.zeros((QK_PAD - dn - dr, tl), BF16)

    kn, vv = kv_raw[:, :nh * dn], kv_raw[:, nh * dn:]
    lane = lax.broadcasted_iota(jnp.int32, sm_ref.shape[1:], 1)
    kr = jnp.where(lane < dr, sm_ref[0], 0.0)
    ssk = sel((kn * kn).astype(BF16), en_ref[...]) + jnp.sum(kr * kr, axis=-1, keepdims=True)
    rsk = lax.rsqrt(ssk * inv_d + EPS)
    kn = kn * gkn_ref[...]
    kr = kr * gkr_ref[...]
    if use_rope:
        kr = kr * cck_ref[...] + _swap_rope_halves(kr) * ssk_ref[...]
    for h in range(nh):
        rs_h = jnp.broadcast_to(rsk[:, h:h + 1], (tl, dn))
        k_ref[0, h, :, 0:dn] = (kn[:, h * dn:(h + 1) * dn] * rs_h).astype(BF16)
        k_ref[0, h, :, dn:] = (kr * rs_h).astype(BF16)
        vt_ref[0, h] = vv[:, h * MLA_V:(h + 1) * MLA_V].T.astype(BF16)


@functools.lru_cache(maxsize=None)
def _head_selectors():
    en = np.kron(np.eye(MLA_HEADS), np.ones((MLA_NOPE, 1)))
    er = np.kron(np.eye(MLA_HEADS), np.ones((MLA_ROPE, 1)))
    pad = lambda m: np.pad(m, ((0, 0), (0, 128 - m.shape[1]))).astype(np.float32)
    return pad(en), pad(er)


@functools.lru_cache(maxsize=None)
def _rope_tables(n_tokens):
    f32 = np.float32
    rows = n_tokens // GRID_W
    row = np.repeat(np.arange(rows, dtype=f32), GRID_W)
    col = np.tile(np.arange(GRID_W, dtype=f32), rows)
    freqs = f32(ROPE_BASE) ** (-np.arange(ROPE_PAIRS_PER_AXIS, dtype=f32) / f32(ROPE_PAIRS_PER_AXIS))
    ang = np.concatenate([row[:, None] * freqs, col[:, None] * freqs], axis=-1).astype(f32)
    cos, sin = np.cos(ang).astype(f32), np.sin(ang).astype(f32)
    cc = np.concatenate([cos, cos], axis=-1)
    ss = np.concatenate([-sin, sin], axis=-1)
    zero = np.zeros((n_tokens, 128 - MLA_ROPE), f32)
    return (np.tile(cc, (1, MLA_HEADS)), np.tile(ss, (1, MLA_HEADS)),
            np.concatenate([cc, zero], axis=-1), np.concatenate([ss, zero], axis=-1))


def mla_prep(proj, small, p, rope_tables, *, tl):
    b, l, _ = proj.shape
    nh = MLA_HEADS
    en, er = _head_selectors()
    en_b, er_b = jnp.asarray(en, BF16), jnp.asarray(er, BF16)
    const = lambda shape: pl.BlockSpec(shape, lambda bi, i: (0,) * len(shape))
    nq = nh * MLA_QK
    nkv = nh * (MLA_NOPE + MLA_V)
    in_specs = [pl.BlockSpec((1, tl, MLA_Q_LORA), lambda bi, i: (bi, i, P_CQ // MLA_Q_LORA)),
                pl.BlockSpec((1, tl, MLA_KV_LORA), lambda bi, i: (bi, i, P_CKV // MLA_KV_LORA)),
                pl.BlockSpec((1, tl, N_SMALL), lambda bi, i: (bi, i, 0)),
                const((1, MLA_Q_LORA)), const((1, MLA_KV_LORA)), const((MLA_Q_LORA, nq)), const((MLA_KV_LORA, nkv)),
                const((1, nh * MLA_NOPE)), const((1, nh * MLA_ROPE)), const((1, nh * MLA_NOPE)), const((1, 128)),
                const(en.shape), const(er.shape)]
    operands = [proj, proj, small, p['gqa'], p['gkva'], p['wq'], p['wkv'], p['gqn'], p['gqr'], p['gkn'], p['gkr'],
                en_b, er_b]
    use_rope = rope_tables is not None
    if use_rope:
        in_specs += [pl.BlockSpec((tl, nh * MLA_ROPE), lambda bi, i: (i, 0)),
                     pl.BlockSpec((tl, nh * MLA_ROPE), lambda bi, i: (i, 0)),
                     pl.BlockSpec((tl, 128), lambda bi, i: (i, 0)),
                     pl.BlockSpec((tl, 128), lambda bi, i: (i, 0))]
        operands += list(rope_tables)
    return pl.pallas_call(
        functools.partial(_mla_prep_kernel, use_rope=use_rope),
        grid=(b, l // tl),
        in_specs=in_specs,
        out_specs=[pl.BlockSpec((1, nh, QK_PAD, tl), lambda bi, i: (bi, 0, 0, i)),
                   pl.BlockSpec((1, nh, tl, QK_PAD), lambda bi, i: (bi, 0, i, 0)),
                   pl.BlockSpec((1, nh, MLA_V, tl), lambda bi, i: (bi, 0, 0, i))],
        out_shape=[jax.ShapeDtypeStruct((b, nh, QK_PAD, l), BF16),
                   jax.ShapeDtypeStruct((b, nh, l, QK_PAD), BF16),
                   jax.ShapeDtypeStruct((b, nh, MLA_V, l), BF16)],
        compiler_params=_cparams(("parallel", "parallel")),
        name="mla_prep",
    )(*operands)


def _combine_out_kernel(a_ref, y_ref, hf_ref, hb_ref, o_ref, g_ref, w_ref, r_ref, gt_ref, out_ref, cat_ref):
    @pl.when(pl.program_id(2) == 0)
    def _():
        g = g_ref[...]
        cat_ref[:, :MLA_WIDTH] = (_rms(a_ref[0]) * g[:, :MLA_WIDTH]).astype(BF16)
        hy0 = MLA_WIDTH
        cat_ref[:, hy0:hy0 + HY_WIDTH] = (_rms(y_ref[0]) * g[:, hy0:hy0 + HY_WIDTH]).astype(BF16)
        ml0 = MLA_WIDTH + HY_WIDTH
        hh = hf_ref[0] + hb_ref[0]
        og = 0.5 * jnp.tanh(0.5 * o_ref[0].astype(F32)) + 0.5
        for hd in range(ML_HEADS):
            sl = slice(hd * ML_HEAD_DIM, (hd + 1) * ML_HEAD_DIM)
            gl = g[:, ml0 + hd * ML_HEAD_DIM:ml0 + (hd + 1) * ML_HEAD_DIM]
            cat_ref[:, ml0 + hd * ML_HEAD_DIM:ml0 + (hd + 1) * ML_HEAD_DIM] = (
                _rms(hh[:, sl]) * gl * og[:, sl]).astype(BF16)

    y = jnp.dot(cat_ref[...], w_ref[...], preferred_element_type=F32)
    out_ref[0] = r_ref[0] + gt_ref[0] * y


def combine_out_proj(a, y, h2, proj, g_mix, w_out, res, gate, *, tm, tn, name="out_proj"):
    b, s, d = res.shape
    return pl.pallas_call(
        _combine_out_kernel,
        grid=(b, s // tm, d // tn),
        in_specs=[pl.BlockSpec((1, tm, MLA_WIDTH), lambda bi, i, j: (bi, i, 0)),
                  pl.BlockSpec((1, tm, HY_WIDTH), lambda bi, i, j: (bi, i, 0)),
                  pl.BlockSpec((1, tm, ML_WIDTH), lambda bi, i, j: (bi, i, 0)),
                  pl.BlockSpec((1, tm, ML_WIDTH), lambda bi, i, j: (bi, i, 0)),
                  pl.BlockSpec((1, tm, ML_WIDTH), lambda bi, i, j: (bi, i, P_MLO // ML_WIDTH)),
                  pl.BlockSpec((1, MIX_WIDTH), lambda bi, i, j: (0, 0)),
                  pl.BlockSpec((MIX_WIDTH, tn), lambda bi, i, j: (0, j)),
                  pl.BlockSpec((1, tm, tn), lambda bi, i, j: (bi, i, j)),
                  pl.BlockSpec((1, 1, tn), lambda bi, i, j: (bi, 0, j))],
        out_specs=pl.BlockSpec((1, tm, tn), lambda bi, i, j: (bi, i, j)),
        out_shape=jax.ShapeDtypeStruct((b, s, d), F32),
        scratch_shapes=[pltpu.VMEM((tm, MIX_WIDTH), BF16)],
        compiler_params=_cparams(("parallel", "parallel", "arbitrary")),
        name=name,
    )(a, y, h2[0], h2[1], proj, g_mix.reshape(1, MIX_WIDTH), w_out, res, gate)


@functools.lru_cache(maxsize=None)
def _direct_dft_tables(l):
    n = 2 * l
    k = np.arange(n)[:, None]
    t = np.arange(l)[None, :]
    ang = 2.0 * np.pi * ((k * t) % n) / n
    c, s = np.cos(ang), np.sin(ang)
    gf = np.block([[c, s], [-s, c]])
    gi = gf.T.copy()
    t_full = np.arange(n)[None, :]
    ang = 2.0 * np.pi * ((k * t_full) % n) / n
    g_real = np.concatenate([np.cos(ang), -np.sin(ang)], axis=0)
    return gf.astype(np.float32), gi.astype(np.float32), g_real.astype(np.float32)


def _direct_conv_kernel(gf_ref, gi_ref, z_ref, gate_ref, h_ref, skip_ref, o_ref, *, scale):
    z = z_ref[0]
    x = jnp.dot(gf_ref[...].astype(BF16), z.astype(BF16), preferred_element_type=F32)
    n = x.shape[0] // 2
    xr, xi = x[:n], x[n:]
    hr, hi = h_ref[:n] * scale, h_ref[n:] * scale
    prod = jnp.concatenate([xr * hr - xi * hi, xr * hi + xi * hr], axis=0).astype(BF16)
    conv = jnp.dot(gi_ref[...].astype(BF16), prod, preferred_element_type=F32)
    o_ref[0] = gate_ref[0] * (conv + skip_ref[...] * z)


def hyena_short(proj, p):
    bsz, l, _ = proj.shape
    c = HY_WIDTH
    npair = bsz // 2
    n = 2 * l
    gf, gi, g_real = _direct_dft_tables(l)
    buf = hyena_filter_buffer(l, p, tl=l)
    hspec = dft_first(jnp.asarray(g_real), buf.reshape(1, n, HY_ORDER * c), n_out=1, tn=HY_ORDER * c,
                      precise=True)[0]
    vx = short_conv_groups(proj, P_HY, p['hy_conv_w'], p['hy_conv_b'], group=c, tc=CONV_TC, silu=False,
                           name="hyena_short_conv_ctx")
    vx = vx.reshape(3 * npair, 2 * l, c)
    z = vx

    def rows_at(off):
        return lambda q: (off + q, 0, 0)

    def cols_at(col):
        return lambda q: (0, col)

    for order in range(HY_ORDER):
        z = pl.pallas_call(
            functools.partial(_direct_conv_kernel, scale=1.0 / n),
            grid=(npair,),
            in_specs=[pl.BlockSpec(gf.shape, lambda q: (0, 0)),
                      pl.BlockSpec(gi.shape, lambda q: (0, 0)),
                      pl.BlockSpec((1, 2 * l, c), rows_at(0)),
                      pl.BlockSpec((1, 2 * l, c), rows_at((order + 1) * npair)),
                      pl.BlockSpec((2 * n, c), cols_at(order)),
                      pl.BlockSpec((1, c), lambda q: (0, 0))],
            out_specs=pl.BlockSpec((1, 2 * l, c), lambda q: (q, 0, 0)),
            out_shape=jax.ShapeDtypeStruct((npair, 2 * l, c), F32),
            compiler_params=_cparams(("parallel",)),
            name="hyena_direct_conv",
        )(jnp.asarray(gf), jnp.asarray(gi), z, vx, hspec, p['hy_skip'][order].reshape(1, c))
    return z.reshape(bsz, l, c)


def _split_cols(w, sizes):
    out, start = [], 0
    for s in sizes:
        out.append(w[:, start:start + s])
        start += s
    return out


def _prep_layer_weights(l, w_in, mla_qa_norm, mla_kva_norm, mla_w_uq, mla_w_ukv, mla_q_norm, mla_k_norm):
    cq, ckv, kr, hy, mqk, mv, mo, gt = _split_cols(w_in[l], IN_SIZES)
    w_main = jnp.concatenate([cq, ckv, hy, mqk, mv, mo], axis=1).astype(BF16)
    pad = jnp.zeros((D_MODEL, N_SMALL - MLA_ROPE - ML_GATES), F32)
    w_small = jnp.concatenate([kr, gt, pad], axis=1).astype(BF16)
    nh = MLA_HEADS
    wq = mla_w_uq[l].reshape(MLA_Q_LORA, nh, MLA_QK)
    wq = jnp.concatenate([wq[:, :, :MLA_NOPE].reshape(MLA_Q_LORA, -1), wq[:, :, MLA_NOPE:].reshape(MLA_Q_LORA, -1)],
                         axis=1).astype(BF16)
    wkv = mla_w_ukv[l].reshape(MLA_KV_LORA, nh, MLA_NOPE + MLA_V)
    wkv = jnp.concatenate([wkv[:, :, :MLA_NOPE].reshape(MLA_KV_LORA, -1),
                           wkv[:, :, MLA_NOPE:].reshape(MLA_KV_LORA, -1)], axis=1).astype(BF16)
    q_scale = (MLA_QK ** -0.5) * math.log2(math.e)
    gq, gk = mla_q_norm[l] * q_scale, mla_k_norm[l]
    mla = {'gqa': mla_qa_norm[l].reshape(1, -1), 'gkva': mla_kva_norm[l].reshape(1, -1), 'wq': wq, 'wkv': wkv,
           'gqn': jnp.tile(gq[:MLA_NOPE], nh).reshape(1, -1), 'gqr': jnp.tile(gq[MLA_NOPE:], nh).reshape(1, -1),
           'gkn': jnp.tile(gk[:MLA_NOPE], nh).reshape(1, -1),
           'gkr': jnp.concatenate([gk[MLA_NOPE:], jnp.zeros((128 - MLA_ROPE,), F32)]).reshape(1, -1)}
    return w_main, w_small, mla


def kernel(x, c, ctx, c_ctx, ada_w, ada_b, norm1_g, norm2_g, w_in, mla_qa_norm, mla_kva_norm, mla_w_uq,
           mla_w_ukv, mla_q_norm, mla_k_norm, hy_conv_w, hy_conv_b, hy_w1, hy_b1, hy_w2, hy_b2, hy_w3,
           hy_decay, hy_skip, ml_conv_w, ml_conv_b, ml_gate_b, mix_norm_g, w_out, ffn_w1, ffn_w2):
    B, S, D = x.shape
    LC = ctx.shape[1]
    rope_tables = _rope_tables(S)

    silu_rows = jnp.concatenate([jax.nn.silu(c), jax.nn.silu(c_ctx)[None], jnp.zeros((8 - B - 1, D), F32)], axis=0)
    mods = ada_modulation(silu_rows.astype(BF16), ada_w, ada_b)

    for l in range(DEPTH):
        need_ctx = l < DEPTH - 1
        p = {'hy_conv_w': hy_conv_w[l], 'hy_conv_b': hy_conv_b[l], 'hy_w1': hy_w1[l], 'hy_b1': hy_b1[l],
             'hy_w2': hy_w2[l], 'hy_b2': hy_b2[l], 'hy_w3': hy_w3[l], 'hy_decay': hy_decay[l],
             'hy_skip': hy_skip[l], 'ml_conv_w': ml_conv_w[l], 'ml_conv_b': ml_conv_b[l]}
        w_main, w_small, mla = _prep_layer_weights(l, w_in, mla_qa_norm, mla_kva_norm, mla_w_uq, mla_w_ukv,
                                                   mla_q_norm, mla_k_norm)
        w_out_l = w_out[l].astype(BF16)
        w1_l = ffn_w1[l].astype(BF16)
        w2_l = ffn_w2[l].astype(BF16)

        mod_l = [m[:, None, :] for m in jnp.split(mods[l, :B], 6, axis=-1)]
        mod_c = [m[:, None, :] for m in jnp.split(mods[l, B:B + 1], 6, axis=-1)]
        flat = lambda t: t.reshape(1, B * LC, t.shape[-1])

        proj_l, small_l = in_proj(x, norm1_g[l], mod_l[1], mod_l[0], w_main, w_small, tm=1024, tn=IN_PROJ_TN)
        proj_c, small_c = in_proj(flat(ctx), norm1_g[l], mod_c[1], mod_c[0], w_main, w_small, tm=B * LC,
                                  tn=IN_PROJ_TN, name="in_proj_ctx")
        proj_c, small_c = proj_c.reshape(B, LC, N_MAIN), small_c.reshape(B, LC, N_SMALL)

        qt_l, k_l, vt_l = mla_prep(proj_l, small_l, mla, rope_tables, tl=512)
        qt_c, k_c, vt_c = mla_prep(proj_c, small_c, mla, None, tl=LC)
        a_l = attention(qt_l, [(k_c, vt_c), (k_l, vt_l)], tq=ATT_TQ)
        y_l = hyena_long(proj_l, p)

        def mlstm_qk(proj):
            return short_conv_groups(proj, P_MLQK, p['ml_conv_w'], p['ml_conv_b'], group=ML_WIDTH, tc=CONV_TC,
                                     silu=True, name="mlstm_short_conv")

        h_c, h_l = mlstm_bidir(mlstm_qk(proj_c), proj_c, small_c, mlstm_qk(proj_l), proj_l, small_l, ml_gate_b[l])
        x = combine_out_proj(a_l, y_l, h_l, proj_l, mix_norm_g[l], w_out_l, x, mod_l[2], tm=1024, tn=MM_TN)
        hid = norm_swiglu(x, norm2_g[l], mod_l[4], mod_l[3], w1_l, tm=1024, tn=MM_TN, name="ffn_up")
        x = mm_residual(hid, w2_l, x, mod_l[5], tm=1024, tn=MM_TN, name="ffn_down")

        if need_ctx:
            a_c = attention(qt_c, [(k_c, vt_c)], tq=LC)
            y_c = hyena_short(proj_c, p)
            ctx = combine_out_proj(flat(a_c), flat(y_c), (flat(h_c[0]), flat(h_c[1])), flat(proj_c), mix_norm_g[l],
                                   w_out_l, flat(ctx), mod_c[2], tm=B * LC, tn=MM_TN, name="out_proj_ctx")
            hid = norm_swiglu(ctx, norm2_g[l], mod_c[4], mod_c[3], w1_l, tm=B * LC, tn=MM_TN, name="ffn_up_ctx")
            ctx = mm_residual(hid, w2_l, ctx, mod_c[5], tm=B * LC, tn=MM_TN, name="ffn_down_ctx")
            ctx = ctx.reshape(B, LC, D)
    return x
```

```python
import functools
import math

import jax
import jax.numpy as jnp
import numpy as np
from jax import lax
from jax.experimental import pallas as pl
from jax.experimental.pallas import tpu as pltpu

D_MODEL = 2048
DEPTH = 2
GRID_W = 64
EPS = 1e-6

MLA_HEADS = 8
MLA_NOPE = 128
MLA_ROPE = 64
MLA_QK = MLA_NOPE + MLA_ROPE
MLA_V = 128
MLA_Q_LORA = 512
MLA_KV_LORA = 512
MLA_WIDTH = MLA_HEADS * MLA_V
ROPE_BASE = 10000.0
ROPE_PAIRS_PER_AXIS = MLA_ROPE // 4
QK_PAD = 256

HY_WIDTH = 512
HY_ORDER = 2
HY_BANDS = 16
HY_SHIFT = 0.05
SHORT_CONV = 3
CONV_TC = 256

ML_HEADS = 4
ML_HEAD_DIM = 128
ML_WIDTH = ML_HEADS * ML_HEAD_DIM
ML_CHUNK = 256
ML_GATES = 4 * ML_HEADS

MIX_WIDTH = MLA_WIDTH + HY_WIDTH + ML_WIDTH
FFN_HIDDEN = ((8 * D_MODEL // 3 + 255) // 256) * 256

IN_SIZES = (MLA_Q_LORA, MLA_KV_LORA, MLA_ROPE, 3 * HY_WIDTH, 2 * ML_WIDTH, ML_WIDTH, ML_WIDTH, ML_GATES)
N_IN = sum(IN_SIZES)

P_CQ, P_CKV, P_HY, P_MLQK, P_MLV, P_MLO = 0, 512, 1024, 2560, 3584, 4096
N_MAIN = 4608
N_SMALL = 128
S_GATES = MLA_ROPE

VMEM_LIMIT_BYTES = 56 * 1024 * 1024

BF16 = jnp.bfloat16
F32 = jnp.float32


def _cparams(sem):
    return pltpu.CompilerParams(dimension_semantics=sem, vmem_limit_bytes=VMEM_LIMIT_BYTES)


def _ada_kernel(s_ref, w_ref, b_ref, o_ref):
    w = w_ref[0].astype(BF16)
    o_ref[0] = jnp.dot(s_ref[...], w, preferred_element_type=F32) + b_ref[0]


def ada_modulation(silu_rows, ada_w, ada_b, tn=1024):
    depth, d, n = ada_w.shape
    return pl.pallas_call(
        _ada_kernel,
        grid=(depth, n // tn),
        in_specs=[pl.BlockSpec((8, d), lambda l, j: (0, 0)),
                  pl.BlockSpec((1, d, tn), lambda l, j: (l, 0, j)),
                  pl.BlockSpec((1, 1, tn), lambda l, j: (l, 0, j))],
        out_specs=pl.BlockSpec((1, 8, tn), lambda l, j: (l, 0, j)),
        out_shape=jax.ShapeDtypeStruct((depth, 8, n), F32),
        compiler_params=_cparams(("parallel", "parallel")),
        name="ada_modulation",
    )(silu_rows, ada_w, ada_b.reshape(depth, 1, n))


MM_TN = 512
IN_PROJ_TN = 1536


def _normed(x, g, sc, sh):
    gain = g * (1.0 + sc)
    return x * lax.rsqrt(jnp.mean(x * x, axis=-1, keepdims=True) + EPS) * gain + sh


def _in_proj_kernel(x_ref, g_ref, sc_ref, sh_ref, w_ref, ws_ref, o_ref, os_ref, xn_ref):
    @pl.when(pl.program_id(2) == 0)
    def _():
        xn = _normed(x_ref[0], g_ref[...], sc_ref[0], sh_ref[0]).astype(BF16)
        xn_ref[...] = xn
        os_ref[0] = jnp.dot(xn, ws_ref[...], preferred_element_type=F32)

    o_ref[0] = jnp.dot(xn_ref[...], w_ref[...], preferred_element_type=F32).astype(o_ref.dtype)


def in_proj(x, g, sc, sh, w_main, w_small, *, tm, tn, name="in_proj"):
    b, s, k = x.shape
    n = w_main.shape[1]
    ns = w_small.shape[1]
    return pl.pallas_call(
        _in_proj_kernel,
        grid=(b, s // tm, n // tn),
        in_specs=[pl.BlockSpec((1, tm, k), lambda bi, i, j: (bi, i, 0)),
                  pl.BlockSpec((1, k), lambda bi, i, j: (0, 0)),
                  pl.BlockSpec((1, 1, k), lambda bi, i, j: (bi, 0, 0)),
                  pl.BlockSpec((1, 1, k), lambda bi, i, j: (bi, 0, 0)),
                  pl.BlockSpec((k, tn), lambda bi, i, j: (0, j)),
                  pl.BlockSpec((k, ns), lambda bi, i, j: (0, 0))],
        out_specs=[pl.BlockSpec((1, tm, tn), lambda bi, i, j: (bi, i, j)),
                   pl.BlockSpec((1, tm, ns), lambda bi, i, j: (bi, i, 0))],
        out_shape=[jax.ShapeDtypeStruct((b, s, n), BF16), jax.ShapeDtypeStruct((b, s, ns), F32)],
        scratch_shapes=[pltpu.VMEM((tm, k), BF16)],
        compiler_params=_cparams(("parallel", "parallel", "arbitrary")),
        name=name,
    )(x, g.reshape(1, k), sc, sh, w_main, w_small)


def _norm_swiglu_kernel(x_ref, g_ref, sc_ref, sh_ref, wg_ref, wu_ref, o_ref, xn_ref):
    @pl.when(pl.program_id(2) == 0)
    def _():
        xn_ref[...] = _normed(x_ref[0], g_ref[...], sc_ref[0], sh_ref[0]).astype(BF16)

    xn = xn_ref[...]
    gate = jnp.dot(xn, wg_ref[...], preferred_element_type=F32)
    up = jnp.dot(xn, wu_ref[...], preferred_element_type=F32)
    o_ref[0] = (gate * (0.5 * jnp.tanh(0.5 * gate) + 0.5) * up).astype(o_ref.dtype)


def norm_swiglu(x, g, sc, sh, w1, *, tm, tn, name="norm_swiglu"):
    b, s, d = x.shape
    h = w1.shape[1] // 2
    nj = h // tn
    return pl.pallas_call(
        _norm_swiglu_kernel,
        grid=(b, s // tm, nj),
        in_specs=[pl.BlockSpec((1, tm, d), lambda bi, i, j: (bi, i, 0)),
                  pl.BlockSpec((1, d), lambda bi, i, j: (0, 0)),
                  pl.BlockSpec((1, 1, d), lambda bi, i, j: (bi, 0, 0)),
                  pl.BlockSpec((1, 1, d), lambda bi, i, j: (bi, 0, 0)),
                  pl.BlockSpec((d, tn), lambda bi, i, j: (0, j)),
                  pl.BlockSpec((d, tn), lambda bi, i, j: (0, j + nj))],
        out_specs=pl.BlockSpec((1, tm, tn), lambda bi, i, j: (bi, i, j)),
        out_shape=jax.ShapeDtypeStruct((b, s, h), BF16),
        scratch_shapes=[pltpu.VMEM((tm, d), BF16)],
        compiler_params=_cparams(("parallel", "parallel", "arbitrary")),
        name=name,
    )(x, g.reshape(1, d), sc, sh, w1, w1)


def _mm_res_kernel(a_ref, w_ref, r_ref, gt_ref, o_ref):
    y = jnp.dot(a_ref[0], w_ref[...], preferred_element_type=F32)
    o_ref[0] = r_ref[0] + gt_ref[0] * y


def mm_residual(a, w, res, gate, *, tm, tn, name="mm_residual"):
    b, s, k = a.shape
    n = w.shape[1]
    return pl.pallas_call(
        _mm_res_kernel,
        grid=(b, s // tm, n // tn),
        in_specs=[pl.BlockSpec((1, tm, k), lambda bi, i, j: (bi, i, 0)),
                  pl.BlockSpec((k, tn), lambda bi, i, j: (0, j)),
                  pl.BlockSpec((1, tm, tn), lambda bi, i, j: (bi, i, j)),
                  pl.BlockSpec((1, 1, tn), lambda bi, i, j: (bi, 0, j))],
        out_specs=pl.BlockSpec((1, tm, tn), lambda bi, i, j: (bi, i, j)),
        out_shape=jax.ShapeDtypeStruct((b, s, n), F32),
        compiler_params=_cparams(("parallel", "parallel", "parallel")),
        name=name,
    )(a, w, res, gate)


ATT_KEY_CHUNK = 512
ATT_TQ = 1024


def _attn_kernel(qt_ref, *refs, n_seg):
    o_ref, s_ref = refs[2 * n_seg], refs[2 * n_seg + 1]
    qt = qt_ref[0, 0]
    tq = qt.shape[1]
    chunks, off = [], 0
    for sgi in range(n_seg):
        k_ref, vt_ref = refs[2 * sgi], refs[2 * sgi + 1]
        lk = k_ref.shape[2]
        ch = min(ATT_KEY_CHUNK, lk)
        for c in range(lk // ch):
            chunks.append((k_ref, vt_ref, c * ch, ch, off))
            off += ch

    m8 = jnp.full((8, tq), -jnp.inf, F32)
    for k_ref, _, start, ch, off in chunks:
        s = jnp.dot(k_ref[0, 0, start:start + ch, :], qt, preferred_element_type=F32)
        s_ref[off:off + ch, :] = s
        m8 = jnp.maximum(m8, jnp.max(s.reshape(ch // 8, 8, tq), axis=0))
    m = jnp.max(m8, axis=0, keepdims=True)
    l8 = jnp.zeros((8, tq), F32)
    acc = jnp.zeros((vt_ref.shape[2], tq), F32)
    for _, vt_ref, start, ch, off in chunks:
        p = jnp.exp2(s_ref[off:off + ch, :] - m)
        l8 = l8 + jnp.sum(p.reshape(ch // 8, 8, tq), axis=0)
        acc = acc + jnp.dot(vt_ref[0, 0, :, start:start + ch], p.astype(BF16), preferred_element_type=F32)
    l = jnp.sum(l8, axis=0, keepdims=True)
    o_ref[0] = (acc / l).T


def attention(qt, segments, *, tq=256):
    b, h, dk, lq = qt.shape
    dv = segments[0][1].shape[2]
    in_specs = [pl.BlockSpec((1, 1, dk, tq), lambda bi, hi, i: (bi, hi, 0, i))]
    operands = [qt]
    for k, vt in segments:
        lk = k.shape[2]
        in_specs += [pl.BlockSpec((1, 1, lk, dk), lambda bi, hi, i: (bi, hi, 0, 0)),
                     pl.BlockSpec((1, 1, dv, lk), lambda bi, hi, i: (bi, hi, 0, 0))]
        operands += [k, vt]
    return pl.pallas_call(
        functools.partial(_attn_kernel, n_seg=len(segments)),
        grid=(b, h, lq // tq),
        in_specs=in_specs,
        out_specs=pl.BlockSpec((1, tq, dv), lambda bi, hi, i: (bi, i, hi)),
        out_shape=jax.ShapeDtypeStruct((b, lq, h * dv), F32),
        scratch_shapes=[pltpu.VMEM((sum(k.shape[2] for k, _ in segments), tq), F32)],
        compiler_params=_cparams(("parallel", "parallel", "parallel")),
        name="attention",
    )(*operands)


HY_NA = 64
HY_NB = 128
HY_N = HY_NA * HY_NB
HY_KG = 8


@functools.lru_cache(maxsize=None)
def _dft_tables():
    na, nb, n = HY_NA, HY_NB, HY_N
    a = np.arange(na // 2)[None, :]
    ka = np.arange(na)[:, None]
    ang = 2.0 * np.pi * ((a * ka) % na) / na
    c, s = np.cos(ang), np.sin(ang)
    g1 = np.block([[c, s], [-s, c]])
    g1_inv = g1.T.copy()
    a_full = np.arange(na)[None, :]
    ang = 2.0 * np.pi * ((a_full * ka) % na) / na
    g1_real = np.concatenate([np.cos(ang), -np.sin(ang)], axis=0)
    b = np.arange(nb)[None, None, :]
    k = np.arange(na)[:, None, None] + na * np.arange(nb)[None, :, None]
    ang = 2.0 * np.pi * ((b * k) % n) / n
    cr, si = np.cos(ang), np.sin(ang)
    g2 = np.concatenate([np.concatenate([cr, si], axis=2), np.concatenate([-si, cr], axis=2)], axis=1)
    g2_inv = np.transpose(g2, (0, 2, 1)).copy()
    return (g1.astype(np.float32), g1_inv.astype(np.float32), g1_real.astype(np.float32),
            g2.astype(np.float32), g2_inv.astype(np.float32))


def _dot_3pass(a, b):
    a_hi, b_hi = a.astype(BF16), b.astype(BF16)
    a_lo = (a - a_hi.astype(F32)).astype(BF16)
    b_lo = (b - b_hi.astype(F32)).astype(BF16)
    dot = functools.partial(jnp.dot, preferred_element_type=F32)
    return dot(a_hi, b_hi) + (dot(a_hi, b_lo) + dot(a_lo, b_hi))


def _short_conv_kernel(u_ref, w_ref, b_ref, o_ref, *, silu):
    u = u_ref[0].astype(F32)
    n = u.shape[0]
    rows = lax.broadcasted_iota(jnp.int32, u.shape, 0)
    prev = jnp.where(rows == 0, 0.0, pltpu.roll(u, 1, axis=0))
    nxt = jnp.where(rows == n - 1, 0.0, pltpu.roll(u, n - 1, axis=0))
    y = b_ref[...] + prev * w_ref[0:1, :] + u * w_ref[1:2, :] + nxt * w_ref[2:3, :]
    if silu:
        y = y * jax.nn.sigmoid(y)
    o_ref[0, 0] = y.astype(o_ref.dtype)


def short_conv_groups(proj, col0, w, b, *, group, tc, silu, name, out_dtype=F32):
    bsz, l, _ = proj.shape
    width = w.shape[1]
    per = group // tc
    return pl.pallas_call(
        functools.partial(_short_conv_kernel, silu=silu),
        grid=(bsz, width // tc),
        in_specs=[pl.BlockSpec((1, l, tc), lambda bi, j: (bi, 0, col0 // tc + j)),
                  pl.BlockSpec((SHORT_CONV, tc), lambda bi, j: (0, j)),
                  pl.BlockSpec((1, tc), lambda bi, j: (0, j))],
        out_specs=pl.BlockSpec((1, 1, l, tc), lambda bi, j: (j // per, bi, 0, j % per)),
        out_shape=jax.ShapeDtypeStruct((width // group, bsz, l, group), out_dtype),
        compiler_params=_cparams(("parallel", "parallel")),
        name=name,
    )(proj, w, b.reshape(1, width))


def _hy_filter_kernel(f_ref, w1_ref, b1_ref, w2_ref, b2_ref, w3_ref, dec_ref, o_ref, *, zero_row):
    hp = lax.Precision.HIGHEST
    f = f_ref[...]
    tl = f.shape[0]
    h = jnp.sin(jnp.dot(f, w1_ref[...], precision=hp, preferred_element_type=F32) + b1_ref[...])
    h = jnp.sin(jnp.dot(h, w2_ref[...], precision=hp, preferred_element_type=F32) + b2_ref[...])
    h = _dot_3pass(h, w3_ref[0])
    h = h * (jnp.exp(-f[:, 0:1] * dec_ref[0]) + HY_SHIFT)
    rows = lax.broadcasted_iota(jnp.int32, h.shape, 0) + pl.program_id(0) * tl
    o_ref[...] = jnp.where(rows == zero_row, 0.0, h)


@functools.lru_cache(maxsize=None)
def _filter_features(L):
    f32 = np.float32
    n = 2 * L
    r = np.arange(n, dtype=f32)
    t = (np.where(r < L, r, n - r) / f32(L)).astype(f32)
    bands = np.arange(1, HY_BANDS + 1, dtype=f32)
    ang = (f32(2.0 * math.pi) * t[:, None] * bands).astype(f32)
    emb = 1 + 2 * HY_BANDS
    return np.concatenate([t[:, None], np.cos(ang).astype(f32), np.sin(ang).astype(f32),
                           np.zeros((n, 128 - emb), f32)], axis=-1)


def hyena_filter_buffer(L, p, tl=512):
    n = 2 * L
    emb = 1 + 2 * HY_BANDS
    feats = jnp.asarray(_filter_features(L))
    hid = p['hy_w1'].shape[1]
    w1 = jnp.concatenate([p['hy_w1'], jnp.zeros((128 - emb, hid), F32)], axis=0)
    oc = HY_ORDER * HY_WIDTH
    w3 = p['hy_w3'].reshape(hid, HY_ORDER, 2, HY_WIDTH).transpose(2, 0, 1, 3).reshape(2, hid, oc)
    dec = p['hy_decay'].reshape(HY_ORDER, 2, HY_WIDTH).transpose(1, 0, 2).reshape(2, 1, oc)
    half = L // tl
    return pl.pallas_call(
        functools.partial(_hy_filter_kernel, zero_row=L),
        grid=(n // tl,),
        in_specs=[pl.BlockSpec((tl, 128), lambda i: (i, 0)),
                  pl.BlockSpec((128, hid), lambda i: (0, 0)),
                  pl.BlockSpec((1, hid), lambda i: (0, 0)),
                  pl.BlockSpec((hid, hid), lambda i: (0, 0)),
                  pl.BlockSpec((1, hid), lambda i: (0, 0)),
                  pl.BlockSpec((1, hid, oc), lambda i: (i // half, 0, 0)),
                  pl.BlockSpec((1, 1, oc), lambda i: (i // half, 0, 0))],
        out_specs=pl.BlockSpec((tl, oc), lambda i: (i, 0)),
        out_shape=jax.ShapeDtypeStruct((n, oc), F32),
        compiler_params=_cparams(("parallel",)),
        name="hyena_filter",
    )(feats, w1, p['hy_b1'].reshape(1, hid), p['hy_w2'], p['hy_b2'].reshape(1, hid), w3, dec)


def _dft_first_kernel(g_ref, x_ref, o_ref, *, precise):
    if precise:
        o_ref[0] = _dot_3pass(g_ref[...], x_ref[0])
    else:
        y = jnp.dot(g_ref[...].astype(BF16), x_ref[0].astype(BF16), preferred_element_type=F32)
        o_ref[0] = y.astype(o_ref.dtype)


def dft_first(g, x, *, n_out, offset=0, tn=8192, precise=False):
    _, r, lanes = x.shape
    m = g.shape[0]
    return pl.pallas_call(
        functools.partial(_dft_first_kernel, precise=precise),
        grid=(n_out, lanes // tn),
        in_specs=[pl.BlockSpec((m, r), lambda p, j: (0, 0)),
                  pl.BlockSpec((1, r, tn), lambda p, j: (offset + p, 0, j))],
        out_specs=pl.BlockSpec((1, m, tn), lambda p, j: (p, 0, j)),
        out_shape=jax.ShapeDtypeStruct((n_out, m, lanes), F32 if precise else BF16),
        compiler_params=_cparams(("parallel", "parallel")),
        name="dft_first",
    )(g, x)


def _dft_filter_mid_kernel(g_ref, y_ref, o_ref, *, scale):
    for i in range(g_ref.shape[0]):
        y = jnp.concatenate([y_ref[0, i], y_ref[1, i]], axis=0)
        o_ref[i] = scale * _dot_3pass(g_ref[i], y)


def dft_filter_mid(g2, y1):
    _, na, nb, oc = y1.shape
    kg = HY_KG // 2
    return pl.pallas_call(
        functools.partial(_dft_filter_mid_kernel, scale=1.0 / HY_N),
        grid=(na // kg,),
        in_specs=[pl.BlockSpec((kg, 2 * nb, 2 * nb), lambda g: (g, 0, 0)),
                  pl.BlockSpec((2, kg, nb, oc), lambda g: (0, g, 0, 0))],
        out_specs=pl.BlockSpec((kg, 2 * nb, oc), lambda g: (g, 0, 0)),
        out_shape=jax.ShapeDtypeStruct((na, 2 * nb, oc), F32),
        compiler_params=_cparams(("parallel",)),
        name="dft_filter_mid",
    )(g2, y1)


def _dft_mid_kernel(gf_ref, gi_ref, y_ref, h_ref, o_ref):
    nb = y_ref.shape[3]
    for i in range(gf_ref.shape[0]):
        y = jnp.concatenate([y_ref[0, 0, i], y_ref[0, 1, i]], axis=0)
        x = jnp.dot(gf_ref[i].astype(BF16), y, preferred_element_type=F32)
        xr, xi = x[:nb], x[nb:]
        hr, hi = h_ref[i, :nb], h_ref[i, nb:]
        prod = jnp.concatenate([xr * hr - xi * hi, xr * hi + xi * hr], axis=0).astype(BF16)
        u = jnp.dot(gi_ref[i].astype(BF16), prod, preferred_element_type=F32)
        o_ref[0, 0, i] = u[:nb].astype(o_ref.dtype)
        o_ref[0, 1, i] = u[nb:].astype(o_ref.dtype)


def dft_mid(g2, g2_inv, y1, h, order):
    npair, _, na, nb, c = y1.shape
    kg = HY_KG
    return pl.pallas_call(
        _dft_mid_kernel,
        grid=(na // kg, npair),
        in_specs=[pl.BlockSpec((kg, 2 * nb, 2 * nb), lambda g, p: (g, 0, 0)),
                  pl.BlockSpec((kg, 2 * nb, 2 * nb), lambda g, p: (g, 0, 0)),
                  pl.BlockSpec((1, 2, kg, nb, c), lambda g, p: (p, 0, g, 0, 0)),
                  pl.BlockSpec((kg, 2 * nb, c), lambda g, p: (g, 0, order))],
        out_specs=pl.BlockSpec((1, 2, kg, nb, c), lambda g, p: (p, 0, g, 0, 0)),
        out_shape=jax.ShapeDtypeStruct(y1.shape, BF16),
        compiler_params=_cparams(("parallel", "parallel")),
        name="dft_mid",
    )(g2, g2_inv, y1, h)


def _dft_last_kernel(g_ref, u_ref, z_ref, gate_ref, skip_ref, o_ref):
    conv = jnp.dot(g_ref[...].astype(BF16), u_ref[0], preferred_element_type=F32)
    y = gate_ref[0].astype(F32) * (conv + skip_ref[...] * z_ref[0].astype(F32))
    o_ref[0] = y.astype(o_ref.dtype)


def dft_last(g1_inv, u, z, z_off, gate, gate_off, skip_lanes, *, out_dtype, tn=8192):
    npair, m2, lanes = u.shape
    r = g1_inv.shape[0]
    return pl.pallas_call(
        _dft_last_kernel,
        grid=(npair, lanes // tn),
        in_specs=[pl.BlockSpec((r, m2), lambda p, j: (0, 0)),
                  pl.BlockSpec((1, m2, tn), lambda p, j: (p, 0, j)),
                  pl.BlockSpec((1, r, tn), lambda p, j: (z_off + p, 0, j)),
                  pl.BlockSpec((1, r, tn), lambda p, j: (gate_off + p, 0, j)),
                  pl.BlockSpec((1, tn), lambda p, j: (0, 0))],
        out_specs=pl.BlockSpec((1, r, tn), lambda p, j: (p, 0, j)),
        out_shape=jax.ShapeDtypeStruct((npair, r, lanes), out_dtype),
        compiler_params=_cparams(("parallel", "parallel")),
        name="dft_last",
    )(g1_inv, u, z, gate, skip_lanes)


def hyena_long(proj, p):
    bsz, l, _ = proj.shape
    assert 2 * l == HY_N and bsz % 2 == 0
    c = HY_WIDTH
    npair = bsz // 2
    lanes = HY_NB * c
    g1, g1_inv, g1_real, g2, g2_inv = _dft_tables()
    g1_bf, g1_inv_bf = jnp.asarray(g1), jnp.asarray(g1_inv)
    g2_bf, g2_inv_bf = jnp.asarray(g2), jnp.asarray(g2_inv)

    buf = hyena_filter_buffer(l, p)
    y1h = dft_first(jnp.asarray(g1_real), buf.reshape(1, HY_NA, HY_NB * HY_ORDER * c), n_out=1, precise=True)
    hspec = dft_filter_mid(jnp.asarray(g2), y1h.reshape(2, HY_NA, HY_NB, HY_ORDER * c))

    vx = short_conv_groups(proj, P_HY, p['hy_conv_w'], p['hy_conv_b'], group=c, tc=CONV_TC, silu=False,
                           name="hyena_short_conv", out_dtype=BF16)
    vx = vx.reshape(3 * npair, HY_NA, lanes)
    z, z_off = vx, 0
    for n in range(HY_ORDER):
        y1 = dft_first(g1_bf, z, n_out=npair, offset=z_off)
        u = dft_mid(g2_bf, g2_inv_bf, y1.reshape(npair, 2, HY_NA, HY_NB, c), hspec, n)
        skip_lanes = jnp.tile(p['hy_skip'][n], 8192 // c).reshape(1, 8192)
        z = dft_last(g1_inv_bf, u.reshape(npair, 2 * HY_NA, lanes), z, z_off, vx, (n + 1) * npair, skip_lanes,
                     out_dtype=BF16 if n + 1 < HY_ORDER else F32)
        z_off = 0
    return z.reshape(bsz, l, c)


def _log_sigmoid(x):
    return jnp.minimum(x, 0.0) - jnp.log1p(jnp.exp(-jnp.abs(x)))


def _mlstm_kernel(*refs, nc_ctx, direction):
    ins, bias_ref, outs, (c_ref, m_ref) = refs[:8], refs[8], refs[9:11], refs[11:13]
    d = ML_HEAD_DIM
    t = ML_CHUNK
    nh = ML_HEADS
    step = pl.program_id(1)

    @pl.when(step == 0)
    def _():
        c_ref[...] = jnp.zeros_like(c_ref)
        m_ref[...] = jnp.zeros_like(m_ref)

    i0 = S_GATES + direction * nh
    f0 = S_GATES + (2 + direction) * nh

    def chunk(q_ref, k_ref, v_ref, sm_ref, o_ref):
        g = sm_ref[0] + bias_ref[...]
        gt = g.T
        lf_cols = _log_sigmoid(g)
        lf_rows = _log_sigmoid(gt)
        r = lax.broadcasted_iota(jnp.int32, (t, t), 0)
        cidx = lax.broadcasted_iota(jnp.int32, (t, t), 1)
        mask = cidx <= r if direction == 0 else cidx >= r
        mask_t = r <= cidx if direction == 0 else r >= cidx
        tri = mask.astype(F32)
        tri_t = mask_t.astype(F32)
        lane = lax.broadcasted_iota(jnp.int32, (t, d), 1)
        ones_col = jnp.where(lane == 0, 1.0, 0.0).astype(BF16)
        for h in range(nh):
            sl = slice(h * d, (h + 1) * d)
            lf_row = lf_rows[f0 + h:f0 + h + 1, :]
            bc = jnp.sum(tri * lf_row, axis=1, keepdims=True)
            br = jnp.sum(tri_t * lf_cols[:, f0 + h:f0 + h + 1], axis=0, keepdims=True)
            lir = gt[i0 + h:i0 + h + 1, :]
            m_prev = m_ref[h, 0:1, 0:1]
            dmat = jnp.where(mask, bc - br + lir, -jnp.inf)
            inter = bc + m_prev
            m_t = jnp.maximum(inter, jnp.max(dmat, axis=1, keepdims=True))
            dexp = jnp.exp(dmat - m_t)
            inter_w = jnp.exp(inter - m_t)
            qh = (q_ref[0, 0, :, sl] * (d ** -0.5)).astype(BF16)
            kt = k_ref[0, 0, :, sl].T
            v_ext = jnp.concatenate([v_ref[0, :, sl].astype(BF16), ones_col], axis=1)
            s = jnp.dot(qh, kt.astype(BF16), preferred_element_type=F32) * dexp
            c_ext = c_ref[h]
            acc = inter_w * jnp.dot(qh, c_ext.astype(BF16), preferred_element_type=F32)
            acc = acc + jnp.dot(s.astype(BF16), v_ext, preferred_element_type=F32)
            den = jnp.maximum(jnp.abs(acc[:, d:d + 1]), jnp.exp(-m_t))
            o_ref[0, :, sl] = acc[:, :d] / den
            b_last = jnp.sum(lf_row, axis=1, keepdims=True)
            g_row = b_last - br + lir
            m_new = jnp.maximum(b_last + m_prev, jnp.max(g_row, axis=1, keepdims=True))
            a = jnp.exp(b_last + m_prev - m_new)
            w_row = jnp.exp(g_row - m_new)
            c_ref[h] = a * c_ext + jnp.dot((kt * w_row).astype(BF16), v_ext, preferred_element_type=F32)
            m_ref[h] = jnp.broadcast_to(m_new, m_ref.shape[1:])

    @pl.when(step < nc_ctx)
    def _():
        chunk(*ins[:4], outs[0])

    @pl.when(step >= nc_ctx)
    def _():
        chunk(*ins[4:], outs[1])


def mlstm_bidir(qk_c, proj_c, small_c, qk_l, proj_l, small_l, gate_bias):
    _, bsz, lc, w = qk_c.shape
    ll = qk_l.shape[2]
    t, nh = ML_CHUNK, ML_HEADS
    nc_c, nc_l = lc // t, ll // t
    bias = jnp.zeros((1, N_SMALL), F32).at[0, S_GATES:S_GATES + ML_GATES].set(gate_bias)

    def idx(dd, step_off, nc):
        def f(g):
            cc = jnp.clip(g - step_off, 0, nc - 1)
            return cc + dd * (nc - 1 - 2 * cc)
        return f

    vcol = P_MLV // w

    def seg_specs(ix):
        return [pl.BlockSpec((1, 1, t, w), lambda b, g: (0, b, ix(g), 0)),
                pl.BlockSpec((1, 1, t, w), lambda b, g: (1, b, ix(g), 0)),
                pl.BlockSpec((1, t, w), lambda b, g: (b, ix(g), vcol)),
                pl.BlockSpec((1, t, N_SMALL), lambda b, g: (b, ix(g), 0))]

    def out_spec(ix):
        return pl.BlockSpec((1, t, w), lambda b, g: (b, ix(g), 0))

    h_c, h_l = [], []
    for dd in range(2):
        ic, il = idx(dd, 0, nc_c), idx(dd, nc_c, nc_l)
        hc, hl = pl.pallas_call(
            functools.partial(_mlstm_kernel, nc_ctx=nc_c, direction=dd),
            grid=(bsz, nc_c + nc_l),
            in_specs=seg_specs(ic) + seg_specs(il) + [pl.BlockSpec((1, N_SMALL), lambda b, g: (0, 0))],
            out_specs=[out_spec(ic), out_spec(il)],
            out_shape=[jax.ShapeDtypeStruct((bsz, lc, w), F32), jax.ShapeDtypeStruct((bsz, ll, w), F32)],
            scratch_shapes=[pltpu.VMEM((nh, ML_HEAD_DIM, 2 * ML_HEAD_DIM), F32), pltpu.VMEM((nh, 8, 128), F32)],
            compiler_params=_cparams(("parallel", "arbitrary")),
            name="mlstm_fwd" if dd == 0 else "mlstm_bwd",
        )(qk_c, qk_c, proj_c, small_c, qk_l, qk_l, proj_l, small_l, bias)
        h_c.append(hc)
        h_l.append(hl)
    return tuple(h_c), tuple(h_l)


def _rms(x):
    return x * lax.rsqrt(jnp.mean(x * x, axis=-1, keepdims=True) + EPS)


def _swap_rope_halves(x):
    w = x.shape[1]
    lane = lax.broadcasted_iota(jnp.int32, x.shape, 1) % MLA_ROPE
    return jnp.where(lane < MLA_ROPE // 2, pltpu.roll(x, w - MLA_ROPE // 2, axis=1),
                     pltpu.roll(x, MLA_ROPE // 2, axis=1))


def _mla_prep_kernel(pq_ref, pkv_ref, sm_ref, gqa_ref, gkva_ref, wq_ref, wkv_ref, gqn_ref, gqr_ref, gkn_ref,
                     gkr_ref, en_ref, er_ref, *rest, use_rope):
    if use_rope:
        ccq_ref, ssq_ref, cck_ref, ssk_ref, qt_ref, k_ref, vt_ref = rest
    else:
        qt_ref, k_ref, vt_ref = rest
    nh, dn, dr = MLA_HEADS, MLA_NOPE, MLA_ROPE
    qa = (_rms(pq_ref[0].astype(F32)) * gqa_ref[...]).astype(BF16)
    kva = (_rms(pkv_ref[0].astype(F32)) * gkva_ref[...]).astype(BF16)
    q_raw = jnp.dot(qa, wq_ref[...], preferred_element_type=F32)
    kv_raw = jnp.dot(kva, wkv_ref[...], preferred_element_type=F32)
    inv_d = 1.0 / MLA_QK

    sel = functools.partial(jnp.dot, preferred_element_type=F32)
    qn, qr = q_raw[:, :nh * dn], q_raw[:, nh * dn:]
    ss = sel((qn * qn).astype(BF16), en_ref[...]) + sel((qr * qr).astype(BF16), er_ref[...])
    rs_rows = lax.rsqrt(ss * inv_d + EPS).T
    qn = qn * gqn_ref[...]
    qr = qr * gqr_ref[...]
    if use_rope:
        qr = qr * ccq_ref[...] + _swap_rope_halves(qr) * ssq_ref[...]
    qrt = qr.T
    tl = qn.shape[0]
    for h in range(nh):
        rs_h = rs_rows[h:h + 1, :]
        qt_ref[0, h, 0:dn, :] = (qn[:, h * dn:(h + 1) * dn].T * rs_h).astype(BF16)
        qt_ref[0, h, dn:dn + dr, :] = (qrt[h * dr:(h + 1) * dr] * rs_h).astype(BF16)
        qt_ref[0, h, dn + dr:, :] = jnp.zeros((QK_PAD - dn - dr, tl), BF16)

    kn, vv = kv_raw[:, :nh * dn], kv_raw[:, nh * dn:]
    lane = lax.broadcasted_iota(jnp.int32, sm_ref.shape[1:], 1)
    kr = jnp.where(lane < dr, sm_ref[0], 0.0)
    ssk = sel((kn * kn).astype(BF16), en_ref[...]) + jnp.sum(kr * kr, axis=-1, keepdims=True)
    rsk = lax.rsqrt(ssk * inv_d + EPS)
    kn = kn * gkn_ref[...]
    kr = kr * gkr_ref[...]
    if use_rope:
        kr = kr * cck_ref[...] + _swap_rope_halves(kr) * ssk_ref[...]
    for h in range(nh):
        rs_h = jnp.broadcast_to(rsk[:, h:h + 1], (tl, dn))
        k_ref[0, h, :, 0:dn] = (kn[:, h * dn:(h + 1) * dn] * rs_h).astype(BF16)
        k_ref[0, h, :, dn:] = (kr * rs_h).astype(BF16)
        vt_ref[0, h] = vv[:, h * MLA_V:(h + 1) * MLA_V].T.astype(BF16)


@functools.lru_cache(maxsize=None)
def _head_selectors():
    en = np.kron(np.eye(MLA_HEADS), np.ones((MLA_NOPE, 1)))
    er = np.kron(np.eye(MLA_HEADS), np.ones((MLA_ROPE, 1)))
    pad = lambda m: np.pad(m, ((0, 0), (0, 128 - m.shape[1]))).astype(np.float32)
    return pad(en), pad(er)


@functools.lru_cache(maxsize=None)
def _rope_tables(n_tokens):
    f32 = np.float32
    rows = n_tokens // GRID_W
    row = np.repeat(np.arange(rows, dtype=f32), GRID_W)
    col = np.tile(np.arange(GRID_W, dtype=f32), rows)
    freqs = f32(ROPE_BASE) ** (-np.arange(ROPE_PAIRS_PER_AXIS, dtype=f32) / f32(ROPE_PAIRS_PER_AXIS))
    ang = np.concatenate([row[:, None] * freqs, col[:, None] * freqs], axis=-1).astype(f32)
    cos, sin = np.cos(ang).astype(f32), np.sin(ang).astype(f32)
    cc = np.concatenate([cos, cos], axis=-1)
    ss = np.concatenate([-sin, sin], axis=-1)
    zero = np.zeros((n_tokens, 128 - MLA_ROPE), f32)
    return (np.tile(cc, (1, MLA_HEADS)), np.tile(ss, (1, MLA_HEADS)),
            np.concatenate([cc, zero], axis=-1), np.concatenate([ss, zero], axis=-1))


def mla_prep(proj, small, p, rope_tables, *, tl):
    b, l, _ = proj.shape
    nh = MLA_HEADS
    en, er = _head_selectors()
    en_b, er_b = jnp.asarray(en, BF16), jnp.asarray(er, BF16)
    const = lambda shape: pl.BlockSpec(shape, lambda bi, i: (0,) * len(shape))
    nq = nh * MLA_QK
    nkv = nh * (MLA_NOPE + MLA_V)
    in_specs = [pl.BlockSpec((1, tl, MLA_Q_LORA), lambda bi, i: (bi, i, P_CQ // MLA_Q_LORA)),
                pl.BlockSpec((1, tl, MLA_KV_LORA), lambda bi, i: (bi, i, P_CKV // MLA_KV_LORA)),
                pl.BlockSpec((1, tl, N_SMALL), lambda bi, i: (bi, i, 0)),
                const((1, MLA_Q_LORA)), const((1, MLA_KV_LORA)), const((MLA_Q_LORA, nq)), const((MLA_KV_LORA, nkv)),
                const((1, nh * MLA_NOPE)), const((1, nh * MLA_ROPE)), const((1, nh * MLA_NOPE)), const((1, 128)),
                const(en.shape), const(er.shape)]
    operands = [proj, proj, small, p['gqa'], p['gkva'], p['wq'], p['wkv'], p['gqn'], p['gqr'], p['gkn'], p['gkr'],
                en_b, er_b]
    use_rope = rope_tables is not None
    if use_rope:
        in_specs += [pl.BlockSpec((tl, nh * MLA_ROPE), lambda bi, i: (i, 0)),
                     pl.BlockSpec((tl, nh * MLA_ROPE), lambda bi, i: (i, 0)),
                     pl.BlockSpec((tl, 128), lambda bi, i: (i, 0)),
                     pl.BlockSpec((tl, 128), lambda bi, i: (i, 0))]
        operands += list(rope_tables)
    return pl.pallas_call(
        functools.partial(_mla_prep_kernel, use_rope=use_rope),
        grid=(b, l // tl),
        in_specs=in_specs,
        out_specs=[pl.BlockSpec((1, nh, QK_PAD, tl), lambda bi, i: (bi, 0, 0, i)),
                   pl.BlockSpec((1, nh, tl, QK_PAD), lambda bi, i: (bi, 0, i, 0)),
                   pl.BlockSpec((1, nh, MLA_V, tl), lambda bi, i: (bi, 0, 0, i))],
        out_shape=[jax.ShapeDtypeStruct((b, nh, QK_PAD, l), BF16),
                   jax.ShapeDtypeStruct((b, nh, l, QK_PAD), BF16),
                   jax.ShapeDtypeStruct((b, nh, MLA_V, l), BF16)],
        compiler_params=_cparams(("parallel", "parallel")),
        name="mla_prep",
    )(*operands)


def _combine_out_kernel(a_ref, y_ref, hf_ref, hb_ref, o_ref, g_ref, w_ref, r_ref, gt_ref, out_ref, cat_ref):
    @pl.when(pl.program_id(2) == 0)
    def _():
        g = g_ref[...]
        cat_ref[:, :MLA_WIDTH] = (_rms(a_ref[0]) * g[:, :MLA_WIDTH]).astype(BF16)
        hy0 = MLA_WIDTH
        cat_ref[:, hy0:hy0 + HY_WIDTH] = (_rms(y_ref[0]) * g[:, hy0:hy0 + HY_WIDTH]).astype(BF16)
        ml0 = MLA_WIDTH + HY_WIDTH
        hh = hf_ref[0] + hb_ref[0]
        og = 0.5 * jnp.tanh(0.5 * o_ref[0].astype(F32)) + 0.5
        for hd in range(ML_HEADS):
            sl = slice(hd * ML_HEAD_DIM, (hd + 1) * ML_HEAD_DIM)
            gl = g[:, ml0 + hd * ML_HEAD_DIM:ml0 + (hd + 1) * ML_HEAD_DIM]
            cat_ref[:, ml0 + hd * ML_HEAD_DIM:ml0 + (hd + 1) * ML_HEAD_DIM] = (
                _rms(hh[:, sl]) * gl * og[:, sl]).astype(BF16)

    y = jnp.dot(cat_ref[...], w_ref[...], preferred_element_type=F32)
    out_ref[0] = r_ref[0] + gt_ref[0] * y


def combine_out_proj(a, y, h2, proj, g_mix, w_out, res, gate, *, tm, tn, name="out_proj"):
    b, s, d = res.shape
    return pl.pallas_call(
        _combine_out_kernel,
        grid=(b, s // tm, d // tn),
        in_specs=[pl.BlockSpec((1, tm, MLA_WIDTH), lambda bi, i, j: (bi, i, 0)),
                  pl.BlockSpec((1, tm, HY_WIDTH), lambda bi, i, j: (bi, i, 0)),
                  pl.BlockSpec((1, tm, ML_WIDTH), lambda bi, i, j: (bi, i, 0)),
                  pl.BlockSpec((1, tm, ML_WIDTH), lambda bi, i, j: (bi, i, 0)),
                  pl.BlockSpec((1, tm, ML_WIDTH), lambda bi, i, j: (bi, i, P_MLO // ML_WIDTH)),
                  pl.BlockSpec((1, MIX_WIDTH), lambda bi, i, j: (0, 0)),
                  pl.BlockSpec((MIX_WIDTH, tn), lambda bi, i, j: (0, j)),
                  pl.BlockSpec((1, tm, tn), lambda bi, i, j: (bi, i, j)),
                  pl.BlockSpec((1, 1, tn), lambda bi, i, j: (bi, 0, j))],
        out_specs=pl.BlockSpec((1, tm, tn), lambda bi, i, j: (bi, i, j)),
        out_shape=jax.ShapeDtypeStruct((b, s, d), F32),
        scratch_shapes=[pltpu.VMEM((tm, MIX_WIDTH), BF16)],
        compiler_params=_cparams(("parallel", "parallel", "arbitrary")),
        name=name,
    )(a, y, h2[0], h2[1], proj, g_mix.reshape(1, MIX_WIDTH), w_out, res, gate)


@functools.lru_cache(maxsize=None)
def _direct_dft_tables(l):
    n = 2 * l
    k = np.arange(n)[:, None]
    t = np.arange(l)[None, :]
    ang = 2.0 * np.pi * ((k * t) % n) / n
    c, s = np.cos(ang), np.sin(ang)
    gf = np.block([[c, s], [-s, c]])
    gi = gf.T.copy()
    t_full = np.arange(n)[None, :]
    ang = 2.0 * np.pi * ((k * t_full) % n) / n
    g_real = np.concatenate([np.cos(ang), -np.sin(ang)], axis=0)
    return gf.astype(np.float32), gi.astype(np.float32), g_real.astype(np.float32)


def _direct_conv_kernel(gf_ref, gi_ref, z_ref, gate_ref, h_ref, skip_ref, o_ref, *, scale):
    z = z_ref[0]
    x = jnp.dot(gf_ref[...].astype(BF16), z.astype(BF16), preferred_element_type=F32)
    n = x.shape[0] // 2
    xr, xi = x[:n], x[n:]
    hr, hi = h_ref[:n] * scale, h_ref[n:] * scale
    prod = jnp.concatenate([xr * hr - xi * hi, xr * hi + xi * hr], axis=0).astype(BF16)
    conv = jnp.dot(gi_ref[...].astype(BF16), prod, preferred_element_type=F32)
    o_ref[0] = gate_ref[0] * (conv + skip_ref[...] * z)


def hyena_short(proj, p):
    bsz, l, _ = proj.shape
    c = HY_WIDTH
    npair = bsz // 2
    n = 2 * l
    gf, gi, g_real = _direct_dft_tables(l)
    buf = hyena_filter_buffer(l, p, tl=l)
    hspec = dft_first(jnp.asarray(g_real), buf.reshape(1, n, HY_ORDER * c), n_out=1, tn=HY_ORDER * c,
                      precise=True)[0]
    vx = short_conv_groups(proj, P_HY, p['hy_conv_w'], p['hy_conv_b'], group=c, tc=CONV_TC, silu=False,
                           name="hyena_short_conv_ctx")
    vx = vx.reshape(3 * npair, 2 * l, c)
    z = vx

    def rows_at(off):
        return lambda q: (off + q, 0, 0)

    def cols_at(col):
        return lambda q: (0, col)

    for order in range(HY_ORDER):
        z = pl.pallas_call(
            functools.partial(_direct_conv_kernel, scale=1.0 / n),
            grid=(npair,),
            in_specs=[pl.BlockSpec(gf.shape, lambda q: (0, 0)),
                      pl.BlockSpec(gi.shape, lambda q: (0, 0)),
                      pl.BlockSpec((1, 2 * l, c), rows_at(0)),
                      pl.BlockSpec((1, 2 * l, c), rows_at((order + 1) * npair)),
                      pl.BlockSpec((2 * n, c), cols_at(order)),
                      pl.BlockSpec((1, c), lambda q: (0, 0))],
            out_specs=pl.BlockSpec((1, 2 * l, c), lambda q: (q, 0, 0)),
            out_shape=jax.ShapeDtypeStruct((npair, 2 * l, c), F32),
            compiler_params=_cparams(("parallel",)),
            name="hyena_direct_conv",
        )(jnp.asarray(gf), jnp.asarray(gi), z, vx, hspec, p['hy_skip'][order].reshape(1, c))
    return z.reshape(bsz, l, c)


def _split_cols(w, sizes):
    out, start = [], 0
    for s in sizes:
        out.append(w[:, start:start + s])
        start += s
    return out


def _prep_layer_weights(l, w_in, mla_qa_norm, mla_kva_norm, mla_w_uq, mla_w_ukv, mla_q_norm, mla_k_norm):
    cq, ckv, kr, hy, mqk, mv, mo, gt = _split_cols(w_in[l], IN_SIZES)
    w_main = jnp.concatenate([cq, ckv, hy, mqk, mv, mo], axis=1).astype(BF16)
    pad = jnp.zeros((D_MODEL, N_SMALL - MLA_ROPE - ML_GATES), F32)
    w_small = jnp.concatenate([kr, gt, pad], axis=1).astype(BF16)
    nh = MLA_HEADS
    wq = mla_w_uq[l].reshape(MLA_Q_LORA, nh, MLA_QK)
    wq = jnp.concatenate([wq[:, :, :MLA_NOPE].reshape(MLA_Q_LORA, -1), wq[:, :, MLA_NOPE:].reshape(MLA_Q_LORA, -1)],
                         axis=1).astype(BF16)
    wkv = mla_w_ukv[l].reshape(MLA_KV_LORA, nh, MLA_NOPE + MLA_V)
    wkv = jnp.concatenate([wkv[:, :, :MLA_NOPE].reshape(MLA_KV_LORA, -1),
                           wkv[:, :, MLA_NOPE:].reshape(MLA_KV_LORA, -1)], axis=1).astype(BF16)
    q_scale = (MLA_QK ** -0.5) * math.log2(math.e)
    gq, gk = mla_q_norm[l] * q_scale, mla_k_norm[l]
    mla = {'gqa': mla_qa_norm[l].reshape(1, -1), 'gkva': mla_kva_norm[l].reshape(1, -1), 'wq': wq, 'wkv': wkv,
           'gqn': jnp.tile(gq[:MLA_NOPE], nh).reshape(1, -1), 'gqr': jnp.tile(gq[MLA_NOPE:], nh).reshape(1, -1),
           'gkn': jnp.tile(gk[:MLA_NOPE], nh).reshape(1, -1),
           'gkr': jnp.concatenate([gk[MLA_NOPE:], jnp.zeros((128 - MLA_ROPE,), F32)]).reshape(1, -1)}
    return w_main, w_small, mla


def kernel(x, c, ctx, c_ctx, ada_w, ada_b, norm1_g, norm2_g, w_in, mla_qa_norm, mla_kva_norm, mla_w_uq,
           mla_w_ukv, mla_q_norm, mla_k_norm, hy_conv_w, hy_conv_b, hy_w1, hy_b1, hy_w2, hy_b2, hy_w3,
           hy_decay, hy_skip, ml_conv_w, ml_conv_b, ml_gate_b, mix_norm_g, w_out, ffn_w1, ffn_w2):
    B, S, D = x.shape
    LC = ctx.shape[1]
    rope_tables = _rope_tables(S)

    silu_rows = jnp.concatenate([jax.nn.silu(c), jax.nn.silu(c_ctx)[None], jnp.zeros((8 - B - 1, D), F32)], axis=0)
    mods = ada_modulation(silu_rows.astype(BF16), ada_w, ada_b)

    for l in range(DEPTH):
        need_ctx = l < DEPTH - 1
        p = {'hy_conv_w': hy_conv_w[l], 'hy_conv_b': hy_conv_b[l], 'hy_w1': hy_w1[l], 'hy_b1': hy_b1[l],
             'hy_w2': hy_w2[l], 'hy_b2': hy_b2[l], 'hy_w3': hy_w3[l], 'hy_decay': hy_decay[l],
             'hy_skip': hy_skip[l], 'ml_conv_w': ml_conv_w[l], 'ml_conv_b': ml_conv_b[l]}
        w_main, w_small, mla = _prep_layer_weights(l, w_in, mla_qa_norm, mla_kva_norm, mla_w_uq, mla_w_ukv,
                                                   mla_q_norm, mla_k_norm)
        w_out_l = w_out[l].astype(BF16)
        w1_l = ffn_w1[l].astype(BF16)
        w2_l = ffn_w2[l].astype(BF16)

        mod_l = [m[:, None, :] for m in jnp.split(mods[l, :B], 6, axis=-1)]
        mod_c = [m[:, None, :] for m in jnp.split(mods[l, B:B + 1], 6, axis=-1)]
        flat = lambda t: t.reshape(1, B * LC, t.shape[-1])

        proj_l, small_l = in_proj(x, norm1_g[l], mod_l[1], mod_l[0], w_main, w_small, tm=1024, tn=IN_PROJ_TN)
        proj_c, small_c = in_proj(flat(ctx), norm1_g[l], mod_c[1], mod_c[0], w_main, w_small, tm=B * LC,
                                  tn=IN_PROJ_TN, name="in_proj_ctx")
        proj_c, small_c = proj_c.reshape(B, LC, N_MAIN), small_c.reshape(B, LC, N_SMALL)

        qt_l, k_l, vt_l = mla_prep(proj_l, small_l, mla, rope_tables, tl=512)
        qt_c, k_c, vt_c = mla_prep(proj_c, small_c, mla, None, tl=LC)
        a_l = attention(qt_l, [(k_c, vt_c), (k_l, vt_l)], tq=ATT_TQ)
        y_l = hyena_long(proj_l, p)

        def mlstm_qk(proj):
            return short_conv_groups(proj, P_MLQK, p['ml_conv_w'], p['ml_conv_b'], group=ML_WIDTH, tc=CONV_TC,
                                     silu=True, name="mlstm_short_conv")

        h_c, h_l = mlstm_bidir(mlstm_qk(proj_c), proj_c, small_c, mlstm_qk(proj_l), proj_l, small_l, ml_gate_b[l])
        x = combine_out_proj(a_l, y_l, h_l, proj_l, mix_norm_g[l], w_out_l, x, mod_l[2], tm=1024, tn=MM_TN)
        hid = norm_swiglu(x, norm2_g[l], mod_l[4], mod_l[3], w1_l, tm=1024, tn=MM_TN, name="ffn_up")
        x = mm_residual(hid, w2_l, x, mod_l[5], tm=1024, tn=MM_TN, name="ffn_down")

        if need_ctx:
            a_c = attention(qt_c, [(k_c, vt_c)], tq=LC)
            y_c = hyena_short(proj_c, p)
            ctx = combine_out_proj(flat(a_c), flat(y_c), (flat(h_c[0]), flat(h_c[1])), flat(proj_c), mix_norm_g[l],
                                   w_out_l, flat(ctx), mod_c[2], tm=B * LC, tn=MM_TN, name="out_proj_ctx")
            hid = norm_swiglu(ctx, norm2_g[l], mod_c[4], mod_c[3], w1_l, tm=B * LC, tn=MM_TN, name="ffn_up_ctx")
            ctx = mm_residual(hid, w2_l, ctx, mod_c[5], tm=B * LC, tn=MM_TN, name="ffn_down_ctx")
            ctx = ctx.reshape(B, LC, D)
    return x
```

```python
import functools
import math

import jax
import jax.numpy as jnp
import numpy as np
from jax import lax
from jax.experimental import pallas as pl
from jax.experimental.pallas import tpu as pltpu

D_MODEL = 2048
DEPTH = 2
GRID_W = 64
EPS = 1e-6

MLA_HEADS = 8
MLA_NOPE = 128
MLA_ROPE = 64
MLA_QK = MLA_NOPE + MLA_ROPE
MLA_V = 128
MLA_Q_LORA = 512
MLA_KV_LORA = 512
MLA_WIDTH = MLA_HEADS * MLA_V
ROPE_BASE = 10000.0
ROPE_PAIRS_PER_AXIS = MLA_ROPE // 4
QK_PAD = 256

HY_WIDTH = 512
HY_ORDER = 2
HY_BANDS = 16
HY_SHIFT = 0.05
SHORT_CONV = 3
CONV_TC = 256

ML_HEADS = 4
ML_HEAD_DIM = 128
ML_WIDTH = ML_HEADS * ML_HEAD_DIM
ML_CHUNK = 256
ML_CHUNK_LATENT = 512
ML_GATES = 4 * ML_HEADS

MIX_WIDTH = MLA_WIDTH + HY_WIDTH + ML_WIDTH
FFN_HIDDEN = ((8 * D_MODEL // 3 + 255) // 256) * 256

IN_SIZES = (MLA_Q_LORA, MLA_KV_LORA, MLA_ROPE, 3 * HY_WIDTH, 2 * ML_WIDTH, ML_WIDTH, ML_WIDTH, ML_GATES)
N_IN = sum(IN_SIZES)

P_CQ, P_CKV, P_HY, P_MLQK, P_MLV, P_MLO = 0, 512, 1024, 2560, 3584, 4096
N_MAIN = 4608
N_SMALL = 128
S_GATES = MLA_ROPE

VMEM_LIMIT_BYTES = 56 * 1024 * 1024

BF16 = jnp.bfloat16
F32 = jnp.float32


def _cparams(sem):
    return pltpu.CompilerParams(dimension_semantics=sem, vmem_limit_bytes=VMEM_LIMIT_BYTES)


def _ada_kernel(s_ref, w_ref, b_ref, o_ref):
    w = w_ref[0].astype(BF16)
    o_ref[0] = jnp.dot(s_ref[...], w, preferred_element_type=F32) + b_ref[0]


def ada_modulation(silu_rows, ada_w, ada_b, tn=1024):
    depth, d, n = ada_w.shape
    return pl.pallas_call(
        _ada_kernel,
        grid=(depth, n // tn),
        in_specs=[pl.BlockSpec((8, d), lambda l, j: (0, 0)),
                  pl.BlockSpec((1, d, tn), lambda l, j: (l, 0, j)),
                  pl.BlockSpec((1, 1, tn), lambda l, j: (l, 0, j))],
        out_specs=pl.BlockSpec((1, 8, tn), lambda l, j: (l, 0, j)),
        out_shape=jax.ShapeDtypeStruct((depth, 8, n), F32),
        compiler_params=_cparams(("parallel", "parallel")),
        name="ada_modulation",
    )(silu_rows, ada_w, ada_b.reshape(depth, 1, n))


MM_TN = 512
IN_PROJ_TN = 1536


def _normed(x, g, sc, sh):
    gain = g * (1.0 + sc)
    return x * lax.rsqrt(jnp.mean(x * x, axis=-1, keepdims=True) + EPS) * gain + sh


def _in_proj_kernel(x_ref, g_ref, sc_ref, sh_ref, w_ref, ws_ref, o_ref, os_ref, xn_ref):
    @pl.when(pl.program_id(2) == 0)
    def _():
        xn = _normed(x_ref[0], g_ref[...], sc_ref[0], sh_ref[0]).astype(BF16)
        xn_ref[...] = xn
        os_ref[0] = jnp.dot(xn, ws_ref[...], preferred_element_type=F32)

    o_ref[0] = jnp.dot(xn_ref[...], w_ref[...], preferred_element_type=F32).astype(o_ref.dtype)


def in_proj(x, g, sc, sh, w_main, w_small, *, tm, tn, name="in_proj"):
    b, s, k = x.shape
    n = w_main.shape[1]
    ns = w_small.shape[1]
    return pl.pallas_call(
        _in_proj_kernel,
        grid=(b, s // tm, n // tn),
        in_specs=[pl.BlockSpec((1, tm, k), lambda bi, i, j: (bi, i, 0)),
                  pl.BlockSpec((1, k), lambda bi, i, j: (0, 0)),
                  pl.BlockSpec((1, 1, k), lambda bi, i, j: (bi, 0, 0)),
                  pl.BlockSpec((1, 1, k), lambda bi, i, j: (bi, 0, 0)),
                  pl.BlockSpec((k, tn), lambda bi, i, j: (0, j)),
                  pl.BlockSpec((k, ns), lambda bi, i, j: (0, 0))],
        out_specs=[pl.BlockSpec((1, tm, tn), lambda bi, i, j: (bi, i, j)),
                   pl.BlockSpec((1, tm, ns), lambda bi, i, j: (bi, i, 0))],
        out_shape=[jax.ShapeDtypeStruct((b, s, n), BF16), jax.ShapeDtypeStruct((b, s, ns), F32)],
        scratch_shapes=[pltpu.VMEM((tm, k), BF16)],
        compiler_params=_cparams(("parallel", "parallel", "arbitrary")),
        name=name,
    )(x, g.reshape(1, k), sc, sh, w_main, w_small)


def _norm_swiglu_kernel(x_ref, g_ref, sc_ref, sh_ref, wg_ref, wu_ref, o_ref, xn_ref):
    @pl.when(pl.program_id(2) == 0)
    def _():
        xn_ref[...] = _normed(x_ref[0], g_ref[...], sc_ref[0], sh_ref[0]).astype(BF16)

    xn = xn_ref[...]
    gate = jnp.dot(xn, wg_ref[...], preferred_element_type=F32)
    up = jnp.dot(xn, wu_ref[...], preferred_element_type=F32)
    o_ref[0] = (gate * (0.5 * jnp.tanh(0.5 * gate) + 0.5) * up).astype(o_ref.dtype)


def norm_swiglu(x, g, sc, sh, w1, *, tm, tn, name="norm_swiglu"):
    b, s, d = x.shape
    h = w1.shape[1] // 2
    nj = h // tn
    return pl.pallas_call(
        _norm_swiglu_kernel,
        grid=(b, s // tm, nj),
        in_specs=[pl.BlockSpec((1, tm, d), lambda bi, i, j: (bi, i, 0)),
                  pl.BlockSpec((1, d), lambda bi, i, j: (0, 0)),
                  pl.BlockSpec((1, 1, d), lambda bi, i, j: (bi, 0, 0)),
                  pl.BlockSpec((1, 1, d), lambda bi, i, j: (bi, 0, 0)),
                  pl.BlockSpec((d, tn), lambda bi, i, j: (0, j)),
                  pl.BlockSpec((d, tn), lambda bi, i, j: (0, j + nj))],
        out_specs=pl.BlockSpec((1, tm, tn), lambda bi, i, j: (bi, i, j)),
        out_shape=jax.ShapeDtypeStruct((b, s, h), BF16),
        scratch_shapes=[pltpu.VMEM((tm, d), BF16)],
        compiler_params=_cparams(("parallel", "parallel", "arbitrary")),
        name=name,
    )(x, g.reshape(1, d), sc, sh, w1, w1)


def _mm_res_kernel(a_ref, w_ref, r_ref, gt_ref, o_ref):
    y = jnp.dot(a_ref[0], w_ref[...], preferred_element_type=F32)
    o_ref[0] = r_ref[0] + gt_ref[0] * y


def mm_residual(a, w, res, gate, *, tm, tn, name="mm_residual"):
    b, s, k = a.shape
    n = w.shape[1]
    return pl.pallas_call(
        _mm_res_kernel,
        grid=(b, s // tm, n // tn),
        in_specs=[pl.BlockSpec((1, tm, k), lambda bi, i, j: (bi, i, 0)),
                  pl.BlockSpec((k, tn), lambda bi, i, j: (0, j)),
                  pl.BlockSpec((1, tm, tn), lambda bi, i, j: (bi, i, j)),
                  pl.BlockSpec((1, 1, tn), lambda bi, i, j: (bi, 0, j))],
        out_specs=pl.BlockSpec((1, tm, tn), lambda bi, i, j: (bi, i, j)),
        out_shape=jax.ShapeDtypeStruct((b, s, n), F32),
        compiler_params=_cparams(("parallel", "parallel", "parallel")),
        name=name,
    )(a, w, res, gate)


ATT_KEY_CHUNK = 512
ATT_TQ = 1024


def _attn_kernel(qt_ref, *refs, n_seg):
    o_ref, s_ref = refs[2 * n_seg], refs[2 * n_seg + 1]
    qt = qt_ref[0, 0]
    tq = qt.shape[1]
    chunks, off = [], 0
    for sgi in range(n_seg):
        k_ref, vt_ref = refs[2 * sgi], refs[2 * sgi + 1]
        lk = k_ref.shape[2]
        ch = min(ATT_KEY_CHUNK, lk)
        for c in range(lk // ch):
            chunks.append((k_ref, vt_ref, c * ch, ch, off))
            off += ch

    m8 = jnp.full((8, tq), -jnp.inf, F32)
    for k_ref, _, start, ch, off in chunks:
        s = jnp.dot(k_ref[0, 0, start:start + ch, :], qt, preferred_element_type=F32)
        s_ref[off:off + ch, :] = s
        m8 = jnp.maximum(m8, jnp.max(s.reshape(ch // 8, 8, tq), axis=0))
    m = jnp.max(m8, axis=0, keepdims=True)
    l8 = jnp.zeros((8, tq), F32)
    acc = jnp.zeros((vt_ref.shape[2], tq), F32)
    for _, vt_ref, start, ch, off in chunks:
        p = jnp.exp2(s_ref[off:off + ch, :] - m)
        l8 = l8 + jnp.sum(p.reshape(ch // 8, 8, tq), axis=0)
        acc = acc + jnp.dot(vt_ref[0, 0, :, start:start + ch], p.astype(BF16), preferred_element_type=F32)
    l = jnp.sum(l8, axis=0, keepdims=True)
    o_ref[0] = (acc / l).T


def attention(qt, segments, *, tq=256):
    b, h, dk, lq = qt.shape
    dv = segments[0][1].shape[2]
    in_specs = [pl.BlockSpec((1, 1, dk, tq), lambda bi, hi, i: (bi, hi, 0, i))]
    operands = [qt]
    for k, vt in segments:
        lk = k.shape[2]
        in_specs += [pl.BlockSpec((1, 1, lk, dk), lambda bi, hi, i: (bi, hi, 0, 0)),
                     pl.BlockSpec((1, 1, dv, lk), lambda bi, hi, i: (bi, hi, 0, 0))]
        operands += [k, vt]
    return pl.pallas_call(
        functools.partial(_attn_kernel, n_seg=len(segments)),
        grid=(b, h, lq // tq),
        in_specs=in_specs,
        out_specs=pl.BlockSpec((1, tq, dv), lambda bi, hi, i: (bi, i, hi)),
        out_shape=jax.ShapeDtypeStruct((b, lq, h * dv), F32),
        scratch_shapes=[pltpu.VMEM((sum(k.shape[2] for k, _ in segments), tq), F32)],
        compiler_params=_cparams(("parallel", "parallel", "parallel")),
        name="attention",
    )(*operands)


HY_NA = 64
HY_NB = 128
HY_N = HY_NA * HY_NB
HY_KG = 8


@functools.lru_cache(maxsize=None)
def _dft_tables():
    na, nb, n = HY_NA, HY_NB, HY_N
    a = np.arange(na // 2)[None, :]
    ka = np.arange(na)[:, None]
    ang = 2.0 * np.pi * ((a * ka) % na) / na
    c, s = np.cos(ang), np.sin(ang)
    g1 = np.block([[c, s], [-s, c]])
    g1_inv = g1.T.copy()
    a_full = np.arange(na)[None, :]
    ang = 2.0 * np.pi * ((a_full * ka) % na) / na
    g1_real = np.concatenate([np.cos(ang), -np.sin(ang)], axis=0)
    b = np.arange(nb)[None, None, :]
    k = np.arange(na)[:, None, None] + na * np.arange(nb)[None, :, None]
    ang = 2.0 * np.pi * ((b * k) % n) / n
    cr, si = np.cos(ang), np.sin(ang)
    g2 = np.concatenate([np.concatenate([cr, si], axis=2), np.concatenate([-si, cr], axis=2)], axis=1)
    g2_inv = np.transpose(g2, (0, 2, 1)).copy()
    return (g1.astype(np.float32), g1_inv.astype(np.float32), g1_real.astype(np.float32),
            g2.astype(np.float32), g2_inv.astype(np.float32))


def _dot_3pass(a, b):
    a_hi, b_hi = a.astype(BF16), b.astype(BF16)
    a_lo = (a - a_hi.astype(F32)).astype(BF16)
    b_lo = (b - b_hi.astype(F32)).astype(BF16)
    dot = functools.partial(jnp.dot, preferred_element_type=F32)
    return dot(a_hi, b_hi) + (dot(a_hi, b_lo) + dot(a_lo, b_hi))


def _short_conv_kernel(u_ref, w_ref, b_ref, o_ref, *, silu):
    u = u_ref[0].astype(F32)
    n = u.shape[0]
    rows = lax.broadcasted_iota(jnp.int32, u.shape, 0)
    prev = jnp.where(rows == 0, 0.0, pltpu.roll(u, 1, axis=0))
    nxt = jnp.where(rows == n - 1, 0.0, pltpu.roll(u, n - 1, axis=0))
    y = b_ref[...] + prev * w_ref[0:1, :] + u * w_ref[1:2, :] + nxt * w_ref[2:3, :]
    if silu:
        y = y * jax.nn.sigmoid(y)
    o_ref[0, 0] = y.astype(o_ref.dtype)


def short_conv_groups(proj, col0, w, b, *, group, tc, silu, name, out_dtype=F32):
    bsz, l, _ = proj.shape
    width = w.shape[1]
    per = group // tc
    return pl.pallas_call(
        functools.partial(_short_conv_kernel, silu=silu),
        grid=(bsz, width // tc),
        in_specs=[pl.BlockSpec((1, l, tc), lambda bi, j: (bi, 0, col0 // tc + j)),
                  pl.BlockSpec((SHORT_CONV, tc), lambda bi, j: (0, j)),
                  pl.BlockSpec((1, tc), lambda bi, j: (0, j))],
        out_specs=pl.BlockSpec((1, 1, l, tc), lambda bi, j: (j // per, bi, 0, j % per)),
        out_shape=jax.ShapeDtypeStruct((width // group, bsz, l, group), out_dtype),
        compiler_params=_cparams(("parallel", "parallel")),
        name=name,
    )(proj, w, b.reshape(1, width))


def _hy_filter_kernel(f_ref, w1_ref, b1_ref, w2_ref, b2_ref, w3_ref, dec_ref, o_ref, *, zero_row):
    hp = lax.Precision.HIGHEST
    f = f_ref[...]
    tl = f.shape[0]
    h = jnp.sin(jnp.dot(f, w1_ref[...], precision=hp, preferred_element_type=F32) + b1_ref[...])
    h = jnp.sin(jnp.dot(h, w2_ref[...], precision=hp, preferred_element_type=F32) + b2_ref[...])
    h = _dot_3pass(h, w3_ref[0])
    h = h * (jnp.exp(-f[:, 0:1] * dec_ref[0]) + HY_SHIFT)
    rows = lax.broadcasted_iota(jnp.int32, h.shape, 0) + pl.program_id(0) * tl
    o_ref[...] = jnp.where(rows == zero_row, 0.0, h)


@functools.lru_cache(maxsize=None)
def _filter_features(L):
    f32 = np.float32
    n = 2 * L
    r = np.arange(n, dtype=f32)
    t = (np.where(r < L, r, n - r) / f32(L)).astype(f32)
    bands = np.arange(1, HY_BANDS + 1, dtype=f32)
    ang = (f32(2.0 * math.pi) * t[:, None] * bands).astype(f32)
    emb = 1 + 2 * HY_BANDS
    return np.concatenate([t[:, None], np.cos(ang).astype(f32), np.sin(ang).astype(f32),
                           np.zeros((n, 128 - emb), f32)], axis=-1)


def hyena_filter_buffer(L, p, tl=512):
    n = 2 * L
    emb = 1 + 2 * HY_BANDS
    feats = jnp.asarray(_filter_features(L))
    hid = p['hy_w1'].shape[1]
    w1 = jnp.concatenate([p['hy_w1'], jnp.zeros((128 - emb, hid), F32)], axis=0)
    oc = HY_ORDER * HY_WIDTH
    w3 = p['hy_w3'].reshape(hid, HY_ORDER, 2, HY_WIDTH).transpose(2, 0, 1, 3).reshape(2, hid, oc)
    dec = p['hy_decay'].reshape(HY_ORDER, 2, HY_WIDTH).transpose(1, 0, 2).reshape(2, 1, oc)
    half = L // tl
    return pl.pallas_call(
        functools.partial(_hy_filter_kernel, zero_row=L),
        grid=(n // tl,),
        in_specs=[pl.BlockSpec((tl, 128), lambda i: (i, 0)),
                  pl.BlockSpec((128, hid), lambda i: (0, 0)),
                  pl.BlockSpec((1, hid), lambda i: (0, 0)),
                  pl.BlockSpec((hid, hid), lambda i: (0, 0)),
                  pl.BlockSpec((1, hid), lambda i: (0, 0)),
                  pl.BlockSpec((1, hid, oc), lambda i: (i // half, 0, 0)),
                  pl.BlockSpec((1, 1, oc), lambda i: (i // half, 0, 0))],
        out_specs=pl.BlockSpec((tl, oc), lambda i: (i, 0)),
        out_shape=jax.ShapeDtypeStruct((n, oc), F32),
        compiler_params=_cparams(("parallel",)),
        name="hyena_filter",
    )(feats, w1, p['hy_b1'].reshape(1, hid), p['hy_w2'], p['hy_b2'].reshape(1, hid), w3, dec)


def _dft_first_kernel(g_ref, x_ref, o_ref, *, precise):
    if precise:
        o_ref[0] = _dot_3pass(g_ref[...], x_ref[0])
    else:
        y = jnp.dot(g_ref[...].astype(BF16), x_ref[0].astype(BF16), preferred_element_type=F32)
        o_ref[0] = y.astype(o_ref.dtype)


def dft_first(g, x, *, n_out, offset=0, tn=8192, precise=False):
    _, r, lanes = x.shape
    m = g.shape[0]
    return pl.pallas_call(
        functools.partial(_dft_first_kernel, precise=precise),
        grid=(n_out, lanes // tn),
        in_specs=[pl.BlockSpec((m, r), lambda p, j: (0, 0)),
                  pl.BlockSpec((1, r, tn), lambda p, j: (offset + p, 0, j))],
        out_specs=pl.BlockSpec((1, m, tn), lambda p, j: (p, 0, j)),
        out_shape=jax.ShapeDtypeStruct((n_out, m, lanes), F32 if precise else BF16),
        compiler_params=_cparams(("parallel", "parallel")),
        name="dft_first",
    )(g, x)


def _dft_filter_mid_kernel(g_ref, y_ref, o_ref, *, scale):
    for i in range(g_ref.shape[0]):
        y = jnp.concatenate([y_ref[0, i], y_ref[1, i]], axis=0)
        o_ref[i] = scale * _dot_3pass(g_ref[i], y)


def dft_filter_mid(g2, y1):
    _, na, nb, oc = y1.shape
    kg = HY_KG // 2
    return pl.pallas_call(
        functools.partial(_dft_filter_mid_kernel, scale=1.0 / HY_N),
        grid=(na // kg,),
        in_specs=[pl.BlockSpec((kg, 2 * nb, 2 * nb), lambda g: (g, 0, 0)),
                  pl.BlockSpec((2, kg, nb, oc), lambda g: (0, g, 0, 0))],
        out_specs=pl.BlockSpec((kg, 2 * nb, oc), lambda g: (g, 0, 0)),
        out_shape=jax.ShapeDtypeStruct((na, 2 * nb, oc), F32),
        compiler_params=_cparams(("parallel",)),
        name="dft_filter_mid",
    )(g2, y1)


def _dft_mid_kernel(gf_ref, gi_ref, y_ref, h_ref, o_ref):
    nb = y_ref.shape[3]
    for i in range(gf_ref.shape[0]):
        y = jnp.concatenate([y_ref[0, 0, i], y_ref[0, 1, i]], axis=0)
        x = jnp.dot(gf_ref[i].astype(BF16), y, preferred_element_type=F32)
        xr, xi = x[:nb], x[nb:]
        hr, hi = h_ref[i, :nb], h_ref[i, nb:]
        prod = jnp.concatenate([xr * hr - xi * hi, xr * hi + xi * hr], axis=0).astype(BF16)
        u = jnp.dot(gi_ref[i].astype(BF16), prod, preferred_element_type=F32)
        o_ref[0, 0, i] = u[:nb].astype(o_ref.dtype)
        o_ref[0, 1, i] = u[nb:].astype(o_ref.dtype)


def dft_mid(g2, g2_inv, y1, h, order):
    npair, _, na, nb, c = y1.shape
    kg = HY_KG
    return pl.pallas_call(
        _dft_mid_kernel,
        grid=(na // kg, npair),
        in_specs=[pl.BlockSpec((kg, 2 * nb, 2 * nb), lambda g, p: (g, 0, 0)),
                  pl.BlockSpec((kg, 2 * nb, 2 * nb), lambda g, p: (g, 0, 0)),
                  pl.BlockSpec((1, 2, kg, nb, c), lambda g, p: (p, 0, g, 0, 0)),
                  pl.BlockSpec((kg, 2 * nb, c), lambda g, p: (g, 0, order))],
        out_specs=pl.BlockSpec((1, 2, kg, nb, c), lambda g, p: (p, 0, g, 0, 0)),
        out_shape=jax.ShapeDtypeStruct(y1.shape, BF16),
        compiler_params=_cparams(("parallel", "parallel")),
        name="dft_mid",
    )(g2, g2_inv, y1, h)


def _dft_last_kernel(g_ref, u_ref, z_ref, gate_ref, skip_ref, o_ref):
    conv = jnp.dot(g_ref[...].astype(BF16), u_ref[0], preferred_element_type=F32)
    y = gate_ref[0].astype(F32) * (conv + skip_ref[...] * z_ref[0].astype(F32))
    o_ref[0] = y.astype(o_ref.dtype)


def dft_last(g1_inv, u, z, z_off, gate, gate_off, skip_lanes, *, out_dtype, tn=8192):
    npair, m2, lanes = u.shape
    r = g1_inv.shape[0]
    return pl.pallas_call(
        _dft_last_kernel,
        grid=(npair, lanes // tn),
        in_specs=[pl.BlockSpec((r, m2), lambda p, j: (0, 0)),
                  pl.BlockSpec((1, m2, tn), lambda p, j: (p, 0, j)),
                  pl.BlockSpec((1, r, tn), lambda p, j: (z_off + p, 0, j)),
                  pl.BlockSpec((1, r, tn), lambda p, j: (gate_off + p, 0, j)),
                  pl.BlockSpec((1, tn), lambda p, j: (0, 0))],
        out_specs=pl.BlockSpec((1, r, tn), lambda p, j: (p, 0, j)),
        out_shape=jax.ShapeDtypeStruct((npair, r, lanes), out_dtype),
        compiler_params=_cparams(("parallel", "parallel")),
        name="dft_last",
    )(g1_inv, u, z, gate, skip_lanes)


def hyena_long(proj, p):
    bsz, l, _ = proj.shape
    assert 2 * l == HY_N and bsz % 2 == 0
    c = HY_WIDTH
    npair = bsz // 2
    lanes = HY_NB * c
    g1, g1_inv, g1_real, g2, g2_inv = _dft_tables()
    g1_bf, g1_inv_bf = jnp.asarray(g1), jnp.asarray(g1_inv)
    g2_bf, g2_inv_bf = jnp.asarray(g2), jnp.asarray(g2_inv)

    buf = hyena_filter_buffer(l, p)
    y1h = dft_first(jnp.asarray(g1_real), buf.reshape(1, HY_NA, HY_NB * HY_ORDER * c), n_out=1, precise=True)
    hspec = dft_filter_mid(jnp.asarray(g2), y1h.reshape(2, HY_NA, HY_NB, HY_ORDER * c))

    vx = short_conv_groups(proj, P_HY, p['hy_conv_w'], p['hy_conv_b'], group=c, tc=CONV_TC, silu=False,
                           name="hyena_short_conv", out_dtype=BF16)
    vx = vx.reshape(3 * npair, HY_NA, lanes)
    z, z_off = vx, 0
    for n in range(HY_ORDER):
        y1 = dft_first(g1_bf, z, n_out=npair, offset=z_off)
        u = dft_mid(g2_bf, g2_inv_bf, y1.reshape(npair, 2, HY_NA, HY_NB, c), hspec, n)
        skip_lanes = jnp.tile(p['hy_skip'][n], 8192 // c).reshape(1, 8192)
        z = dft_last(g1_inv_bf, u.reshape(npair, 2 * HY_NA, lanes), z, z_off, vx, (n + 1) * npair, skip_lanes,
                     out_dtype=BF16 if n + 1 < HY_ORDER else F32)
        z_off = 0
    return z.reshape(bsz, l, c)


def _log_sigmoid(x):
    return jnp.minimum(x, 0.0) - jnp.log1p(jnp.exp(-jnp.abs(x)))


def _mlstm_kernel(*refs, nc_ctx, direction):
    ins, bias_ref, outs, (c_ref, m_ref) = refs[:8], refs[8], refs[9:11], refs[11:13]
    d = ML_HEAD_DIM
    nh = ML_HEADS
    step = pl.program_id(1)

    @pl.when(step == 0)
    def _():
        c_ref[...] = jnp.zeros_like(c_ref)
        m_ref[...] = jnp.zeros_like(m_ref)

    i0 = S_GATES + direction * nh
    f0 = S_GATES + (2 + direction) * nh

    def chunk(q_ref, k_ref, v_ref, sm_ref, o_ref):
        t = sm_ref.shape[1]
        g = sm_ref[0] + bias_ref[...]
        gt = g.T
        lf_cols = _log_sigmoid(g)
        lf_rows = _log_sigmoid(gt)
        r = lax.broadcasted_iota(jnp.int32, (t, t), 0)
        cidx = lax.broadcasted_iota(jnp.int32, (t, t), 1)
        mask = cidx <= r if direction == 0 else cidx >= r
        mask_t = r <= cidx if direction == 0 else r >= cidx
        tri = mask.astype(F32)
        tri_t = mask_t.astype(F32)
        lane = lax.broadcasted_iota(jnp.int32, (t, d), 1)
        ones_col = jnp.where(lane == 0, 1.0, 0.0).astype(BF16)
        for h in range(nh):
            sl = slice(h * d, (h + 1) * d)
            lf_row = lf_rows[f0 + h:f0 + h + 1, :]
            bc = jnp.sum(tri * lf_row, axis=1, keepdims=True)
            br = jnp.sum(tri_t * lf_cols[:, f0 + h:f0 + h + 1], axis=0, keepdims=True)
            lir = gt[i0 + h:i0 + h + 1, :]
            m_prev = m_ref[h, 0:1, 0:1]
            dmat = jnp.where(mask, bc - br + lir, -jnp.inf)
            inter = bc + m_prev
            m_t = jnp.maximum(inter, jnp.max(dmat, axis=1, keepdims=True))
            dexp = jnp.exp(dmat - m_t)
            inter_w = jnp.exp(inter - m_t)
            qh = (q_ref[0, 0, :, sl] * (d ** -0.5)).astype(BF16)
            kt = k_ref[0, 0, :, sl].T
            v_ext = jnp.concatenate([v_ref[0, :, sl].astype(BF16), ones_col], axis=1)
            s = jnp.dot(qh, kt.astype(BF16), preferred_element_type=F32) * dexp
            c_ext = c_ref[h]
            acc = inter_w * jnp.dot(qh, c_ext.astype(BF16), preferred_element_type=F32)
            acc = acc + jnp.dot(s.astype(BF16), v_ext, preferred_element_type=F32)
            den = jnp.maximum(jnp.abs(acc[:, d:d + 1]), jnp.exp(-m_t))
            o_ref[0, :, sl] = acc[:, :d] / den
            b_last = jnp.sum(lf_row, axis=1, keepdims=True)
            g_row = b_last - br + lir
            m_new = jnp.maximum(b_last + m_prev, jnp.max(g_row, axis=1, keepdims=True))
            a = jnp.exp(b_last + m_prev - m_new)
            w_row = jnp.exp(g_row - m_new)
            c_ref[h] = a * c_ext + jnp.dot((kt * w_row).astype(BF16), v_ext, preferred_element_type=F32)
            m_ref[h] = jnp.broadcast_to(m_new, m_ref.shape[1:])

    @pl.when(step < nc_ctx)
    def _():
        chunk(*ins[:4], outs[0])

    @pl.when(step >= nc_ctx)
    def _():
        chunk(*ins[4:], outs[1])


def mlstm_bidir(qk_c, proj_c, small_c, qk_l, proj_l, small_l, gate_bias):
    _, bsz, lc, w = qk_c.shape
    ll = qk_l.shape[2]
    nh = ML_HEADS
    t_c, t_l = min(ML_CHUNK, lc), min(ML_CHUNK_LATENT, ll)
    nc_c, nc_l = lc // t_c, ll // t_l
    bias = jnp.zeros((1, N_SMALL), F32).at[0, S_GATES:S_GATES + ML_GATES].set(gate_bias)

    def idx(dd, step_off, nc):
        def f(g):
            cc = jnp.clip(g - step_off, 0, nc - 1)
            return cc + dd * (nc - 1 - 2 * cc)
        return f

    vcol = P_MLV // w

    def seg_specs(ix, t):
        return [pl.BlockSpec((1, 1, t, w), lambda b, g: (0, b, ix(g), 0)),
                pl.BlockSpec((1, 1, t, w), lambda b, g: (1, b, ix(g), 0)),
                pl.BlockSpec((1, t, w), lambda b, g: (b, ix(g), vcol)),
                pl.BlockSpec((1, t, N_SMALL), lambda b, g: (b, ix(g), 0))]

    def out_spec(ix, t):
        return pl.BlockSpec((1, t, w), lambda b, g: (b, ix(g), 0))

    h_c, h_l = [], []
    for dd in range(2):
        ic, il = idx(dd, 0, nc_c), idx(dd, nc_c, nc_l)
        hc, hl = pl.pallas_call(
            functools.partial(_mlstm_kernel, nc_ctx=nc_c, direction=dd),
            grid=(bsz, nc_c + nc_l),
            in_specs=seg_specs(ic, t_c) + seg_specs(il, t_l) + [pl.BlockSpec((1, N_SMALL), lambda b, g: (0, 0))],
            out_specs=[out_spec(ic, t_c), out_spec(il, t_l)],
            out_shape=[jax.ShapeDtypeStruct((bsz, lc, w), F32), jax.ShapeDtypeStruct((bsz, ll, w), F32)],
            scratch_shapes=[pltpu.VMEM((nh, ML_HEAD_DIM, 2 * ML_HEAD_DIM), F32), pltpu.VMEM((nh, 8, 128), F32)],
            compiler_params=_cparams(("parallel", "arbitrary")),
            name="mlstm_fwd" if dd == 0 else "mlstm_bwd",
        )(qk_c, qk_c, proj_c, small_c, qk_l, qk_l, proj_l, small_l, bias)
        h_c.append(hc)
        h_l.append(hl)
    return tuple(h_c), tuple(h_l)


def _rms(x):
    return x * lax.rsqrt(jnp.mean(x * x, axis=-1, keepdims=True) + EPS)


def _swap_rope_halves(x):
    w = x.shape[1]
    lane = lax.broadcasted_iota(jnp.int32, x.shape, 1) % MLA_ROPE
    return jnp.where(lane < MLA_ROPE // 2, pltpu.roll(x, w - MLA_ROPE // 2, axis=1),
                     pltpu.roll(x, MLA_ROPE // 2, axis=1))


def _mla_prep_kernel(pq_ref, pkv_ref, sm_ref, gqa_ref, gkva_ref, wq_ref, wkv_ref, gqn_ref, gqr_ref, gkn_ref,
                     gkr_ref, en_ref, er_ref, *rest, use_rope):
    if use_rope:
        ccq_ref, ssq_ref, cck_ref, ssk_ref, qt_ref, k_ref, vt_ref = rest
    else:
        qt_ref, k_ref, vt_ref = rest
    nh, dn, dr = MLA_HEADS, MLA_NOPE, MLA_ROPE
    qa = (_rms(pq_ref[0].astype(F32)) * gqa_ref[...]).astype(BF16)
    kva = (_rms(pkv_ref[0].astype(F32)) * gkva_ref[...]).astype(BF16)
    q_raw = jnp.dot(qa, wq_ref[...], preferred_element_type=F32)
    kv_raw = jnp.dot(kva, wkv_ref[...], preferred_element_type=F32)
    inv_d = 1.0 / MLA_QK

    sel = functools.partial(jnp.dot, preferred_element_type=F32)
    qn, qr = q_raw[:, :nh * dn], q_raw[:, nh * dn:]
    ss = sel((qn * qn).astype(BF16), en_ref[...]) + sel((qr * qr).astype(BF16), er_ref[...])
    rs_rows = lax.rsqrt(ss * inv_d + EPS).T
    qn = qn * gqn_ref[...]
    qr = qr * gqr_ref[...]
    if use_rope:
        qr = qr * ccq_ref[...] + _swap_rope_halves(qr) * ssq_ref[...]
    qrt = qr.T
    tl = qn.shape[0]
    for h in range(nh):
        rs_h = rs_rows[h:h + 1, :]
        qt_ref[0, h, 0:dn, :] = (qn[:, h * dn:(h + 1) * dn].T * rs_h).astype(BF16)
        qt_ref[0, h, dn:dn + dr, :] = (qrt[h * dr:(h + 1) * dr] * rs_h).astype(BF16)
        qt_ref[0, h, dn + dr:, :] = jnp.zeros((QK_PAD - dn - dr, tl), BF16)

    kn, vv = kv_raw[:, :nh * dn], kv_raw[:, nh * dn:]
    lane = lax.broadcasted_iota(jnp.int32, sm_ref.shape[1:], 1)
    kr = jnp.where(lane < dr, sm_ref[0], 0.0)
    ssk = sel((kn * kn).astype(BF16), en_ref[...]) + jnp.sum(kr * kr, axis=-1, keepdims=True)
    rsk = lax.rsqrt(ssk * inv_d + EPS)
    kn = kn * gkn_ref[...]
    kr = kr * gkr_ref[...]
    if use_rope:
        kr = kr * cck_ref[...] + _swap_rope_halves(kr) * ssk_ref[...]
    for h in range(nh):
        rs_h = jnp.broadcast_to(rsk[:, h:h + 1], (tl, dn))
        k_ref[0, h, :, 0:dn] = (kn[:, h * dn:(h + 1) * dn] * rs_h).astype(BF16)
        k_ref[0, h, :, dn:] = (kr * rs_h).astype(BF16)
        vt_ref[0, h] = vv[:, h * MLA_V:(h + 1) * MLA_V].T.astype(BF16)


@functools.lru_cache(maxsize=None)
def _head_selectors():
    en = np.kron(np.eye(MLA_HEADS), np.ones((MLA_NOPE, 1)))
    er = np.kron(np.eye(MLA_HEADS), np.ones((MLA_ROPE, 1)))
    pad = lambda m: np.pad(m, ((0, 0), (0, 128 - m.shape[1]))).astype(np.float32)
    return pad(en), pad(er)


@functools.lru_cache(maxsize=None)
def _rope_tables(n_tokens):
    f32 = np.float32
    rows = n_tokens // GRID_W
    row = np.repeat(np.arange(rows, dtype=f32), GRID_W)
    col = np.tile(np.arange(GRID_W, dtype=f32), rows)
    freqs = f32(ROPE_BASE) ** (-np.arange(ROPE_PAIRS_PER_AXIS, dtype=f32) / f32(ROPE_PAIRS_PER_AXIS))
    ang = np.concatenate([row[:, None] * freqs, col[:, None] * freqs], axis=-1).astype(f32)
    cos, sin = np.cos(ang).astype(f32), np.sin(ang).astype(f32)
    cc = np.concatenate([cos, cos], axis=-1)
    ss = np.concatenate([-sin, sin], axis=-1)
    zero = np.zeros((n_tokens, 128 - MLA_ROPE), f32)
    return (np.tile(cc, (1, MLA_HEADS)), np.tile(ss, (1, MLA_HEADS)),
            np.concatenate([cc, zero], axis=-1), np.concatenate([ss, zero], axis=-1))


def mla_prep(proj, small, p, rope_tables, *, tl):
    b, l, _ = proj.shape
    nh = MLA_HEADS
    en, er = _head_selectors()
    en_b, er_b = jnp.asarray(en, BF16), jnp.asarray(er, BF16)
    const = lambda shape: pl.BlockSpec(shape, lambda bi, i: (0,) * len(shape))
    nq = nh * MLA_QK
    nkv = nh * (MLA_NOPE + MLA_V)
    in_specs = [pl.BlockSpec((1, tl, MLA_Q_LORA), lambda bi, i: (bi, i, P_CQ // MLA_Q_LORA)),
                pl.BlockSpec((1, tl, MLA_KV_LORA), lambda bi, i: (bi, i, P_CKV // MLA_KV_LORA)),
                pl.BlockSpec((1, tl, N_SMALL), lambda bi, i: (bi, i, 0)),
                const((1, MLA_Q_LORA)), const((1, MLA_KV_LORA)), const((MLA_Q_LORA, nq)), const((MLA_KV_LORA, nkv)),
                const((1, nh * MLA_NOPE)), const((1, nh * MLA_ROPE)), const((1, nh * MLA_NOPE)), const((1, 128)),
                const(en.shape), const(er.shape)]
    operands = [proj, proj, small, p['gqa'], p['gkva'], p['wq'], p['wkv'], p['gqn'], p['gqr'], p['gkn'], p['gkr'],
                en_b, er_b]
    use_rope = rope_tables is not None
    if use_rope:
        in_specs += [pl.BlockSpec((tl, nh * MLA_ROPE), lambda bi, i: (i, 0)),
                     pl.BlockSpec((tl, nh * MLA_ROPE), lambda bi, i: (i, 0)),
                     pl.BlockSpec((tl, 128), lambda bi, i: (i, 0)),
                     pl.BlockSpec((tl, 128), lambda bi, i: (i, 0))]
        operands += list(rope_tables)
    return pl.pallas_call(
        functools.partial(_mla_prep_kernel, use_rope=use_rope),
        grid=(b, l // tl),
        in_specs=in_specs,
        out_specs=[pl.BlockSpec((1, nh, QK_PAD, tl), lambda bi, i: (bi, 0, 0, i)),
                   pl.BlockSpec((1, nh, tl, QK_PAD), lambda bi, i: (bi, 0, i, 0)),
                   pl.BlockSpec((1, nh, MLA_V, tl), lambda bi, i: (bi, 0, 0, i))],
        out_shape=[jax.ShapeDtypeStruct((b, nh, QK_PAD, l), BF16),
                   jax.ShapeDtypeStruct((b, nh, l, QK_PAD), BF16),
                   jax.ShapeDtypeStruct((b, nh, MLA_V, l), BF16)],
        compiler_params=_cparams(("parallel", "parallel")),
        name="mla_prep",
    )(*operands)


def _combine_out_kernel(a_ref, y_ref, hf_ref, hb_ref, o_ref, g_ref, w_ref, r_ref, gt_ref, out_ref, cat_ref):
    @pl.when(pl.program_id(2) == 0)
    def _():
        g = g_ref[...]
        cat_ref[:, :MLA_WIDTH] = (_rms(a_ref[0]) * g[:, :MLA_WIDTH]).astype(BF16)
        hy0 = MLA_WIDTH
        cat_ref[:, hy0:hy0 + HY_WIDTH] = (_rms(y_ref[0]) * g[:, hy0:hy0 + HY_WIDTH]).astype(BF16)
        ml0 = MLA_WIDTH + HY_WIDTH
        hh = hf_ref[0] + hb_ref[0]
        og = 0.5 * jnp.tanh(0.5 * o_ref[0].astype(F32)) + 0.5
        for hd in range(ML_HEADS):
            sl = slice(hd * ML_HEAD_DIM, (hd + 1) * ML_HEAD_DIM)
            gl = g[:, ml0 + hd * ML_HEAD_DIM:ml0 + (hd + 1) * ML_HEAD_DIM]
            cat_ref[:, ml0 + hd * ML_HEAD_DIM:ml0 + (hd + 1) * ML_HEAD_DIM] = (
                _rms(hh[:, sl]) * gl * og[:, sl]).astype(BF16)

    y = jnp.dot(cat_ref[...], w_ref[...], preferred_element_type=F32)
    out_ref[0] = r_ref[0] + gt_ref[0] * y


def combine_out_proj(a, y, h2, proj, g_mix, w_out, res, gate, *, tm, tn, name="out_proj"):
    b, s, d = res.shape
    return pl.pallas_call(
        _combine_out_kernel,
        grid=(b, s // tm, d // tn),
        in_specs=[pl.BlockSpec((1, tm, MLA_WIDTH), lambda bi, i, j: (bi, i, 0)),
                  pl.BlockSpec((1, tm, HY_WIDTH), lambda bi, i, j: (bi, i, 0)),
                  pl.BlockSpec((1, tm, ML_WIDTH), lambda bi, i, j: (bi, i, 0)),
                  pl.BlockSpec((1, tm, ML_WIDTH), lambda bi, i, j: (bi, i, 0)),
                  pl.BlockSpec((1, tm, ML_WIDTH), lambda bi, i, j: (bi, i, P_MLO // ML_WIDTH)),
                  pl.BlockSpec((1, MIX_WIDTH), lambda bi, i, j: (0, 0)),
                  pl.BlockSpec((MIX_WIDTH, tn), lambda bi, i, j: (0, j)),
                  pl.BlockSpec((1, tm, tn), lambda bi, i, j: (bi, i, j)),
                  pl.BlockSpec((1, 1, tn), lambda bi, i, j: (bi, 0, j))],
        out_specs=pl.BlockSpec((1, tm, tn), lambda bi, i, j: (bi, i, j)),
        out_shape=jax.ShapeDtypeStruct((b, s, d), F32),
        scratch_shapes=[pltpu.VMEM((tm, MIX_WIDTH), BF16)],
        compiler_params=_cparams(("parallel", "parallel", "arbitrary")),
        name=name,
    )(a, y, h2[0], h2[1], proj, g_mix.reshape(1, MIX_WIDTH), w_out, res, gate)


@functools.lru_cache(maxsize=None)
def _direct_dft_tables(l):
    n = 2 * l
    k = np.arange(n)[:, None]
    t = np.arange(l)[None, :]
    ang = 2.0 * np.pi * ((k * t) % n) / n
    c, s = np.cos(ang), np.sin(ang)
    gf = np.block([[c, s], [-s, c]])
    gi = gf.T.copy()
    t_full = np.arange(n)[None, :]
    ang = 2.0 * np.pi * ((k * t_full) % n) / n
    g_real = np.concatenate([np.cos(ang), -np.sin(ang)], axis=0)
    return gf.astype(np.float32), gi.astype(np.float32), g_real.astype(np.float32)


def _direct_conv_kernel(gf_ref, gi_ref, z_ref, gate_ref, h_ref, skip_ref, o_ref, *, scale):
    z = z_ref[0]
    x = jnp.dot(gf_ref[...].astype(BF16), z.astype(BF16), preferred_element_type=F32)
    n = x.shape[0] // 2
    xr, xi = x[:n], x[n:]
    hr, hi = h_ref[:n] * scale, h_ref[n:] * scale
    prod = jnp.concatenate([xr * hr - xi * hi, xr * hi + xi * hr], axis=0).astype(BF16)
    conv = jnp.dot(gi_ref[...].astype(BF16), prod, preferred_element_type=F32)
    o_ref[0] = gate_ref[0] * (conv + skip_ref[...] * z)


def hyena_short(proj, p):
    bsz, l, _ = proj.shape
    c = HY_WIDTH
    npair = bsz // 2
    n = 2 * l
    gf, gi, g_real = _direct_dft_tables(l)
    buf = hyena_filter_buffer(l, p, tl=l)
    hspec = dft_first(jnp.asarray(g_real), buf.reshape(1, n, HY_ORDER * c), n_out=1, tn=HY_ORDER * c,
                      precise=True)[0]
    vx = short_conv_groups(proj, P_HY, p['hy_conv_w'], p['hy_conv_b'], group=c, tc=CONV_TC, silu=False,
                           name="hyena_short_conv_ctx")
    vx = vx.reshape(3 * npair, 2 * l, c)
    z = vx

    def rows_at(off):
        return lambda q: (off + q, 0, 0)

    def cols_at(col):
        return lambda q: (0, col)

    for order in range(HY_ORDER):
        z = pl.pallas_call(
            functools.partial(_direct_conv_kernel, scale=1.0 / n),
            grid=(npair,),
            in_specs=[pl.BlockSpec(gf.shape, lambda q: (0, 0)),
                      pl.BlockSpec(gi.shape, lambda q: (0, 0)),
                      pl.BlockSpec((1, 2 * l, c), rows_at(0)),
                      pl.BlockSpec((1, 2 * l, c), rows_at((order + 1) * npair)),
                      pl.BlockSpec((2 * n, c), cols_at(order)),
                      pl.BlockSpec((1, c), lambda q: (0, 0))],
            out_specs=pl.BlockSpec((1, 2 * l, c), lambda q: (q, 0, 0)),
            out_shape=jax.ShapeDtypeStruct((npair, 2 * l, c), F32),
            compiler_params=_cparams(("parallel",)),
            name="hyena_direct_conv",
        )(jnp.asarray(gf), jnp.asarray(gi), z, vx, hspec, p['hy_skip'][order].reshape(1, c))
    return z.reshape(bsz, l, c)


def _split_cols(w, sizes):
    out, start = [], 0
    for s in sizes:
        out.append(w[:, start:start + s])
        start += s
    return out


def _prep_layer_weights(l, w_in, mla_qa_norm, mla_kva_norm, mla_w_uq, mla_w_ukv, mla_q_norm, mla_k_norm):
    cq, ckv, kr, hy, mqk, mv, mo, gt = _split_cols(w_in[l], IN_SIZES)
    w_main = jnp.concatenate([cq, ckv, hy, mqk, mv, mo], axis=1).astype(BF16)
    pad = jnp.zeros((D_MODEL, N_SMALL - MLA_ROPE - ML_GATES), F32)
    w_small = jnp.concatenate([kr, gt, pad], axis=1).astype(BF16)
    nh = MLA_HEADS
    wq = mla_w_uq[l].reshape(MLA_Q_LORA, nh, MLA_QK)
    wq = jnp.concatenate([wq[:, :, :MLA_NOPE].reshape(MLA_Q_LORA, -1), wq[:, :, MLA_NOPE:].reshape(MLA_Q_LORA, -1)],
                         axis=1).astype(BF16)
    wkv = mla_w_ukv[l].reshape(MLA_KV_LORA, nh, MLA_NOPE + MLA_V)
    wkv = jnp.concatenate([wkv[:, :, :MLA_NOPE].reshape(MLA_KV_LORA, -1),
                           wkv[:, :, MLA_NOPE:].reshape(MLA_KV_LORA, -1)], axis=1).astype(BF16)
    q_scale = (MLA_QK ** -0.5) * math.log2(math.e)
    gq, gk = mla_q_norm[l] * q_scale, mla_k_norm[l]
    mla = {'gqa': mla_qa_norm[l].reshape(1, -1), 'gkva': mla_kva_norm[l].reshape(1, -1), 'wq': wq, 'wkv': wkv,
           'gqn': jnp.tile(gq[:MLA_NOPE], nh).reshape(1, -1), 'gqr': jnp.tile(gq[MLA_NOPE:], nh).reshape(1, -1),
           'gkn': jnp.tile(gk[:MLA_NOPE], nh).reshape(1, -1),
           'gkr': jnp.concatenate([gk[MLA_NOPE:], jnp.zeros((128 - MLA_ROPE,), F32)]).reshape(1, -1)}
    return w_main, w_small, mla


def kernel(x, c, ctx, c_ctx, ada_w, ada_b, norm1_g, norm2_g, w_in, mla_qa_norm, mla_kva_norm, mla_w_uq,
           mla_w_ukv, mla_q_norm, mla_k_norm, hy_conv_w, hy_conv_b, hy_w1, hy_b1, hy_w2, hy_b2, hy_w3,
           hy_decay, hy_skip, ml_conv_w, ml_conv_b, ml_gate_b, mix_norm_g, w_out, ffn_w1, ffn_w2):
    B, S, D = x.shape
    LC = ctx.shape[1]
    rope_tables = _rope_tables(S)

    silu_rows = jnp.concatenate([jax.nn.silu(c), jax.nn.silu(c_ctx)[None], jnp.zeros((8 - B - 1, D), F32)], axis=0)
    mods = ada_modulation(silu_rows.astype(BF16), ada_w, ada_b)

    for l in range(DEPTH):
        need_ctx = l < DEPTH - 1
        p = {'hy_conv_w': hy_conv_w[l], 'hy_conv_b': hy_conv_b[l], 'hy_w1': hy_w1[l], 'hy_b1': hy_b1[l],
             'hy_w2': hy_w2[l], 'hy_b2': hy_b2[l], 'hy_w3': hy_w3[l], 'hy_decay': hy_decay[l],
             'hy_skip': hy_skip[l], 'ml_conv_w': ml_conv_w[l], 'ml_conv_b': ml_conv_b[l]}
        w_main, w_small, mla = _prep_layer_weights(l, w_in, mla_qa_norm, mla_kva_norm, mla_w_uq, mla_w_ukv,
                                                   mla_q_norm, mla_k_norm)
        w_out_l = w_out[l].astype(BF16)
        w1_l = ffn_w1[l].astype(BF16)
        w2_l = ffn_w2[l].astype(BF16)

        mod_l = [m[:, None, :] for m in jnp.split(mods[l, :B], 6, axis=-1)]
        mod_c = [m[:, None, :] for m in jnp.split(mods[l, B:B + 1], 6, axis=-1)]
        flat = lambda t: t.reshape(1, B * LC, t.shape[-1])

        proj_l, small_l = in_proj(x, norm1_g[l], mod_l[1], mod_l[0], w_main, w_small, tm=1024, tn=IN_PROJ_TN)
        proj_c, small_c = in_proj(flat(ctx), norm1_g[l], mod_c[1], mod_c[0], w_main, w_small, tm=B * LC,
                                  tn=IN_PROJ_TN, name="in_proj_ctx")
        proj_c, small_c = proj_c.reshape(B, LC, N_MAIN), small_c.reshape(B, LC, N_SMALL)

        qt_l, k_l, vt_l = mla_prep(proj_l, small_l, mla, rope_tables, tl=512)
        qt_c, k_c, vt_c = mla_prep(proj_c, small_c, mla, None, tl=LC)
        a_l = attention(qt_l, [(k_c, vt_c), (k_l, vt_l)], tq=ATT_TQ)
        y_l = hyena_long(proj_l, p)

        def mlstm_qk(proj):
            return short_conv_groups(proj, P_MLQK, p['ml_conv_w'], p['ml_conv_b'], group=ML_WIDTH, tc=CONV_TC,
                                     silu=True, name="mlstm_short_conv")

        h_c, h_l = mlstm_bidir(mlstm_qk(proj_c), proj_c, small_c, mlstm_qk(proj_l), proj_l, small_l, ml_gate_b[l])
        x = combine_out_proj(a_l, y_l, h_l, proj_l, mix_norm_g[l], w_out_l, x, mod_l[2], tm=1024, tn=MM_TN)
        hid = norm_swiglu(x, norm2_g[l], mod_l[4], mod_l[3], w1_l, tm=1024, tn=MM_TN, name="ffn_up")
        x = mm_residual(hid, w2_l, x, mod_l[5], tm=1024, tn=MM_TN, name="ffn_down")

        if need_ctx:
            a_c = attention(qt_c, [(k_c, vt_c)], tq=LC)
            y_c = hyena_short(proj_c, p)
            ctx = combine_out_proj(flat(a_c), flat(y_c), (flat(h_c[0]), flat(h_c[1])), flat(proj_c), mix_norm_g[l],
                                   w_out_l, flat(ctx), mod_c[2], tm=B * LC, tn=MM_TN, name="out_proj_ctx")
            hid = norm_swiglu(ctx, norm2_g[l], mod_c[4], mod_c[3], w1_l, tm=B * LC, tn=MM_TN, name="ffn_up_ctx")
            ctx = mm_residual(hid, w2_l, ctx, mod_c[5], tm=B * LC, tn=MM_TN, name="ffn_down_ctx")
            ctx = ctx.reshape(B, LC, D)
    return x
```

```python
import functools
import math

import jax
import jax.numpy as jnp
import numpy as np
from jax import lax
from jax.experimental import pallas as pl
from jax.experimental.pallas import tpu as pltpu

D_MODEL = 2048
DEPTH = 2
GRID_W = 64
EPS = 1e-6

MLA_HEADS = 8
MLA_NOPE = 128
MLA_ROPE = 64
MLA_QK = MLA_NOPE + MLA_ROPE
MLA_V = 128
MLA_Q_LORA = 512
MLA_KV_LORA = 512
MLA_WIDTH = MLA_HEADS * MLA_V
ROPE_BASE = 10000.0
ROPE_PAIRS_PER_AXIS = MLA_ROPE // 4
QK_PAD = 256

HY_WIDTH = 512
HY_ORDER = 2
HY_BANDS = 16
HY_SHIFT = 0.05
SHORT_CONV = 3
CONV_TC = 256

ML_HEADS = 4
ML_HEAD_DIM = 128
ML_WIDTH = ML_HEADS * ML_HEAD_DIM
ML_CHUNK = 256
ML_CHUNK_LATENT = 1024
ML_GATES = 4 * ML_HEADS

MIX_WIDTH = MLA_WIDTH + HY_WIDTH + ML_WIDTH
FFN_HIDDEN = ((8 * D_MODEL // 3 + 255) // 256) * 256

IN_SIZES = (MLA_Q_LORA, MLA_KV_LORA, MLA_ROPE, 3 * HY_WIDTH, 2 * ML_WIDTH, ML_WIDTH, ML_WIDTH, ML_GATES)
N_IN = sum(IN_SIZES)

P_CQ, P_CKV, P_HY, P_MLQK, P_MLV, P_MLO = 0, 512, 1024, 2560, 3584, 4096
N_MAIN = 4608
N_SMALL = 128
S_GATES = MLA_ROPE

VMEM_LIMIT_BYTES = 56 * 1024 * 1024

BF16 = jnp.bfloat16
F32 = jnp.float32


def _cparams(sem):
    return pltpu.CompilerParams(dimension_semantics=sem, vmem_limit_bytes=VMEM_LIMIT_BYTES)


def _ada_kernel(s_ref, w_ref, b_ref, o_ref):
    w = w_ref[0].astype(BF16)
    o_ref[0] = jnp.dot(s_ref[...], w, preferred_element_type=F32) + b_ref[0]


def ada_modulation(silu_rows, ada_w, ada_b, tn=1024):
    depth, d, n = ada_w.shape
    return pl.pallas_call(
        _ada_kernel,
        grid=(depth, n // tn),
        in_specs=[pl.BlockSpec((8, d), lambda l, j: (0, 0)),
                  pl.BlockSpec((1, d, tn), lambda l, j: (l, 0, j)),
                  pl.BlockSpec((1, 1, tn), lambda l, j: (l, 0, j))],
        out_specs=pl.BlockSpec((1, 8, tn), lambda l, j: (l, 0, j)),
        out_shape=jax.ShapeDtypeStruct((depth, 8, n), F32),
        compiler_params=_cparams(("parallel", "parallel")),
        name="ada_modulation",
    )(silu_rows, ada_w, ada_b.reshape(depth, 1, n))


MM_TN = 512
IN_PROJ_TN = 1536


def _normed(x, g, sc, sh):
    gain = g * (1.0 + sc)
    return x * lax.rsqrt(jnp.mean(x * x, axis=-1, keepdims=True) + EPS) * gain + sh


def _in_proj_kernel(x_ref, g_ref, sc_ref, sh_ref, w_ref, ws_ref, o_ref, os_ref, xn_ref):
    @pl.when(pl.program_id(2) == 0)
    def _():
        xn = _normed(x_ref[0], g_ref[...], sc_ref[0], sh_ref[0]).astype(BF16)
        xn_ref[...] = xn
        os_ref[0] = jnp.dot(xn, ws_ref[...], preferred_element_type=F32)

    o_ref[0] = jnp.dot(xn_ref[...], w_ref[...], preferred_element_type=F32).astype(o_ref.dtype)


def in_proj(x, g, sc, sh, w_main, w_small, *, tm, tn, name="in_proj"):
    b, s, k = x.shape
    n = w_main.shape[1]
    ns = w_small.shape[1]
    return pl.pallas_call(
        _in_proj_kernel,
        grid=(b, s // tm, n // tn),
        in_specs=[pl.BlockSpec((1, tm, k), lambda bi, i, j: (bi, i, 0)),
                  pl.BlockSpec((1, k), lambda bi, i, j: (0, 0)),
                  pl.BlockSpec((1, 1, k), lambda bi, i, j: (bi, 0, 0)),
                  pl.BlockSpec((1, 1, k), lambda bi, i, j: (bi, 0, 0)),
                  pl.BlockSpec((k, tn), lambda bi, i, j: (0, j)),
                  pl.BlockSpec((k, ns), lambda bi, i, j: (0, 0))],
        out_specs=[pl.BlockSpec((1, tm, tn), lambda bi, i, j: (bi, i, j)),
                   pl.BlockSpec((1, tm, ns), lambda bi, i, j: (bi, i, 0))],
        out_shape=[jax.ShapeDtypeStruct((b, s, n), BF16), jax.ShapeDtypeStruct((b, s, ns), F32)],
        scratch_shapes=[pltpu.VMEM((tm, k), BF16)],
        compiler_params=_cparams(("parallel", "parallel", "arbitrary")),
        name=name,
    )(x, g.reshape(1, k), sc, sh, w_main, w_small)


def _norm_swiglu_kernel(x_ref, g_ref, sc_ref, sh_ref, wg_ref, wu_ref, o_ref, xn_ref):
    @pl.when(pl.program_id(2) == 0)
    def _():
        xn_ref[...] = _normed(x_ref[0], g_ref[...], sc_ref[0], sh_ref[0]).astype(BF16)

    xn = xn_ref[...]
    gate = jnp.dot(xn, wg_ref[...], preferred_element_type=F32)
    up = jnp.dot(xn, wu_ref[...], preferred_element_type=F32)
    o_ref[0] = (gate * (0.5 * jnp.tanh(0.5 * gate) + 0.5) * up).astype(o_ref.dtype)


def norm_swiglu(x, g, sc, sh, w1, *, tm, tn, name="norm_swiglu"):
    b, s, d = x.shape
    h = w1.shape[1] // 2
    nj = h // tn
    return pl.pallas_call(
        _norm_swiglu_kernel,
        grid=(b, s // tm, nj),
        in_specs=[pl.BlockSpec((1, tm, d), lambda bi, i, j: (bi, i, 0)),
                  pl.BlockSpec((1, d), lambda bi, i, j: (0, 0)),
                  pl.BlockSpec((1, 1, d), lambda bi, i, j: (bi, 0, 0)),
                  pl.BlockSpec((1, 1, d), lambda bi, i, j: (bi, 0, 0)),
                  pl.BlockSpec((d, tn), lambda bi, i, j: (0, j)),
                  pl.BlockSpec((d, tn), lambda bi, i, j: (0, j + nj))],
        out_specs=pl.BlockSpec((1, tm, tn), lambda bi, i, j: (bi, i, j)),
        out_shape=jax.ShapeDtypeStruct((b, s, h), BF16),
        scratch_shapes=[pltpu.VMEM((tm, d), BF16)],
        compiler_params=_cparams(("parallel", "parallel", "arbitrary")),
        name=name,
    )(x, g.reshape(1, d), sc, sh, w1, w1)


def _mm_res_kernel(a_ref, w_ref, r_ref, gt_ref, o_ref):
    y = jnp.dot(a_ref[0], w_ref[...], preferred_element_type=F32)
    o_ref[0] = r_ref[0] + gt_ref[0] * y


def mm_residual(a, w, res, gate, *, tm, tn, name="mm_residual"):
    b, s, k = a.shape
    n = w.shape[1]
    return pl.pallas_call(
        _mm_res_kernel,
        grid=(b, s // tm, n // tn),
        in_specs=[pl.BlockSpec((1, tm, k), lambda bi, i, j: (bi, i, 0)),
                  pl.BlockSpec((k, tn), lambda bi, i, j: (0, j)),
                  pl.BlockSpec((1, tm, tn), lambda bi, i, j: (bi, i, j)),
                  pl.BlockSpec((1, 1, tn), lambda bi, i, j: (bi, 0, j))],
        out_specs=pl.BlockSpec((1, tm, tn), lambda bi, i, j: (bi, i, j)),
        out_shape=jax.ShapeDtypeStruct((b, s, n), F32),
        compiler_params=_cparams(("parallel", "parallel", "parallel")),
        name=name,
    )(a, w, res, gate)


ATT_KEY_CHUNK = 512
ATT_TQ = 1024


def _attn_kernel(qt_ref, *refs, n_seg):
    o_ref, s_ref = refs[2 * n_seg], refs[2 * n_seg + 1]
    qt = qt_ref[0, 0]
    tq = qt.shape[1]
    chunks, off = [], 0
    for sgi in range(n_seg):
        k_ref, vt_ref = refs[2 * sgi], refs[2 * sgi + 1]
        lk = k_ref.shape[2]
        ch = min(ATT_KEY_CHUNK, lk)
        for c in range(lk // ch):
            chunks.append((k_ref, vt_ref, c * ch, ch, off))
            off += ch

    m8 = jnp.full((8, tq), -jnp.inf, F32)
    for k_ref, _, start, ch, off in chunks:
        s = jnp.dot(k_ref[0, 0, start:start + ch, :], qt, preferred_element_type=F32)
        s_ref[off:off + ch, :] = s
        m8 = jnp.maximum(m8, jnp.max(s.reshape(ch // 8, 8, tq), axis=0))
    m = jnp.max(m8, axis=0, keepdims=True)
    l8 = jnp.zeros((8, tq), F32)
    acc = jnp.zeros((vt_ref.shape[2], tq), F32)
    for _, vt_ref, start, ch, off in chunks:
        p = jnp.exp2(s_ref[off:off + ch, :] - m)
        l8 = l8 + jnp.sum(p.reshape(ch // 8, 8, tq), axis=0)
        acc = acc + jnp.dot(vt_ref[0, 0, :, start:start + ch], p.astype(BF16), preferred_element_type=F32)
    l = jnp.sum(l8, axis=0, keepdims=True)
    o_ref[0] = (acc / l).T


def attention(qt, segments, *, tq=256):
    b, h, dk, lq = qt.shape
    dv = segments[0][1].shape[2]
    in_specs = [pl.BlockSpec((1, 1, dk, tq), lambda bi, hi, i: (bi, hi, 0, i))]
    operands = [qt]
    for k, vt in segments:
        lk = k.shape[2]
        in_specs += [pl.BlockSpec((1, 1, lk, dk), lambda bi, hi, i: (bi, hi, 0, 0)),
                     pl.BlockSpec((1, 1, dv, lk), lambda bi, hi, i: (bi, hi, 0, 0))]
        operands += [k, vt]
    return pl.pallas_call(
        functools.partial(_attn_kernel, n_seg=len(segments)),
        grid=(b, h, lq // tq),
        in_specs=in_specs,
        out_specs=pl.BlockSpec((1, tq, dv), lambda bi, hi, i: (bi, i, hi)),
        out_shape=jax.ShapeDtypeStruct((b, lq, h * dv), F32),
        scratch_shapes=[pltpu.VMEM((sum(k.shape[2] for k, _ in segments), tq), F32)],
        compiler_params=_cparams(("parallel", "parallel", "parallel")),
        name="attention",
    )(*operands)


HY_NA = 64
HY_NB = 128
HY_N = HY_NA * HY_NB
HY_KG = 16


@functools.lru_cache(maxsize=None)
def _dft_tables():
    na, nb, n = HY_NA, HY_NB, HY_N
    a = np.arange(na // 2)[None, :]
    ka = np.arange(na)[:, None]
    ang = 2.0 * np.pi * ((a * ka) % na) / na
    c, s = np.cos(ang), np.sin(ang)
    g1 = np.block([[c, s], [-s, c]])
    g1_inv = g1.T.copy()
    a_full = np.arange(na)[None, :]
    ang = 2.0 * np.pi * ((a_full * ka) % na) / na
    g1_real = np.concatenate([np.cos(ang), -np.sin(ang)], axis=0)
    b = np.arange(nb)[None, None, :]
    k = np.arange(na)[:, None, None] + na * np.arange(nb)[None, :, None]
    ang = 2.0 * np.pi * ((b * k) % n) / n
    cr, si = np.cos(ang), np.sin(ang)
    g2 = np.concatenate([np.concatenate([cr, si], axis=2), np.concatenate([-si, cr], axis=2)], axis=1)
    g2_inv = np.transpose(g2, (0, 2, 1)).copy()
    return (g1.astype(np.float32), g1_inv.astype(np.float32), g1_real.astype(np.float32),
            g2.astype(np.float32), g2_inv.astype(np.float32))


def _dot_3pass(a, b):
    a_hi, b_hi = a.astype(BF16), b.astype(BF16)
    a_lo = (a - a_hi.astype(F32)).astype(BF16)
    b_lo = (b - b_hi.astype(F32)).astype(BF16)
    dot = functools.partial(jnp.dot, preferred_element_type=F32)
    return dot(a_hi, b_hi) + (dot(a_hi, b_lo) + dot(a_lo, b_hi))


def _short_conv_kernel(u_ref, w_ref, b_ref, o_ref, *, silu):
    u = u_ref[0].astype(F32)
    n = u.shape[0]
    rows = lax.broadcasted_iota(jnp.int32, u.shape, 0)
    prev = jnp.where(rows == 0, 0.0, pltpu.roll(u, 1, axis=0))
    nxt = jnp.where(rows == n - 1, 0.0, pltpu.roll(u, n - 1, axis=0))
    y = b_ref[...] + prev * w_ref[0:1, :] + u * w_ref[1:2, :] + nxt * w_ref[2:3, :]
    if silu:
        y = y * jax.nn.sigmoid(y)
    o_ref[0, 0] = y.astype(o_ref.dtype)


def short_conv_groups(proj, col0, w, b, *, group, tc, silu, name, out_dtype=F32):
    bsz, l, _ = proj.shape
    width = w.shape[1]
    per = group // tc
    return pl.pallas_call(
        functools.partial(_short_conv_kernel, silu=silu),
        grid=(bsz, width // tc),
        in_specs=[pl.BlockSpec((1, l, tc), lambda bi, j: (bi, 0, col0 // tc + j)),
                  pl.BlockSpec((SHORT_CONV, tc), lambda bi, j: (0, j)),
                  pl.BlockSpec((1, tc), lambda bi, j: (0, j))],
        out_specs=pl.BlockSpec((1, 1, l, tc), lambda bi, j: (j // per, bi, 0, j % per)),
        out_shape=jax.ShapeDtypeStruct((width // group, bsz, l, group), out_dtype),
        compiler_params=_cparams(("parallel", "parallel")),
        name=name,
    )(proj, w, b.reshape(1, width))


def _hy_filter_kernel(f_ref, w1_ref, b1_ref, w2_ref, b2_ref, w3_ref, dec_ref, o_ref, *, zero_row):
    hp = lax.Precision.HIGHEST
    f = f_ref[...]
    tl = f.shape[0]
    h = jnp.sin(jnp.dot(f, w1_ref[...], precision=hp, preferred_element_type=F32) + b1_ref[...])
    h = jnp.sin(jnp.dot(h, w2_ref[...], precision=hp, preferred_element_type=F32) + b2_ref[...])
    h = _dot_3pass(h, w3_ref[0])
    h = h * (jnp.exp(-f[:, 0:1] * dec_ref[0]) + HY_SHIFT)
    rows = lax.broadcasted_iota(jnp.int32, h.shape, 0) + pl.program_id(0) * tl
    o_ref[...] = jnp.where(rows == zero_row, 0.0, h)


@functools.lru_cache(maxsize=None)
def _filter_features(L):
    f32 = np.float32
    n = 2 * L
    r = np.arange(n, dtype=f32)
    t = (np.where(r < L, r, n - r) / f32(L)).astype(f32)
    bands = np.arange(1, HY_BANDS + 1, dtype=f32)
    ang = (f32(2.0 * math.pi) * t[:, None] * bands).astype(f32)
    emb = 1 + 2 * HY_BANDS
    return np.concatenate([t[:, None], np.cos(ang).astype(f32), np.sin(ang).astype(f32),
                           np.zeros((n, 128 - emb), f32)], axis=-1)


def hyena_filter_buffer(L, p, tl=512):
    n = 2 * L
    emb = 1 + 2 * HY_BANDS
    feats = jnp.asarray(_filter_features(L))
    hid = p['hy_w1'].shape[1]
    w1 = jnp.concatenate([p['hy_w1'], jnp.zeros((128 - emb, hid), F32)], axis=0)
    oc = HY_ORDER * HY_WIDTH
    w3 = p['hy_w3'].reshape(hid, HY_ORDER, 2, HY_WIDTH).transpose(2, 0, 1, 3).reshape(2, hid, oc)
    dec = p['hy_decay'].reshape(HY_ORDER, 2, HY_WIDTH).transpose(1, 0, 2).reshape(2, 1, oc)
    half = L // tl
    return pl.pallas_call(
        functools.partial(_hy_filter_kernel, zero_row=L),
        grid=(n // tl,),
        in_specs=[pl.BlockSpec((tl, 128), lambda i: (i, 0)),
                  pl.BlockSpec((128, hid), lambda i: (0, 0)),
                  pl.BlockSpec((1, hid), lambda i: (0, 0)),
                  pl.BlockSpec((hid, hid), lambda i: (0, 0)),
                  pl.BlockSpec((1, hid), lambda i: (0, 0)),
                  pl.BlockSpec((1, hid, oc), lambda i: (i // half, 0, 0)),
                  pl.BlockSpec((1, 1, oc), lambda i: (i // half, 0, 0))],
        out_specs=pl.BlockSpec((tl, oc), lambda i: (i, 0)),
        out_shape=jax.ShapeDtypeStruct((n, oc), F32),
        compiler_params=_cparams(("parallel",)),
        name="hyena_filter",
    )(feats, w1, p['hy_b1'].reshape(1, hid), p['hy_w2'], p['hy_b2'].reshape(1, hid), w3, dec)


def _dft_first_kernel(g_ref, x_ref, o_ref, *, precise):
    if precise:
        o_ref[0] = _dot_3pass(g_ref[...], x_ref[0])
    else:
        y = jnp.dot(g_ref[...].astype(BF16), x_ref[0].astype(BF16), preferred_element_type=F32)
        o_ref[0] = y.astype(o_ref.dtype)


def dft_first(g, x, *, n_out, offset=0, tn=8192, precise=False):
    _, r, lanes = x.shape
    m = g.shape[0]
    return pl.pallas_call(
        functools.partial(_dft_first_kernel, precise=precise),
        grid=(n_out, lanes // tn),
        in_specs=[pl.BlockSpec((m, r), lambda p, j: (0, 0)),
                  pl.BlockSpec((1, r, tn), lambda p, j: (offset + p, 0, j))],
        out_specs=pl.BlockSpec((1, m, tn), lambda p, j: (p, 0, j)),
        out_shape=jax.ShapeDtypeStruct((n_out, m, lanes), F32 if precise else BF16),
        compiler_params=_cparams(("parallel", "parallel")),
        name="dft_first",
    )(g, x)


def _dft_filter_mid_kernel(g_ref, y_ref, o_ref, *, scale):
    for i in range(g_ref.shape[0]):
        y = jnp.concatenate([y_ref[0, i], y_ref[1, i]], axis=0)
        o_ref[i] = scale * _dot_3pass(g_ref[i], y)


def dft_filter_mid(g2, y1):
    _, na, nb, oc = y1.shape
    kg = HY_KG // 2
    return pl.pallas_call(
        functools.partial(_dft_filter_mid_kernel, scale=1.0 / HY_N),
        grid=(na // kg,),
        in_specs=[pl.BlockSpec((kg, 2 * nb, 2 * nb), lambda g: (g, 0, 0)),
                  pl.BlockSpec((2, kg, nb, oc), lambda g: (0, g, 0, 0))],
        out_specs=pl.BlockSpec((kg, 2 * nb, oc), lambda g: (g, 0, 0)),
        out_shape=jax.ShapeDtypeStruct((na, 2 * nb, oc), F32),
        compiler_params=_cparams(("parallel",)),
        name="dft_filter_mid",
    )(g2, y1)


def _dft_mid_kernel(gf_ref, gi_ref, y_ref, h_ref, o_ref):
    nb = y_ref.shape[3]
    for i in range(gf_ref.shape[0]):
        y = jnp.concatenate([y_ref[0, 0, i], y_ref[0, 1, i]], axis=0)
        x = jnp.dot(gf_ref[i].astype(BF16), y, preferred_element_type=F32)
        xr, xi = x[:nb], x[nb:]
        hr, hi = h_ref[i, :nb], h_ref[i, nb:]
        prod = jnp.concatenate([xr * hr - xi * hi, xr * hi + xi * hr], axis=0).astype(BF16)
        u = jnp.dot(gi_ref[i].astype(BF16), prod, preferred_element_type=F32)
        o_ref[0, 0, i] = u[:nb].astype(o_ref.dtype)
        o_ref[0, 1, i] = u[nb:].astype(o_ref.dtype)


def dft_mid(g2, g2_inv, y1, h, order):
    npair, _, na, nb, c = y1.shape
    kg = HY_KG
    return pl.pallas_call(
        _dft_mid_kernel,
        grid=(na // kg, npair),
        in_specs=[pl.BlockSpec((kg, 2 * nb, 2 * nb), lambda g, p: (g, 0, 0)),
                  pl.BlockSpec((kg, 2 * nb, 2 * nb), lambda g, p: (g, 0, 0)),
                  pl.BlockSpec((1, 2, kg, nb, c), lambda g, p: (p, 0, g, 0, 0)),
                  pl.BlockSpec((kg, 2 * nb, c), lambda g, p: (g, 0, order))],
        out_specs=pl.BlockSpec((1, 2, kg, nb, c), lambda g, p: (p, 0, g, 0, 0)),
        out_shape=jax.ShapeDtypeStruct(y1.shape, BF16),
        compiler_params=_cparams(("parallel", "parallel")),
        name="dft_mid",
    )(g2, g2_inv, y1, h)


def _dft_last_kernel(g_ref, u_ref, z_ref, gate_ref, skip_ref, o_ref):
    conv = jnp.dot(g_ref[...].astype(BF16), u_ref[0], preferred_element_type=F32)
    y = gate_ref[0].astype(F32) * (conv + skip_ref[...] * z_ref[0].astype(F32))
    o_ref[0] = y.astype(o_ref.dtype)


def dft_last(g1_inv, u, z, z_off, gate, gate_off, skip_lanes, *, out_dtype, tn=8192):
    npair, m2, lanes = u.shape
    r = g1_inv.shape[0]
    return pl.pallas_call(
        _dft_last_kernel,
        grid=(npair, lanes // tn),
        in_specs=[pl.BlockSpec((r, m2), lambda p, j: (0, 0)),
                  pl.BlockSpec((1, m2, tn), lambda p, j: (p, 0, j)),
                  pl.BlockSpec((1, r, tn), lambda p, j: (z_off + p, 0, j)),
                  pl.BlockSpec((1, r, tn), lambda p, j: (gate_off + p, 0, j)),
                  pl.BlockSpec((1, tn), lambda p, j: (0, 0))],
        out_specs=pl.BlockSpec((1, r, tn), lambda p, j: (p, 0, j)),
        out_shape=jax.ShapeDtypeStruct((npair, r, lanes), out_dtype),
        compiler_params=_cparams(("parallel", "parallel")),
        name="dft_last",
    )(g1_inv, u, z, gate, skip_lanes)


def hyena_long(proj, p):
    bsz, l, _ = proj.shape
    assert 2 * l == HY_N and bsz % 2 == 0
    c = HY_WIDTH
    npair = bsz // 2
    lanes = HY_NB * c
    g1, g1_inv, g1_real, g2, g2_inv = _dft_tables()
    g1_bf, g1_inv_bf = jnp.asarray(g1), jnp.asarray(g1_inv)
    g2_bf, g2_inv_bf = jnp.asarray(g2), jnp.asarray(g2_inv)

    buf = hyena_filter_buffer(l, p)
    y1h = dft_first(jnp.asarray(g1_real), buf.reshape(1, HY_NA, HY_NB * HY_ORDER * c), n_out=1, precise=True)
    hspec = dft_filter_mid(jnp.asarray(g2), y1h.reshape(2, HY_NA, HY_NB, HY_ORDER * c))

    vx = short_conv_groups(proj, P_HY, p['hy_conv_w'], p['hy_conv_b'], group=c, tc=CONV_TC, silu=False,
                           name="hyena_short_conv", out_dtype=BF16)
    vx = vx.reshape(3 * npair, HY_NA, lanes)
    z, z_off = vx, 0
    for n in range(HY_ORDER):
        y1 = dft_first(g1_bf, z, n_out=npair, offset=z_off)
        u = dft_mid(g2_bf, g2_inv_bf, y1.reshape(npair, 2, HY_NA, HY_NB, c), hspec, n)
        skip_lanes = jnp.tile(p['hy_skip'][n], 8192 // c).reshape(1, 8192)
        z = dft_last(g1_inv_bf, u.reshape(npair, 2 * HY_NA, lanes), z, z_off, vx, (n + 1) * npair, skip_lanes,
                     out_dtype=BF16 if n + 1 < HY_ORDER else F32)
        z_off = 0
    return z.reshape(bsz, l, c)


def _log_sigmoid(x):
    return jnp.minimum(x, 0.0) - jnp.log1p(jnp.exp(-jnp.abs(x)))


def _mlstm_kernel(*refs, nc_ctx, direction):
    ins, bias_ref, outs, (c_ref, m_ref) = refs[:8], refs[8], refs[9:11], refs[11:13]
    d = ML_HEAD_DIM
    nh = ML_HEADS
    step = pl.program_id(1)

    @pl.when(step == 0)
    def _():
        c_ref[...] = jnp.zeros_like(c_ref)
        m_ref[...] = jnp.zeros_like(m_ref)

    i0 = S_GATES + direction * nh
    f0 = S_GATES + (2 + direction) * nh

    def chunk(q_ref, k_ref, v_ref, sm_ref, o_ref):
        t = sm_ref.shape[1]
        g = sm_ref[0] + bias_ref[...]
        gt = g.T
        lf_cols = _log_sigmoid(g)
        lf_rows = _log_sigmoid(gt)
        r = lax.broadcasted_iota(jnp.int32, (t, t), 0)
        cidx = lax.broadcasted_iota(jnp.int32, (t, t), 1)
        mask = cidx <= r if direction == 0 else cidx >= r
        mask_t = r <= cidx if direction == 0 else r >= cidx
        tri = mask.astype(F32)
        tri_t = mask_t.astype(F32)
        lane = lax.broadcasted_iota(jnp.int32, (t, d), 1)
        ones_col = jnp.where(lane == 0, 1.0, 0.0).astype(BF16)
        for h in range(nh):
            sl = slice(h * d, (h + 1) * d)
            lf_row = lf_rows[f0 + h:f0 + h + 1, :]
            bc = jnp.sum(tri * lf_row, axis=1, keepdims=True)
            br = jnp.sum(tri_t * lf_cols[:, f0 + h:f0 + h + 1], axis=0, keepdims=True)
            lir = gt[i0 + h:i0 + h + 1, :]
            m_prev = m_ref[h, 0:1, 0:1]
            dmat = jnp.where(mask, bc - br + lir, -jnp.inf)
            inter = bc + m_prev
            m_t = jnp.maximum(inter, jnp.max(dmat, axis=1, keepdims=True))
            dexp = jnp.exp(dmat - m_t)
            inter_w = jnp.exp(inter - m_t)
            qh = (q_ref[0, 0, :, sl] * (d ** -0.5)).astype(BF16)
            kt = k_ref[0, 0, :, sl].T
            v_ext = jnp.concatenate([v_ref[0, :, sl].astype(BF16), ones_col], axis=1)
            s = jnp.dot(qh, kt.astype(BF16), preferred_element_type=F32) * dexp
            c_ext = c_ref[h]
            acc = inter_w * jnp.dot(qh, c_ext.astype(BF16), preferred_element_type=F32)
            acc = acc + jnp.dot(s.astype(BF16), v_ext, preferred_element_type=F32)
            den = jnp.maximum(jnp.abs(acc[:, d:d + 1]), jnp.exp(-m_t))
            o_ref[0, :, sl] = acc[:, :d] / den
            b_last = jnp.sum(lf_row, axis=1, keepdims=True)
            g_row = b_last - br + lir
            m_new = jnp.maximum(b_last + m_prev, jnp.max(g_row, axis=1, keepdims=True))
            a = jnp.exp(b_last + m_prev - m_new)
            w_row = jnp.exp(g_row - m_new)
            c_ref[h] = a * c_ext + jnp.dot((kt * w_row).astype(BF16), v_ext, preferred_element_type=F32)
            m_ref[h] = jnp.broadcast_to(m_new, m_ref.shape[1:])

    @pl.when(step < nc_ctx)
    def _():
        chunk(*ins[:4], outs[0])

    @pl.when(step >= nc_ctx)
    def _():
        chunk(*ins[4:], outs[1])


def mlstm_bidir(qk_c, proj_c, small_c, qk_l, proj_l, small_l, gate_bias):
    _, bsz, lc, w = qk_c.shape
    ll = qk_l.shape[2]
    nh = ML_HEADS
    t_c, t_l = min(ML_CHUNK, lc), min(ML_CHUNK_LATENT, ll)
    nc_c, nc_l = lc // t_c, ll // t_l
    bias = jnp.zeros((1, N_SMALL), F32).at[0, S_GATES:S_GATES + ML_GATES].set(gate_bias)

    def idx(dd, step_off, nc):
        def f(g):
            cc = jnp.clip(g - step_off, 0, nc - 1)
            return cc + dd * (nc - 1 - 2 * cc)
        return f

    vcol = P_MLV // w

    def seg_specs(ix, t):
        return [pl.BlockSpec((1, 1, t, w), lambda b, g: (0, b, ix(g), 0)),
                pl.BlockSpec((1, 1, t, w), lambda b, g: (1, b, ix(g), 0)),
                pl.BlockSpec((1, t, w), lambda b, g: (b, ix(g), vcol)),
                pl.BlockSpec((1, t, N_SMALL), lambda b, g: (b, ix(g), 0))]

    def out_spec(ix, t):
        return pl.BlockSpec((1, t, w), lambda b, g: (b, ix(g), 0))

    h_c, h_l = [], []
    for dd in range(2):
        ic, il = idx(dd, 0, nc_c), idx(dd, nc_c, nc_l)
        hc, hl = pl.pallas_call(
            functools.partial(_mlstm_kernel, nc_ctx=nc_c, direction=dd),
            grid=(bsz, nc_c + nc_l),
            in_specs=seg_specs(ic, t_c) + seg_specs(il, t_l) + [pl.BlockSpec((1, N_SMALL), lambda b, g: (0, 0))],
            out_specs=[out_spec(ic, t_c), out_spec(il, t_l)],
            out_shape=[jax.ShapeDtypeStruct((bsz, lc, w), F32), jax.ShapeDtypeStruct((bsz, ll, w), F32)],
            scratch_shapes=[pltpu.VMEM((nh, ML_HEAD_DIM, 2 * ML_HEAD_DIM), F32), pltpu.VMEM((nh, 8, 128), F32)],
            compiler_params=_cparams(("parallel", "arbitrary")),
            name="mlstm_fwd" if dd == 0 else "mlstm_bwd",
        )(qk_c, qk_c, proj_c, small_c, qk_l, qk_l, proj_l, small_l, bias)
        h_c.append(hc)
        h_l.append(hl)
    return tuple(h_c), tuple(h_l)


def _rms(x):
    return x * lax.rsqrt(jnp.mean(x * x, axis=-1, keepdims=True) + EPS)


def _swap_rope_halves(x):
    w = x.shape[1]
    lane = lax.broadcasted_iota(jnp.int32, x.shape, 1) % MLA_ROPE
    return jnp.where(lane < MLA_ROPE // 2, pltpu.roll(x, w - MLA_ROPE // 2, axis=1),
                     pltpu.roll(x, MLA_ROPE // 2, axis=1))


def _mla_prep_kernel(pq_ref, pkv_ref, sm_ref, gqa_ref, gkva_ref, wq_ref, wkv_ref, gqn_ref, gqr_ref, gkn_ref,
                     gkr_ref, en_ref, er_ref, *rest, use_rope):
    if use_rope:
        ccq_ref, ssq_ref, cck_ref, ssk_ref, qt_ref, k_ref, vt_ref = rest
    else:
        qt_ref, k_ref, vt_ref = rest
    nh, dn, dr = MLA_HEADS, MLA_NOPE, MLA_ROPE
    qa = (_rms(pq_ref[0].astype(F32)) * gqa_ref[...]).astype(BF16)
    kva = (_rms(pkv_ref[0].astype(F32)) * gkva_ref[...]).astype(BF16)
    q_raw = jnp.dot(qa, wq_ref[...], preferred_element_type=F32)
    kv_raw = jnp.dot(kva, wkv_ref[...], preferred_element_type=F32)
    inv_d = 1.0 / MLA_QK

    sel = functools.partial(jnp.dot, preferred_element_type=F32)
    qn, qr = q_raw[:, :nh * dn], q_raw[:, nh * dn:]
    ss = sel((qn * qn).astype(BF16), en_ref[...]) + sel((qr * qr).astype(BF16), er_ref[...])
    rs_rows = lax.rsqrt(ss * inv_d + EPS).T
    qn = qn * gqn_ref[...]
    qr = qr * gqr_ref[...]
    if use_rope:
        qr = qr * ccq_ref[...] + _swap_rope_halves(qr) * ssq_ref[...]
    qrt = qr.T
    tl = qn.shape[0]
    for h in range(nh):
        rs_h = rs_rows[h:h + 1, :]
        qt_ref[0, h, 0:dn, :] = (qn[:, h * dn:(h + 1) * dn].T * rs_h).astype(BF16)
        qt_ref[0, h, dn:dn + dr, :] = (qrt[h * dr:(h + 1) * dr] * rs_h).astype(BF16)
        qt_ref[0, h, dn + dr:, :] = jnp.zeros((QK_PAD - dn - dr, tl), BF16)

    kn, vv = kv_raw[:, :nh * dn], kv_raw[:, nh * dn:]
    lane = lax.broadcasted_iota(jnp.int32, sm_ref.shape[1:], 1)
    kr = jnp.where(lane < dr, sm_ref[0], 0.0)
    ssk = sel((kn * kn).astype(BF16), en_ref[...]) + jnp.sum(kr * kr, axis=-1, keepdims=True)
    rsk = lax.rsqrt(ssk * inv_d + EPS)
    kn = kn * gkn_ref[...]
    kr = kr * gkr_ref[...]
    if use_rope:
        kr = kr * cck_ref[...] + _swap_rope_halves(kr) * ssk_ref[...]
    for h in range(nh):
        rs_h = jnp.broadcast_to(rsk[:, h:h + 1], (tl, dn))
        k_ref[0, h, :, 0:dn] = (kn[:, h * dn:(h + 1) * dn] * rs_h).astype(BF16)
        k_ref[0, h, :, dn:] = (kr * rs_h).astype(BF16)
        vt_ref[0, h] = vv[:, h * MLA_V:(h + 1) * MLA_V].T.astype(BF16)


@functools.lru_cache(maxsize=None)
def _head_selectors():
    en = np.kron(np.eye(MLA_HEADS), np.ones((MLA_NOPE, 1)))
    er = np.kron(np.eye(MLA_HEADS), np.ones((MLA_ROPE, 1)))
    pad = lambda m: np.pad(m, ((0, 0), (0, 128 - m.shape[1]))).astype(np.float32)
    return pad(en), pad(er)


@functools.lru_cache(maxsize=None)
def _rope_tables(n_tokens):
    f32 = np.float32
    rows = n_tokens // GRID_W
    row = np.repeat(np.arange(rows, dtype=f32), GRID_W)
    col = np.tile(np.arange(GRID_W, dtype=f32), rows)
    freqs = f32(ROPE_BASE) ** (-np.arange(ROPE_PAIRS_PER_AXIS, dtype=f32) / f32(ROPE_PAIRS_PER_AXIS))
    ang = np.concatenate([row[:, None] * freqs, col[:, None] * freqs], axis=-1).astype(f32)
    cos, sin = np.cos(ang).astype(f32), np.sin(ang).astype(f32)
    cc = np.concatenate([cos, cos], axis=-1)
    ss = np.concatenate([-sin, sin], axis=-1)
    zero = np.zeros((n_tokens, 128 - MLA_ROPE), f32)
    return (np.tile(cc, (1, MLA_HEADS)), np.tile(ss, (1, MLA_HEADS)),
            np.concatenate([cc, zero], axis=-1), np.concatenate([ss, zero], axis=-1))


def mla_prep(proj, small, p, rope_tables, *, tl):
    b, l, _ = proj.shape
    nh = MLA_HEADS
    en, er = _head_selectors()
    en_b, er_b = jnp.asarray(en, BF16), jnp.asarray(er, BF16)
    const = lambda shape: pl.BlockSpec(shape, lambda bi, i: (0,) * len(shape))
    nq = nh * MLA_QK
    nkv = nh * (MLA_NOPE + MLA_V)
    in_specs = [pl.BlockSpec((1, tl, MLA_Q_LORA), lambda bi, i: (bi, i, P_CQ // MLA_Q_LORA)),
                pl.BlockSpec((1, tl, MLA_KV_LORA), lambda bi, i: (bi, i, P_CKV // MLA_KV_LORA)),
                pl.BlockSpec((1, tl, N_SMALL), lambda bi, i: (bi, i, 0)),
                const((1, MLA_Q_LORA)), const((1, MLA_KV_LORA)), const((MLA_Q_LORA, nq)), const((MLA_KV_LORA, nkv)),
                const((1, nh * MLA_NOPE)), const((1, nh * MLA_ROPE)), const((1, nh * MLA_NOPE)), const((1, 128)),
                const(en.shape), const(er.shape)]
    operands = [proj, proj, small, p['gqa'], p['gkva'], p['wq'], p['wkv'], p['gqn'], p['gqr'], p['gkn'], p['gkr'],
                en_b, er_b]
    use_rope = rope_tables is not None
    if use_rope:
        in_specs += [pl.BlockSpec((tl, nh * MLA_ROPE), lambda bi, i: (i, 0)),
                     pl.BlockSpec((tl, nh * MLA_ROPE), lambda bi, i: (i, 0)),
                     pl.BlockSpec((tl, 128), lambda bi, i: (i, 0)),
                     pl.BlockSpec((tl, 128), lambda bi, i: (i, 0))]
        operands += list(rope_tables)
    return pl.pallas_call(
        functools.partial(_mla_prep_kernel, use_rope=use_rope),
        grid=(b, l // tl),
        in_specs=in_specs,
        out_specs=[pl.BlockSpec((1, nh, QK_PAD, tl), lambda bi, i: (bi, 0, 0, i)),
                   pl.BlockSpec((1, nh, tl, QK_PAD), lambda bi, i: (bi, 0, i, 0)),
                   pl.BlockSpec((1, nh, MLA_V, tl), lambda bi, i: (bi, 0, 0, i))],
        out_shape=[jax.ShapeDtypeStruct((b, nh, QK_PAD, l), BF16),
                   jax.ShapeDtypeStruct((b, nh, l, QK_PAD), BF16),
                   jax.ShapeDtypeStruct((b, nh, MLA_V, l), BF16)],
        compiler_params=_cparams(("parallel", "parallel")),
        name="mla_prep",
    )(*operands)


def _combine_out_kernel(a_ref, y_ref, hf_ref, hb_ref, o_ref, g_ref, w_ref, r_ref, gt_ref, out_ref, cat_ref):
    @pl.when(pl.program_id(2) == 0)
    def _():
        g = g_ref[...]
        cat_ref[:, :MLA_WIDTH] = (_rms(a_ref[0]) * g[:, :MLA_WIDTH]).astype(BF16)
        hy0 = MLA_WIDTH
        cat_ref[:, hy0:hy0 + HY_WIDTH] = (_rms(y_ref[0]) * g[:, hy0:hy0 + HY_WIDTH]).astype(BF16)
        ml0 = MLA_WIDTH + HY_WIDTH
        hh = hf_ref[0] + hb_ref[0]
        og = 0.5 * jnp.tanh(0.5 * o_ref[0].astype(F32)) + 0.5
        for hd in range(ML_HEADS):
            sl = slice(hd * ML_HEAD_DIM, (hd + 1) * ML_HEAD_DIM)
            gl = g[:, ml0 + hd * ML_HEAD_DIM:ml0 + (hd + 1) * ML_HEAD_DIM]
            cat_ref[:, ml0 + hd * ML_HEAD_DIM:ml0 + (hd + 1) * ML_HEAD_DIM] = (
                _rms(hh[:, sl]) * gl * og[:, sl]).astype(BF16)

    y = jnp.dot(cat_ref[...], w_ref[...], preferred_element_type=F32)
    out_ref[0] = r_ref[0] + gt_ref[0] * y


def combine_out_proj(a, y, h2, proj, g_mix, w_out, res, gate, *, tm, tn, name="out_proj"):
    b, s, d = res.shape
    return pl.pallas_call(
        _combine_out_kernel,
        grid=(b, s // tm, d // tn),
        in_specs=[pl.BlockSpec((1, tm, MLA_WIDTH), lambda bi, i, j: (bi, i, 0)),
                  pl.BlockSpec((1, tm, HY_WIDTH), lambda bi, i, j: (bi, i, 0)),
                  pl.BlockSpec((1, tm, ML_WIDTH), lambda bi, i, j: (bi, i, 0)),
                  pl.BlockSpec((1, tm, ML_WIDTH), lambda bi, i, j: (bi, i, 0)),
                  pl.BlockSpec((1, tm, ML_WIDTH), lambda bi, i, j: (bi, i, P_MLO // ML_WIDTH)),
                  pl.BlockSpec((1, MIX_WIDTH), lambda bi, i, j: (0, 0)),
                  pl.BlockSpec((MIX_WIDTH, tn), lambda bi, i, j: (0, j)),
                  pl.BlockSpec((1, tm, tn), lambda bi, i, j: (bi, i, j)),
                  pl.BlockSpec((1, 1, tn), lambda bi, i, j: (bi, 0, j))],
        out_specs=pl.BlockSpec((1, tm, tn), lambda bi, i, j: (bi, i, j)),
        out_shape=jax.ShapeDtypeStruct((b, s, d), F32),
        scratch_shapes=[pltpu.VMEM((tm, MIX_WIDTH), BF16)],
        compiler_params=_cparams(("parallel", "parallel", "arbitrary")),
        name=name,
    )(a, y, h2[0], h2[1], proj, g_mix.reshape(1, MIX_WIDTH), w_out, res, gate)


@functools.lru_cache(maxsize=None)
def _direct_dft_tables(l):
    n = 2 * l
    k = np.arange(n)[:, None]
    t = np.arange(l)[None, :]
    ang = 2.0 * np.pi * ((k * t) % n) / n
    c, s = np.cos(ang), np.sin(ang)
    gf = np.block([[c, s], [-s, c]])
    gi = gf.T.copy()
    t_full = np.arange(n)[None, :]
    ang = 2.0 * np.pi * ((k * t_full) % n) / n
    g_real = np.concatenate([np.cos(ang), -np.sin(ang)], axis=0)
    return gf.astype(np.float32), gi.astype(np.float32), g_real.astype(np.float32)


def _direct_conv_kernel(gf_ref, gi_ref, z_ref, gate_ref, h_ref, skip_ref, o_ref, *, scale):
    z = z_ref[0]
    x = jnp.dot(gf_ref[...].astype(BF16), z.astype(BF16), preferred_element_type=F32)
    n = x.shape[0] // 2
    xr, xi = x[:n], x[n:]
    hr, hi = h_ref[:n] * scale, h_ref[n:] * scale
    prod = jnp.concatenate([xr * hr - xi * hi, xr * hi + xi * hr], axis=0).astype(BF16)
    conv = jnp.dot(gi_ref[...].astype(BF16), prod, preferred_element_type=F32)
    o_ref[0] = gate_ref[0] * (conv + skip_ref[...] * z)


def hyena_short(proj, p):
    bsz, l, _ = proj.shape
    c = HY_WIDTH
    npair = bsz // 2
    n = 2 * l
    gf, gi, g_real = _direct_dft_tables(l)
    buf = hyena_filter_buffer(l, p, tl=l)
    hspec = dft_first(jnp.asarray(g_real), buf.reshape(1, n, HY_ORDER * c), n_out=1, tn=HY_ORDER * c,
                      precise=True)[0]
    vx = short_conv_groups(proj, P_HY, p['hy_conv_w'], p['hy_conv_b'], group=c, tc=CONV_TC, silu=False,
                           name="hyena_short_conv_ctx")
    vx = vx.reshape(3 * npair, 2 * l, c)
    z = vx

    def rows_at(off):
        return lambda q: (off + q, 0, 0)

    def cols_at(col):
        return lambda q: (0, col)

    for order in range(HY_ORDER):
        z = pl.pallas_call(
            functools.partial(_direct_conv_kernel, scale=1.0 / n),
            grid=(npair,),
            in_specs=[pl.BlockSpec(gf.shape, lambda q: (0, 0)),
                      pl.BlockSpec(gi.shape, lambda q: (0, 0)),
                      pl.BlockSpec((1, 2 * l, c), rows_at(0)),
                      pl.BlockSpec((1, 2 * l, c), rows_at((order + 1) * npair)),
                      pl.BlockSpec((2 * n, c), cols_at(order)),
                      pl.BlockSpec((1, c), lambda q: (0, 0))],
            out_specs=pl.BlockSpec((1, 2 * l, c), lambda q: (q, 0, 0)),
            out_shape=jax.ShapeDtypeStruct((npair, 2 * l, c), F32),
            compiler_params=_cparams(("parallel",)),
            name="hyena_direct_conv",
        )(jnp.asarray(gf), jnp.asarray(gi), z, vx, hspec, p['hy_skip'][order].reshape(1, c))
    return z.reshape(bsz, l, c)


def _split_cols(w, sizes):
    out, start = [], 0
    for s in sizes:
        out.append(w[:, start:start + s])
        start += s
    return out


def _prep_layer_weights(l, w_in, mla_qa_norm, mla_kva_norm, mla_w_uq, mla_w_ukv, mla_q_norm, mla_k_norm):
    cq, ckv, kr, hy, mqk, mv, mo, gt = _split_cols(w_in[l], IN_SIZES)
    w_main = jnp.concatenate([cq, ckv, hy, mqk, mv, mo], axis=1).astype(BF16)
    pad = jnp.zeros((D_MODEL, N_SMALL - MLA_ROPE - ML_GATES), F32)
    w_small = jnp.concatenate([kr, gt, pad], axis=1).astype(BF16)
    nh = MLA_HEADS
    wq = mla_w_uq[l].reshape(MLA_Q_LORA, nh, MLA_QK)
    wq = jnp.concatenate([wq[:, :, :MLA_NOPE].reshape(MLA_Q_LORA, -1), wq[:, :, MLA_NOPE:].reshape(MLA_Q_LORA, -1)],
                         axis=1).astype(BF16)
    wkv = mla_w_ukv[l].reshape(MLA_KV_LORA, nh, MLA_NOPE + MLA_V)
    wkv = jnp.concatenate([wkv[:, :, :MLA_NOPE].reshape(MLA_KV_LORA, -1),
                           wkv[:, :, MLA_NOPE:].reshape(MLA_KV_LORA, -1)], axis=1).astype(BF16)
    q_scale = (MLA_QK ** -0.5) * math.log2(math.e)
    gq, gk = mla_q_norm[l] * q_scale, mla_k_norm[l]
    mla = {'gqa': mla_qa_norm[l].reshape(1, -1), 'gkva': mla_kva_norm[l].reshape(1, -1), 'wq': wq, 'wkv': wkv,
           'gqn': jnp.tile(gq[:MLA_NOPE], nh).reshape(1, -1), 'gqr': jnp.tile(gq[MLA_NOPE:], nh).reshape(1, -1),
           'gkn': jnp.tile(gk[:MLA_NOPE], nh).reshape(1, -1),
           'gkr': jnp.concatenate([gk[MLA_NOPE:], jnp.zeros((128 - MLA_ROPE,), F32)]).reshape(1, -1)}
    return w_main, w_small, mla


def kernel(x, c, ctx, c_ctx, ada_w, ada_b, norm1_g, norm2_g, w_in, mla_qa_norm, mla_kva_norm, mla_w_uq,
           mla_w_ukv, mla_q_norm, mla_k_norm, hy_conv_w, hy_conv_b, hy_w1, hy_b1, hy_w2, hy_b2, hy_w3,
           hy_decay, hy_skip, ml_conv_w, ml_conv_b, ml_gate_b, mix_norm_g, w_out, ffn_w1, ffn_w2):
    B, S, D = x.shape
    LC = ctx.shape[1]
    rope_tables = _rope_tables(S)

    silu_rows = jnp.concatenate([jax.nn.silu(c), jax.nn.silu(c_ctx)[None], jnp.zeros((8 - B - 1, D), F32)], axis=0)
    mods = ada_modulation(silu_rows.astype(BF16), ada_w, ada_b)

    for l in range(DEPTH):
        need_ctx = l < DEPTH - 1
        p = {'hy_conv_w': hy_conv_w[l], 'hy_conv_b': hy_conv_b[l], 'hy_w1': hy_w1[l], 'hy_b1': hy_b1[l],
             'hy_w2': hy_w2[l], 'hy_b2': hy_b2[l], 'hy_w3': hy_w3[l], 'hy_decay': hy_decay[l],
             'hy_skip': hy_skip[l], 'ml_conv_w': ml_conv_w[l], 'ml_conv_b': ml_conv_b[l]}
        w_main, w_small, mla = _prep_layer_weights(l, w_in, mla_qa_norm, mla_kva_norm, mla_w_uq, mla_w_ukv,
                                                   mla_q_norm, mla_k_norm)
        w_out_l = w_out[l].astype(BF16)
        w1_l = ffn_w1[l].astype(BF16)
        w2_l = ffn_w2[l].astype(BF16)

        mod_l = [m[:, None, :] for m in jnp.split(mods[l, :B], 6, axis=-1)]
        mod_c = [m[:, None, :] for m in jnp.split(mods[l, B:B + 1], 6, axis=-1)]
        flat = lambda t: t.reshape(1, B * LC, t.shape[-1])

        proj_l, small_l = in_proj(x, norm1_g[l], mod_l[1], mod_l[0], w_main, w_small, tm=1024, tn=IN_PROJ_TN)
        proj_c, small_c = in_proj(flat(ctx), norm1_g[l], mod_c[1], mod_c[0], w_main, w_small, tm=B * LC,
                                  tn=IN_PROJ_TN, name="in_proj_ctx")
        proj_c, small_c = proj_c.reshape(B, LC, N_MAIN), small_c.reshape(B, LC, N_SMALL)

        qt_l, k_l, vt_l = mla_prep(proj_l, small_l, mla, rope_tables, tl=512)
        qt_c, k_c, vt_c = mla_prep(proj_c, small_c, mla, None, tl=LC)
        a_l = attention(qt_l, [(k_c, vt_c), (k_l, vt_l)], tq=ATT_TQ)
        y_l = hyena_long(proj_l, p)

        def mlstm_qk(proj):
            return short_conv_groups(proj, P_MLQK, p['ml_conv_w'], p['ml_conv_b'], group=ML_WIDTH, tc=CONV_TC,
                                     silu=True, name="mlstm_short_conv")

        h_c, h_l = mlstm_bidir(mlstm_qk(proj_c), proj_c, small_c, mlstm_qk(proj_l), proj_l, small_l, ml_gate_b[l])
        x = combine_out_proj(a_l, y_l, h_l, proj_l, mix_norm_g[l], w_out_l, x, mod_l[2], tm=1024, tn=MM_TN)
        hid = norm_swiglu(x, norm2_g[l], mod_l[4], mod_l[3], w1_l, tm=1024, tn=MM_TN, name="ffn_up")
        x = mm_residual(hid, w2_l, x, mod_l[5], tm=1024, tn=MM_TN, name="ffn_down")

        if need_ctx:
            a_c = attention(qt_c, [(k_c, vt_c)], tq=LC)
            y_c = hyena_short(proj_c, p)
            ctx = combine_out_proj(flat(a_c), flat(y_c), (flat(h_c[0]), flat(h_c[1])), flat(proj_c), mix_norm_g[l],
                                   w_out_l, flat(ctx), mod_c[2], tm=B * LC, tn=MM_TN, name="out_proj_ctx")
            hid = norm_swiglu(ctx, norm2_g[l], mod_c[4], mod_c[3], w1_l, tm=B * LC, tn=MM_TN, name="ffn_up_ctx")
            ctx = mm_residual(hid, w2_l, ctx, mod_c[5], tm=B * LC, tn=MM_TN, name="ffn_down_ctx")
            ctx = ctx.reshape(B, LC, D)
    return x
```
